```python
import jax, jax.numpy as jnp
from jax import lax
import numpy as np

D_MODEL = 2048
BATCH = 8
SEQ = 8192
DEPTH = 4

MIX_WIDTH = D_MODEL
ATT_HEADS = 8
ATT_KV_HEADS = 2
ATT_WIDTH = MIX_WIDTH // 2
ATT_HEAD_DIM = ATT_WIDTH // ATT_HEADS
KV_WIDTH = ATT_KV_HEADS * ATT_HEAD_DIM
WINDOW = 128
ATT_BLOCK = 128
ROPE_THETA = 10000.0
M_WIDTH = MIX_WIDTH - ATT_WIDTH
M_HEADS = 4
M_HEAD_DIM = M_WIDTH // M_HEADS
M_CHUNK = 128
CONV_WIDTH = 5
D_FF = ((8 * D_MODEL // 3 + 255) // 256) * 256
EPS = 1e-6
IN_SIZES = (ATT_WIDTH, KV_WIDTH, KV_WIDTH, M_WIDTH, M_WIDTH, M_WIDTH, M_WIDTH, 4 * M_HEADS)
IN_WIDTH = sum(IN_SIZES)
IN_SPLITS = tuple(int(s) for s in np.cumsum(IN_SIZES[:-1]))

kernel_name = "hymba_style_mlstm_swa_macaron_encoder"


def rms_norm(x, g):
    xf = x.astype(jnp.float32)
    y = xf * lax.rsqrt(jnp.mean(xf * xf, axis=-1, keepdims=True) + EPS)
    return (y * g.astype(jnp.float32)).astype(x.dtype)


def swiglu(x, w_gate, w_up, w_down):
    return (jax.nn.silu(x @ w_gate) * (x @ w_up)) @ w_down


def rope(x, pos):
    half = x.shape[-1] // 2
    inv_freq = ROPE_THETA ** (-jnp.arange(half, dtype=jnp.float32) / half)
    ang = pos[:, None] * inv_freq[None, :]
    cos = jnp.cos(ang)[None, :, None, :]
    sin = jnp.sin(ang)[None, :, None, :]
    xf = x.astype(jnp.float32)
    x1, x2 = xf[..., :half], xf[..., half:]
    out = jnp.concatenate([x1 * cos - x2 * sin, x2 * cos + x1 * sin], axis=-1)
    return out.astype(x.dtype)


def window_attention(q, k, v, sink):
    B, S, _, D = q.shape
    nb = S // ATT_BLOCK
    G = ATT_HEADS // ATT_KV_HEADS
    qb = q.reshape(B, nb, ATT_BLOCK, ATT_KV_HEADS, G, D)

    def neighbours(t):
        tp = jnp.pad(t, ((0, 0), (ATT_BLOCK, ATT_BLOCK), (0, 0), (0, 0)))
        tp = tp.reshape(B, nb + 2, ATT_BLOCK, ATT_KV_HEADS, D)
        return jnp.concatenate([tp[:, :-2], tp[:, 1:-1], tp[:, 2:]], axis=2)

    kb, vb = neighbours(k), neighbours(v)
    scores = jnp.einsum('bnqhgd,bnkhd->bnhgqk', qb, kb).astype(jnp.float32) * (D ** -0.5)
    r = jnp.arange(ATT_BLOCK)[:, None]
    c = jnp.arange(3 * ATT_BLOCK)[None, :]
    kpos = (jnp.arange(nb)[:, None, None] - 1) * ATT_BLOCK + c[None]
    mask = (jnp.abs(c - ATT_BLOCK - r) <= WINDOW)[None] & (kpos >= 0) & (kpos < S)
    scores = jnp.where(mask[None, :, None, None], scores, -jnp.inf)
    sink_l = sink.astype(jnp.float32).reshape(ATT_KV_HEADS, G)[None, None, :, :, None, None]
    m = jnp.maximum(scores.max(axis=-1, keepdims=True), sink_l)
    p = jnp.exp(scores - m)
    probs = p / (p.sum(axis=-1, keepdims=True) + jnp.exp(sink_l - m))
    out = jnp.einsum('bnhgqk,bnkhd->bnqhgd', probs.astype(v.dtype), vb)
    return out.reshape(B, S, ATT_HEADS * D)


def mlstm_scan(q, k, v, ig, lf):
    B, S, H, D = q.shape
    L = M_CHUNK
    nc = S // L
    to_c = lambda t: t.reshape(B, nc, L, H, D).transpose(1, 0, 3, 2, 4)
    to_cg = lambda g: g.reshape(B, nc, L, H).transpose(1, 0, 3, 2)
    tri = jnp.tril(jnp.ones((L, L), dtype=bool))

    def step(carry, xs):
        C, n, m = carry
        qc, kc, vc, igc, lfc = xs
        b = jnp.cumsum(lfc, axis=-1)
        log_d = b[..., :, None] - b[..., None, :] + igc[..., None, :]
        log_d = jnp.where(tri, log_d, -jnp.inf)
        log_inter = b + m[..., None]
        m_t = jnp.maximum(log_inter, log_d.max(axis=-1))
        d_mat = jnp.exp(log_d - m_t[..., None])
        inter = jnp.exp(log_inter - m_t)
        s = jnp.einsum('bhtd,bhsd->bhts', qc, kc) * d_mat
        num = jnp.einsum('bhts,bhsd->bhtd', s, vc) + inter[..., None] * jnp.einsum('bhvk,bhtk->bhtv', C, qc)
        den = s.sum(axis=-1) + inter * jnp.einsum('bhk,bhtk->bht', n, qc)
        h = num / jnp.maximum(jnp.abs(den), jnp.exp(-m_t))[..., None]
        b_last = b[..., -1]
        log_w = b_last[..., None] - b + igc
        m_new = jnp.maximum(b_last + m, log_w.max(axis=-1))
        w = jnp.exp(log_w - m_new[..., None])
        decay = jnp.exp(b_last + m - m_new)
        C_new = decay[..., None, None] * C + jnp.einsum('bhsv,bhsk->bhvk', w[..., None] * vc, kc)
        n_new = decay[..., None] * n + jnp.einsum('bhs,bhsk->bhk', w, kc)
        return (C_new, n_new, m_new), h

    init = (jnp.zeros((B, H, D, D), jnp.float32), jnp.zeros((B, H, D), jnp.float32),
            jnp.zeros((B, H), jnp.float32))
    _, hs = lax.scan(step, init, (to_c(q), to_c(k), to_c(v), to_cg(ig), to_cg(lf)))
    return hs.transpose(1, 0, 3, 2, 4).reshape(B, S, H, D)


def depthwise_conv_centred(x, w):
    C = x.shape[-1]
    pad = CONV_WIDTH // 2
    return lax.conv_general_dilated(x, w[:, None, :].astype(x.dtype), window_strides=(1,),
                                    padding=[(pad, pad)], dimension_numbers=('NWC', 'WIO', 'NWC'),
                                    feature_group_count=C)


def hybrid_mixer(h, w_in, b_gate, conv_w, sink, m_norm, w_out, pos):
    B, S, _ = h.shape
    proj = h @ w_in
    qa, ka, va, qm, km, vm, om, gates = jnp.split(proj, IN_SPLITS, axis=-1)
    qa = rope(qa.reshape(B, S, ATT_HEADS, ATT_HEAD_DIM), pos)
    ka = rope(ka.reshape(B, S, ATT_KV_HEADS, ATT_HEAD_DIM), pos)
    va = va.reshape(B, S, ATT_KV_HEADS, ATT_HEAD_DIM)
    y_att = window_attention(qa, ka, va, sink)
    qk = jax.nn.silu(depthwise_conv_centred(jnp.concatenate([qm, km], axis=-1), conv_w))
    qm, km = qk[..., :M_WIDTH], qk[..., M_WIDTH:]
    qm = qm.astype(jnp.float32).reshape(B, S, M_HEADS, M_HEAD_DIM)
    km = km.astype(jnp.float32).reshape(B, S, M_HEADS, M_HEAD_DIM) * (M_HEAD_DIM ** -0.5)
    vm = vm.astype(jnp.float32).reshape(B, S, M_HEADS, M_HEAD_DIM)
    g = gates.astype(jnp.float32) + b_gate.astype(jnp.float32)
    ig_f, ig_b, fg_f, fg_b = jnp.split(g, 4, axis=-1)
    h_fwd = mlstm_scan(qm, km, vm, ig_f, jax.nn.log_sigmoid(fg_f))
    flip = lambda t: jnp.flip(t, axis=1)
    h_bwd = flip(mlstm_scan(flip(qm), flip(km), flip(vm), flip(ig_b), flip(jax.nn.log_sigmoid(fg_b))))
    hm = h_fwd + h_bwd
    hm = hm * lax.rsqrt(jnp.mean(hm * hm, axis=-1, keepdims=True) + EPS)
    hm = hm * m_norm.astype(jnp.float32).reshape(M_HEADS, M_HEAD_DIM)
    y_m = (jax.nn.sigmoid(om.astype(jnp.float32)) * hm.reshape(B, S, M_WIDTH)).astype(h.dtype)
    return jnp.concatenate([y_att, y_m], axis=-1) @ w_out


def _fwd_setup_inputs(seed: int = 0) -> dict:
    key = jax.random.key(seed)
    ks = jax.random.split(key, 24)
    f32 = jnp.float32
    nrm = lambda k, shape, scale: jax.random.normal(k, shape, f32) * scale
    gain = lambda k: 1.0 + nrm(k, (DEPTH, D_MODEL), 0.05)
    f_bias = jnp.linspace(3.0, 6.0, M_HEADS, dtype=f32)
    b_gate = jnp.concatenate([
        nrm(ks[20], (DEPTH, 2 * M_HEADS), 0.1),
        jnp.tile(f_bias, 2)[None, :] + nrm(ks[21], (DEPTH, 2 * M_HEADS), 0.1),
    ], axis=-1)
    return {
        "x": nrm(ks[0], (BATCH, SEQ, D_MODEL), 1.0),
        "ffn1_norm_pre": gain(ks[1]),
        "ffn1_norm_post": gain(ks[2]),
        "ffn1_w_gate": nrm(ks[3], (DEPTH, D_MODEL, D_FF), D_MODEL ** -0.5),
        "ffn1_w_up": nrm(ks[4], (DEPTH, D_MODEL, D_FF), D_MODEL ** -0.5),
        "ffn1_w_down": nrm(ks[5], (DEPTH, D_FF, D_MODEL), D_FF ** -0.5),
        "mix_norm_pre": gain(ks[6]),
        "mix_norm_post": gain(ks[7]),
        "w_in": nrm(ks[8], (DEPTH, D_MODEL, IN_WIDTH), D_MODEL ** -0.5),
        "b_gate": b_gate,
        "conv_w": nrm(ks[9], (DEPTH, CONV_WIDTH, 2 * M_WIDTH), CONV_WIDTH ** -0.5),
        "attn_sink": nrm(ks[10], (DEPTH, ATT_HEADS), 0.5),
        "mlstm_norm": 1.0 + nrm(ks[11], (DEPTH, M_WIDTH), 0.05),
        "w_out": nrm(ks[12], (DEPTH, MIX_WIDTH, D_MODEL), MIX_WIDTH ** -0.5),
        "ffn2_norm_pre": gain(ks[13]),
        "ffn2_norm_post": gain(ks[14]),
        "ffn2_w_gate": nrm(ks[15], (DEPTH, D_MODEL, D_FF), D_MODEL ** -0.5),
        "ffn2_w_up": nrm(ks[16], (DEPTH, D_MODEL, D_FF), D_MODEL ** -0.5),
        "ffn2_w_down": nrm(ks[17], (DEPTH, D_FF, D_MODEL), D_FF ** -0.5),
    }


def _fwd_reference(x, ffn1_norm_pre, ffn1_norm_post, ffn1_w_gate, ffn1_w_up, ffn1_w_down,
              mix_norm_pre, mix_norm_post, w_in, b_gate, conv_w, attn_sink, mlstm_norm, w_out,
              ffn2_norm_pre, ffn2_norm_post, ffn2_w_gate, ffn2_w_up, ffn2_w_down):
    pos = jnp.arange(x.shape[1], dtype=jnp.float32)
    for l in range(DEPTH):
        f = swiglu(rms_norm(x, ffn1_norm_pre[l]), ffn1_w_gate[l], ffn1_w_up[l], ffn1_w_down[l])
        x = x + 0.5 * rms_norm(f, ffn1_norm_post[l])
        m = hybrid_mixer(rms_norm(x, mix_norm_pre[l]), w_in[l], b_gate[l], conv_w[l],
                         attn_sink[l], mlstm_norm[l], w_out[l], pos)
        x = x + rms_norm(m, mix_norm_post[l])
        f = swiglu(rms_norm(x, ffn2_norm_pre[l]), ffn2_w_gate[l], ffn2_w_up[l], ffn2_w_down[l])
        x = x + 0.5 * rms_norm(f, ffn2_norm_post[l])
    return x


import jax as _jax
import jax.numpy as _jnp

TWIN_FORMAT = 'train_step'
FWD_PARAMS = ['x', 'ffn1_norm_pre', 'ffn1_norm_post', 'ffn1_w_gate', 'ffn1_w_up', 'ffn1_w_down', 'mix_norm_pre', 'mix_norm_post', 'w_in', 'b_gate', 'conv_w', 'attn_sink', 'mlstm_norm', 'w_out', 'ffn2_norm_pre', 'ffn2_norm_post', 'ffn2_w_gate', 'ffn2_w_up', 'ffn2_w_down']
TWIN_WEIGHTS = ['ffn1_norm_pre', 'ffn1_norm_post', 'ffn1_w_gate', 'ffn1_w_up', 'ffn1_w_down', 'mix_norm_pre', 'mix_norm_post', 'w_in', 'b_gate', 'conv_w', 'attn_sink', 'mlstm_norm', 'w_out', 'ffn2_norm_pre', 'ffn2_norm_post', 'ffn2_w_gate', 'ffn2_w_up', 'ffn2_w_down']
TWIN_DIFF_INPUT = 'x'
TWIN_INPUTS = ['x', 'ffn1_norm_pre', 'ffn1_norm_post', 'ffn1_w_gate', 'ffn1_w_up', 'ffn1_w_down', 'mix_norm_pre', 'mix_norm_post', 'w_in', 'b_gate', 'conv_w', 'attn_sink', 'mlstm_norm', 'w_out', 'ffn2_norm_pre', 'ffn2_norm_post', 'ffn2_w_gate', 'ffn2_w_up', 'ffn2_w_down', 'loss_target', 'm_ffn1_norm_pre', 'm_ffn1_norm_post', 'm_ffn1_w_gate', 'm_ffn1_w_up', 'm_ffn1_w_down', 'm_mix_norm_pre', 'm_mix_norm_post', 'm_w_in', 'm_b_gate', 'm_conv_w', 'm_attn_sink', 'm_mlstm_norm', 'm_w_out', 'm_ffn2_norm_pre', 'm_ffn2_norm_post', 'm_ffn2_w_gate', 'm_ffn2_w_up', 'm_ffn2_w_down', 'v_ffn1_norm_pre', 'v_ffn1_norm_post', 'v_ffn1_w_gate', 'v_ffn1_w_up', 'v_ffn1_w_down', 'v_mix_norm_pre', 'v_mix_norm_post', 'v_w_in', 'v_b_gate', 'v_conv_w', 'v_attn_sink', 'v_mlstm_norm', 'v_w_out', 'v_ffn2_norm_pre', 'v_ffn2_norm_post', 'v_ffn2_w_gate', 'v_ffn2_w_up', 'v_ffn2_w_down']
TWIN_OUTPUTS = ['loss', 'grad_x', 'grad_ffn1_norm_pre', 'grad_ffn1_norm_post', 'grad_ffn1_w_gate', 'grad_ffn1_w_up', 'grad_ffn1_w_down', 'grad_mix_norm_pre', 'grad_mix_norm_post', 'grad_w_in', 'grad_b_gate', 'grad_conv_w', 'grad_attn_sink', 'grad_mlstm_norm', 'grad_w_out', 'grad_ffn2_norm_pre', 'grad_ffn2_norm_post', 'grad_ffn2_w_gate', 'grad_ffn2_w_up', 'grad_ffn2_w_down', 'delta_ffn1_norm_pre', 'delta_ffn1_norm_post', 'delta_ffn1_w_gate', 'delta_ffn1_w_up', 'delta_ffn1_w_down', 'delta_mix_norm_pre', 'delta_mix_norm_post', 'delta_w_in', 'delta_b_gate', 'delta_conv_w', 'delta_attn_sink', 'delta_mlstm_norm', 'delta_w_out', 'delta_ffn2_norm_pre', 'delta_ffn2_norm_post', 'delta_ffn2_w_gate', 'delta_ffn2_w_up', 'delta_ffn2_w_down', 'new_m_ffn1_norm_pre', 'new_m_ffn1_norm_post', 'new_m_ffn1_w_gate', 'new_m_ffn1_w_up', 'new_m_ffn1_w_down', 'new_m_mix_norm_pre', 'new_m_mix_norm_post', 'new_m_w_in', 'new_m_b_gate', 'new_m_conv_w', 'new_m_attn_sink', 'new_m_mlstm_norm', 'new_m_w_out', 'new_m_ffn2_norm_pre', 'new_m_ffn2_norm_post', 'new_m_ffn2_w_gate', 'new_m_ffn2_w_up', 'new_m_ffn2_w_down', 'new_v_ffn1_norm_pre', 'new_v_ffn1_norm_post', 'new_v_ffn1_w_gate', 'new_v_ffn1_w_up', 'new_v_ffn1_w_down', 'new_v_mix_norm_pre', 'new_v_mix_norm_post', 'new_v_w_in', 'new_v_b_gate', 'new_v_conv_w', 'new_v_attn_sink', 'new_v_mlstm_norm', 'new_v_w_out', 'new_v_ffn2_norm_pre', 'new_v_ffn2_norm_post', 'new_v_ffn2_w_gate', 'new_v_ffn2_w_up', 'new_v_ffn2_w_down']
TWIN_LEAF_KINDS = {'loss': 'loss', 'grad_x': 'grad_x', 'grad_ffn1_norm_pre': 'grad_w', 'grad_ffn1_norm_post': 'grad_w', 'grad_ffn1_w_gate': 'grad_w', 'grad_ffn1_w_up': 'grad_w', 'grad_ffn1_w_down': 'grad_w', 'grad_mix_norm_pre': 'grad_w', 'grad_mix_norm_post': 'grad_w', 'grad_w_in': 'grad_w', 'grad_b_gate': 'grad_w', 'grad_conv_w': 'grad_w', 'grad_attn_sink': 'grad_w', 'grad_mlstm_norm': 'grad_w', 'grad_w_out': 'grad_w', 'grad_ffn2_norm_pre': 'grad_w', 'grad_ffn2_norm_post': 'grad_w', 'grad_ffn2_w_gate': 'grad_w', 'grad_ffn2_w_up': 'grad_w', 'grad_ffn2_w_down': 'grad_w', 'delta_ffn1_norm_pre': 'delta_w', 'delta_ffn1_norm_post': 'delta_w', 'delta_ffn1_w_gate': 'delta_w', 'delta_ffn1_w_up': 'delta_w', 'delta_ffn1_w_down': 'delta_w', 'delta_mix_norm_pre': 'delta_w', 'delta_mix_norm_post': 'delta_w', 'delta_w_in': 'delta_w', 'delta_b_gate': 'delta_w', 'delta_conv_w': 'delta_w', 'delta_attn_sink': 'delta_w', 'delta_mlstm_norm': 'delta_w', 'delta_w_out': 'delta_w', 'delta_ffn2_norm_pre': 'delta_w', 'delta_ffn2_norm_post': 'delta_w', 'delta_ffn2_w_gate': 'delta_w', 'delta_ffn2_w_up': 'delta_w', 'delta_ffn2_w_down': 'delta_w', 'new_m_ffn1_norm_pre': 'new_m', 'new_m_ffn1_norm_post': 'new_m', 'new_m_ffn1_w_gate': 'new_m', 'new_m_ffn1_w_up': 'new_m', 'new_m_ffn1_w_down': 'new_m', 'new_m_mix_norm_pre': 'new_m', 'new_m_mix_norm_post': 'new_m', 'new_m_w_in': 'new_m', 'new_m_b_gate': 'new_m', 'new_m_conv_w': 'new_m', 'new_m_attn_sink': 'new_m', 'new_m_mlstm_norm': 'new_m', 'new_m_w_out': 'new_m', 'new_m_ffn2_norm_pre': 'new_m', 'new_m_ffn2_norm_post': 'new_m', 'new_m_ffn2_w_gate': 'new_m', 'new_m_ffn2_w_up': 'new_m', 'new_m_ffn2_w_down': 'new_m', 'new_v_ffn1_norm_pre': 'new_v', 'new_v_ffn1_norm_post': 'new_v', 'new_v_ffn1_w_gate': 'new_v', 'new_v_ffn1_w_up': 'new_v', 'new_v_ffn1_w_down': 'new_v', 'new_v_mix_norm_pre': 'new_v', 'new_v_mix_norm_post': 'new_v', 'new_v_w_in': 'new_v', 'new_v_b_gate': 'new_v', 'new_v_conv_w': 'new_v', 'new_v_attn_sink': 'new_v', 'new_v_mlstm_norm': 'new_v', 'new_v_w_out': 'new_v', 'new_v_ffn2_norm_pre': 'new_v', 'new_v_ffn2_norm_post': 'new_v', 'new_v_ffn2_w_gate': 'new_v', 'new_v_ffn2_w_up': 'new_v', 'new_v_ffn2_w_down': 'new_v'}


def _forward(args):
    return _fwd_reference(*[args[k] for k in FWD_PARAMS])


def _output_shape():
    def fwd():
        inp = _fwd_setup_inputs(0)
        return _fwd_reference(*[inp[k] for k in FWD_PARAMS])
    out = _jax.eval_shape(fwd)
    return out.shape, out.dtype

N_MICROBATCH = 1
ADAM_LR = 0.001
ADAM_B1 = 0.9
ADAM_B2 = 0.999
ADAM_EPS = 1e-08
ADAM_WD = 0.01
ADAM_STEP = 10
PER_EXAMPLE_BATCH_AXIS = {'x': 0, 'loss_target': 0}
SHARED_INPUTS = []
_WEIGHT_DTYPES = {'ffn1_norm_pre': _jnp.float32, 'ffn1_norm_post': _jnp.float32, 'ffn1_w_gate': _jnp.float32, 'ffn1_w_up': _jnp.float32, 'ffn1_w_down': _jnp.float32, 'mix_norm_pre': _jnp.float32, 'mix_norm_post': _jnp.float32, 'w_in': _jnp.float32, 'b_gate': _jnp.float32, 'conv_w': _jnp.float32, 'attn_sink': _jnp.float32, 'mlstm_norm': _jnp.float32, 'w_out': _jnp.float32, 'ffn2_norm_pre': _jnp.float32, 'ffn2_norm_post': _jnp.float32, 'ffn2_w_gate': _jnp.float32, 'ffn2_w_up': _jnp.float32, 'ffn2_w_down': _jnp.float32}
MOMENT_SCALE = {'ffn1_norm_pre': 5.225815e+00, 'ffn1_norm_post': 7.687446e+00, 'ffn1_w_gate': 2.031147e+00, 'ffn1_w_up': 2.082098e+00, 'ffn1_w_down': 3.522748e+00, 'mix_norm_pre': 1.153463e+01, 'mix_norm_post': 3.067359e+01, 'w_in': 6.126113e+00, 'b_gate': 2.273454e+01, 'conv_w': 1.220204e+00, 'attn_sink': 2.983920e-01, 'mlstm_norm': 1.068031e+01, 'w_out': 8.772680e+00, 'ffn2_norm_pre': 1.923414e+00, 'ffn2_norm_post': 7.372755e+00, 'ffn2_w_gate': 6.968613e-01, 'ffn2_w_up': 9.184379e-01, 'ffn2_w_down': 1.534112e+00}


def _to_microbatches(a, axis):
    t = _jnp.moveaxis(a, axis, 0)
    t = t.reshape((N_MICROBATCH, t.shape[0] // N_MICROBATCH) + t.shape[1:])
    return _jnp.moveaxis(t, 1, axis + 1)


def setup_inputs(seed: int = 0) -> dict:
    inp = _fwd_setup_inputs(seed)
    key = _jax.random.fold_in(_jax.random.key(seed), 7919)
    shape, _ = _output_shape()
    out = dict(inp)
    out["loss_target"] = _jax.random.normal(_jax.random.fold_in(key, 0), shape, _jnp.float32)
    for i, name in enumerate(TWIN_WEIGHTS):
        w = inp[name].astype(_jnp.float32)
        if MOMENT_SCALE is None:
            s = _jnp.sqrt(_jnp.mean(_jnp.square(w)) + 1e-30)
        else:
            s = MOMENT_SCALE[name]
        km, kv = _jax.random.split(_jax.random.fold_in(key, i + 1))
        out[name] = w
        out["m_" + name] = s * _jax.random.normal(km, w.shape, _jnp.float32)
        out["v_" + name] = (s * s) * _jax.random.uniform(kv, w.shape, _jnp.float32, 0.5, 1.5)
    if N_MICROBATCH > 1:
        for name, axis in PER_EXAMPLE_BATCH_AXIS.items():
            out[name] = _to_microbatches(out[name], axis)
    return {'x': out['x'], 'ffn1_norm_pre': out['ffn1_norm_pre'], 'ffn1_norm_post': out['ffn1_norm_post'], 'ffn1_w_gate': out['ffn1_w_gate'], 'ffn1_w_up': out['ffn1_w_up'], 'ffn1_w_down': out['ffn1_w_down'], 'mix_norm_pre': out['mix_norm_pre'], 'mix_norm_post': out['mix_norm_post'], 'w_in': out['w_in'], 'b_gate': out['b_gate'], 'conv_w': out['conv_w'], 'attn_sink': out['attn_sink'], 'mlstm_norm': out['mlstm_norm'], 'w_out': out['w_out'], 'ffn2_norm_pre': out['ffn2_norm_pre'], 'ffn2_norm_post': out['ffn2_norm_post'], 'ffn2_w_gate': out['ffn2_w_gate'], 'ffn2_w_up': out['ffn2_w_up'], 'ffn2_w_down': out['ffn2_w_down'], 'loss_target': out['loss_target'], 'm_ffn1_norm_pre': out['m_ffn1_norm_pre'], 'm_ffn1_norm_post': out['m_ffn1_norm_post'], 'm_ffn1_w_gate': out['m_ffn1_w_gate'], 'm_ffn1_w_up': out['m_ffn1_w_up'], 'm_ffn1_w_down': out['m_ffn1_w_down'], 'm_mix_norm_pre': out['m_mix_norm_pre'], 'm_mix_norm_post': out['m_mix_norm_post'], 'm_w_in': out['m_w_in'], 'm_b_gate': out['m_b_gate'], 'm_conv_w': out['m_conv_w'], 'm_attn_sink': out['m_attn_sink'], 'm_mlstm_norm': out['m_mlstm_norm'], 'm_w_out': out['m_w_out'], 'm_ffn2_norm_pre': out['m_ffn2_norm_pre'], 'm_ffn2_norm_post': out['m_ffn2_norm_post'], 'm_ffn2_w_gate': out['m_ffn2_w_gate'], 'm_ffn2_w_up': out['m_ffn2_w_up'], 'm_ffn2_w_down': out['m_ffn2_w_down'], 'v_ffn1_norm_pre': out['v_ffn1_norm_pre'], 'v_ffn1_norm_post': out['v_ffn1_norm_post'], 'v_ffn1_w_gate': out['v_ffn1_w_gate'], 'v_ffn1_w_up': out['v_ffn1_w_up'], 'v_ffn1_w_down': out['v_ffn1_w_down'], 'v_mix_norm_pre': out['v_mix_norm_pre'], 'v_mix_norm_post': out['v_mix_norm_post'], 'v_w_in': out['v_w_in'], 'v_b_gate': out['v_b_gate'], 'v_conv_w': out['v_conv_w'], 'v_attn_sink': out['v_attn_sink'], 'v_mlstm_norm': out['v_mlstm_norm'], 'v_w_out': out['v_w_out'], 'v_ffn2_norm_pre': out['v_ffn2_norm_pre'], 'v_ffn2_norm_post': out['v_ffn2_norm_post'], 'v_ffn2_w_gate': out['v_ffn2_w_gate'], 'v_ffn2_w_up': out['v_ffn2_w_up'], 'v_ffn2_w_down': out['v_ffn2_w_down']}


def _loss(weights, diff, rest, loss_target):
    with _jax.named_scope("forward"):
        args = {**rest, TWIN_DIFF_INPUT: diff, **{k: w.astype(_WEIGHT_DTYPES[k]) for k, w in weights.items()}}
        y = _forward(args)
    with _jax.named_scope("loss_head"):
        err = _jnp.square(y.astype(_jnp.float32) - loss_target)
        return 0.5 * _jnp.sum(_jnp.mean(err, axis=-1)) if err.ndim else 0.5 * err


def _adamw(w, g, m, v):
    m = ADAM_B1 * m + (1.0 - ADAM_B1) * g
    v = ADAM_B2 * v + (1.0 - ADAM_B2) * _jnp.square(g)
    m_hat = m / (1.0 - ADAM_B1 ** ADAM_STEP)
    v_hat = v / (1.0 - ADAM_B2 ** ADAM_STEP)
    delta = -ADAM_LR * (m_hat / (_jnp.sqrt(v_hat) + ADAM_EPS) + ADAM_WD * w)
    return delta, m, v


def reference(x, ffn1_norm_pre, ffn1_norm_post, ffn1_w_gate, ffn1_w_up, ffn1_w_down, mix_norm_pre, mix_norm_post, w_in, b_gate, conv_w, attn_sink, mlstm_norm, w_out, ffn2_norm_pre, ffn2_norm_post, ffn2_w_gate, ffn2_w_up, ffn2_w_down, loss_target, m_ffn1_norm_pre, m_ffn1_norm_post, m_ffn1_w_gate, m_ffn1_w_up, m_ffn1_w_down, m_mix_norm_pre, m_mix_norm_post, m_w_in, m_b_gate, m_conv_w, m_attn_sink, m_mlstm_norm, m_w_out, m_ffn2_norm_pre, m_ffn2_norm_post, m_ffn2_w_gate, m_ffn2_w_up, m_ffn2_w_down, v_ffn1_norm_pre, v_ffn1_norm_post, v_ffn1_w_gate, v_ffn1_w_up, v_ffn1_w_down, v_mix_norm_pre, v_mix_norm_post, v_w_in, v_b_gate, v_conv_w, v_attn_sink, v_mlstm_norm, v_w_out, v_ffn2_norm_pre, v_ffn2_norm_post, v_ffn2_w_gate, v_ffn2_w_up, v_ffn2_w_down):
    given = dict(x=x, ffn1_norm_pre=ffn1_norm_pre, ffn1_norm_post=ffn1_norm_post, ffn1_w_gate=ffn1_w_gate, ffn1_w_up=ffn1_w_up, ffn1_w_down=ffn1_w_down, mix_norm_pre=mix_norm_pre, mix_norm_post=mix_norm_post, w_in=w_in, b_gate=b_gate, conv_w=conv_w, attn_sink=attn_sink, mlstm_norm=mlstm_norm, w_out=w_out, ffn2_norm_pre=ffn2_norm_pre, ffn2_norm_post=ffn2_norm_post, ffn2_w_gate=ffn2_w_gate, ffn2_w_up=ffn2_w_up, ffn2_w_down=ffn2_w_down, loss_target=loss_target, m_ffn1_norm_pre=m_ffn1_norm_pre, m_ffn1_norm_post=m_ffn1_norm_post, m_ffn1_w_gate=m_ffn1_w_gate, m_ffn1_w_up=m_ffn1_w_up, m_ffn1_w_down=m_ffn1_w_down, m_mix_norm_pre=m_mix_norm_pre, m_mix_norm_post=m_mix_norm_post, m_w_in=m_w_in, m_b_gate=m_b_gate, m_conv_w=m_conv_w, m_attn_sink=m_attn_sink, m_mlstm_norm=m_mlstm_norm, m_w_out=m_w_out, m_ffn2_norm_pre=m_ffn2_norm_pre, m_ffn2_norm_post=m_ffn2_norm_post, m_ffn2_w_gate=m_ffn2_w_gate, m_ffn2_w_up=m_ffn2_w_up, m_ffn2_w_down=m_ffn2_w_down, v_ffn1_norm_pre=v_ffn1_norm_pre, v_ffn1_norm_post=v_ffn1_norm_post, v_ffn1_w_gate=v_ffn1_w_gate, v_ffn1_w_up=v_ffn1_w_up, v_ffn1_w_down=v_ffn1_w_down, v_mix_norm_pre=v_mix_norm_pre, v_mix_norm_post=v_mix_norm_post, v_w_in=v_w_in, v_b_gate=v_b_gate, v_conv_w=v_conv_w, v_attn_sink=v_attn_sink, v_mlstm_norm=v_mlstm_norm, v_w_out=v_w_out, v_ffn2_norm_pre=v_ffn2_norm_pre, v_ffn2_norm_post=v_ffn2_norm_post, v_ffn2_w_gate=v_ffn2_w_gate, v_ffn2_w_up=v_ffn2_w_up, v_ffn2_w_down=v_ffn2_w_down)
    weights = {n: given[n] for n in TWIN_WEIGHTS}
    shared = {n: given[n] for n in SHARED_INPUTS}
    per_example = {n: given[n] for n in ['x']}
    grad_fn = _jax.value_and_grad(_loss, argnums=(0, 1))

    def one_microbatch(ex, loss_target):
        ex = dict(ex)
        diff = ex.pop(TWIN_DIFF_INPUT)
        return grad_fn(weights, diff, {**shared, **ex}, loss_target)

    if N_MICROBATCH == 1:
        loss, (grad_w, grad_x) = one_microbatch(per_example, given["loss_target"])
    else:
        def body(carry, xs):
            loss_sum, grad_sum = carry
            l_k, (gw_k, gx_k) = one_microbatch(xs[0], xs[1])
            with _jax.named_scope("update"):
                return (loss_sum + l_k, _jax.tree.map(_jnp.add, grad_sum, gw_k)), gx_k

        init = (_jnp.zeros((), _jnp.float32), _jax.tree.map(_jnp.zeros_like, weights))
        (loss, grad_w), grad_x = _jax.lax.scan(body, init, (per_example, given["loss_target"]))
    with _jax.named_scope("update"):
        delta_w, new_m, new_v = {}, {}, {}
        for n in TWIN_WEIGHTS:
            delta_w[n], new_m[n], new_v[n] = _adamw(weights[n], grad_w[n], given["m_" + n], given["v_" + n])
    return (loss, grad_x, *[grad_w[n] for n in TWIN_WEIGHTS], *[delta_w[n] for n in TWIN_WEIGHTS],
            *[new_m[n] for n in TWIN_WEIGHTS], *[new_v[n] for n in TWIN_WEIGHTS])
```

```python
import jax
import jax.numpy as jnp
from jax import lax
from jax.experimental import pallas as pl
from jax.experimental.pallas import tpu as pltpu

F32 = jnp.float32
BF16 = jnp.bfloat16
MESH = pl.DeviceIdType.MESH
ANY = pl.BlockSpec(memory_space=pl.ANY)

VMEM_LIMIT_BYTES = 56 * 1024 * 1024
LANES = 128

EPS = 1e-6
ATT_HEADS = 8
ATT_KV_HEADS = 2
ATT_GROUP = ATT_HEADS // ATT_KV_HEADS
ATT_HEAD_DIM = 128
ATT_WIDTH = ATT_HEADS * ATT_HEAD_DIM
KV_WIDTH = ATT_KV_HEADS * ATT_HEAD_DIM
BLK = 128
M_HEADS = 4
M_HEAD_DIM = 256
M_WIDTH = M_HEADS * M_HEAD_DIM
CONV_WIDTH = 5
CONV_HALO = 8
ROPE_THETA = 10000.0
N_GATES = 4 * M_HEADS
OFF_QA, OFF_KA, OFF_VA = 0, ATT_WIDTH, ATT_WIDTH + KV_WIDTH
OFF_QM = ATT_WIDTH + 2 * KV_WIDTH
OFF_KM = OFF_QM + M_WIDTH
OFF_VM = OFF_KM + M_WIDTH
OFF_OM = OFF_VM + M_WIDTH
OFF_G = OFF_OM + M_WIDTH
IN_WIDTH = OFF_G + N_GATES
IN_PAD = OFF_G + LANES
NEG_BIG = -1e30

ADAM_LR, ADAM_B1, ADAM_B2, ADAM_EPS, ADAM_WD, ADAM_STEP = 0.001, 0.9, 0.999, 1e-08, 0.01, 10


def _params(**kw):
    return pltpu.CompilerParams(vmem_limit_bytes=VMEM_LIMIT_BYTES, **kw)


def _dot(a, b):
    return lax.dot_general(a, b, (((1,), (0,)), ((), ())), preferred_element_type=F32)


def _dot_nt(a, b):
    return lax.dot_general(a, b, (((1,), (1,)), ((), ())), preferred_element_type=F32)


def _dot_tn(a, b):
    return lax.dot_general(a, b, (((0,), (0,)), ((), ())), preferred_element_type=F32)


def _sigmoid(x):
    return 1.0 / (1.0 + jnp.exp(-x))


def _eye_mask(n):
    r = lax.broadcasted_iota(jnp.int32, (n, n), 0)
    c = lax.broadcasted_iota(jnp.int32, (n, n), 1)
    return r == c


def _row_to_col(row, eye):
    n = eye.shape[0]
    return jnp.sum(jnp.where(eye, jnp.broadcast_to(row, (n, n)), 0.0), axis=1, keepdims=True)


def _col_to_row(col, eye):
    n = eye.shape[0]
    return jnp.sum(jnp.where(eye, jnp.broadcast_to(col, (n, n)), 0.0), axis=0, keepdims=True)


def _mm(a, b, kind, *, tm, tn, tk, name, out_dtype=F32, ja=None, jb=None):
    a2, b2 = a.shape[-2:], b.shape[-2:]
    if kind == "nn":
        (M, K), (_, N) = a2, b2
    elif kind == "nt":
        (M, K), (N, _) = a2, b2
    else:
        (K, M), (_, N) = a2, b2
    J = a.shape[0] if ja else (b.shape[0] if jb else 1)
    batch = "b" in (ja, jb)
    red = "r" in (ja, jb)
    nk = K // tk
    nr = nk * (J if red else 1)
    grid = ((J if batch else 1), M // tm, N // tn, nr)

    def lead(mode, g, r):
        return g if mode == "b" else r // nk

    def a_map(g, i, n, r):
        kk = r % nk
        idx = (i, kk) if kind != "tn" else (kk, i)
        return idx if ja is None else (lead(ja, g, r),) + idx

    def b_map(g, i, n, r):
        kk = r % nk
        idx = (kk, n) if kind != "nt" else (n, kk)
        return idx if jb is None else (lead(jb, g, r),) + idx

    def o_map(g, i, n, r):
        return (g, i, n) if batch else (i, n)

    a_blk = (tm, tk) if kind != "tn" else (tk, tm)
    b_blk = (tk, tn) if kind != "nt" else (tn, tk)
    dot = {"nn": _dot, "nt": _dot_nt, "tn": _dot_tn}[kind]

    def body(a_ref, b_ref, o_ref, *scratch):
        part = dot(a_ref[...], b_ref[...])
        if nr == 1:
            o_ref[...] = part.astype(out_dtype)
        else:
            acc = scratch[0]
            r = pl.program_id(3)

            @pl.when(r == 0)
            def _():
                acc[...] = part

            @pl.when(r > 0)
            def _():
                acc[...] += part

            @pl.when(r == nr - 1)
            def _():
                o_ref[...] = acc[...].astype(out_dtype)

    return pl.pallas_call(
        body, name=name, grid=grid,
        in_specs=[pl.BlockSpec(a_blk if ja is None else (None,) + a_blk, a_map),
                  pl.BlockSpec(b_blk if jb is None else (None,) + b_blk, b_map)],
        out_specs=pl.BlockSpec((None, tm, tn) if batch else (tm, tn), o_map),
        out_shape=jax.ShapeDtypeStruct((J, M, N) if batch else (M, N), out_dtype),
        scratch_shapes=[pltpu.VMEM((tm, tn), F32)] if nr > 1 else [],
        compiler_params=_params(),
    )(a, b)


def _row_tile(S, want):
    return min(S, want)


ELEMENTWISE_TILE_BYTES = 1 << 20


def _elementwise_rows(R, C):
    for cand in (1024, 512, 256, 128, 64, 32, 16, 8):
        if R % cand == 0 and R > cand and cand * C * 4 <= ELEMENTWISE_TILE_BYTES:
            return cand
    return R if R * C * 4 <= ELEMENTWISE_TILE_BYTES or R % 8 else 8


def _ffn_up(xn, wgu):
    S, D = xn.shape
    J, _, F2 = wgu.shape
    F = F2 // 2
    tm = _row_tile(S, 256)

    def body(x_ref, w_ref, gu_ref, h_ref):
        gu = _dot(x_ref[...], w_ref[...])
        g, u = gu[:, :F], gu[:, F:]
        gu_ref[...] = gu.astype(BF16)
        h_ref[...] = (g * _sigmoid(g) * u).astype(BF16)

    return pl.pallas_call(
        body, name="ffn_up", grid=(J, S // tm),
        in_specs=[pl.BlockSpec((tm, D), lambda j, i: (i, 0)),
                  pl.BlockSpec((None, D, F2), lambda j, i: (j, 0, 0))],
        out_specs=[pl.BlockSpec((None, tm, F2), lambda j, i: (j, i, 0)),
                   pl.BlockSpec((None, tm, F), lambda j, i: (j, i, 0))],
        out_shape=[jax.ShapeDtypeStruct((J, S, F2), BF16), jax.ShapeDtypeStruct((J, S, F), BF16)],
        compiler_params=_params(),
    )(xn, wgu)


def _ffn_bwd_hidden(df, wd, gu):
    S, D = df.shape
    J, F, _ = wd.shape
    F2 = 2 * F
    tm = _row_tile(S, 256)

    def body(df_ref, w_ref, gu_ref, o_ref):
        dh = _dot_nt(df_ref[...], w_ref[...])
        g = gu_ref[:, :F].astype(F32)
        u = gu_ref[:, F:].astype(F32)
        sg = _sigmoid(g)
        o_ref[:, :F] = (dh * u * (sg * (1.0 + g * (1.0 - sg)))).astype(BF16)
        o_ref[:, F:] = (dh * (g * sg)).astype(BF16)

    return pl.pallas_call(
        body, name="ffn_bwd_hidden", grid=(J, S // tm),
        in_specs=[pl.BlockSpec((tm, D), lambda j, i: (i, 0)),
                  pl.BlockSpec((None, F, D), lambda j, i: (j, 0, 0)),
                  pl.BlockSpec((None, tm, F2), lambda j, i: (j, i, 0))],
        out_specs=pl.BlockSpec((None, tm, F2), lambda j, i: (j, i, 0)),
        out_shape=jax.ShapeDtypeStruct((J, S, F2), BF16),
        compiler_params=_params(),
    )(df, wd, gu)


def _norm_fwd(x, g):
    S, D = x.shape
    tm = _row_tile(S, 512)

    def body(x_ref, g_ref, o_ref):
        xv = x_ref[...]
        r = lax.rsqrt(jnp.mean(xv * xv, axis=-1, keepdims=True) + EPS)
        o_ref[...] = (xv * r * g_ref[...]).astype(BF16)

    return pl.pallas_call(
        body, name="norm_fwd", grid=(S // tm,),
        in_specs=[pl.BlockSpec((tm, D), lambda i: (i, 0)), pl.BlockSpec((1, D), lambda i: (0, 0))],
        out_specs=pl.BlockSpec((tm, D), lambda i: (i, 0)),
        out_shape=jax.ShapeDtypeStruct((S, D), BF16),
        compiler_params=_params(),
    )(x, g.reshape(1, D))


def _resid_norm_fwd(x, f, g, alpha):
    S, D = x.shape
    tm = _row_tile(S, 512)

    def body(x_ref, f_ref, g_ref, o_ref):
        fv = f_ref[...]
        r = lax.rsqrt(jnp.mean(fv * fv, axis=-1, keepdims=True) + EPS)
        o_ref[...] = x_ref[...] + alpha * (fv * r * g_ref[...])

    return pl.pallas_call(
        body, name="resid_norm_fwd", grid=(S // tm,),
        in_specs=[pl.BlockSpec((tm, D), lambda i: (i, 0)), pl.BlockSpec((tm, D), lambda i: (i, 0)),
                  pl.BlockSpec((1, D), lambda i: (0, 0))],
        out_specs=pl.BlockSpec((tm, D), lambda i: (i, 0)),
        out_shape=jax.ShapeDtypeStruct((S, D), F32),
        compiler_params=_params(),
    )(x, f, g.reshape(1, D))


def _norm_bwd(x, g, dy, alpha, resid=None, out_dtype=F32):
    S, D = x.shape
    tm = _row_tile(S, 256)
    has_resid = resid is not None

    def body(*refs):
        x_ref, g_ref, dy_ref = refs[:3]
        res_ref = refs[3] if has_resid else None
        dx_ref, dg_ref = refs[-2:]
        xv = x_ref[...]
        r = lax.rsqrt(jnp.mean(xv * xv, axis=-1, keepdims=True) + EPS)
        xh = xv * r
        dyv = dy_ref[...].astype(F32) * alpha
        gdy = dyv * g_ref[...]
        dx = r * (gdy - xh * jnp.mean(xh * gdy, axis=-1, keepdims=True))
        if has_resid:
            dx = dx + res_ref[...]
        dx_ref[...] = dx.astype(out_dtype)
        part = jnp.sum(dyv * xh, axis=0, keepdims=True)

        @pl.when(pl.program_id(0) == 0)
        def _():
            dg_ref[...] = part

        @pl.when(pl.program_id(0) > 0)
        def _():
            dg_ref[...] += part

    row = pl.BlockSpec((tm, D), lambda i: (i, 0))
    vec = pl.BlockSpec((1, D), lambda i: (0, 0))
    ins = [x, g.reshape(1, D), dy] + ([resid] if has_resid else [])
    dx, dg = pl.pallas_call(
        body, name="norm_bwd_res" if has_resid else "norm_bwd", grid=(S // tm,),
        in_specs=[row, vec, row] + ([row] if has_resid else []),
        out_specs=[row, vec],
        out_shape=[jax.ShapeDtypeStruct((S, D), out_dtype), jax.ShapeDtypeStruct((1, D), F32)],
        compiler_params=_params(),
    )(*ins)
    return dx, dg.reshape(D)


def _loss_grad(y, target):
    S, D = y.shape
    tm = _row_tile(S, 512)

    def body(y_ref, t_ref, dy_ref, l_ref):
        err = y_ref[...] - t_ref[...]
        dy_ref[...] = err * (1.0 / D)
        part = jnp.sum(jnp.sum(err * err, axis=-1, keepdims=True) * (0.5 / D), axis=0, keepdims=True)
        part = jnp.broadcast_to(part, (8, LANES))

        @pl.when(pl.program_id(0) == 0)
        def _():
            l_ref[...] = part

        @pl.when(pl.program_id(0) > 0)
        def _():
            l_ref[...] += part

    row = pl.BlockSpec((tm, D), lambda i: (i, 0))
    return pl.pallas_call(
        body, name="loss_grad", grid=(S // tm,),
        in_specs=[row, row],
        out_specs=[row, pl.BlockSpec((8, LANES), lambda i: (0, 0))],
        out_shape=[jax.ShapeDtypeStruct((S, D), F32), jax.ShapeDtypeStruct((8, LANES), F32)],
        compiler_params=_params(),
    )(y, target)


def _add_n(xs, name):
    shape = xs[0].shape
    C = shape[-1]
    R = 1
    for s in shape[:-1]:
        R *= s
    tm = _elementwise_rows(R, C)

    def body(*refs):
        acc = refs[0][...]
        for r in refs[1:-1]:
            acc = acc + r[...]
        refs[-1][...] = acc

    row = pl.BlockSpec((tm, C), lambda i: (i, 0))
    out = pl.pallas_call(
        body, name=name, grid=(R // tm,),
        in_specs=[row] * len(xs), out_specs=row,
        out_shape=jax.ShapeDtypeStruct((R, C), F32),
        compiler_params=_params(),
    )(*[x.reshape(R, C) for x in xs])
    return out.reshape(shape)


def _rope_tables(S):
    half = ATT_HEAD_DIM // 2
    inv_freq = ROPE_THETA ** (-jnp.arange(half, dtype=F32) / half)
    ang = jnp.arange(S, dtype=F32)[:, None] * inv_freq[None, :]
    cos, sin = jnp.cos(ang), jnp.sin(ang)
    return jnp.concatenate([cos, cos], axis=-1), jnp.concatenate([-sin, sin], axis=-1)


def _rotate(x, cosf, sinf):
    return x * cosf + pltpu.roll(x, ATT_HEAD_DIM // 2, 1) * sinf


def _rope_fwd(proj, cosf, sinf):
    S = proj.shape[0]
    tm = _row_tile(S, 512)

    def body(q_ref, k_ref, v_ref, c_ref, s_ref, qo_ref, ko_ref, vo_ref):
        c, s = c_ref[...], s_ref[...]
        for h in range(ATT_HEADS):
            sl = slice(h * ATT_HEAD_DIM, (h + 1) * ATT_HEAD_DIM)
            qo_ref[:, sl] = _rotate(q_ref[:, sl], c, s).astype(BF16)
        for h in range(ATT_KV_HEADS):
            sl = slice(h * ATT_HEAD_DIM, (h + 1) * ATT_HEAD_DIM)
            ko_ref[:, sl] = _rotate(k_ref[:, sl], c, s).astype(BF16)
        vo_ref[...] = v_ref[...].astype(BF16)

    tab = pl.BlockSpec((tm, ATT_HEAD_DIM), lambda i: (i, 0))
    return pl.pallas_call(
        body, name="rope_fwd", grid=(S // tm,),
        in_specs=[pl.BlockSpec((tm, ATT_WIDTH), lambda i: (i, 0)),
                  pl.BlockSpec((tm, KV_WIDTH), lambda i: (i, OFF_KA // KV_WIDTH)),
                  pl.BlockSpec((tm, KV_WIDTH), lambda i: (i, OFF_VA // KV_WIDTH)), tab, tab],
        out_specs=[pl.BlockSpec((tm, ATT_WIDTH), lambda i: (i, 0)),
                   pl.BlockSpec((tm, KV_WIDTH), lambda i: (i, 0)),
                   pl.BlockSpec((tm, KV_WIDTH), lambda i: (i, 0))],
        out_shape=[jax.ShapeDtypeStruct((S, ATT_WIDTH), BF16), jax.ShapeDtypeStruct((S, KV_WIDTH), BF16),
                   jax.ShapeDtypeStruct((S, KV_WIDTH), BF16)],
        compiler_params=_params(),
    )(proj, proj, proj, cosf, sinf)


def _rope_bwd(dq, dk, dv, cosf, sinf):
    S = dq.shape[0]
    tm = _row_tile(S, 512)
    W = ATT_WIDTH + 2 * KV_WIDTH

    def body(q_ref, k_ref, v_ref, c_ref, s_ref, o_ref):
        c, s = c_ref[...], -s_ref[...]
        for h in range(ATT_HEADS):
            sl = slice(h * ATT_HEAD_DIM, (h + 1) * ATT_HEAD_DIM)
            o_ref[:, sl] = _rotate(q_ref[:, sl], c, s).astype(BF16)
        for h in range(ATT_KV_HEADS):
            sl = slice(h * ATT_HEAD_DIM, (h + 1) * ATT_HEAD_DIM)
            o_ref[:, ATT_WIDTH + h * ATT_HEAD_DIM:ATT_WIDTH + (h + 1) * ATT_HEAD_DIM] = (
                _rotate(k_ref[:, sl], c, s).astype(BF16))
        o_ref[:, ATT_WIDTH + KV_WIDTH:] = v_ref[...].astype(BF16)

    tab = pl.BlockSpec((tm, ATT_HEAD_DIM), lambda i: (i, 0))
    return pl.pallas_call(
        body, name="rope_bwd", grid=(S // tm,),
        in_specs=[pl.BlockSpec((tm, ATT_WIDTH), lambda i: (i, 0)),
                  pl.BlockSpec((tm, KV_WIDTH), lambda i: (i, 0)),
                  pl.BlockSpec((tm, KV_WIDTH), lambda i: (i, 0)), tab, tab],
        out_specs=pl.BlockSpec((tm, W), lambda i: (i, 0)),
        out_shape=jax.ShapeDtypeStruct((S, W), BF16),
        compiler_params=_params(),
    )(dq, dk, dv, cosf, sinf)


def _attn_probs(q_ref, k_refs, sink_ref, n, nb):
    G, L = ATT_GROUP, BLK
    q4 = jnp.concatenate([q_ref[:, g * ATT_HEAD_DIM:(g + 1) * ATT_HEAD_DIM] for g in range(G)], axis=0)
    kcat = jnp.concatenate([r[...] for r in k_refs], axis=0)
    s = _dot_nt(q4, kcat) * (ATT_HEAD_DIM ** -0.5)
    row = lax.broadcasted_iota(jnp.int32, (G * L, 3 * L), 0) % L
    col = lax.broadcasted_iota(jnp.int32, (G * L, 3 * L), 1)
    kpos = (n - 1) * L + col
    mask = (jnp.abs(col - L - row) <= L) & (kpos >= 0) & (kpos < nb * L)
    s = jnp.where(mask, s, -jnp.inf)
    sink = jnp.concatenate([jnp.broadcast_to(sink_ref[:, g:g + 1], (L, 1)) for g in range(G)], axis=0)
    m = jnp.maximum(jnp.max(s, axis=-1, keepdims=True), sink)
    p = jnp.exp(s - m)
    es = jnp.exp(sink - m)
    inv = 1.0 / (jnp.sum(p, axis=-1, keepdims=True) + es)
    return q4, kcat, p * inv, es * inv


def _kv_specs(nb):
    w = ATT_HEAD_DIM
    return [pl.BlockSpec((BLK, w), lambda kh, n: (jnp.maximum(n - 1, 0), kh)),
            pl.BlockSpec((BLK, w), lambda kh, n: (n, kh)),
            pl.BlockSpec((BLK, w), lambda kh, n: (jnp.minimum(n + 1, nb - 1), kh))]


def _attn_fwd(qr, kr, va, sink):
    S = qr.shape[0]
    nb = S // BLK
    G = ATT_GROUP

    def body(q_ref, k0, k1, k2, v0, v1, v2, sink_ref, o_ref):
        n = pl.program_id(1)
        _, _, probs, _ = _attn_probs(q_ref, (k0, k1, k2), sink_ref, n, nb)
        vcat = jnp.concatenate([v0[...], v1[...], v2[...]], axis=0)
        out = _dot(probs.astype(BF16), vcat)
        for g in range(G):
            o_ref[:, g * ATT_HEAD_DIM:(g + 1) * ATT_HEAD_DIM] = out[g * BLK:(g + 1) * BLK, :].astype(BF16)

    qspec = pl.BlockSpec((BLK, G * ATT_HEAD_DIM), lambda kh, n: (n, kh))
    return pl.pallas_call(
        body, name="attn_fwd", grid=(ATT_KV_HEADS, nb),
        in_specs=[qspec] + _kv_specs(nb) + _kv_specs(nb) + [pl.BlockSpec((None, 1, G), lambda kh, n: (kh, 0, 0))],
        out_specs=qspec,
        out_shape=jax.ShapeDtypeStruct((S, ATT_WIDTH), BF16),
        compiler_params=_params(),
    )(qr, kr, kr, kr, va, va, va, sink.reshape(ATT_KV_HEADS, 1, G))


def _attn_bwd(qr, kr, va, sink, dycat):
    S = qr.shape[0]
    nb = S // BLK
    G, L, Dh = ATT_GROUP, BLK, ATT_HEAD_DIM
    SP = S + 2 * L

    def body(q_ref, k0, k1, k2, v0, v1, v2, sink_ref, do_ref, dq_ref, dk_ref, dv_ref, ds_ref):
        n = pl.program_id(1)

        @pl.when(n == 0)
        def _():
            dk_ref[...] = jnp.zeros_like(dk_ref)
            dv_ref[...] = jnp.zeros_like(dv_ref)
            ds_ref[...] = jnp.zeros_like(ds_ref)

        q4, kcat, probs, psink = _attn_probs(q_ref, (k0, k1, k2), sink_ref, n, nb)
        vcat = jnp.concatenate([v0[...], v1[...], v2[...]], axis=0)
        do4 = jnp.concatenate([do_ref[:, g * Dh:(g + 1) * Dh] for g in range(G)], axis=0)
        pb = probs.astype(BF16)
        dob = do4.astype(BF16)
        out = _dot(pb, vcat)
        delta = jnp.sum(do4 * out, axis=-1, keepdims=True)
        dp = _dot_nt(dob, vcat)
        dsc = (probs * (dp - delta) * (Dh ** -0.5)).astype(BF16)
        dq4 = _dot(dsc, kcat)
        for g in range(G):
            dq_ref[:, g * Dh:(g + 1) * Dh] = dq4[g * L:(g + 1) * L, :]
        rows = pl.ds(pl.multiple_of(n * L, L), 3 * L)
        dk_ref[rows, :] += _dot_tn(dsc, q4)
        dv_ref[rows, :] += _dot_tn(pb, dob)
        lane = lax.broadcasted_iota(jnp.int32, (8, LANES), 1)
        dsink = jnp.zeros((8, LANES), F32)
        for g in range(G):
            val = -jnp.sum(psink[g * L:(g + 1) * L] * delta[g * L:(g + 1) * L], axis=0, keepdims=True)
            dsink = dsink + jnp.where(lane == g, jnp.broadcast_to(val, (8, LANES)), 0.0)
        ds_ref[...] += dsink

    qspec = pl.BlockSpec((L, G * Dh), lambda kh, n: (n, kh))
    accspec = pl.BlockSpec((None, SP, Dh), lambda kh, n: (kh, 0, 0))
    dq, dkp, dvp, dsink = pl.pallas_call(
        body, name="attn_bwd", grid=(ATT_KV_HEADS, nb),
        in_specs=[qspec] + _kv_specs(nb) + _kv_specs(nb)
        + [pl.BlockSpec((None, 1, G), lambda kh, n: (kh, 0, 0)), qspec],
        out_specs=[qspec, accspec, accspec, pl.BlockSpec((None, 8, LANES), lambda kh, n: (kh, 0, 0))],
        out_shape=[jax.ShapeDtypeStruct((S, ATT_WIDTH), F32), jax.ShapeDtypeStruct((ATT_KV_HEADS, SP, Dh), F32),
                   jax.ShapeDtypeStruct((ATT_KV_HEADS, SP, Dh), F32),
                   jax.ShapeDtypeStruct((ATT_KV_HEADS, 8, LANES), F32)],
        compiler_params=_params(),
    )(qr, kr, kr, kr, va, va, va, sink.reshape(ATT_KV_HEADS, 1, G), dycat)
    unpad = lambda t: jnp.transpose(t[:, L:L + S, :], (1, 0, 2)).reshape(S, KV_WIDTH)
    return dq, unpad(dkp), unpad(dvp), dsink[:, 0, :G].reshape(ATT_HEADS)


CONV_COLS = 256
CONV_ROWS = 512


def _conv_taps(xs, w_ref, rows):
    total = xs.shape[0]
    acc = None
    for j in range(CONV_WIDTH):
        shift = (CONV_WIDTH // 2 - j) % total
        term = (pltpu.roll(xs, shift, 0) if shift else xs)[CONV_HALO:CONV_HALO + rows, :] * w_ref[j:j + 1, :]
        acc = term if acc is None else acc + term
    return acc


def _conv5(xpad, w, *, act, out_dtype, name):
    SP, C = xpad.shape
    S = SP - 2 * CONV_HALO
    R = _row_tile(S, CONV_ROWS)
    tc = CONV_COLS
    half_blocks = (C // 2) // tc

    def body(x_ref, w_ref, o_ref):
        scale = jnp.where(pl.program_id(0) >= half_blocks, M_HEAD_DIM ** -0.5, 1.0)
        r0 = pl.multiple_of(pl.program_id(1) * R, R)
        y = _conv_taps(x_ref[pl.ds(r0, R + 2 * CONV_HALO), :], w_ref, R)
        if act:
            y = y * _sigmoid(y) * scale
        o_ref[...] = y.astype(out_dtype)

    return pl.pallas_call(
        body, name=name, grid=(C // tc, S // R),
        in_specs=[pl.BlockSpec((SP, tc), lambda c, i: (0, c)), pl.BlockSpec((CONV_WIDTH, tc), lambda c, i: (0, c))],
        out_specs=pl.BlockSpec((R, tc), lambda c, i: (i, c)),
        out_shape=jax.ShapeDtypeStruct((S, C), out_dtype),
        compiler_params=_params(),
    )(xpad, w)


def _conv_bwd_pre(xpad, w, dq2, dk2):
    SP, C = xpad.shape
    S = SP - 2 * CONV_HALO
    R = _row_tile(S, CONV_ROWS)
    tc = CONV_COLS
    half_blocks = (C // 2) // tc
    nrow = S // R

    def body(x_ref, w_ref, dqa_ref, dqb_ref, dka_ref, dkb_ref, o_ref, dw_ref, acc):
        is_k = pl.program_id(0) >= half_blocks
        i = pl.program_id(1)
        scale = jnp.where(is_k, M_HEAD_DIM ** -0.5, 1.0)

        @pl.when(i == 0)
        def _():
            acc[...] = jnp.zeros_like(acc)

        r0 = pl.multiple_of(i * R, R)
        xs = x_ref[pl.ds(r0, R + 2 * CONV_HALO), :]
        y = _conv_taps(xs, w_ref, R)
        sg = _sigmoid(y)
        dqv = dqa_ref[...] + dqb_ref[...]
        dkv = dka_ref[...] + dkb_ref[...]
        dpre = jnp.where(is_k, dkv, dqv) * scale * (sg * (1.0 + y * (1.0 - sg)))
        o_ref[...] = dpre
        total = xs.shape[0]
        for j in range(CONV_WIDTH):
            shift = (CONV_WIDTH // 2 - j) % total
            xj = (pltpu.roll(xs, shift, 0) if shift else xs)[CONV_HALO:CONV_HALO + R, :]
            acc[j:j + 1, :] += jnp.sum(dpre * xj, axis=0, keepdims=True)

        @pl.when(i == nrow - 1)
        def _():
            dw_ref[...] = acc[0:CONV_WIDTH, :]

    nqb = (C // 2) // tc
    qmap = lambda d: (lambda c, i: (d, i, jnp.minimum(c, nqb - 1)))
    kmap = lambda d: (lambda c, i: (d, i, jnp.maximum(c - nqb, 0)))
    gspec = lambda m: pl.BlockSpec((None, R, tc), m)
    return pl.pallas_call(
        body, name="conv_bwd_pre", grid=(C // tc, nrow),
        in_specs=[pl.BlockSpec((SP, tc), lambda c, i: (0, c)), pl.BlockSpec((CONV_WIDTH, tc), lambda c, i: (0, c)),
                  gspec(qmap(0)), gspec(qmap(1)), gspec(kmap(0)), gspec(kmap(1))],
        out_specs=[pl.BlockSpec((R, tc), lambda c, i: (i, c)), pl.BlockSpec((CONV_WIDTH, tc), lambda c, i: (0, c))],
        out_shape=[jax.ShapeDtypeStruct((S, C), F32), jax.ShapeDtypeStruct((CONV_WIDTH, C), F32)],
        scratch_shapes=[pltpu.VMEM((8, tc), F32)],
        compiler_params=_params(),
    )(xpad, w, dq2, dq2, dk2, dk2)


def _pad_rows(x):
    return jnp.pad(x, ((CONV_HALO, CONV_HALO), (0, 0)))


def _lane_cumsum(x, reverse):
    lane = lax.broadcasted_iota(jnp.int32, x.shape, 1)
    sh = 1
    while sh < LANES:
        if reverse:
            x = x + jnp.where(lane < LANES - sh, pltpu.roll(x, LANES - sh, 1), 0.0)
        else:
            x = x + jnp.where(lane >= sh, pltpu.roll(x, sh, 1), 0.0)
        sh *= 2
    return x


def _gate_prep(gates_t, bias):
    R = gates_t.shape[0]
    half, quarter = R // 2, R // 4

    def body(g_ref, b_ref, ig_ref, cum_ref):
        ig_ref[...] = g_ref[0:half, :] + b_ref[0:half, :]
        fg = g_ref[half:R, :] + b_ref[half:R, :]
        lf = jnp.minimum(fg, 0.0) - jnp.log(1.0 + jnp.exp(-jnp.abs(fg)))
        cum_ref[0:quarter, :] = _lane_cumsum(lf[0:quarter, :], False)
        cum_ref[quarter:half, :] = _lane_cumsum(lf[quarter:half, :], True)

    return pl.pallas_call(
        body, name="gate_prep",
        out_shape=[jax.ShapeDtypeStruct((half, LANES), F32), jax.ShapeDtypeStruct((half, LANES), F32)],
        compiler_params=_params(),
    )(gates_t, bias)


def _gate_bwd(gates_t, bias, di, dfc, dbl, nc):
    R = gates_t.shape[0]
    half, quarter = R // 2, R // 4

    def body(g_ref, b_ref, di_ref, df_ref, dbl_ref, dg_ref, db_ref):
        dg_ref[0:half, :] = di_ref[...]
        dfv = df_ref[...]
        within = jnp.concatenate([_lane_cumsum(dfv[0:quarter, :], True),
                                  _lane_cumsum(dfv[quarter:half, :], False)], axis=0)
        fg = g_ref[half:R, :] + b_ref[half:R, :]
        dg_ref[half:R, :] = (within + dbl_ref[...]) * _sigmoid(-fg)
        rows = jnp.broadcast_to(jnp.sum(dg_ref[...], axis=-1, keepdims=True), (R, LANES))
        gr = lax.broadcasted_iota(jnp.int32, (N_GATES, R), 0)
        gc = lax.broadcasted_iota(jnp.int32, (N_GATES, R), 1)
        db_ref[...] = lax.dot_general((gc // nc == gr).astype(F32), rows, (((1,), (0,)), ((), ())),
                                      precision=lax.Precision.HIGHEST, preferred_element_type=F32)

    return pl.pallas_call(
        body, name="gate_bwd",
        out_shape=[jax.ShapeDtypeStruct((R, LANES), F32), jax.ShapeDtypeStruct((N_GATES, LANES), F32)],
        compiler_params=_params(),
    )(gates_t, bias, di, dfc, dbl)


def _chunk_index(d, c, nc, reverse):
    j = (nc - 1 - c) if reverse else c
    return j + d * (nc - 1 - 2 * j)


def _mlstm_chunk(d, q, k, vb, brow, igrow, C, nvec, m_prev, eye):
    L = BLK
    r = lax.broadcasted_iota(jnp.int32, (L, L), 0)
    c = lax.broadcasted_iota(jnp.int32, (L, L), 1)
    mask = (r - c) * (1 - 2 * d) >= 0
    bcol = _row_to_col(brow, eye)
    igcol = _row_to_col(igrow, eye)
    log_d = jnp.where(mask, bcol - brow + igrow, -jnp.inf)
    log_inter = bcol + m_prev
    m_t = jnp.maximum(log_inter, jnp.max(log_d, axis=-1, keepdims=True))
    d_mat = jnp.exp(log_d - m_t)
    inter = jnp.exp(log_inter - m_t)
    s = _dot_nt(q, k) * d_mat
    sb = s.astype(BF16)
    cb = C.astype(BF16)
    num = _dot(sb, vb) + inter * _dot_nt(q, cb)
    den = jnp.sum(s, axis=-1, keepdims=True) + inter * jnp.sum(q.astype(F32) * nvec, axis=-1, keepdims=True)
    floor = jnp.exp(-m_t)
    denom = jnp.maximum(jnp.abs(den), floor)
    b_last = jnp.where(d == 0, brow[:, L - 1:L], brow[:, 0:1])
    return dict(bcol=bcol, igcol=igcol, m_t=m_t, d_mat=d_mat, inter=inter, sb=sb, cb=cb, num=num, den=den,
                floor=floor, denom=denom, b_last=b_last)


def _mlstm_fwd(qk, proj, ig3, b3):
    S = qk.shape[0]
    nc = S // BLK
    L, Dh, H = BLK, M_HEAD_DIM, M_HEADS

    def body(q_ref, k_ref, v_ref, ig_ref, b_ref, h_ref, cs_ref, ns_ref, ms_ref, C, nvec, m):
        d = pl.program_id(0)

        @pl.when(pl.program_id(2) == 0)
        def _():
            C[...] = jnp.zeros_like(C)
            nvec[...] = jnp.zeros_like(nvec)
            m[...] = jnp.zeros_like(m)

        eye = _eye_mask(L)
        q, k = q_ref[...], k_ref[...]
        vf = v_ref[...]
        brow, igrow = b_ref[...], ig_ref[...]
        m_prev = m[:, 0:1]
        cs_ref[...] = C[...].astype(BF16)
        ns_ref[...] = nvec[...]
        ms_ref[...] = m[...]
        f = _mlstm_chunk(d, q, k, vf.astype(BF16), brow, igrow, C[...], nvec[...], m_prev, eye)
        h_ref[...] = f["num"] / f["denom"]
        b_last = f["b_last"]
        log_w = b_last - brow + igrow
        m_new = jnp.maximum(b_last + m_prev, jnp.max(log_w, axis=-1, keepdims=True))
        w_col = jnp.exp(b_last - f["bcol"] + f["igcol"] - m_new)
        decay = jnp.exp(b_last + m_prev - m_new)
        C[...] = decay * C[...] + _dot_tn((w_col * vf).astype(BF16), k)
        nvec[...] = decay * nvec[...] + jnp.sum(w_col * k.astype(F32), axis=0, keepdims=True)
        m[...] = jnp.broadcast_to(m_new, (1, LANES))

    cidx = lambda d, c: _chunk_index(d, c, nc, False)
    gspec = pl.BlockSpec((None, 1, LANES), lambda d, h, c: ((d * H + h) * nc + cidx(d, c), 0, 0))
    st = lambda *blk: pl.BlockSpec((None, None, None) + blk, lambda d, h, c: (d, h, cidx(d, c), 0, 0))
    return pl.pallas_call(
        body, name="mlstm_fwd", grid=(2, H, nc),
        in_specs=[pl.BlockSpec((L, Dh), lambda d, h, c: (cidx(d, c), h)),
                  pl.BlockSpec((L, Dh), lambda d, h, c: (cidx(d, c), H + h)),
                  pl.BlockSpec((L, Dh), lambda d, h, c: (cidx(d, c), OFF_VM // Dh + h)), gspec, gspec],
        out_specs=[pl.BlockSpec((None, L, Dh), lambda d, h, c: (d, cidx(d, c), h)),
                   st(Dh, Dh), st(1, Dh), st(1, LANES)],
        out_shape=[jax.ShapeDtypeStruct((2, S, M_WIDTH), F32), jax.ShapeDtypeStruct((2, H, nc, Dh, Dh), BF16),
                   jax.ShapeDtypeStruct((2, H, nc, 1, Dh), F32), jax.ShapeDtypeStruct((2, H, nc, 1, LANES), F32)],
        scratch_shapes=[pltpu.VMEM((Dh, Dh), F32), pltpu.VMEM((1, Dh), F32), pltpu.VMEM((1, LANES), F32)],
        compiler_params=_params(),
    )(qk, qk, proj, ig3, b3)


def _mlstm_bwd(qk, proj, ig3, b3, cs, ns, ms, dhm):
    S = qk.shape[0]
    nc = S // BLK
    L, Dh, H = BLK, M_HEAD_DIM, M_HEADS

    def body(q_ref, k_ref, v_ref, ig_ref, b_ref, cs_ref, ns_ref, ms_ref, dh_ref,
             dq_ref, dk_ref, dv_ref, di_ref, df_ref, dbl_ref, R, rvec, mu):
        d = pl.program_id(0)

        @pl.when(pl.program_id(2) == 0)
        def _():
            R[...] = jnp.zeros_like(R)
            rvec[...] = jnp.zeros_like(rvec)
            mu[...] = jnp.full(mu.shape, NEG_BIG, F32)

        eye = _eye_mask(L)
        q, k = q_ref[...], k_ref[...]
        qf, kf = q.astype(F32), k.astype(F32)
        vb = v_ref[...].astype(BF16)
        brow, igrow = b_ref[...], ig_ref[...]
        nprev = ns_ref[...]
        f = _mlstm_chunk(d, q, k, vb, brow, igrow, cs_ref[...], nprev, ms_ref[:, 0:1], eye)
        dh = dh_ref[...]
        inv = 1.0 / f["denom"]
        hcur = f["num"] * inv
        dnum = dh * inv
        active = jnp.abs(f["den"]) >= f["floor"]
        dden = jnp.where(active, -jnp.sign(f["den"]), 0.0) * jnp.sum(dh * hcur, axis=-1, keepdims=True) * inv
        dnb = dnum.astype(BF16)
        dqk = ((_dot_nt(dnb, vb) + dden) * f["d_mat"]).astype(BF16)
        dq = _dot(dqk, k) + f["inter"] * (_dot(dnb, f["cb"]) + dden * nprev)
        mu_prev = mu[:, 0:1]
        a_col = jnp.exp(f["b_last"] - f["bcol"] + f["igcol"] + mu_prev)
        rb = R[...].astype(BF16)
        dv = _dot_tn(f["sb"], dnb) + a_col * _dot_nt(k, rb)
        dk_inter = a_col * (_dot(vb, rb) + rvec[...])
        dk = _dot_tn(dqk, q) + dk_inter
        dq_ref[...] = dq
        dk_ref[...] = dk
        dv_ref[...] = dv
        kdk = jnp.sum(kf * dk, axis=-1, keepdims=True)
        qdq = jnp.sum(qf * dq, axis=-1, keepdims=True)
        di_ref[...] = _col_to_row(kdk, eye)
        df_ref[...] = _col_to_row(qdq - kdk, eye)
        older = jnp.sum(jnp.sum(R[...] * cs_ref[...].astype(F32), axis=-1, keepdims=True), axis=0, keepdims=True)
        older = older + jnp.sum(rvec[...] * nprev, axis=-1, keepdims=True)
        dbl = (jnp.sum(jnp.sum(kf * dk_inter, axis=-1, keepdims=True), axis=0, keepdims=True)
               + jnp.exp(f["b_last"] + mu_prev + ms_ref[:, 0:1]) * older)
        dbl_ref[...] = jnp.broadcast_to(dbl, (1, LANES))
        lw = f["bcol"] - f["m_t"]
        mu_new = jnp.maximum(f["b_last"] + mu_prev, jnp.max(lw, axis=0, keepdims=True))
        wq = jnp.exp(lw - mu_new)
        decay = jnp.exp(f["b_last"] + mu_prev - mu_new)
        R[...] = decay * R[...] + _dot_tn((wq * dnum).astype(BF16), q)
        rvec[...] = decay * rvec[...] + jnp.sum(wq * dden * qf, axis=0, keepdims=True)
        mu[...] = jnp.broadcast_to(mu_new, (1, LANES))

    cidx = lambda d, c: _chunk_index(d, c, nc, True)
    gin = pl.BlockSpec((None, 1, LANES), lambda d, h, c: ((d * H + h) * nc + cidx(d, c), 0, 0))
    st = lambda *blk: pl.BlockSpec((None, None, None) + blk, lambda d, h, c: (d, h, cidx(d, c), 0, 0))
    per_dir = pl.BlockSpec((None, L, Dh), lambda d, h, c: (d, cidx(d, c), h))
    big = jax.ShapeDtypeStruct((2, S, M_WIDTH), F32)
    small = jax.ShapeDtypeStruct((2 * H * nc, 1, LANES), F32)
    return pl.pallas_call(
        body, name="mlstm_bwd", grid=(2, H, nc),
        in_specs=[pl.BlockSpec((L, Dh), lambda d, h, c: (cidx(d, c), h)),
                  pl.BlockSpec((L, Dh), lambda d, h, c: (cidx(d, c), H + h)),
                  pl.BlockSpec((L, Dh), lambda d, h, c: (cidx(d, c), OFF_VM // Dh + h)), gin, gin,
                  st(Dh, Dh), st(1, Dh), st(1, LANES),
                  pl.BlockSpec((L, Dh), lambda d, h, c: (cidx(d, c), h))],
        out_specs=[per_dir, per_dir, per_dir, gin, gin, gin],
        out_shape=[big, big, big, small, small, small],
        scratch_shapes=[pltpu.VMEM((Dh, Dh), F32), pltpu.VMEM((1, Dh), F32), pltpu.VMEM((1, LANES), F32)],
        compiler_params=_params(),
    )(qk, qk, proj, ig3, b3, cs, ns, ms, dhm)


def _mlstm_out_fwd(hs, proj, gamma):
    S = hs.shape[1]
    tm = _row_tile(S, 512)
    Dh, H = M_HEAD_DIM, M_HEADS

    def body(hf_ref, hb_ref, o_ref, g_ref, y_ref):
        hm = hf_ref[...] + hb_ref[...]
        r = lax.rsqrt(jnp.mean(hm * hm, axis=-1, keepdims=True) + EPS)
        y_ref[...] = (_sigmoid(o_ref[...]) * (hm * r * g_ref[...])).astype(BF16)

    hspec = lambda d: pl.BlockSpec((None, tm, Dh), lambda i, h: (d, i, h))
    return pl.pallas_call(
        body, name="mlstm_out_fwd", grid=(S // tm, H),
        in_specs=[hspec(0), hspec(1), pl.BlockSpec((tm, Dh), lambda i, h: (i, OFF_OM // Dh + h)),
                  pl.BlockSpec((1, Dh), lambda i, h: (0, h))],
        out_specs=pl.BlockSpec((tm, Dh), lambda i, h: (i, h)),
        out_shape=jax.ShapeDtypeStruct((S, M_WIDTH), BF16),
        compiler_params=_params(),
    )(hs, hs, proj, gamma.reshape(1, M_WIDTH))


def _mlstm_out_bwd(hs, proj, gamma, dycat):
    S = hs.shape[1]
    tm = _row_tile(S, 512)
    Dh, H = M_HEAD_DIM, M_HEADS
    D_OFF = ATT_WIDTH // Dh

    def body(hf_ref, hb_ref, o_ref, g_ref, dy_ref, dh_ref, do_ref, dg_ref):
        hm = hf_ref[...] + hb_ref[...]
        r = lax.rsqrt(jnp.mean(hm * hm, axis=-1, keepdims=True) + EPS)
        hh = hm * r
        so = _sigmoid(o_ref[...])
        dy = dy_ref[...]
        gam = g_ref[...]
        do_ref[...] = (dy * hh * gam * so * (1.0 - so)).astype(BF16)
        dyn = dy * so
        gd = dyn * gam
        dh_ref[...] = r * (gd - hh * jnp.mean(hh * gd, axis=-1, keepdims=True))
        part = jnp.sum(dyn * hh, axis=0, keepdims=True)

        @pl.when(pl.program_id(1) == 0)
        def _():
            dg_ref[...] = part

        @pl.when(pl.program_id(1) > 0)
        def _():
            dg_ref[...] += part

    hspec = lambda d: pl.BlockSpec((None, tm, Dh), lambda h, i: (d, i, h))
    out = pl.BlockSpec((tm, Dh), lambda h, i: (i, h))
    vec = pl.BlockSpec((1, Dh), lambda h, i: (0, h))
    dhm, do, dg = pl.pallas_call(
        body, name="mlstm_out_bwd", grid=(H, S // tm),
        in_specs=[hspec(0), hspec(1), pl.BlockSpec((tm, Dh), lambda h, i: (i, OFF_OM // Dh + h)), vec,
                  pl.BlockSpec((tm, Dh), lambda h, i: (i, D_OFF + h))],
        out_specs=[out, out, vec],
        out_shape=[jax.ShapeDtypeStruct((S, M_WIDTH), F32), jax.ShapeDtypeStruct((S, M_WIDTH), BF16),
                   jax.ShapeDtypeStruct((1, M_WIDTH), F32)],
        compiler_params=_params(),
    )(hs, hs, proj, gamma.reshape(1, M_WIDTH), dycat)
    return dhm, do, dg.reshape(M_WIDTH)


def _sum_cast(a2, name):
    _, S, C = a2.shape
    tm = _row_tile(S, 512)

    def body(a_ref, b_ref, o_ref):
        o_ref[...] = (a_ref[...] + b_ref[...]).astype(BF16)

    spec = lambda d: pl.BlockSpec((None, tm, C), lambda i: (d, i, 0))
    return pl.pallas_call(
        body, name=name, grid=(S // tm,),
        in_specs=[spec(0), spec(1)], out_specs=pl.BlockSpec((tm, C), lambda i: (i, 0)),
        out_shape=jax.ShapeDtypeStruct((S, C), BF16),
        compiler_params=_params(),
    )(a2, a2)


def _adamw(w, g, m, v):
    shape = w.shape
    C = shape[-1]
    R = w.size // C
    tm = _elementwise_rows(R, C)

    def body(w_ref, g_ref, m_ref, v_ref, d_ref, mo_ref, vo_ref):
        gv = g_ref[...]
        mn = ADAM_B1 * m_ref[...] + (1.0 - ADAM_B1) * gv
        vn = ADAM_B2 * v_ref[...] + (1.0 - ADAM_B2) * (gv * gv)
        m_hat = mn / (1.0 - ADAM_B1 ** ADAM_STEP)
        v_hat = vn / (1.0 - ADAM_B2 ** ADAM_STEP)
        d_ref[...] = -ADAM_LR * (m_hat / (jnp.sqrt(v_hat) + ADAM_EPS) + ADAM_WD * w_ref[...])
        mo_ref[...] = mn
        vo_ref[...] = vn

    row = pl.BlockSpec((tm, C), lambda i: (i, 0))
    sds = jax.ShapeDtypeStruct((R, C), F32)
    outs = pl.pallas_call(
        body, name="adamw", grid=(R // tm,),
        in_specs=[row] * 4, out_specs=[row] * 3, out_shape=[sds] * 3,
        compiler_params=_params(),
    )(*[t.reshape(R, C) for t in (w, g, m, v)])
    return tuple(o.reshape(shape) for o in outs)


def _other_chips(x, y):
    return [(1 - x, y), (x, 1 - y), (1 - x, 1 - y)]


def _comm_weight_gather(shards):
    T = len(shards)

    def body(*refs):
        ins, outs = refs[:T], refs[T:2 * T]
        send, recv, local = refs[2 * T:]
        x, y, c = lax.axis_index("x"), lax.axis_index("y"), lax.axis_index("c")
        me = 2 * x + y
        chips = _other_chips(x, y)
        mine = []
        for t in range(T):
            cp = pltpu.make_async_copy(ins[t], outs[t].at[me], local.at[t])
            cp.start()
            mine.append(cp)
            for k, (px, py) in enumerate(chips):
                pltpu.make_async_remote_copy(
                    src_ref=ins[t], dst_ref=outs[t].at[me], send_sem=send.at[3 * t + k], recv_sem=recv.at[3 * t + k],
                    device_id=(px, py, c), device_id_type=MESH).start()
        for t in range(T):
            for k, (px, py) in enumerate(chips):
                cp = pltpu.make_async_remote_copy(
                    src_ref=ins[t], dst_ref=outs[t].at[2 * px + py], send_sem=send.at[3 * t + k],
                    recv_sem=recv.at[3 * t + k], device_id=(px, py, c), device_id_type=MESH)
                cp.wait_send()
                cp.wait_recv()
            mine[t].wait()

    return pl.pallas_call(
        body, name="comm_weight_gather",
        in_specs=[ANY] * T, out_specs=[ANY] * T,
        out_shape=[jax.ShapeDtypeStruct((4,) + s.shape, s.dtype) for s in shards],
        scratch_shapes=[pltpu.SemaphoreType.DMA((3 * T,)), pltpu.SemaphoreType.DMA((3 * T,)),
                        pltpu.SemaphoreType.DMA((T,))],
    )(*shards)


def _comm_to_sibling(xs, name):
    T = len(xs)

    def body(*refs):
        ins, outs = refs[:T], refs[T:2 * T]
        send, recv = refs[2 * T:]
        x, y, c = lax.axis_index("x"), lax.axis_index("y"), lax.axis_index("c")
        cps = [pltpu.make_async_remote_copy(
            src_ref=ins[t], dst_ref=outs[t], send_sem=send.at[t], recv_sem=recv.at[t],
            device_id=(x, y, 1 - c), device_id_type=MESH) for t in range(T)]
        for cp in cps:
            cp.start()
        for cp in cps:
            cp.wait_send()
            cp.wait_recv()

    return pl.pallas_call(
        body, name=name,
        in_specs=[ANY] * T, out_specs=[ANY] * T,
        out_shape=[jax.ShapeDtypeStruct(s.shape, s.dtype) for s in xs],
        scratch_shapes=[pltpu.SemaphoreType.DMA((T,)), pltpu.SemaphoreType.DMA((T,))],
    )(*xs)


def _comm_scatter_to_owners(ps):
    T = len(ps)

    def body(*refs):
        ins, outs = refs[:T], refs[T:2 * T]
        send, recv = refs[2 * T:]
        x, y, c = lax.axis_index("x"), lax.axis_index("y"), lax.axis_index("c")
        chips = _other_chips(x, y)
        cps = []
        for t in range(T):
            for k, (px, py) in enumerate(chips):
                cp = pltpu.make_async_remote_copy(
                    src_ref=ins[t].at[2 * px + py], dst_ref=outs[t].at[k], send_sem=send.at[3 * t + k],
                    recv_sem=recv.at[3 * t + k], device_id=(px, py, c), device_id_type=MESH)
                cp.start()
                cps.append(cp)
        for cp in cps:
            cp.wait_send()
            cp.wait_recv()

    return pl.pallas_call(
        body, name="comm_scatter_to_owners",
        in_specs=[ANY] * T, out_specs=[ANY] * T,
        out_shape=[jax.ShapeDtypeStruct((3,) + p.shape[1:], p.dtype) for p in ps],
        scratch_shapes=[pltpu.SemaphoreType.DMA((3 * T,)), pltpu.SemaphoreType.DMA((3 * T,))],
    )(*ps)


def _comm_pair_gather(hs):
    T = len(hs)

    def body(*refs):
        ins, outs = refs[:T], refs[T:2 * T]
        send, recv, local = refs[2 * T:]
        x, y, c = lax.axis_index("x"), lax.axis_index("y"), lax.axis_index("c")
        mine = []
        for t in range(T):
            cp = pltpu.make_async_copy(ins[t], outs[t].at[c], local.at[t])
            cp.start()
            mine.append(cp)
            pltpu.make_async_remote_copy(
                src_ref=ins[t], dst_ref=outs[t].at[c], send_sem=send.at[t], recv_sem=recv.at[t],
                device_id=(x, y, 1 - c), device_id_type=MESH).start()
        for t in range(T):
            cp = pltpu.make_async_remote_copy(
                src_ref=ins[t], dst_ref=outs[t].at[1 - c], send_sem=send.at[t], recv_sem=recv.at[t],
                device_id=(x, y, 1 - c), device_id_type=MESH)
            cp.wait_send()
            cp.wait_recv()
            mine[t].wait()

    return pl.pallas_call(
        body, name="comm_pair_gather",
        in_specs=[ANY] * T, out_specs=[ANY] * T,
        out_shape=[jax.ShapeDtypeStruct((2,) + h.shape, h.dtype) for h in hs],
        scratch_shapes=[pltpu.SemaphoreType.DMA((T,)), pltpu.SemaphoreType.DMA((T,)), pltpu.SemaphoreType.DMA((T,))],
    )(*hs)


def _comm_all_to_all_small(pack):
    R = pack.shape[0]

    def body(in_ref, out_ref, send, recv, local):
        x, y, c = lax.axis_index("x"), lax.axis_index("y"), lax.axis_index("c")
        me = 4 * x + 2 * y + c
        mine = pltpu.make_async_copy(in_ref, out_ref.at[me], local.at[0])
        mine.start()
        peers = []
        for k in range(1, 8):
            fx, fy, fc = (k >> 2) & 1, (k >> 1) & 1, k & 1
            peers.append((x + fx * (1 - 2 * x), y + fy * (1 - 2 * y), c + fc * (1 - 2 * c)))
        for k, peer in enumerate(peers):
            pltpu.make_async_remote_copy(
                src_ref=in_ref, dst_ref=out_ref.at[me], send_sem=send.at[k], recv_sem=recv.at[k],
                device_id=peer, device_id_type=MESH).start()
        for k, (px, py, pc) in enumerate(peers):
            cp = pltpu.make_async_remote_copy(
                src_ref=in_ref, dst_ref=out_ref.at[4 * px + 2 * py + pc], send_sem=send.at[k], recv_sem=recv.at[k],
                device_id=(px, py, pc), device_id_type=MESH)
            cp.wait_send()
            cp.wait_recv()
        mine.wait()

    return pl.pallas_call(
        body, name="comm_all_to_all_small",
        in_specs=[ANY], out_specs=ANY,
        out_shape=jax.ShapeDtypeStruct((8, R, LANES), F32),
        scratch_shapes=[pltpu.SemaphoreType.DMA((7,)), pltpu.SemaphoreType.DMA((7,)), pltpu.SemaphoreType.DMA((1,))],
    )(pack)


def _reduce_scatter_layer(grads):
    x, y, c = lax.axis_index("x"), lax.axis_index("y"), lax.axis_index("c")
    split = [g.reshape(4, 2, g.shape[1] // 2, g.shape[2]) for g in grads]
    keep = [lax.dynamic_index_in_dim(s, c, 1, keepdims=False) for s in split]
    give = [lax.dynamic_index_in_dim(s, 1 - c, 1, keepdims=False) for s in split]
    got = _comm_to_sibling(give, "comm_pair_swap")
    pair = [_add_n([a, b], "pair_sum") for a, b in zip(keep, got)]
    arrived = _comm_scatter_to_owners(pair)
    own = [lax.dynamic_index_in_dim(p, 2 * x + y, 0, keepdims=False) for p in pair]
    halves = [_add_n([o, a[0], a[1], a[2]], "chip_sum") for o, a in zip(own, arrived)]
    full = _comm_pair_gather(halves)
    return [f.reshape(2 * f.shape[1], f.shape[2]) for f in full]


def _ffn_fwd(x, g_pre, g_post, wgu, wd):
    S, D = x.shape
    F = wd.shape[1]
    xn = _norm_fwd(x, g_pre)
    gu, h = _ffn_up(xn, wgu)
    f = _mm(h, wd, "nn", ja="r", jb="r", tm=_row_tile(S, 512), tn=min(D, 1024), tk=F, name="ffn_down")
    x_out = _resid_norm_fwd(x, f, g_post, 0.5)
    return x_out, (x, xn, gu, h, f)


def _ffn_bwd(dx, saved, g_pre, g_post, wgu, wd):
    x, xn, gu, h, f = saved
    S, D = x.shape
    J, F, _ = wd.shape
    ts = _row_tile(S, 512)
    df, dg_post = _norm_bwd(f, g_post, dx, 0.5, out_dtype=BF16)
    d_wd = _mm(h, df, "tn", ja="b", tm=F, tn=min(D, 1024), tk=ts, name="ffn_dwd")
    dgu = _ffn_bwd_hidden(df, wd, gu)
    d_wgu = _mm(xn, dgu, "tn", jb="b", tm=min(D, 512), tn=2 * F, tk=ts, name="ffn_dwgu")
    dxn = _mm(dgu, wgu, "nt", ja="r", jb="r", tm=ts, tn=min(D, 1024), tk=2 * F, name="ffn_dxn")
    dx_in, dg_pre = _norm_bwd(x, g_pre, dxn, 1.0, resid=dx)
    return dx_in, dg_pre, dg_post, d_wgu, d_wd


def _gates_layout(proj, b_gate, nc):
    gt = jnp.transpose(proj[:, OFF_G:OFF_G + N_GATES]).reshape(N_GATES * nc, LANES)
    bias = jnp.repeat(b_gate, nc).reshape(N_GATES * nc, 1)
    return gt, bias


def _mixer_fwd(x, p, cosf, sinf):
    S, D = x.shape
    nc = S // BLK
    hn = _norm_fwd(x, p["g_pre"])
    proj = _mm(hn, p["w_in"], "nn", tm=_row_tile(S, 512), tn=IN_PAD // 5, tk=D, name="in_proj")
    qr, kr, va = _rope_fwd(proj, cosf, sinf)
    y_att = _attn_fwd(qr, kr, va, p["sink"])
    xpad = _pad_rows(proj[:, OFF_QM:OFF_VM])
    qk = _conv5(xpad, p["conv_w"], act=True, out_dtype=BF16, name="conv_silu")
    gt, bias = _gates_layout(proj, p["b_gate"], nc)
    ig, cum = _gate_prep(gt, bias)
    ig3, b3 = ig.reshape(-1, 1, LANES), cum.reshape(-1, 1, LANES)
    hs, cs, ns, ms = _mlstm_fwd(qk, proj, ig3, b3)
    y_m = _mlstm_out_fwd(hs, proj, p["gamma"])
    ycat = jnp.concatenate([y_att, y_m], axis=-1)
    mix = _mm(ycat, p["w_out"], "nn", tm=_row_tile(S, 512), tn=min(D, 1024), tk=D, name="out_proj")
    x_out = _resid_norm_fwd(x, mix, p["g_post"], 1.0)
    return x_out, (x, hn, proj, qr, kr, va, xpad, qk, gt, bias, ig3, b3, hs, cs, ns, ms, ycat, mix)


def _mixer_bwd(dx, saved, p, cosf, sinf):
    x, hn, proj, qr, kr, va, xpad, qk, gt, bias, ig3, b3, hs, cs, ns, ms, ycat, mix = saved
    S, D = x.shape
    nc = S // BLK
    ts = _row_tile(S, 512)
    dmix, dg_post = _norm_bwd(mix, p["g_post"], dx, 1.0, out_dtype=BF16)
    d_wout = _mm(ycat, dmix, "tn", tm=min(D, 1024), tn=min(D, 1024), tk=ts, name="dw_out")
    dycat = _mm(dmix, p["w_out"], "nt", tm=ts, tn=min(D, 1024), tk=D, name="d_ycat")
    dhm, d_om, d_gamma = _mlstm_out_bwd(hs, proj, p["gamma"], dycat)
    dq2, dk2, dv2, di, dfc, dbl = _mlstm_bwd(qk, proj, ig3, b3, cs, ns, ms, dhm)
    d_vm = _sum_cast(dv2, "dv_sum")
    dgt, db = _gate_bwd(gt, bias, di.reshape(-1, LANES), dfc.reshape(-1, LANES), dbl.reshape(-1, LANES), nc)
    dpre, d_conv = _conv_bwd_pre(xpad, p["conv_w"], dq2, dk2)
    d_qkm = _conv5(_pad_rows(dpre), p["conv_w"][::-1], act=False, out_dtype=BF16, name="conv_bwd_x")
    d_gates = jnp.transpose(dgt.reshape(N_GATES, S)).astype(BF16)
    d_gates = jnp.pad(d_gates, ((0, 0), (0, IN_PAD - IN_WIDTH)))
    dqr, dkr, dva, d_sink = _attn_bwd(qr, kr, va, p["sink"], dycat)
    d_att = _rope_bwd(dqr, dkr, dva, cosf, sinf)
    dproj = jnp.concatenate([d_att, d_qkm, d_vm, d_om, d_gates], axis=-1)
    d_win = _mm(hn, dproj, "tn", tm=min(D, 1024), tn=IN_PAD // 5, tk=ts, name="dw_in")
    dhn = _mm(dproj, p["w_in"], "nt", tm=ts, tn=min(D, 1024), tk=IN_PAD // 5, name="d_hn")
    dx_in, dg_pre = _norm_bwd(x, p["g_pre"], dhn, 1.0, resid=dx)
    small = dict(g_pre=dg_pre, g_post=dg_post, gamma=d_gamma, conv=d_conv, b_gate=db[:, 0], sink=d_sink)
    return dx_in, small, d_win, d_wout


def _pad_lanes(v):
    v = v.reshape(-1)
    return jnp.pad(v, (0, (-v.shape[0]) % LANES))


def kernel(x, ffn1_norm_pre, ffn1_norm_post, ffn1_w_gate, ffn1_w_up, ffn1_w_down, mix_norm_pre, mix_norm_post, w_in, b_gate, conv_w, attn_sink, mlstm_norm, w_out, ffn2_norm_pre, ffn2_norm_post, ffn2_w_gate, ffn2_w_up, ffn2_w_down, loss_target, m_ffn1_norm_pre, m_ffn1_norm_post, m_ffn1_w_gate, m_ffn1_w_up, m_ffn1_w_down, m_mix_norm_pre, m_mix_norm_post, m_w_in, m_b_gate, m_conv_w, m_attn_sink, m_mlstm_norm, m_w_out, m_ffn2_norm_pre, m_ffn2_norm_post, m_ffn2_w_gate, m_ffn2_w_up, m_ffn2_w_down, v_ffn1_norm_pre, v_ffn1_norm_post, v_ffn1_w_gate, v_ffn1_w_up, v_ffn1_w_down, v_mix_norm_pre, v_mix_norm_post, v_w_in, v_b_gate, v_conv_w, v_attn_sink, v_mlstm_norm, v_w_out, v_ffn2_norm_pre, v_ffn2_norm_post, v_ffn2_w_gate, v_ffn2_w_up, v_ffn2_w_down):
    names = ["ffn1_norm_pre", "ffn1_norm_post", "ffn1_w_gate", "ffn1_w_up", "ffn1_w_down", "mix_norm_pre",
             "mix_norm_post", "w_in", "b_gate", "conv_w", "attn_sink", "mlstm_norm", "w_out", "ffn2_norm_pre",
             "ffn2_norm_post", "ffn2_w_gate", "ffn2_w_up", "ffn2_w_down"]
    w = dict(zip(names, [ffn1_norm_pre, ffn1_norm_post, ffn1_w_gate, ffn1_w_up, ffn1_w_down, mix_norm_pre,
                         mix_norm_post, w_in, b_gate, conv_w, attn_sink, mlstm_norm, w_out, ffn2_norm_pre,
                         ffn2_norm_post, ffn2_w_gate, ffn2_w_up, ffn2_w_down]))
    mom_m = dict(zip(names, [m_ffn1_norm_pre, m_ffn1_norm_post, m_ffn1_w_gate, m_ffn1_w_up, m_ffn1_w_down,
                             m_mix_norm_pre, m_mix_norm_post, m_w_in, m_b_gate, m_conv_w, m_attn_sink,
                             m_mlstm_norm, m_w_out, m_ffn2_norm_pre, m_ffn2_norm_post, m_ffn2_w_gate,
                             m_ffn2_w_up, m_ffn2_w_down]))
    mom_v = dict(zip(names, [v_ffn1_norm_pre, v_ffn1_norm_post, v_ffn1_w_gate, v_ffn1_w_up, v_ffn1_w_down,
                             v_mix_norm_pre, v_mix_norm_post, v_w_in, v_b_gate, v_conv_w, v_attn_sink,
                             v_mlstm_norm, v_w_out, v_ffn2_norm_pre, v_ffn2_norm_post, v_ffn2_w_gate,
                             v_ffn2_w_up, v_ffn2_w_down]))
    xs = x[0]
    target = loss_target[0]
    S, D = xs.shape
    depth = w_in.shape[0]
    F = ffn1_w_gate.shape[-1]
    in_shard = w_in.shape[-1]
    conv_shard = conv_w.shape[-1]
    chip = 2 * lax.axis_index("x") + lax.axis_index("y")
    cosf, sinf = _rope_tables(S)

    layers = []
    for l in range(depth):
        shards = [jnp.concatenate([w["ffn1_w_gate"][l], w["ffn1_w_up"][l]], axis=-1).astype(BF16),
                  w["ffn1_w_down"][l].astype(BF16),
                  w["w_in"][l].astype(BF16), w["w_out"][l].astype(BF16),
                  jnp.concatenate([w["ffn2_w_gate"][l], w["ffn2_w_up"][l]], axis=-1).astype(BF16),
                  w["ffn2_w_down"][l].astype(BF16),
                  jnp.pad(w["conv_w"][l], ((0, 8 - CONV_WIDTH), (0, 0)))]
        wgu1, wd1, win, wout, wgu2, wd2, convg = _comm_weight_gather(shards)
        win_full = jnp.transpose(win, (1, 0, 2)).reshape(D, 4 * in_shard)
        conv_full = jnp.transpose(convg, (1, 0, 2)).reshape(8, 4 * conv_shard)[:CONV_WIDTH]
        layers.append(dict(
            wgu1=wgu1, wd1=wd1, wgu2=wgu2, wd2=wd2,
            mix=dict(w_in=jnp.pad(win_full, ((0, 0), (0, IN_PAD - IN_WIDTH))), w_out=wout.reshape(D, D),
                     conv_w=conv_full, g_pre=w["mix_norm_pre"][l], g_post=w["mix_norm_post"][l],
                     b_gate=w["b_gate"][l], sink=w["attn_sink"][l], gamma=w["mlstm_norm"][l])))

    h = xs
    saved = []
    for l in range(depth):
        p = layers[l]
        h, s1 = _ffn_fwd(h, w["ffn1_norm_pre"][l], w["ffn1_norm_post"][l], p["wgu1"], p["wd1"])
        h, s2 = _mixer_fwd(h, p["mix"], cosf, sinf)
        h, s3 = _ffn_fwd(h, w["ffn2_norm_pre"][l], w["ffn2_norm_post"][l], p["wgu2"], p["wd2"])
        saved.append((s1, s2, s3))
    dh, loss_tile = _loss_grad(h, target)

    big = {}
    small_rows = []
    for l in reversed(range(depth)):
        p = layers[l]
        s1, s2, s3 = saved[l]
        dh, dg2_pre, dg2_post, d_wgu2, d_wd2 = _ffn_bwd(dh, s3, w["ffn2_norm_pre"][l], w["ffn2_norm_post"][l],
                                                       p["wgu2"], p["wd2"])
        dh, sm, d_win, d_wout = _mixer_bwd(dh, s2, p["mix"], cosf, sinf)
        dh, dg1_pre, dg1_post, d_wgu1, d_wd1 = _ffn_bwd(dh, s1, w["ffn1_norm_pre"][l], w["ffn1_norm_post"][l],
                                                       p["wgu1"], p["wd1"])
        d_win4 = jnp.transpose(d_win[:, :IN_WIDTH].reshape(D, 4, in_shard), (1, 0, 2))
        big[l] = _reduce_scatter_layer([d_wgu1, d_wd1, d_win4, d_wout.reshape(4, D // 4, D), d_wgu2, d_wd2])
        small_rows.append((l, [dg1_pre, dg1_post, sm["g_pre"], sm["g_post"], dg2_pre, dg2_post, sm["gamma"],
                               sm["conv"], sm["b_gate"], sm["sink"]]))

    small_rows.sort(key=lambda t: t[0])
    flat = [_pad_lanes(v) for _, vs in small_rows for v in vs] + [loss_tile[0]]
    sizes = [f.shape[0] for f in flat]
    pack = jnp.concatenate(flat)
    pack = jnp.pad(pack, (0, (-pack.shape[0]) % (8 * LANES))).reshape(-1, LANES)
    every = _comm_all_to_all_small(pack)
    total = _add_n([every[i] for i in range(8)], "small_sum").reshape(-1)
    pieces, off = [], 0
    for n in sizes:
        pieces.append(total[off:off + n])
        off += n
    loss = pieces[-1][0]
    per_layer = [pieces[10 * l:10 * l + 10] for l in range(depth)]

    def stack_small(i, shape):
        n = 1
        for s in shape:
            n *= s
        return jnp.stack([per_layer[l][i][:n].reshape(shape) for l in range(depth)])

    conv_full_grad = stack_small(7, (CONV_WIDTH, 4 * conv_shard))
    grads = {
        "ffn1_norm_pre": stack_small(0, (D,)), "ffn1_norm_post": stack_small(1, (D,)),
        "mix_norm_pre": stack_small(2, (D,)), "mix_norm_post": stack_small(3, (D,)),
        "ffn2_norm_pre": stack_small(4, (D,)), "ffn2_norm_post": stack_small(5, (D,)),
        "mlstm_norm": stack_small(6, (M_WIDTH,)),
        "conv_w": lax.dynamic_slice_in_dim(conv_full_grad, chip * conv_shard, conv_shard, 2),
        "b_gate": stack_small(8, (N_GATES,)), "attn_sink": stack_small(9, (ATT_HEADS,)),
    }
    gu1 = jnp.stack([big[l][0] for l in range(depth)])
    gu2 = jnp.stack([big[l][4] for l in range(depth)])
    grads["ffn1_w_gate"], grads["ffn1_w_up"] = gu1[:, :, :F], gu1[:, :, F:]
    grads["ffn2_w_gate"], grads["ffn2_w_up"] = gu2[:, :, :F], gu2[:, :, F:]
    grads["ffn1_w_down"] = jnp.stack([big[l][1] for l in range(depth)])
    grads["ffn2_w_down"] = jnp.stack([big[l][5] for l in range(depth)])
    grads["w_in"] = jnp.stack([big[l][2] for l in range(depth)])
    grads["w_out"] = jnp.stack([big[l][3] for l in range(depth)])

    deltas, new_m, new_v = {}, {}, {}
    for n in names:
        deltas[n], new_m[n], new_v[n] = _adamw(w[n], grads[n], mom_m[n], mom_v[n])
    grad_x = dh[None]
    return (loss, grad_x, *[grads[n] for n in names], *[deltas[n] for n in names],
            *[new_m[n] for n in names], *[new_v[n] for n in names])
```

```python
import jax
import jax.numpy as jnp
from jax import lax
from jax.experimental import pallas as pl
from jax.experimental.pallas import tpu as pltpu
from jax.experimental.pallas import tpu_sc as plsc

F32 = jnp.float32
BF16 = jnp.bfloat16
MESH = pl.DeviceIdType.MESH
ANY = pl.BlockSpec(memory_space=pl.ANY)

VMEM_LIMIT_BYTES = 56 * 1024 * 1024
LANES = 128

EPS = 1e-6
ATT_HEADS = 8
ATT_KV_HEADS = 2
ATT_GROUP = ATT_HEADS // ATT_KV_HEADS
ATT_HEAD_DIM = 128
ATT_WIDTH = ATT_HEADS * ATT_HEAD_DIM
KV_WIDTH = ATT_KV_HEADS * ATT_HEAD_DIM
BLK = 128
M_HEADS = 4
M_HEAD_DIM = 256
M_WIDTH = M_HEADS * M_HEAD_DIM
CONV_WIDTH = 5
CONV_HALO = 8
ROPE_THETA = 10000.0
N_GATES = 4 * M_HEADS
OFF_QA, OFF_KA, OFF_VA = 0, ATT_WIDTH, ATT_WIDTH + KV_WIDTH
OFF_QM = ATT_WIDTH + 2 * KV_WIDTH
OFF_KM = OFF_QM + M_WIDTH
OFF_VM = OFF_KM + M_WIDTH
OFF_OM = OFF_VM + M_WIDTH
OFF_G = OFF_OM + M_WIDTH
IN_WIDTH = OFF_G + N_GATES
IN_PAD = OFF_G + LANES
NEG_BIG = -1e30

ADAM_LR, ADAM_B1, ADAM_B2, ADAM_EPS, ADAM_WD, ADAM_STEP = 0.001, 0.9, 0.999, 1e-08, 0.01, 10


def _params(**kw):
    return pltpu.CompilerParams(vmem_limit_bytes=VMEM_LIMIT_BYTES, **kw)


def _dot(a, b):
    return lax.dot_general(a, b, (((1,), (0,)), ((), ())), preferred_element_type=F32)


def _dot_nt(a, b):
    return lax.dot_general(a, b, (((1,), (1,)), ((), ())), preferred_element_type=F32)


def _dot_tn(a, b):
    return lax.dot_general(a, b, (((0,), (0,)), ((), ())), preferred_element_type=F32)


def _sigmoid(x):
    return 1.0 / (1.0 + jnp.exp(-x))


def _eye_mask(n):
    r = lax.broadcasted_iota(jnp.int32, (n, n), 0)
    c = lax.broadcasted_iota(jnp.int32, (n, n), 1)
    return r == c


def _row_to_col(row, eye):
    n = eye.shape[0]
    return jnp.sum(jnp.where(eye, jnp.broadcast_to(row, (n, n)), 0.0), axis=1, keepdims=True)


def _col_to_row(col, eye):
    n = eye.shape[0]
    return jnp.sum(jnp.where(eye, jnp.broadcast_to(col, (n, n)), 0.0), axis=0, keepdims=True)


def _mm(a, b, kind, *, tm, tn, tk, name, out_dtype=F32, ja=None, jb=None):
    a2, b2 = a.shape[-2:], b.shape[-2:]
    if kind == "nn":
        (M, K), (_, N) = a2, b2
    elif kind == "nt":
        (M, K), (N, _) = a2, b2
    else:
        (K, M), (_, N) = a2, b2
    J = a.shape[0] if ja else (b.shape[0] if jb else 1)
    batch = "b" in (ja, jb)
    red = "r" in (ja, jb)
    nk = K // tk
    nr = nk * (J if red else 1)
    grid = ((J if batch else 1), M // tm, N // tn, nr)

    def lead(mode, g, r):
        return g if mode == "b" else r // nk

    def a_map(g, i, n, r):
        kk = r % nk
        idx = (i, kk) if kind != "tn" else (kk, i)
        return idx if ja is None else (lead(ja, g, r),) + idx

    def b_map(g, i, n, r):
        kk = r % nk
        idx = (kk, n) if kind != "nt" else (n, kk)
        return idx if jb is None else (lead(jb, g, r),) + idx

    def o_map(g, i, n, r):
        return (g, i, n) if batch else (i, n)

    a_blk = (tm, tk) if kind != "tn" else (tk, tm)
    b_blk = (tk, tn) if kind != "nt" else (tn, tk)
    dot = {"nn": _dot, "nt": _dot_nt, "tn": _dot_tn}[kind]

    def body(a_ref, b_ref, o_ref, *scratch):
        part = dot(a_ref[...], b_ref[...])
        if nr == 1:
            o_ref[...] = part.astype(out_dtype)
        else:
            acc = scratch[0]
            r = pl.program_id(3)

            @pl.when(r == 0)
            def _():
                acc[...] = part

            @pl.when(r > 0)
            def _():
                acc[...] += part

            @pl.when(r == nr - 1)
            def _():
                o_ref[...] = acc[...].astype(out_dtype)

    return pl.pallas_call(
        body, name=name, grid=grid,
        in_specs=[pl.BlockSpec(a_blk if ja is None else (None,) + a_blk, a_map),
                  pl.BlockSpec(b_blk if jb is None else (None,) + b_blk, b_map)],
        out_specs=pl.BlockSpec((None, tm, tn) if batch else (tm, tn), o_map),
        out_shape=jax.ShapeDtypeStruct((J, M, N) if batch else (M, N), out_dtype),
        scratch_shapes=[pltpu.VMEM((tm, tn), F32)] if nr > 1 else [],
        compiler_params=_params(),
    )(a, b)


def _row_tile(S, want):
    return min(S, want)


ELEMENTWISE_TILE_BYTES = 1 << 20


def _elementwise_rows(R, C):
    for cand in (1024, 512, 256, 128, 64, 32, 16, 8):
        if R % cand == 0 and R > cand and cand * C * 4 <= ELEMENTWISE_TILE_BYTES:
            return cand
    return R if R * C * 4 <= ELEMENTWISE_TILE_BYTES or R % 8 else 8


def _ffn_up(xn, wgu):
    S, D = xn.shape
    J, _, F2 = wgu.shape
    F = F2 // 2
    tm = _row_tile(S, 256)

    def body(x_ref, w_ref, gu_ref, h_ref):
        gu = _dot(x_ref[...], w_ref[...])
        g, u = gu[:, :F], gu[:, F:]
        gu_ref[...] = gu.astype(BF16)
        h_ref[...] = (g * _sigmoid(g) * u).astype(BF16)

    return pl.pallas_call(
        body, name="ffn_up", grid=(J, S // tm),
        in_specs=[pl.BlockSpec((tm, D), lambda j, i: (i, 0)),
                  pl.BlockSpec((None, D, F2), lambda j, i: (j, 0, 0))],
        out_specs=[pl.BlockSpec((None, tm, F2), lambda j, i: (j, i, 0)),
                   pl.BlockSpec((None, tm, F), lambda j, i: (j, i, 0))],
        out_shape=[jax.ShapeDtypeStruct((J, S, F2), BF16), jax.ShapeDtypeStruct((J, S, F), BF16)],
        compiler_params=_params(),
    )(xn, wgu)


def _ffn_bwd_hidden(df, wd, gu):
    S, D = df.shape
    J, F, _ = wd.shape
    F2 = 2 * F
    tm = _row_tile(S, 256)

    def body(df_ref, w_ref, gu_ref, o_ref):
        dh = _dot_nt(df_ref[...], w_ref[...])
        g = gu_ref[:, :F].astype(F32)
        u = gu_ref[:, F:].astype(F32)
        sg = _sigmoid(g)
        o_ref[:, :F] = (dh * u * (sg * (1.0 + g * (1.0 - sg)))).astype(BF16)
        o_ref[:, F:] = (dh * (g * sg)).astype(BF16)

    return pl.pallas_call(
        body, name="ffn_bwd_hidden", grid=(J, S // tm),
        in_specs=[pl.BlockSpec((tm, D), lambda j, i: (i, 0)),
                  pl.BlockSpec((None, F, D), lambda j, i: (j, 0, 0)),
                  pl.BlockSpec((None, tm, F2), lambda j, i: (j, i, 0))],
        out_specs=pl.BlockSpec((None, tm, F2), lambda j, i: (j, i, 0)),
        out_shape=jax.ShapeDtypeStruct((J, S, F2), BF16),
        compiler_params=_params(),
    )(df, wd, gu)


def _norm_fwd(x, g):
    S, D = x.shape
    tm = _row_tile(S, 512)

    def body(x_ref, g_ref, o_ref):
        xv = x_ref[...]
        r = lax.rsqrt(jnp.mean(xv * xv, axis=-1, keepdims=True) + EPS)
        o_ref[...] = (xv * r * g_ref[...]).astype(BF16)

    return pl.pallas_call(
        body, name="norm_fwd", grid=(S // tm,),
        in_specs=[pl.BlockSpec((tm, D), lambda i: (i, 0)), pl.BlockSpec((1, D), lambda i: (0, 0))],
        out_specs=pl.BlockSpec((tm, D), lambda i: (i, 0)),
        out_shape=jax.ShapeDtypeStruct((S, D), BF16),
        compiler_params=_params(),
    )(x, g.reshape(1, D))


def _resid_norm_fwd(x, f, g, alpha):
    S, D = x.shape
    tm = _row_tile(S, 512)

    def body(x_ref, f_ref, g_ref, o_ref):
        fv = f_ref[...]
        r = lax.rsqrt(jnp.mean(fv * fv, axis=-1, keepdims=True) + EPS)
        o_ref[...] = x_ref[...] + alpha * (fv * r * g_ref[...])

    return pl.pallas_call(
        body, name="resid_norm_fwd", grid=(S // tm,),
        in_specs=[pl.BlockSpec((tm, D), lambda i: (i, 0)), pl.BlockSpec((tm, D), lambda i: (i, 0)),
                  pl.BlockSpec((1, D), lambda i: (0, 0))],
        out_specs=pl.BlockSpec((tm, D), lambda i: (i, 0)),
        out_shape=jax.ShapeDtypeStruct((S, D), F32),
        compiler_params=_params(),
    )(x, f, g.reshape(1, D))


def _norm_bwd(x, g, dy, alpha, resid=None, out_dtype=F32):
    S, D = x.shape
    tm = _row_tile(S, 256)
    has_resid = resid is not None

    def body(*refs):
        x_ref, g_ref, dy_ref = refs[:3]
        res_ref = refs[3] if has_resid else None
        dx_ref, dg_ref = refs[-2:]
        xv = x_ref[...]
        r = lax.rsqrt(jnp.mean(xv * xv, axis=-1, keepdims=True) + EPS)
        xh = xv * r
        dyv = dy_ref[...].astype(F32) * alpha
        gdy = dyv * g_ref[...]
        dx = r * (gdy - xh * jnp.mean(xh * gdy, axis=-1, keepdims=True))
        if has_resid:
            dx = dx + res_ref[...]
        dx_ref[...] = dx.astype(out_dtype)
        part = jnp.sum(dyv * xh, axis=0, keepdims=True)

        @pl.when(pl.program_id(0) == 0)
        def _():
            dg_ref[...] = part

        @pl.when(pl.program_id(0) > 0)
        def _():
            dg_ref[...] += part

    row = pl.BlockSpec((tm, D), lambda i: (i, 0))
    vec = pl.BlockSpec((1, D), lambda i: (0, 0))
    ins = [x, g.reshape(1, D), dy] + ([resid] if has_resid else [])
    dx, dg = pl.pallas_call(
        body, name="norm_bwd_res" if has_resid else "norm_bwd", grid=(S // tm,),
        in_specs=[row, vec, row] + ([row] if has_resid else []),
        out_specs=[row, vec],
        out_shape=[jax.ShapeDtypeStruct((S, D), out_dtype), jax.ShapeDtypeStruct((1, D), F32)],
        compiler_params=_params(),
    )(*ins)
    return dx, dg.reshape(D)


def _loss_grad(y, target):
    S, D = y.shape
    tm = _row_tile(S, 512)

    def body(y_ref, t_ref, dy_ref, l_ref):
        err = y_ref[...] - t_ref[...]
        dy_ref[...] = err * (1.0 / D)
        part = jnp.sum(jnp.sum(err * err, axis=-1, keepdims=True) * (0.5 / D), axis=0, keepdims=True)
        part = jnp.broadcast_to(part, (8, LANES))

        @pl.when(pl.program_id(0) == 0)
        def _():
            l_ref[...] = part

        @pl.when(pl.program_id(0) > 0)
        def _():
            l_ref[...] += part

    row = pl.BlockSpec((tm, D), lambda i: (i, 0))
    return pl.pallas_call(
        body, name="loss_grad", grid=(S // tm,),
        in_specs=[row, row],
        out_specs=[row, pl.BlockSpec((8, LANES), lambda i: (0, 0))],
        out_shape=[jax.ShapeDtypeStruct((S, D), F32), jax.ShapeDtypeStruct((8, LANES), F32)],
        compiler_params=_params(),
    )(y, target)


def _add_n(xs, name):
    shape = xs[0].shape
    C = shape[-1]
    R = 1
    for s in shape[:-1]:
        R *= s
    tm = _elementwise_rows(R, C)

    def body(*refs):
        acc = refs[0][...]
        for r in refs[1:-1]:
            acc = acc + r[...]
        refs[-1][...] = acc

    row = pl.BlockSpec((tm, C), lambda i: (i, 0))
    out = pl.pallas_call(
        body, name=name, grid=(R // tm,),
        in_specs=[row] * len(xs), out_specs=row,
        out_shape=jax.ShapeDtypeStruct((R, C), F32),
        compiler_params=_params(),
    )(*[x.reshape(R, C) for x in xs])
    return out.reshape(shape)


def _rope_tables(S):
    half = ATT_HEAD_DIM // 2
    inv_freq = ROPE_THETA ** (-jnp.arange(half, dtype=F32) / half)
    ang = jnp.arange(S, dtype=F32)[:, None] * inv_freq[None, :]
    cos, sin = jnp.cos(ang), jnp.sin(ang)
    return jnp.concatenate([cos, cos], axis=-1), jnp.concatenate([-sin, sin], axis=-1)


def _rotate(x, cosf, sinf):
    return x * cosf + pltpu.roll(x, ATT_HEAD_DIM // 2, 1) * sinf


def _rope_fwd(proj, cosf, sinf):
    S = proj.shape[0]
    tm = _row_tile(S, 512)

    def body(q_ref, k_ref, v_ref, c_ref, s_ref, qo_ref, ko_ref, vo_ref):
        c, s = c_ref[...], s_ref[...]
        for h in range(ATT_HEADS):
            sl = slice(h * ATT_HEAD_DIM, (h + 1) * ATT_HEAD_DIM)
            qo_ref[:, sl] = _rotate(q_ref[:, sl], c, s).astype(BF16)
        for h in range(ATT_KV_HEADS):
            sl = slice(h * ATT_HEAD_DIM, (h + 1) * ATT_HEAD_DIM)
            ko_ref[:, sl] = _rotate(k_ref[:, sl], c, s).astype(BF16)
        vo_ref[...] = v_ref[...].astype(BF16)

    tab = pl.BlockSpec((tm, ATT_HEAD_DIM), lambda i: (i, 0))
    return pl.pallas_call(
        body, name="rope_fwd", grid=(S // tm,),
        in_specs=[pl.BlockSpec((tm, ATT_WIDTH), lambda i: (i, 0)),
                  pl.BlockSpec((tm, KV_WIDTH), lambda i: (i, OFF_KA // KV_WIDTH)),
                  pl.BlockSpec((tm, KV_WIDTH), lambda i: (i, OFF_VA // KV_WIDTH)), tab, tab],
        out_specs=[pl.BlockSpec((tm, ATT_WIDTH), lambda i: (i, 0)),
                   pl.BlockSpec((tm, KV_WIDTH), lambda i: (i, 0)),
                   pl.BlockSpec((tm, KV_WIDTH), lambda i: (i, 0))],
        out_shape=[jax.ShapeDtypeStruct((S, ATT_WIDTH), BF16), jax.ShapeDtypeStruct((S, KV_WIDTH), BF16),
                   jax.ShapeDtypeStruct((S, KV_WIDTH), BF16)],
        compiler_params=_params(),
    )(proj, proj, proj, cosf, sinf)


def _rope_bwd(dq, dk, dv, cosf, sinf):
    S = dq.shape[0]
    tm = _row_tile(S, 512)
    W = ATT_WIDTH + 2 * KV_WIDTH

    def body(q_ref, k_ref, v_ref, c_ref, s_ref, o_ref):
        c, s = c_ref[...], -s_ref[...]
        for h in range(ATT_HEADS):
            sl = slice(h * ATT_HEAD_DIM, (h + 1) * ATT_HEAD_DIM)
            o_ref[:, sl] = _rotate(q_ref[:, sl], c, s).astype(BF16)
        for h in range(ATT_KV_HEADS):
            sl = slice(h * ATT_HEAD_DIM, (h + 1) * ATT_HEAD_DIM)
            o_ref[:, ATT_WIDTH + h * ATT_HEAD_DIM:ATT_WIDTH + (h + 1) * ATT_HEAD_DIM] = (
                _rotate(k_ref[:, sl], c, s).astype(BF16))
        o_ref[:, ATT_WIDTH + KV_WIDTH:] = v_ref[...].astype(BF16)

    tab = pl.BlockSpec((tm, ATT_HEAD_DIM), lambda i: (i, 0))
    return pl.pallas_call(
        body, name="rope_bwd", grid=(S // tm,),
        in_specs=[pl.BlockSpec((tm, ATT_WIDTH), lambda i: (i, 0)),
                  pl.BlockSpec((tm, KV_WIDTH), lambda i: (i, 0)),
                  pl.BlockSpec((tm, KV_WIDTH), lambda i: (i, 0)), tab, tab],
        out_specs=pl.BlockSpec((tm, W), lambda i: (i, 0)),
        out_shape=jax.ShapeDtypeStruct((S, W), BF16),
        compiler_params=_params(),
    )(dq, dk, dv, cosf, sinf)


def _attn_probs(q_ref, k_refs, sink_ref, n, nb):
    G, L = ATT_GROUP, BLK
    q4 = jnp.concatenate([q_ref[:, g * ATT_HEAD_DIM:(g + 1) * ATT_HEAD_DIM] for g in range(G)], axis=0)
    kcat = jnp.concatenate([r[...] for r in k_refs], axis=0)
    s = _dot_nt(q4, kcat) * (ATT_HEAD_DIM ** -0.5)
    row = lax.broadcasted_iota(jnp.int32, (G * L, 3 * L), 0) % L
    col = lax.broadcasted_iota(jnp.int32, (G * L, 3 * L), 1)
    kpos = (n - 1) * L + col
    mask = (jnp.abs(col - L - row) <= L) & (kpos >= 0) & (kpos < nb * L)
    s = jnp.where(mask, s, -jnp.inf)
    sink = jnp.concatenate([jnp.broadcast_to(sink_ref[:, g:g + 1], (L, 1)) for g in range(G)], axis=0)
    m = jnp.maximum(jnp.max(s, axis=-1, keepdims=True), sink)
    p = jnp.exp(s - m)
    es = jnp.exp(sink - m)
    inv = 1.0 / (jnp.sum(p, axis=-1, keepdims=True) + es)
    return q4, kcat, p * inv, es * inv


def _kv_specs(nb):
    w = ATT_HEAD_DIM
    return [pl.BlockSpec((BLK, w), lambda kh, n: (jnp.maximum(n - 1, 0), kh)),
            pl.BlockSpec((BLK, w), lambda kh, n: (n, kh)),
            pl.BlockSpec((BLK, w), lambda kh, n: (jnp.minimum(n + 1, nb - 1), kh))]


def _attn_fwd(qr, kr, va, sink):
    S = qr.shape[0]
    nb = S // BLK
    G = ATT_GROUP

    def body(q_ref, k0, k1, k2, v0, v1, v2, sink_ref, o_ref):
        n = pl.program_id(1)
        _, _, probs, _ = _attn_probs(q_ref, (k0, k1, k2), sink_ref, n, nb)
        vcat = jnp.concatenate([v0[...], v1[...], v2[...]], axis=0)
        out = _dot(probs.astype(BF16), vcat)
        for g in range(G):
            o_ref[:, g * ATT_HEAD_DIM:(g + 1) * ATT_HEAD_DIM] = out[g * BLK:(g + 1) * BLK, :].astype(BF16)

    qspec = pl.BlockSpec((BLK, G * ATT_HEAD_DIM), lambda kh, n: (n, kh))
    return pl.pallas_call(
        body, name="attn_fwd", grid=(ATT_KV_HEADS, nb),
        in_specs=[qspec] + _kv_specs(nb) + _kv_specs(nb) + [pl.BlockSpec((None, 1, G), lambda kh, n: (kh, 0, 0))],
        out_specs=qspec,
        out_shape=jax.ShapeDtypeStruct((S, ATT_WIDTH), BF16),
        compiler_params=_params(),
    )(qr, kr, kr, kr, va, va, va, sink.reshape(ATT_KV_HEADS, 1, G))


def _attn_bwd(qr, kr, va, sink, dycat):
    S = qr.shape[0]
    nb = S // BLK
    G, L, Dh = ATT_GROUP, BLK, ATT_HEAD_DIM
    SP = S + 2 * L

    def body(q_ref, k0, k1, k2, v0, v1, v2, sink_ref, do_ref, dq_ref, dk_ref, dv_ref, ds_ref):
        n = pl.program_id(1)

        @pl.when(n == 0)
        def _():
            dk_ref[...] = jnp.zeros_like(dk_ref)
            dv_ref[...] = jnp.zeros_like(dv_ref)
            ds_ref[...] = jnp.zeros_like(ds_ref)

        q4, kcat, probs, psink = _attn_probs(q_ref, (k0, k1, k2), sink_ref, n, nb)
        vcat = jnp.concatenate([v0[...], v1[...], v2[...]], axis=0)
        do4 = jnp.concatenate([do_ref[:, g * Dh:(g + 1) * Dh] for g in range(G)], axis=0)
        pb = probs.astype(BF16)
        dob = do4.astype(BF16)
        out = _dot(pb, vcat)
        delta = jnp.sum(do4 * out, axis=-1, keepdims=True)
        dp = _dot_nt(dob, vcat)
        dsc = (probs * (dp - delta) * (Dh ** -0.5)).astype(BF16)
        dq4 = _dot(dsc, kcat)
        for g in range(G):
            dq_ref[:, g * Dh:(g + 1) * Dh] = dq4[g * L:(g + 1) * L, :]
        rows = pl.ds(pl.multiple_of(n * L, L), 3 * L)
        dk_ref[rows, :] += _dot_tn(dsc, q4)
        dv_ref[rows, :] += _dot_tn(pb, dob)
        lane = lax.broadcasted_iota(jnp.int32, (8, LANES), 1)
        dsink = jnp.zeros((8, LANES), F32)
        for g in range(G):
            val = -jnp.sum(psink[g * L:(g + 1) * L] * delta[g * L:(g + 1) * L], axis=0, keepdims=True)
            dsink = dsink + jnp.where(lane == g, jnp.broadcast_to(val, (8, LANES)), 0.0)
        ds_ref[...] += dsink

    qspec = pl.BlockSpec((L, G * Dh), lambda kh, n: (n, kh))
    accspec = pl.BlockSpec((None, SP, Dh), lambda kh, n: (kh, 0, 0))
    dq, dkp, dvp, dsink = pl.pallas_call(
        body, name="attn_bwd", grid=(ATT_KV_HEADS, nb),
        in_specs=[qspec] + _kv_specs(nb) + _kv_specs(nb)
        + [pl.BlockSpec((None, 1, G), lambda kh, n: (kh, 0, 0)), qspec],
        out_specs=[qspec, accspec, accspec, pl.BlockSpec((None, 8, LANES), lambda kh, n: (kh, 0, 0))],
        out_shape=[jax.ShapeDtypeStruct((S, ATT_WIDTH), F32), jax.ShapeDtypeStruct((ATT_KV_HEADS, SP, Dh), F32),
                   jax.ShapeDtypeStruct((ATT_KV_HEADS, SP, Dh), F32),
                   jax.ShapeDtypeStruct((ATT_KV_HEADS, 8, LANES), F32)],
        compiler_params=_params(),
    )(qr, kr, kr, kr, va, va, va, sink.reshape(ATT_KV_HEADS, 1, G), dycat)
    unpad = lambda t: jnp.transpose(t[:, L:L + S, :], (1, 0, 2)).reshape(S, KV_WIDTH)
    return dq, unpad(dkp), unpad(dvp), dsink[:, 0, :G].reshape(ATT_HEADS)


CONV_COLS = 256
CONV_ROWS = 512


def _conv_taps(xs, w_ref, rows):
    total = xs.shape[0]
    acc = None
    for j in range(CONV_WIDTH):
        shift = (CONV_WIDTH // 2 - j) % total
        term = (pltpu.roll(xs, shift, 0) if shift else xs)[CONV_HALO:CONV_HALO + rows, :] * w_ref[j:j + 1, :]
        acc = term if acc is None else acc + term
    return acc


def _conv5(xpad, w, *, act, out_dtype, name):
    SP, C = xpad.shape
    S = SP - 2 * CONV_HALO
    R = _row_tile(S, CONV_ROWS)
    tc = CONV_COLS
    half_blocks = (C // 2) // tc

    def body(x_ref, w_ref, o_ref):
        scale = jnp.where(pl.program_id(0) >= half_blocks, M_HEAD_DIM ** -0.5, 1.0)
        r0 = pl.multiple_of(pl.program_id(1) * R, R)
        y = _conv_taps(x_ref[pl.ds(r0, R + 2 * CONV_HALO), :], w_ref, R)
        if act:
            y = y * _sigmoid(y) * scale
        o_ref[...] = y.astype(out_dtype)

    return pl.pallas_call(
        body, name=name, grid=(C // tc, S // R),
        in_specs=[pl.BlockSpec((SP, tc), lambda c, i: (0, c)), pl.BlockSpec((CONV_WIDTH, tc), lambda c, i: (0, c))],
        out_specs=pl.BlockSpec((R, tc), lambda c, i: (i, c)),
        out_shape=jax.ShapeDtypeStruct((S, C), out_dtype),
        compiler_params=_params(),
    )(xpad, w)


def _conv_bwd_pre(xpad, w, dq2, dk2):
    SP, C = xpad.shape
    S = SP - 2 * CONV_HALO
    R = _row_tile(S, CONV_ROWS)
    tc = CONV_COLS
    half_blocks = (C // 2) // tc
    nrow = S // R

    def body(x_ref, w_ref, dqa_ref, dqb_ref, dka_ref, dkb_ref, o_ref, dw_ref, acc):
        is_k = pl.program_id(0) >= half_blocks
        i = pl.program_id(1)
        scale = jnp.where(is_k, M_HEAD_DIM ** -0.5, 1.0)

        @pl.when(i == 0)
        def _():
            acc[...] = jnp.zeros_like(acc)

        r0 = pl.multiple_of(i * R, R)
        xs = x_ref[pl.ds(r0, R + 2 * CONV_HALO), :]
        y = _conv_taps(xs, w_ref, R)
        sg = _sigmoid(y)
        dqv = dqa_ref[...] + dqb_ref[...]
        dkv = dka_ref[...] + dkb_ref[...]
        dpre = jnp.where(is_k, dkv, dqv) * scale * (sg * (1.0 + y * (1.0 - sg)))
        o_ref[...] = dpre
        total = xs.shape[0]
        for j in range(CONV_WIDTH):
            shift = (CONV_WIDTH // 2 - j) % total
            xj = (pltpu.roll(xs, shift, 0) if shift else xs)[CONV_HALO:CONV_HALO + R, :]
            acc[j:j + 1, :] += jnp.sum(dpre * xj, axis=0, keepdims=True)

        @pl.when(i == nrow - 1)
        def _():
            dw_ref[...] = acc[0:CONV_WIDTH, :]

    nqb = (C // 2) // tc
    qmap = lambda d: (lambda c, i: (d, i, jnp.minimum(c, nqb - 1)))
    kmap = lambda d: (lambda c, i: (d, i, jnp.maximum(c - nqb, 0)))
    gspec = lambda m: pl.BlockSpec((None, R, tc), m)
    return pl.pallas_call(
        body, name="conv_bwd_pre", grid=(C // tc, nrow),
        in_specs=[pl.BlockSpec((SP, tc), lambda c, i: (0, c)), pl.BlockSpec((CONV_WIDTH, tc), lambda c, i: (0, c)),
                  gspec(qmap(0)), gspec(qmap(1)), gspec(kmap(0)), gspec(kmap(1))],
        out_specs=[pl.BlockSpec((R, tc), lambda c, i: (i, c)), pl.BlockSpec((CONV_WIDTH, tc), lambda c, i: (0, c))],
        out_shape=[jax.ShapeDtypeStruct((S, C), F32), jax.ShapeDtypeStruct((CONV_WIDTH, C), F32)],
        scratch_shapes=[pltpu.VMEM((8, tc), F32)],
        compiler_params=_params(),
    )(xpad, w, dq2, dq2, dk2, dk2)


def _pad_rows(x):
    return jnp.pad(x, ((CONV_HALO, CONV_HALO), (0, 0)))


def _lane_cumsum(x, reverse):
    lane = lax.broadcasted_iota(jnp.int32, x.shape, 1)
    sh = 1
    while sh < LANES:
        if reverse:
            x = x + jnp.where(lane < LANES - sh, pltpu.roll(x, LANES - sh, 1), 0.0)
        else:
            x = x + jnp.where(lane >= sh, pltpu.roll(x, sh, 1), 0.0)
        sh *= 2
    return x


def _gate_prep(gates_t, bias):
    R = gates_t.shape[0]
    half, quarter = R // 2, R // 4

    def body(g_ref, b_ref, ig_ref, cum_ref):
        ig_ref[...] = g_ref[0:half, :] + b_ref[0:half, :]
        fg = g_ref[half:R, :] + b_ref[half:R, :]
        lf = jnp.minimum(fg, 0.0) - jnp.log(1.0 + jnp.exp(-jnp.abs(fg)))
        cum_ref[0:quarter, :] = _lane_cumsum(lf[0:quarter, :], False)
        cum_ref[quarter:half, :] = _lane_cumsum(lf[quarter:half, :], True)

    return pl.pallas_call(
        body, name="gate_prep",
        out_shape=[jax.ShapeDtypeStruct((half, LANES), F32), jax.ShapeDtypeStruct((half, LANES), F32)],
        compiler_params=_params(),
    )(gates_t, bias)


def _gate_bwd(gates_t, bias, di, dfc, dbl, nc):
    R = gates_t.shape[0]
    half, quarter = R // 2, R // 4

    def body(g_ref, b_ref, di_ref, df_ref, dbl_ref, dg_ref, db_ref):
        dg_ref[0:half, :] = di_ref[...]
        dfv = df_ref[...]
        within = jnp.concatenate([_lane_cumsum(dfv[0:quarter, :], True),
                                  _lane_cumsum(dfv[quarter:half, :], False)], axis=0)
        fg = g_ref[half:R, :] + b_ref[half:R, :]
        dg_ref[half:R, :] = (within + dbl_ref[...]) * _sigmoid(-fg)
        rows = jnp.broadcast_to(jnp.sum(dg_ref[...], axis=-1, keepdims=True), (R, LANES))
        gr = lax.broadcasted_iota(jnp.int32, (N_GATES, R), 0)
        gc = lax.broadcasted_iota(jnp.int32, (N_GATES, R), 1)
        db_ref[...] = lax.dot_general((gc // nc == gr).astype(F32), rows, (((1,), (0,)), ((), ())),
                                      precision=lax.Precision.HIGHEST, preferred_element_type=F32)

    return pl.pallas_call(
        body, name="gate_bwd",
        out_shape=[jax.ShapeDtypeStruct((R, LANES), F32), jax.ShapeDtypeStruct((N_GATES, LANES), F32)],
        compiler_params=_params(),
    )(gates_t, bias, di, dfc, dbl)


def _chunk_index(d, c, nc, reverse):
    j = (nc - 1 - c) if reverse else c
    return j + d * (nc - 1 - 2 * j)


def _mlstm_chunk(d, q, k, vb, brow, igrow, C, nvec, m_prev, eye):
    L = BLK
    r = lax.broadcasted_iota(jnp.int32, (L, L), 0)
    c = lax.broadcasted_iota(jnp.int32, (L, L), 1)
    mask = (r - c) * (1 - 2 * d) >= 0
    bcol = _row_to_col(brow, eye)
    igcol = _row_to_col(igrow, eye)
    log_d = jnp.where(mask, bcol - brow + igrow, -jnp.inf)
    log_inter = bcol + m_prev
    m_t = jnp.maximum(log_inter, jnp.max(log_d, axis=-1, keepdims=True))
    d_mat = jnp.exp(log_d - m_t)
    inter = jnp.exp(log_inter - m_t)
    s = _dot_nt(q, k) * d_mat
    sb = s.astype(BF16)
    cb = C.astype(BF16)
    num = _dot(sb, vb) + inter * _dot_nt(q, cb)
    den = jnp.sum(s, axis=-1, keepdims=True) + inter * jnp.sum(q.astype(F32) * nvec, axis=-1, keepdims=True)
    floor = jnp.exp(-m_t)
    denom = jnp.maximum(jnp.abs(den), floor)
    b_last = jnp.where(d == 0, brow[:, L - 1:L], brow[:, 0:1])
    return dict(bcol=bcol, igcol=igcol, m_t=m_t, d_mat=d_mat, inter=inter, sb=sb, cb=cb, num=num, den=den,
                floor=floor, denom=denom, b_last=b_last)


def _mlstm_fwd(qk, proj, ig3, b3):
    S = qk.shape[0]
    nc = S // BLK
    L, Dh, H = BLK, M_HEAD_DIM, M_HEADS

    def body(q_ref, k_ref, v_ref, ig_ref, b_ref, h_ref, cs_ref, ns_ref, ms_ref, C, nvec, m):
        d = pl.program_id(0)

        @pl.when(pl.program_id(2) == 0)
        def _():
            C[...] = jnp.zeros_like(C)
            nvec[...] = jnp.zeros_like(nvec)
            m[...] = jnp.zeros_like(m)

        eye = _eye_mask(L)
        q, k = q_ref[...], k_ref[...]
        vf = v_ref[...]
        brow, igrow = b_ref[...], ig_ref[...]
        m_prev = m[:, 0:1]
        cs_ref[...] = C[...].astype(BF16)
        ns_ref[...] = nvec[...]
        ms_ref[...] = m[...]
        f = _mlstm_chunk(d, q, k, vf.astype(BF16), brow, igrow, C[...], nvec[...], m_prev, eye)
        h_ref[...] = f["num"] / f["denom"]
        b_last = f["b_last"]
        log_w = b_last - brow + igrow
        m_new = jnp.maximum(b_last + m_prev, jnp.max(log_w, axis=-1, keepdims=True))
        w_col = jnp.exp(b_last - f["bcol"] + f["igcol"] - m_new)
        decay = jnp.exp(b_last + m_prev - m_new)
        C[...] = decay * C[...] + _dot_tn((w_col * vf).astype(BF16), k)
        nvec[...] = decay * nvec[...] + jnp.sum(w_col * k.astype(F32), axis=0, keepdims=True)
        m[...] = jnp.broadcast_to(m_new, (1, LANES))

    cidx = lambda d, c: _chunk_index(d, c, nc, False)
    gspec = pl.BlockSpec((None, 1, LANES), lambda d, h, c: ((d * H + h) * nc + cidx(d, c), 0, 0))
    st = lambda *blk: pl.BlockSpec((None, None, None) + blk, lambda d, h, c: (d, h, cidx(d, c), 0, 0))
    return pl.pallas_call(
        body, name="mlstm_fwd", grid=(2, H, nc),
        in_specs=[pl.BlockSpec((L, Dh), lambda d, h, c: (cidx(d, c), h)),
                  pl.BlockSpec((L, Dh), lambda d, h, c: (cidx(d, c), H + h)),
                  pl.BlockSpec((L, Dh), lambda d, h, c: (cidx(d, c), OFF_VM // Dh + h)), gspec, gspec],
        out_specs=[pl.BlockSpec((None, L, Dh), lambda d, h, c: (d, cidx(d, c), h)),
                   st(Dh, Dh), st(1, Dh), st(1, LANES)],
        out_shape=[jax.ShapeDtypeStruct((2, S, M_WIDTH), F32), jax.ShapeDtypeStruct((2, H, nc, Dh, Dh), BF16),
                   jax.ShapeDtypeStruct((2, H, nc, 1, Dh), F32), jax.ShapeDtypeStruct((2, H, nc, 1, LANES), F32)],
        scratch_shapes=[pltpu.VMEM((Dh, Dh), F32), pltpu.VMEM((1, Dh), F32), pltpu.VMEM((1, LANES), F32)],
        compiler_params=_params(),
    )(qk, qk, proj, ig3, b3)


def _mlstm_bwd(qk, proj, ig3, b3, cs, ns, ms, dhm):
    S = qk.shape[0]
    nc = S // BLK
    L, Dh, H = BLK, M_HEAD_DIM, M_HEADS

    def body(q_ref, k_ref, v_ref, ig_ref, b_ref, cs_ref, ns_ref, ms_ref, dh_ref,
             dq_ref, dk_ref, dv_ref, di_ref, df_ref, dbl_ref, R, rvec, mu):
        d = pl.program_id(0)

        @pl.when(pl.program_id(2) == 0)
        def _():
            R[...] = jnp.zeros_like(R)
            rvec[...] = jnp.zeros_like(rvec)
            mu[...] = jnp.full(mu.shape, NEG_BIG, F32)

        eye = _eye_mask(L)
        q, k = q_ref[...], k_ref[...]
        qf, kf = q.astype(F32), k.astype(F32)
        vb = v_ref[...].astype(BF16)
        brow, igrow = b_ref[...], ig_ref[...]
        nprev = ns_ref[...]
        f = _mlstm_chunk(d, q, k, vb, brow, igrow, cs_ref[...], nprev, ms_ref[:, 0:1], eye)
        dh = dh_ref[...]
        inv = 1.0 / f["denom"]
        hcur = f["num"] * inv
        dnum = dh * inv
        active = jnp.abs(f["den"]) >= f["floor"]
        dden = jnp.where(active, -jnp.sign(f["den"]), 0.0) * jnp.sum(dh * hcur, axis=-1, keepdims=True) * inv
        dnb = dnum.astype(BF16)
        dqk = ((_dot_nt(dnb, vb) + dden) * f["d_mat"]).astype(BF16)
        dq = _dot(dqk, k) + f["inter"] * (_dot(dnb, f["cb"]) + dden * nprev)
        mu_prev = mu[:, 0:1]
        a_col = jnp.exp(f["b_last"] - f["bcol"] + f["igcol"] + mu_prev)
        rb = R[...].astype(BF16)
        dv = _dot_tn(f["sb"], dnb) + a_col * _dot_nt(k, rb)
        dk_inter = a_col * (_dot(vb, rb) + rvec[...])
        dk = _dot_tn(dqk, q) + dk_inter
        dq_ref[...] = dq
        dk_ref[...] = dk
        dv_ref[...] = dv
        kdk = jnp.sum(kf * dk, axis=-1, keepdims=True)
        qdq = jnp.sum(qf * dq, axis=-1, keepdims=True)
        di_ref[...] = _col_to_row(kdk, eye)
        df_ref[...] = _col_to_row(qdq - kdk, eye)
        older = jnp.sum(jnp.sum(R[...] * cs_ref[...].astype(F32), axis=-1, keepdims=True), axis=0, keepdims=True)
        older = older + jnp.sum(rvec[...] * nprev, axis=-1, keepdims=True)
        dbl = (jnp.sum(jnp.sum(kf * dk_inter, axis=-1, keepdims=True), axis=0, keepdims=True)
               + jnp.exp(f["b_last"] + mu_prev + ms_ref[:, 0:1]) * older)
        dbl_ref[...] = jnp.broadcast_to(dbl, (1, LANES))
        lw = f["bcol"] - f["m_t"]
        mu_new = jnp.maximum(f["b_last"] + mu_prev, jnp.max(lw, axis=0, keepdims=True))
        wq = jnp.exp(lw - mu_new)
        decay = jnp.exp(f["b_last"] + mu_prev - mu_new)
        R[...] = decay * R[...] + _dot_tn((wq * dnum).astype(BF16), q)
        rvec[...] = decay * rvec[...] + jnp.sum(wq * dden * qf, axis=0, keepdims=True)
        mu[...] = jnp.broadcast_to(mu_new, (1, LANES))

    cidx = lambda d, c: _chunk_index(d, c, nc, True)
    gin = pl.BlockSpec((None, 1, LANES), lambda d, h, c: ((d * H + h) * nc + cidx(d, c), 0, 0))
    st = lambda *blk: pl.BlockSpec((None, None, None) + blk, lambda d, h, c: (d, h, cidx(d, c), 0, 0))
    per_dir = pl.BlockSpec((None, L, Dh), lambda d, h, c: (d, cidx(d, c), h))
    big = jax.ShapeDtypeStruct((2, S, M_WIDTH), F32)
    small = jax.ShapeDtypeStruct((2 * H * nc, 1, LANES), F32)
    return pl.pallas_call(
        body, name="mlstm_bwd", grid=(2, H, nc),
        in_specs=[pl.BlockSpec((L, Dh), lambda d, h, c: (cidx(d, c), h)),
                  pl.BlockSpec((L, Dh), lambda d, h, c: (cidx(d, c), H + h)),
                  pl.BlockSpec((L, Dh), lambda d, h, c: (cidx(d, c), OFF_VM // Dh + h)), gin, gin,
                  st(Dh, Dh), st(1, Dh), st(1, LANES),
                  pl.BlockSpec((L, Dh), lambda d, h, c: (cidx(d, c), h))],
        out_specs=[per_dir, per_dir, per_dir, gin, gin, gin],
        out_shape=[big, big, big, small, small, small],
        scratch_shapes=[pltpu.VMEM((Dh, Dh), F32), pltpu.VMEM((1, Dh), F32), pltpu.VMEM((1, LANES), F32)],
        compiler_params=_params(),
    )(qk, qk, proj, ig3, b3, cs, ns, ms, dhm)


def _mlstm_out_fwd(hs, proj, gamma):
    S = hs.shape[1]
    tm = _row_tile(S, 512)
    Dh, H = M_HEAD_DIM, M_HEADS

    def body(hf_ref, hb_ref, o_ref, g_ref, y_ref):
        hm = hf_ref[...] + hb_ref[...]
        r = lax.rsqrt(jnp.mean(hm * hm, axis=-1, keepdims=True) + EPS)
        y_ref[...] = (_sigmoid(o_ref[...]) * (hm * r * g_ref[...])).astype(BF16)

    hspec = lambda d: pl.BlockSpec((None, tm, Dh), lambda i, h: (d, i, h))
    return pl.pallas_call(
        body, name="mlstm_out_fwd", grid=(S // tm, H),
        in_specs=[hspec(0), hspec(1), pl.BlockSpec((tm, Dh), lambda i, h: (i, OFF_OM // Dh + h)),
                  pl.BlockSpec((1, Dh), lambda i, h: (0, h))],
        out_specs=pl.BlockSpec((tm, Dh), lambda i, h: (i, h)),
        out_shape=jax.ShapeDtypeStruct((S, M_WIDTH), BF16),
        compiler_params=_params(),
    )(hs, hs, proj, gamma.reshape(1, M_WIDTH))


def _mlstm_out_bwd(hs, proj, gamma, dycat):
    S = hs.shape[1]
    tm = _row_tile(S, 512)
    Dh, H = M_HEAD_DIM, M_HEADS
    D_OFF = ATT_WIDTH // Dh

    def body(hf_ref, hb_ref, o_ref, g_ref, dy_ref, dh_ref, do_ref, dg_ref):
        hm = hf_ref[...] + hb_ref[...]
        r = lax.rsqrt(jnp.mean(hm * hm, axis=-1, keepdims=True) + EPS)
        hh = hm * r
        so = _sigmoid(o_ref[...])
        dy = dy_ref[...]
        gam = g_ref[...]
        do_ref[...] = (dy * hh * gam * so * (1.0 - so)).astype(BF16)
        dyn = dy * so
        gd = dyn * gam
        dh_ref[...] = r * (gd - hh * jnp.mean(hh * gd, axis=-1, keepdims=True))
        part = jnp.sum(dyn * hh, axis=0, keepdims=True)

        @pl.when(pl.program_id(1) == 0)
        def _():
            dg_ref[...] = part

        @pl.when(pl.program_id(1) > 0)
        def _():
            dg_ref[...] += part

    hspec = lambda d: pl.BlockSpec((None, tm, Dh), lambda h, i: (d, i, h))
    out = pl.BlockSpec((tm, Dh), lambda h, i: (i, h))
    vec = pl.BlockSpec((1, Dh), lambda h, i: (0, h))
    dhm, do, dg = pl.pallas_call(
        body, name="mlstm_out_bwd", grid=(H, S // tm),
        in_specs=[hspec(0), hspec(1), pl.BlockSpec((tm, Dh), lambda h, i: (i, OFF_OM // Dh + h)), vec,
                  pl.BlockSpec((tm, Dh), lambda h, i: (i, D_OFF + h))],
        out_specs=[out, out, vec],
        out_shape=[jax.ShapeDtypeStruct((S, M_WIDTH), F32), jax.ShapeDtypeStruct((S, M_WIDTH), BF16),
                   jax.ShapeDtypeStruct((1, M_WIDTH), F32)],
        compiler_params=_params(),
    )(hs, hs, proj, gamma.reshape(1, M_WIDTH), dycat)
    return dhm, do, dg.reshape(M_WIDTH)


def _sum_cast(a2, name):
    _, S, C = a2.shape
    tm = _row_tile(S, 512)

    def body(a_ref, b_ref, o_ref):
        o_ref[...] = (a_ref[...] + b_ref[...]).astype(BF16)

    spec = lambda d: pl.BlockSpec((None, tm, C), lambda i: (d, i, 0))
    return pl.pallas_call(
        body, name=name, grid=(S // tm,),
        in_specs=[spec(0), spec(1)], out_specs=pl.BlockSpec((tm, C), lambda i: (i, 0)),
        out_shape=jax.ShapeDtypeStruct((S, C), BF16),
        compiler_params=_params(),
    )(a2, a2)


def _adamw(w, g, m, v):
    shape = w.shape
    C = shape[-1]
    R = w.size // C
    tm = _elementwise_rows(R, C)

    def body(w_ref, g_ref, m_ref, v_ref, d_ref, mo_ref, vo_ref):
        gv = g_ref[...]
        mn = ADAM_B1 * m_ref[...] + (1.0 - ADAM_B1) * gv
        vn = ADAM_B2 * v_ref[...] + (1.0 - ADAM_B2) * (gv * gv)
        m_hat = mn / (1.0 - ADAM_B1 ** ADAM_STEP)
        v_hat = vn / (1.0 - ADAM_B2 ** ADAM_STEP)
        d_ref[...] = -ADAM_LR * (m_hat / (jnp.sqrt(v_hat) + ADAM_EPS) + ADAM_WD * w_ref[...])
        mo_ref[...] = mn
        vo_ref[...] = vn

    row = pl.BlockSpec((tm, C), lambda i: (i, 0))
    sds = jax.ShapeDtypeStruct((R, C), F32)
    outs = pl.pallas_call(
        body, name="adamw", grid=(R // tm,),
        in_specs=[row] * 4, out_specs=[row] * 3, out_shape=[sds] * 3,
        compiler_params=_params(),
    )(*[t.reshape(R, C) for t in (w, g, m, v)])
    return tuple(o.reshape(shape) for o in outs)


CID_WEIGHT_GATHER, CID_PAIR_SWAP, CID_SCATTER, CID_PAIR_GATHER = 0, 1, 2, 3


def _other_chips(x, y):
    return [(1 - x, y), (x, 1 - y), (1 - x, 1 - y)]


def _handshake(peers):
    barrier = pltpu.get_barrier_semaphore()
    for peer in peers:
        pl.semaphore_signal(barrier, inc=1, device_id=peer, device_id_type=MESH)
    pl.semaphore_wait(barrier, len(peers))


def _sequencer_call(body, name, out_type, sem_counts, collective_id, operands):
    return pl.kernel(
        body, name=name, out_type=out_type,
        mesh=plsc.ScalarSubcoreMesh(axis_name="sequencer", num_cores=1),
        scratch_types=[pltpu.SemaphoreType.DMA((n,)) for n in sem_counts],
        compiler_params=pltpu.CompilerParams(collective_id=collective_id),
    )(*operands)


def _comm_weight_gather(shards):
    T = len(shards)

    def body(*refs):
        ins, outs = refs[:T], refs[T:2 * T]
        send, recv, local = refs[2 * T:]
        x, y, c = lax.axis_index("x"), lax.axis_index("y"), lax.axis_index("c")
        me = 2 * x + y
        chips = _other_chips(x, y)
        _handshake([(px, py, c) for px, py in chips])
        mine = []
        for t in range(T):
            cp = pltpu.make_async_copy(ins[t], outs[t].at[me], local.at[t])
            cp.start()
            mine.append(cp)
            for k, (px, py) in enumerate(chips):
                pltpu.make_async_remote_copy(
                    src_ref=ins[t], dst_ref=outs[t].at[me], send_sem=send.at[3 * t + k], recv_sem=recv.at[3 * t + k],
                    device_id=(px, py, c), device_id_type=MESH).start()
        for t in range(T):
            for k, (px, py) in enumerate(chips):
                cp = pltpu.make_async_remote_copy(
                    src_ref=ins[t], dst_ref=outs[t].at[2 * px + py], send_sem=send.at[3 * t + k],
                    recv_sem=recv.at[3 * t + k], device_id=(px, py, c), device_id_type=MESH)
                cp.wait_send()
                cp.wait_recv()
            mine[t].wait()

    return _sequencer_call(
        body, "comm_weight_gather", tuple(jax.ShapeDtypeStruct((4,) + s.shape, s.dtype) for s in shards),
        (3 * T, 3 * T, T), CID_WEIGHT_GATHER, shards)


def _comm_to_sibling(xs):
    T = len(xs)

    def body(*refs):
        ins, outs = refs[:T], refs[T:2 * T]
        send, recv = refs[2 * T:]
        x, y, c = lax.axis_index("x"), lax.axis_index("y"), lax.axis_index("c")
        _handshake([(x, y, 1 - c)])
        cps = [pltpu.make_async_remote_copy(
            src_ref=ins[t], dst_ref=outs[t], send_sem=send.at[t], recv_sem=recv.at[t],
            device_id=(x, y, 1 - c), device_id_type=MESH) for t in range(T)]
        for cp in cps:
            cp.start()
        for cp in cps:
            cp.wait_send()
            cp.wait_recv()

    return _sequencer_call(
        body, "comm_pair_swap", tuple(jax.ShapeDtypeStruct(s.shape, s.dtype) for s in xs),
        (T, T), CID_PAIR_SWAP, xs)


def _comm_scatter_to_owners(ps):
    T = len(ps)

    def body(*refs):
        ins, outs = refs[:T], refs[T:2 * T]
        send, recv = refs[2 * T:]
        x, y, c = lax.axis_index("x"), lax.axis_index("y"), lax.axis_index("c")
        chips = _other_chips(x, y)
        _handshake([(px, py, c) for px, py in chips])
        cps = []
        for t in range(T):
            for k, (px, py) in enumerate(chips):
                cp = pltpu.make_async_remote_copy(
                    src_ref=ins[t].at[2 * px + py], dst_ref=outs[t].at[k], send_sem=send.at[3 * t + k],
                    recv_sem=recv.at[3 * t + k], device_id=(px, py, c), device_id_type=MESH)
                cp.start()
                cps.append(cp)
        for cp in cps:
            cp.wait_send()
            cp.wait_recv()

    return _sequencer_call(
        body, "comm_scatter_to_owners", tuple(jax.ShapeDtypeStruct((3,) + p.shape[1:], p.dtype) for p in ps),
        (3 * T, 3 * T), CID_SCATTER, ps)


def _comm_pair_gather(hs):
    T = len(hs)

    def body(*refs):
        ins, outs = refs[:T], refs[T:2 * T]
        send, recv, local = refs[2 * T:]
        x, y, c = lax.axis_index("x"), lax.axis_index("y"), lax.axis_index("c")
        _handshake([(x, y, 1 - c)])
        mine = []
        for t in range(T):
            cp = pltpu.make_async_copy(ins[t], outs[t].at[c], local.at[t])
            cp.start()
            mine.append(cp)
            pltpu.make_async_remote_copy(
                src_ref=ins[t], dst_ref=outs[t].at[c], send_sem=send.at[t], recv_sem=recv.at[t],
                device_id=(x, y, 1 - c), device_id_type=MESH).start()
        for t in range(T):
            cp = pltpu.make_async_remote_copy(
                src_ref=ins[t], dst_ref=outs[t].at[1 - c], send_sem=send.at[t], recv_sem=recv.at[t],
                device_id=(x, y, 1 - c), device_id_type=MESH)
            cp.wait_send()
            cp.wait_recv()
            mine[t].wait()

    return _sequencer_call(
        body, "comm_pair_gather", tuple(jax.ShapeDtypeStruct((2,) + h.shape, h.dtype) for h in hs),
        (T, T, T), CID_PAIR_GATHER, hs)


def _comm_all_to_all_small(pack):
    R = pack.shape[0]

    def body(in_ref, out_ref, send, recv, local):
        x, y, c = lax.axis_index("x"), lax.axis_index("y"), lax.axis_index("c")
        me = 4 * x + 2 * y + c
        mine = pltpu.make_async_copy(in_ref, out_ref.at[me], local.at[0])
        mine.start()
        peers = []
        for k in range(1, 8):
            fx, fy, fc = (k >> 2) & 1, (k >> 1) & 1, k & 1
            peers.append((x + fx * (1 - 2 * x), y + fy * (1 - 2 * y), c + fc * (1 - 2 * c)))
        for k, peer in enumerate(peers):
            pltpu.make_async_remote_copy(
                src_ref=in_ref, dst_ref=out_ref.at[me], send_sem=send.at[k], recv_sem=recv.at[k],
                device_id=peer, device_id_type=MESH).start()
        for k, (px, py, pc) in enumerate(peers):
            cp = pltpu.make_async_remote_copy(
                src_ref=in_ref, dst_ref=out_ref.at[4 * px + 2 * py + pc], send_sem=send.at[k], recv_sem=recv.at[k],
                device_id=(px, py, pc), device_id_type=MESH)
            cp.wait_send()
            cp.wait_recv()
        mine.wait()

    return pl.pallas_call(
        body, name="comm_all_to_all_small",
        in_specs=[ANY], out_specs=ANY,
        out_shape=jax.ShapeDtypeStruct((8, R, LANES), F32),
        scratch_shapes=[pltpu.SemaphoreType.DMA((7,)), pltpu.SemaphoreType.DMA((7,)), pltpu.SemaphoreType.DMA((1,))],
    )(pack)


def _tie(*trees):
    return lax.optimization_barrier(trees)


class _ReduceScatter:
    def __init__(self, grads, dh):
        c = lax.axis_index("c")
        split = [g.reshape(4, 2, g.shape[1] // 2, g.shape[2]) for g in grads]
        keep = [lax.dynamic_index_in_dim(s, c, 1, keepdims=False) for s in split]
        give = [lax.dynamic_index_in_dim(s, 1 - c, 1, keepdims=False) for s in split]
        self.keep, give, self.dh = _tie(keep, give, dh)
        self.got = _comm_to_sibling(give)

    def scatter(self, dh):
        chip = 2 * lax.axis_index("x") + lax.axis_index("y")
        keep, got, dh = _tie(self.keep, list(self.got), dh)
        pair = [_add_n([a, b], "pair_sum") for a, b in zip(keep, got)]
        pair, dh = _tie(pair, dh)
        self.own = [lax.dynamic_index_in_dim(p, chip, 0, keepdims=False) for p in pair]
        self.arrived = _comm_scatter_to_owners(pair)
        return dh

    def gather(self, dh):
        own, arrived, dh = _tie(self.own, list(self.arrived), dh)
        halves = [_add_n([o, a[0], a[1], a[2]], "chip_sum") for o, a in zip(own, arrived)]
        halves, dh = _tie(halves, dh)
        self.full = _comm_pair_gather(halves)
        return dh

    def settle(self, dh):
        self.full, dh = _tie(list(self.full), dh)
        return dh

    def result(self):
        return [f.reshape(2 * f.shape[1], f.shape[2]) for f in self.full]


def _ffn_fwd(x, g_pre, g_post, wgu, wd):
    S, D = x.shape
    F = wd.shape[1]
    xn = _norm_fwd(x, g_pre)
    gu, h = _ffn_up(xn, wgu)
    f = _mm(h, wd, "nn", ja="r", jb="r", tm=_row_tile(S, 512), tn=min(D, 1024), tk=F, name="ffn_down")
    x_out = _resid_norm_fwd(x, f, g_post, 0.5)
    return x_out, (x, xn, gu, h, f)


def _ffn_bwd(dx, saved, g_pre, g_post, wgu, wd):
    x, xn, gu, h, f = saved
    S, D = x.shape
    J, F, _ = wd.shape
    ts = _row_tile(S, 512)
    df, dg_post = _norm_bwd(f, g_post, dx, 0.5, out_dtype=BF16)
    d_wd = _mm(h, df, "tn", ja="b", tm=F, tn=min(D, 1024), tk=ts, name="ffn_dwd")
    dgu = _ffn_bwd_hidden(df, wd, gu)
    d_wgu = _mm(xn, dgu, "tn", jb="b", tm=min(D, 512), tn=2 * F, tk=ts, name="ffn_dwgu")
    dxn = _mm(dgu, wgu, "nt", ja="r", jb="r", tm=ts, tn=min(D, 1024), tk=2 * F, name="ffn_dxn")
    dx_in, dg_pre = _norm_bwd(x, g_pre, dxn, 1.0, resid=dx)
    return dx_in, dg_pre, dg_post, d_wgu, d_wd


def _gates_layout(proj, b_gate, nc):
    gt = jnp.transpose(proj[:, OFF_G:OFF_G + N_GATES]).reshape(N_GATES * nc, LANES)
    bias = jnp.repeat(b_gate, nc).reshape(N_GATES * nc, 1)
    return gt, bias


def _mixer_fwd(x, p, cosf, sinf):
    S, D = x.shape
    nc = S // BLK
    hn = _norm_fwd(x, p["g_pre"])
    proj = _mm(hn, p["w_in"], "nn", tm=_row_tile(S, 512), tn=IN_PAD // 5, tk=D, name="in_proj")
    qr, kr, va = _rope_fwd(proj, cosf, sinf)
    y_att = _attn_fwd(qr, kr, va, p["sink"])
    xpad = _pad_rows(proj[:, OFF_QM:OFF_VM])
    qk = _conv5(xpad, p["conv_w"], act=True, out_dtype=BF16, name="conv_silu")
    gt, bias = _gates_layout(proj, p["b_gate"], nc)
    ig, cum = _gate_prep(gt, bias)
    ig3, b3 = ig.reshape(-1, 1, LANES), cum.reshape(-1, 1, LANES)
    hs, cs, ns, ms = _mlstm_fwd(qk, proj, ig3, b3)
    y_m = _mlstm_out_fwd(hs, proj, p["gamma"])
    ycat = jnp.concatenate([y_att, y_m], axis=-1)
    mix = _mm(ycat, p["w_out"], "nn", tm=_row_tile(S, 512), tn=min(D, 1024), tk=D, name="out_proj")
    x_out = _resid_norm_fwd(x, mix, p["g_post"], 1.0)
    return x_out, (x, hn, proj, qr, kr, va, xpad, qk, gt, bias, ig3, b3, hs, cs, ns, ms, ycat, mix)


def _mixer_bwd(dx, saved, p, cosf, sinf):
    x, hn, proj, qr, kr, va, xpad, qk, gt, bias, ig3, b3, hs, cs, ns, ms, ycat, mix = saved
    S, D = x.shape
    nc = S // BLK
    ts = _row_tile(S, 512)
    dmix, dg_post = _norm_bwd(mix, p["g_post"], dx, 1.0, out_dtype=BF16)
    d_wout = _mm(ycat, dmix, "tn", tm=min(D, 1024), tn=min(D, 1024), tk=ts, name="dw_out")
    dycat = _mm(dmix, p["w_out"], "nt", tm=ts, tn=min(D, 1024), tk=D, name="d_ycat")
    dhm, d_om, d_gamma = _mlstm_out_bwd(hs, proj, p["gamma"], dycat)
    dq2, dk2, dv2, di, dfc, dbl = _mlstm_bwd(qk, proj, ig3, b3, cs, ns, ms, dhm)
    d_vm = _sum_cast(dv2, "dv_sum")
    dgt, db = _gate_bwd(gt, bias, di.reshape(-1, LANES), dfc.reshape(-1, LANES), dbl.reshape(-1, LANES), nc)
    dpre, d_conv = _conv_bwd_pre(xpad, p["conv_w"], dq2, dk2)
    d_qkm = _conv5(_pad_rows(dpre), p["conv_w"][::-1], act=False, out_dtype=BF16, name="conv_bwd_x")
    d_gates = jnp.transpose(dgt.reshape(N_GATES, S)).astype(BF16)
    d_gates = jnp.pad(d_gates, ((0, 0), (0, IN_PAD - IN_WIDTH)))
    dqr, dkr, dva, d_sink = _attn_bwd(qr, kr, va, p["sink"], dycat)
    d_att = _rope_bwd(dqr, dkr, dva, cosf, sinf)
    dproj = jnp.concatenate([d_att, d_qkm, d_vm, d_om, d_gates], axis=-1)
    d_win = _mm(hn, dproj, "tn", tm=min(D, 1024), tn=IN_PAD // 5, tk=ts, name="dw_in")
    dhn = _mm(dproj, p["w_in"], "nt", tm=ts, tn=min(D, 1024), tk=IN_PAD // 5, name="d_hn")
    dx_in, dg_pre = _norm_bwd(x, p["g_pre"], dhn, 1.0, resid=dx)
    small = dict(g_pre=dg_pre, g_post=dg_post, gamma=d_gamma, conv=d_conv, b_gate=db[:, 0], sink=d_sink)
    return dx_in, small, d_win, d_wout


def _pad_lanes(v):
    v = v.reshape(-1)
    return jnp.pad(v, (0, (-v.shape[0]) % LANES))


def kernel(x, ffn1_norm_pre, ffn1_norm_post, ffn1_w_gate, ffn1_w_up, ffn1_w_down, mix_norm_pre, mix_norm_post, w_in, b_gate, conv_w, attn_sink, mlstm_norm, w_out, ffn2_norm_pre, ffn2_norm_post, ffn2_w_gate, ffn2_w_up, ffn2_w_down, loss_target, m_ffn1_norm_pre, m_ffn1_norm_post, m_ffn1_w_gate, m_ffn1_w_up, m_ffn1_w_down, m_mix_norm_pre, m_mix_norm_post, m_w_in, m_b_gate, m_conv_w, m_attn_sink, m_mlstm_norm, m_w_out, m_ffn2_norm_pre, m_ffn2_norm_post, m_ffn2_w_gate, m_ffn2_w_up, m_ffn2_w_down, v_ffn1_norm_pre, v_ffn1_norm_post, v_ffn1_w_gate, v_ffn1_w_up, v_ffn1_w_down, v_mix_norm_pre, v_mix_norm_post, v_w_in, v_b_gate, v_conv_w, v_attn_sink, v_mlstm_norm, v_w_out, v_ffn2_norm_pre, v_ffn2_norm_post, v_ffn2_w_gate, v_ffn2_w_up, v_ffn2_w_down):
    names = ["ffn1_norm_pre", "ffn1_norm_post", "ffn1_w_gate", "ffn1_w_up", "ffn1_w_down", "mix_norm_pre",
             "mix_norm_post", "w_in", "b_gate", "conv_w", "attn_sink", "mlstm_norm", "w_out", "ffn2_norm_pre",
             "ffn2_norm_post", "ffn2_w_gate", "ffn2_w_up", "ffn2_w_down"]
    w = dict(zip(names, [ffn1_norm_pre, ffn1_norm_post, ffn1_w_gate, ffn1_w_up, ffn1_w_down, mix_norm_pre,
                         mix_norm_post, w_in, b_gate, conv_w, attn_sink, mlstm_norm, w_out, ffn2_norm_pre,
                         ffn2_norm_post, ffn2_w_gate, ffn2_w_up, ffn2_w_down]))
    mom_m = dict(zip(names, [m_ffn1_norm_pre, m_ffn1_norm_post, m_ffn1_w_gate, m_ffn1_w_up, m_ffn1_w_down,
                             m_mix_norm_pre, m_mix_norm_post, m_w_in, m_b_gate, m_conv_w, m_attn_sink,
                             m_mlstm_norm, m_w_out, m_ffn2_norm_pre, m_ffn2_norm_post, m_ffn2_w_gate,
                             m_ffn2_w_up, m_ffn2_w_down]))
    mom_v = dict(zip(names, [v_ffn1_norm_pre, v_ffn1_norm_post, v_ffn1_w_gate, v_ffn1_w_up, v_ffn1_w_down,
                             v_mix_norm_pre, v_mix_norm_post, v_w_in, v_b_gate, v_conv_w, v_attn_sink,
                             v_mlstm_norm, v_w_out, v_ffn2_norm_pre, v_ffn2_norm_post, v_ffn2_w_gate,
                             v_ffn2_w_up, v_ffn2_w_down]))
    xs = x[0]
    target = loss_target[0]
    S, D = xs.shape
    depth = w_in.shape[0]
    F = ffn1_w_gate.shape[-1]
    in_shard = w_in.shape[-1]
    conv_shard = conv_w.shape[-1]
    chip = 2 * lax.axis_index("x") + lax.axis_index("y")
    cosf, sinf = _rope_tables(S)

    gathered = []
    for l in range(depth):
        shards = [jnp.concatenate([w["ffn1_w_gate"][l], w["ffn1_w_up"][l]], axis=-1).astype(BF16),
                  w["ffn1_w_down"][l].astype(BF16),
                  w["w_in"][l].astype(BF16), w["w_out"][l].astype(BF16),
                  jnp.concatenate([w["ffn2_w_gate"][l], w["ffn2_w_up"][l]], axis=-1).astype(BF16),
                  w["ffn2_w_down"][l].astype(BF16),
                  jnp.pad(w["conv_w"][l], ((0, 8 - CONV_WIDTH), (0, 0)))]
        gathered.append(list(_comm_weight_gather(shards)))

    def layer_weights(l, parts):
        wgu1, wd1, win, wout, wgu2, wd2, convg = parts
        win_full = jnp.concatenate([win[j] for j in range(4)], axis=-1)
        conv_full = jnp.concatenate([convg[j] for j in range(4)], axis=-1)[:CONV_WIDTH]
        return dict(
            wgu1=wgu1, wd1=wd1, wgu2=wgu2, wd2=wd2,
            mix=dict(w_in=jnp.pad(win_full, ((0, 0), (0, IN_PAD - IN_WIDTH))), w_out=wout.reshape(D, D),
                     conv_w=conv_full, g_pre=w["mix_norm_pre"][l], g_post=w["mix_norm_post"][l],
                     b_gate=w["b_gate"][l], sink=w["attn_sink"][l], gamma=w["mlstm_norm"][l]))

    h = xs
    saved = []
    layers = []
    for l in range(depth):
        parts, h = _tie(gathered[l], h)
        p = layer_weights(l, parts)
        layers.append(p)
        h, s1 = _ffn_fwd(h, w["ffn1_norm_pre"][l], w["ffn1_norm_post"][l], p["wgu1"], p["wd1"])
        h, s2 = _mixer_fwd(h, p["mix"], cosf, sinf)
        h, s3 = _ffn_fwd(h, w["ffn2_norm_pre"][l], w["ffn2_norm_post"][l], p["wgu2"], p["wd2"])
        saved.append((s1, s2, s3))
    dh, loss_tile = _loss_grad(h, target)

    big = {}
    small_rows = []
    pending = None
    for l in reversed(range(depth)):
        p = layers[l]
        s1, s2, s3 = saved[l]
        dh, dg2_pre, dg2_post, d_wgu2, d_wd2 = _ffn_bwd(dh, s3, w["ffn2_norm_pre"][l], w["ffn2_norm_post"][l],
                                                       p["wgu2"], p["wd2"])
        if pending is not None:
            dh = pending.scatter(dh)
        if l + 2 in big:
            dh = big[l + 2].settle(dh)
        dh, sm, d_win, d_wout = _mixer_bwd(dh, s2, p["mix"], cosf, sinf)
        dh, dg1_pre, dg1_post, d_wgu1, d_wd1 = _ffn_bwd(dh, s1, w["ffn1_norm_pre"][l], w["ffn1_norm_post"][l],
                                                       p["wgu1"], p["wd1"])
        if pending is not None:
            dh = pending.gather(dh)
        d_win4 = jnp.stack([d_win[:, j * in_shard:(j + 1) * in_shard] for j in range(4)])
        pending = _ReduceScatter([d_wgu1, d_wd1, d_win4, d_wout.reshape(4, D // 4, D), d_wgu2, d_wd2], dh)
        dh = pending.dh
        big[l] = pending
        small_rows.append((l, [dg1_pre, dg1_post, sm["g_pre"], sm["g_post"], dg2_pre, dg2_post, sm["gamma"],
                               sm["conv"], sm["b_gate"], sm["sink"]]))
    dh = pending.scatter(dh)
    dh = pending.gather(dh)
    big = {l: rs.result() for l, rs in big.items()}

    small_rows.sort(key=lambda t: t[0])
    flat = [_pad_lanes(v) for _, vs in small_rows for v in vs] + [loss_tile[0]]
    sizes = [f.shape[0] for f in flat]
    pack = jnp.concatenate(flat)
    pack = jnp.pad(pack, (0, (-pack.shape[0]) % (8 * LANES))).reshape(-1, LANES)
    every = _comm_all_to_all_small(pack)
    total = _add_n([every[i] for i in range(8)], "small_sum").reshape(-1)
    pieces, off = [], 0
    for n in sizes:
        pieces.append(total[off:off + n])
        off += n
    loss = pieces[-1][0]
    per_layer = [pieces[10 * l:10 * l + 10] for l in range(depth)]

    def stack_small(i, shape):
        n = 1
        for s in shape:
            n *= s
        return jnp.stack([per_layer[l][i][:n].reshape(shape) for l in range(depth)])

    conv_full_grad = stack_small(7, (CONV_WIDTH, 4 * conv_shard))
    grads = {
        "ffn1_norm_pre": stack_small(0, (D,)), "ffn1_norm_post": stack_small(1, (D,)),
        "mix_norm_pre": stack_small(2, (D,)), "mix_norm_post": stack_small(3, (D,)),
        "ffn2_norm_pre": stack_small(4, (D,)), "ffn2_norm_post": stack_small(5, (D,)),
        "mlstm_norm": stack_small(6, (M_WIDTH,)),
        "conv_w": lax.dynamic_slice_in_dim(conv_full_grad, chip * conv_shard, conv_shard, 2),
        "b_gate": stack_small(8, (N_GATES,)), "attn_sink": stack_small(9, (ATT_HEADS,)),
    }
    gu1 = jnp.stack([big[l][0] for l in range(depth)])
    gu2 = jnp.stack([big[l][4] for l in range(depth)])
    grads["ffn1_w_gate"], grads["ffn1_w_up"] = gu1[:, :, :F], gu1[:, :, F:]
    grads["ffn2_w_gate"], grads["ffn2_w_up"] = gu2[:, :, :F], gu2[:, :, F:]
    grads["ffn1_w_down"] = jnp.stack([big[l][1] for l in range(depth)])
    grads["ffn2_w_down"] = jnp.stack([big[l][5] for l in range(depth)])
    grads["w_in"] = jnp.stack([big[l][2] for l in range(depth)])
    grads["w_out"] = jnp.stack([big[l][3] for l in range(depth)])

    deltas, new_m, new_v = {}, {}, {}
    for n in names:
        deltas[n], new_m[n], new_v[n] = _adamw(w[n], grads[n], mom_m[n], mom_v[n])
    grad_x = dh[None]
    return (loss, grad_x, *[grads[n] for n in names], *[deltas[n] for n in names],
            *[new_m[n] for n in names], *[new_v[n] for n in names])
```

```python
import jax
import jax.numpy as jnp
from jax import lax
from jax.experimental import pallas as pl
from jax.experimental.pallas import tpu as pltpu
from jax.experimental.pallas import tpu_sc as plsc

F32 = jnp.float32
BF16 = jnp.bfloat16
MESH = pl.DeviceIdType.MESH
ANY = pl.BlockSpec(memory_space=pl.ANY)

VMEM_LIMIT_BYTES = 56 * 1024 * 1024
LANES = 128

EPS = 1e-6
ATT_HEADS = 8
ATT_KV_HEADS = 2
ATT_GROUP = ATT_HEADS // ATT_KV_HEADS
ATT_HEAD_DIM = 128
ATT_WIDTH = ATT_HEADS * ATT_HEAD_DIM
KV_WIDTH = ATT_KV_HEADS * ATT_HEAD_DIM
BLK = 128
M_HEADS = 4
M_HEAD_DIM = 256
M_WIDTH = M_HEADS * M_HEAD_DIM
CONV_WIDTH = 5
CONV_HALO = 8
ROPE_THETA = 10000.0
N_GATES = 4 * M_HEADS
OFF_QA, OFF_KA, OFF_VA = 0, ATT_WIDTH, ATT_WIDTH + KV_WIDTH
OFF_QM = ATT_WIDTH + 2 * KV_WIDTH
OFF_KM = OFF_QM + M_WIDTH
OFF_VM = OFF_KM + M_WIDTH
OFF_OM = OFF_VM + M_WIDTH
OFF_G = OFF_OM + M_WIDTH
IN_WIDTH = OFF_G + N_GATES
IN_PAD = OFF_G + LANES
NEG_BIG = -1e30

ADAM_LR, ADAM_B1, ADAM_B2, ADAM_EPS, ADAM_WD, ADAM_STEP = 0.001, 0.9, 0.999, 1e-08, 0.01, 10


def _params(**kw):
    return pltpu.CompilerParams(vmem_limit_bytes=VMEM_LIMIT_BYTES, **kw)


def _dot(a, b):
    return lax.dot_general(a, b, (((1,), (0,)), ((), ())), preferred_element_type=F32)


def _dot_nt(a, b):
    return lax.dot_general(a, b, (((1,), (1,)), ((), ())), preferred_element_type=F32)


def _dot_tn(a, b):
    return lax.dot_general(a, b, (((0,), (0,)), ((), ())), preferred_element_type=F32)


def _sigmoid(x):
    return 1.0 / (1.0 + jnp.exp(-x))


def _eye_mask(n):
    r = lax.broadcasted_iota(jnp.int32, (n, n), 0)
    c = lax.broadcasted_iota(jnp.int32, (n, n), 1)
    return r == c


def _row_to_col(row, eye):
    n = eye.shape[0]
    return jnp.sum(jnp.where(eye, jnp.broadcast_to(row, (n, n)), 0.0), axis=1, keepdims=True)


def _col_to_row(col, eye):
    n = eye.shape[0]
    return jnp.sum(jnp.where(eye, jnp.broadcast_to(col, (n, n)), 0.0), axis=0, keepdims=True)


def _mm(a, b, kind, *, tm, tn, tk, name, out_dtype=F32, ja=None, jb=None):
    a2, b2 = a.shape[-2:], b.shape[-2:]
    if kind == "nn":
        (M, K), (_, N) = a2, b2
    elif kind == "nt":
        (M, K), (N, _) = a2, b2
    else:
        (K, M), (_, N) = a2, b2
    J = a.shape[0] if ja else (b.shape[0] if jb else 1)
    batch = "b" in (ja, jb)
    red = "r" in (ja, jb)
    nk = K // tk
    nr = nk * (J if red else 1)
    grid = ((J if batch else 1), M // tm, N // tn, nr)

    def lead(mode, g, r):
        return g if mode == "b" else r // nk

    def a_map(g, i, n, r):
        kk = r % nk
        idx = (i, kk) if kind != "tn" else (kk, i)
        return idx if ja is None else (lead(ja, g, r),) + idx

    def b_map(g, i, n, r):
        kk = r % nk
        idx = (kk, n) if kind != "nt" else (n, kk)
        return idx if jb is None else (lead(jb, g, r),) + idx

    def o_map(g, i, n, r):
        return (g, i, n) if batch else (i, n)

    a_blk = (tm, tk) if kind != "tn" else (tk, tm)
    b_blk = (tk, tn) if kind != "nt" else (tn, tk)
    dot = {"nn": _dot, "nt": _dot_nt, "tn": _dot_tn}[kind]

    def body(a_ref, b_ref, o_ref, *scratch):
        part = dot(a_ref[...], b_ref[...])
        if nr == 1:
            o_ref[...] = part.astype(out_dtype)
        else:
            acc = scratch[0]
            r = pl.program_id(3)

            @pl.when(r == 0)
            def _():
                acc[...] = part

            @pl.when(r > 0)
            def _():
                acc[...] += part

            @pl.when(r == nr - 1)
            def _():
                o_ref[...] = acc[...].astype(out_dtype)

    return pl.pallas_call(
        body, name=name, grid=grid,
        in_specs=[pl.BlockSpec(a_blk if ja is None else (None,) + a_blk, a_map),
                  pl.BlockSpec(b_blk if jb is None else (None,) + b_blk, b_map)],
        out_specs=pl.BlockSpec((None, tm, tn) if batch else (tm, tn), o_map),
        out_shape=jax.ShapeDtypeStruct((J, M, N) if batch else (M, N), out_dtype),
        scratch_shapes=[pltpu.VMEM((tm, tn), F32)] if nr > 1 else [],
        compiler_params=_params(),
    )(a, b)


def _row_tile(S, want):
    return min(S, want)


ELEMENTWISE_TILE_BYTES = 1 << 20
ACT_ROWS = 1024
TOKEN_ROWS = 2048
MODEL_COLS = 1024


def _elementwise_rows(R, C):
    for cand in (1024, 512, 256, 128, 64, 32, 16, 8):
        if R % cand == 0 and R > cand and cand * C * 4 <= ELEMENTWISE_TILE_BYTES:
            return cand
    return R if R * C * 4 <= ELEMENTWISE_TILE_BYTES or R % 8 else 8


def _ffn_up(xn, wgu):
    S, D = xn.shape
    J, _, F2 = wgu.shape
    F = F2 // 2
    tm = _row_tile(S, 256)

    def body(x_ref, w_ref, gu_ref, h_ref):
        gu = _dot(x_ref[...], w_ref[...])
        g, u = gu[:, :F], gu[:, F:]
        gu_ref[...] = gu.astype(BF16)
        h_ref[...] = (g * _sigmoid(g) * u).astype(BF16)

    return pl.pallas_call(
        body, name="ffn_up", grid=(J, S // tm),
        in_specs=[pl.BlockSpec((tm, D), lambda j, i: (i, 0)),
                  pl.BlockSpec((None, D, F2), lambda j, i: (j, 0, 0))],
        out_specs=[pl.BlockSpec((None, tm, F2), lambda j, i: (j, i, 0)),
                   pl.BlockSpec((None, tm, F), lambda j, i: (j, i, 0))],
        out_shape=[jax.ShapeDtypeStruct((J, S, F2), BF16), jax.ShapeDtypeStruct((J, S, F), BF16)],
        compiler_params=_params(),
    )(xn, wgu)


def _ffn_bwd_hidden(df, wd, gu):
    S, D = df.shape
    J, F, _ = wd.shape
    F2 = 2 * F
    tm = _row_tile(S, 256)

    def body(df_ref, w_ref, gu_ref, o_ref):
        dh = _dot_nt(df_ref[...], w_ref[...])
        g = gu_ref[:, :F].astype(F32)
        u = gu_ref[:, F:].astype(F32)
        sg = _sigmoid(g)
        o_ref[:, :F] = (dh * u * (sg * (1.0 + g * (1.0 - sg)))).astype(BF16)
        o_ref[:, F:] = (dh * (g * sg)).astype(BF16)

    return pl.pallas_call(
        body, name="ffn_bwd_hidden", grid=(J, S // tm),
        in_specs=[pl.BlockSpec((tm, D), lambda j, i: (i, 0)),
                  pl.BlockSpec((None, F, D), lambda j, i: (j, 0, 0)),
                  pl.BlockSpec((None, tm, F2), lambda j, i: (j, i, 0))],
        out_specs=pl.BlockSpec((None, tm, F2), lambda j, i: (j, i, 0)),
        out_shape=jax.ShapeDtypeStruct((J, S, F2), BF16),
        compiler_params=_params(),
    )(df, wd, gu)


def _norm_fwd(x, g):
    S, D = x.shape
    tm = _row_tile(S, 512)

    def body(x_ref, g_ref, o_ref):
        xv = x_ref[...]
        r = lax.rsqrt(jnp.mean(xv * xv, axis=-1, keepdims=True) + EPS)
        o_ref[...] = (xv * r * g_ref[...]).astype(BF16)

    return pl.pallas_call(
        body, name="norm_fwd", grid=(S // tm,),
        in_specs=[pl.BlockSpec((tm, D), lambda i: (i, 0)), pl.BlockSpec((1, D), lambda i: (0, 0))],
        out_specs=pl.BlockSpec((tm, D), lambda i: (i, 0)),
        out_shape=jax.ShapeDtypeStruct((S, D), BF16),
        compiler_params=_params(),
    )(x, g.reshape(1, D))


def _resid_norm_fwd(x, f, g, alpha):
    S, D = x.shape
    tm = _row_tile(S, 512)

    def body(x_ref, f_ref, g_ref, o_ref):
        fv = f_ref[...]
        r = lax.rsqrt(jnp.mean(fv * fv, axis=-1, keepdims=True) + EPS)
        o_ref[...] = x_ref[...] + alpha * (fv * r * g_ref[...])

    return pl.pallas_call(
        body, name="resid_norm_fwd", grid=(S // tm,),
        in_specs=[pl.BlockSpec((tm, D), lambda i: (i, 0)), pl.BlockSpec((tm, D), lambda i: (i, 0)),
                  pl.BlockSpec((1, D), lambda i: (0, 0))],
        out_specs=pl.BlockSpec((tm, D), lambda i: (i, 0)),
        out_shape=jax.ShapeDtypeStruct((S, D), F32),
        compiler_params=_params(),
    )(x, f, g.reshape(1, D))


def _norm_bwd(x, g, dy, alpha, resid=None, out_dtype=F32):
    S, D = x.shape
    tm = _row_tile(S, 256)
    has_resid = resid is not None

    def body(*refs):
        x_ref, g_ref, dy_ref = refs[:3]
        res_ref = refs[3] if has_resid else None
        dx_ref, dg_ref = refs[-2:]
        xv = x_ref[...]
        r = lax.rsqrt(jnp.mean(xv * xv, axis=-1, keepdims=True) + EPS)
        xh = xv * r
        dyv = dy_ref[...].astype(F32) * alpha
        gdy = dyv * g_ref[...]
        dx = r * (gdy - xh * jnp.mean(xh * gdy, axis=-1, keepdims=True))
        if has_resid:
            dx = dx + res_ref[...]
        dx_ref[...] = dx.astype(out_dtype)
        part = jnp.sum(dyv * xh, axis=0, keepdims=True)

        @pl.when(pl.program_id(0) == 0)
        def _():
            dg_ref[...] = part

        @pl.when(pl.program_id(0) > 0)
        def _():
            dg_ref[...] += part

    row = pl.BlockSpec((tm, D), lambda i: (i, 0))
    vec = pl.BlockSpec((1, D), lambda i: (0, 0))
    ins = [x, g.reshape(1, D), dy] + ([resid] if has_resid else [])
    dx, dg = pl.pallas_call(
        body, name="norm_bwd_res" if has_resid else "norm_bwd", grid=(S // tm,),
        in_specs=[row, vec, row] + ([row] if has_resid else []),
        out_specs=[row, vec],
        out_shape=[jax.ShapeDtypeStruct((S, D), out_dtype), jax.ShapeDtypeStruct((1, D), F32)],
        compiler_params=_params(),
    )(*ins)
    return dx, dg.reshape(D)


def _loss_grad(y, target):
    S, D = y.shape
    tm = _row_tile(S, 512)

    def body(y_ref, t_ref, dy_ref, l_ref):
        err = y_ref[...] - t_ref[...]
        dy_ref[...] = err * (1.0 / D)
        part = jnp.sum(jnp.sum(err * err, axis=-1, keepdims=True) * (0.5 / D), axis=0, keepdims=True)
        part = jnp.broadcast_to(part, (8, LANES))

        @pl.when(pl.program_id(0) == 0)
        def _():
            l_ref[...] = part

        @pl.when(pl.program_id(0) > 0)
        def _():
            l_ref[...] += part

    row = pl.BlockSpec((tm, D), lambda i: (i, 0))
    return pl.pallas_call(
        body, name="loss_grad", grid=(S // tm,),
        in_specs=[row, row],
        out_specs=[row, pl.BlockSpec((8, LANES), lambda i: (0, 0))],
        out_shape=[jax.ShapeDtypeStruct((S, D), F32), jax.ShapeDtypeStruct((8, LANES), F32)],
        compiler_params=_params(),
    )(y, target)


def _add_n(xs, name):
    shape = xs[0].shape
    C = shape[-1]
    R = 1
    for s in shape[:-1]:
        R *= s
    tm = _elementwise_rows(R, C)

    def body(*refs):
        acc = refs[0][...]
        for r in refs[1:-1]:
            acc = acc + r[...]
        refs[-1][...] = acc

    row = pl.BlockSpec((tm, C), lambda i: (i, 0))
    out = pl.pallas_call(
        body, name=name, grid=(R // tm,),
        in_specs=[row] * len(xs), out_specs=row,
        out_shape=jax.ShapeDtypeStruct((R, C), F32),
        compiler_params=_params(),
    )(*[x.reshape(R, C) for x in xs])
    return out.reshape(shape)


def _rope_tables(S):
    half = ATT_HEAD_DIM // 2
    inv_freq = ROPE_THETA ** (-jnp.arange(half, dtype=F32) / half)
    ang = jnp.arange(S, dtype=F32)[:, None] * inv_freq[None, :]
    cos, sin = jnp.cos(ang), jnp.sin(ang)
    return jnp.concatenate([cos, cos], axis=-1), jnp.concatenate([-sin, sin], axis=-1)


def _rotate(x, cosf, sinf):
    return x * cosf + pltpu.roll(x, ATT_HEAD_DIM // 2, 1) * sinf


def _rope_fwd(proj, cosf, sinf):
    S = proj.shape[0]
    tm = _row_tile(S, 512)

    def body(q_ref, k_ref, v_ref, c_ref, s_ref, qo_ref, ko_ref, vo_ref):
        c, s = c_ref[...], s_ref[...]
        for h in range(ATT_HEADS):
            sl = slice(h * ATT_HEAD_DIM, (h + 1) * ATT_HEAD_DIM)
            qo_ref[:, sl] = _rotate(q_ref[:, sl], c, s).astype(BF16)
        for h in range(ATT_KV_HEADS):
            sl = slice(h * ATT_HEAD_DIM, (h + 1) * ATT_HEAD_DIM)
            ko_ref[:, sl] = _rotate(k_ref[:, sl], c, s).astype(BF16)
        vo_ref[...] = v_ref[...].astype(BF16)

    tab = pl.BlockSpec((tm, ATT_HEAD_DIM), lambda i: (i, 0))
    return pl.pallas_call(
        body, name="rope_fwd", grid=(S // tm,),
        in_specs=[pl.BlockSpec((tm, ATT_WIDTH), lambda i: (i, 0)),
                  pl.BlockSpec((tm, KV_WIDTH), lambda i: (i, OFF_KA // KV_WIDTH)),
                  pl.BlockSpec((tm, KV_WIDTH), lambda i: (i, OFF_VA // KV_WIDTH)), tab, tab],
        out_specs=[pl.BlockSpec((tm, ATT_WIDTH), lambda i: (i, 0)),
                   pl.BlockSpec((tm, KV_WIDTH), lambda i: (i, 0)),
                   pl.BlockSpec((tm, KV_WIDTH), lambda i: (i, 0))],
        out_shape=[jax.ShapeDtypeStruct((S, ATT_WIDTH), BF16), jax.ShapeDtypeStruct((S, KV_WIDTH), BF16),
                   jax.ShapeDtypeStruct((S, KV_WIDTH), BF16)],
        compiler_params=_params(),
    )(proj, proj, proj, cosf, sinf)


def _rope_bwd(dq, dk, dv, cosf, sinf):
    S = dq.shape[0]
    tm = _row_tile(S, 512)
    W = ATT_WIDTH + 2 * KV_WIDTH

    def body(q_ref, k_ref, v_ref, c_ref, s_ref, o_ref):
        c, s = c_ref[...], -s_ref[...]
        for h in range(ATT_HEADS):
            sl = slice(h * ATT_HEAD_DIM, (h + 1) * ATT_HEAD_DIM)
            o_ref[:, sl] = _rotate(q_ref[:, sl], c, s).astype(BF16)
        for h in range(ATT_KV_HEADS):
            sl = slice(h * ATT_HEAD_DIM, (h + 1) * ATT_HEAD_DIM)
            o_ref[:, ATT_WIDTH + h * ATT_HEAD_DIM:ATT_WIDTH + (h + 1) * ATT_HEAD_DIM] = (
                _rotate(k_ref[:, sl], c, s).astype(BF16))
        o_ref[:, ATT_WIDTH + KV_WIDTH:] = v_ref[...].astype(BF16)

    tab = pl.BlockSpec((tm, ATT_HEAD_DIM), lambda i: (i, 0))
    return pl.pallas_call(
        body, name="rope_bwd", grid=(S // tm,),
        in_specs=[pl.BlockSpec((tm, ATT_WIDTH), lambda i: (i, 0)),
                  pl.BlockSpec((tm, KV_WIDTH), lambda i: (i, 0)),
                  pl.BlockSpec((tm, KV_WIDTH), lambda i: (i, 0)), tab, tab],
        out_specs=pl.BlockSpec((tm, W), lambda i: (i, 0)),
        out_shape=jax.ShapeDtypeStruct((S, W), BF16),
        compiler_params=_params(),
    )(dq, dk, dv, cosf, sinf)


def _attn_probs(q_ref, k_refs, sink_ref, n, nb):
    G, L = ATT_GROUP, BLK
    q4 = jnp.concatenate([q_ref[:, g * ATT_HEAD_DIM:(g + 1) * ATT_HEAD_DIM] for g in range(G)], axis=0)
    kcat = jnp.concatenate([r[...] for r in k_refs], axis=0)
    s = _dot_nt(q4, kcat) * (ATT_HEAD_DIM ** -0.5)
    row = lax.broadcasted_iota(jnp.int32, (G * L, 3 * L), 0) % L
    col = lax.broadcasted_iota(jnp.int32, (G * L, 3 * L), 1)
    kpos = (n - 1) * L + col
    mask = (jnp.abs(col - L - row) <= L) & (kpos >= 0) & (kpos < nb * L)
    s = jnp.where(mask, s, -jnp.inf)
    sink = jnp.concatenate([jnp.broadcast_to(sink_ref[:, g:g + 1], (L, 1)) for g in range(G)], axis=0)
    m = jnp.maximum(jnp.max(s, axis=-1, keepdims=True), sink)
    p = jnp.exp(s - m)
    es = jnp.exp(sink - m)
    inv = 1.0 / (jnp.sum(p, axis=-1, keepdims=True) + es)
    return q4, kcat, p * inv, es * inv


def _kv_specs(nb):
    w = ATT_HEAD_DIM
    return [pl.BlockSpec((BLK, w), lambda kh, n: (jnp.maximum(n - 1, 0), kh)),
            pl.BlockSpec((BLK, w), lambda kh, n: (n, kh)),
            pl.BlockSpec((BLK, w), lambda kh, n: (jnp.minimum(n + 1, nb - 1), kh))]


def _attn_fwd(qr, kr, va, sink):
    S = qr.shape[0]
    nb = S // BLK
    G = ATT_GROUP

    def body(q_ref, k0, k1, k2, v0, v1, v2, sink_ref, o_ref):
        n = pl.program_id(1)
        _, _, probs, _ = _attn_probs(q_ref, (k0, k1, k2), sink_ref, n, nb)
        vcat = jnp.concatenate([v0[...], v1[...], v2[...]], axis=0)
        out = _dot(probs.astype(BF16), vcat)
        for g in range(G):
            o_ref[:, g * ATT_HEAD_DIM:(g + 1) * ATT_HEAD_DIM] = out[g * BLK:(g + 1) * BLK, :].astype(BF16)

    qspec = pl.BlockSpec((BLK, G * ATT_HEAD_DIM), lambda kh, n: (n, kh))
    return pl.pallas_call(
        body, name="attn_fwd", grid=(ATT_KV_HEADS, nb),
        in_specs=[qspec] + _kv_specs(nb) + _kv_specs(nb) + [pl.BlockSpec((None, 1, G), lambda kh, n: (kh, 0, 0))],
        out_specs=qspec,
        out_shape=jax.ShapeDtypeStruct((S, ATT_WIDTH), BF16),
        compiler_params=_params(),
    )(qr, kr, kr, kr, va, va, va, sink.reshape(ATT_KV_HEADS, 1, G))


def _attn_bwd(qr, kr, va, sink, dycat):
    S = qr.shape[0]
    nb = S // BLK
    G, L, Dh = ATT_GROUP, BLK, ATT_HEAD_DIM
    SP = S + 2 * L

    def body(q_ref, k0, k1, k2, v0, v1, v2, sink_ref, do_ref, dq_ref, dk_ref, dv_ref, ds_ref):
        n = pl.program_id(1)

        @pl.when(n == 0)
        def _():
            dk_ref[...] = jnp.zeros_like(dk_ref)
            dv_ref[...] = jnp.zeros_like(dv_ref)
            ds_ref[...] = jnp.zeros_like(ds_ref)

        q4, kcat, probs, psink = _attn_probs(q_ref, (k0, k1, k2), sink_ref, n, nb)
        vcat = jnp.concatenate([v0[...], v1[...], v2[...]], axis=0)
        do4 = jnp.concatenate([do_ref[:, g * Dh:(g + 1) * Dh] for g in range(G)], axis=0)
        pb = probs.astype(BF16)
        dob = do4.astype(BF16)
        out = _dot(pb, vcat)
        delta = jnp.sum(do4 * out, axis=-1, keepdims=True)
        dp = _dot_nt(dob, vcat)
        dsc = (probs * (dp - delta) * (Dh ** -0.5)).astype(BF16)
        dq4 = _dot(dsc, kcat)
        for g in range(G):
            dq_ref[:, g * Dh:(g + 1) * Dh] = dq4[g * L:(g + 1) * L, :]
        rows = pl.ds(pl.multiple_of(n * L, L), 3 * L)
        dk_ref[rows, :] += _dot_tn(dsc, q4)
        dv_ref[rows, :] += _dot_tn(pb, dob)
        lane = lax.broadcasted_iota(jnp.int32, (8, LANES), 1)
        dsink = jnp.zeros((8, LANES), F32)
        for g in range(G):
            val = -jnp.sum(psink[g * L:(g + 1) * L] * delta[g * L:(g + 1) * L], axis=0, keepdims=True)
            dsink = dsink + jnp.where(lane == g, jnp.broadcast_to(val, (8, LANES)), 0.0)
        ds_ref[...] += dsink

    qspec = pl.BlockSpec((L, G * Dh), lambda kh, n: (n, kh))
    accspec = pl.BlockSpec((None, SP, Dh), lambda kh, n: (kh, 0, 0))
    dq, dkp, dvp, dsink = pl.pallas_call(
        body, name="attn_bwd", grid=(ATT_KV_HEADS, nb),
        in_specs=[qspec] + _kv_specs(nb) + _kv_specs(nb)
        + [pl.BlockSpec((None, 1, G), lambda kh, n: (kh, 0, 0)), qspec],
        out_specs=[qspec, accspec, accspec, pl.BlockSpec((None, 8, LANES), lambda kh, n: (kh, 0, 0))],
        out_shape=[jax.ShapeDtypeStruct((S, ATT_WIDTH), F32), jax.ShapeDtypeStruct((ATT_KV_HEADS, SP, Dh), F32),
                   jax.ShapeDtypeStruct((ATT_KV_HEADS, SP, Dh), F32),
                   jax.ShapeDtypeStruct((ATT_KV_HEADS, 8, LANES), F32)],
        compiler_params=_params(),
    )(qr, kr, kr, kr, va, va, va, sink.reshape(ATT_KV_HEADS, 1, G), dycat)
    unpad = lambda t: jnp.transpose(t[:, L:L + S, :], (1, 0, 2)).reshape(S, KV_WIDTH)
    return dq, unpad(dkp), unpad(dvp), dsink[:, 0, :G].reshape(ATT_HEADS)


CONV_COLS = 256
CONV_ROWS = 512


def _conv_taps(xs, w_ref, rows):
    total = xs.shape[0]
    acc = None
    for j in range(CONV_WIDTH):
        shift = (CONV_WIDTH // 2 - j) % total
        term = (pltpu.roll(xs, shift, 0) if shift else xs)[CONV_HALO:CONV_HALO + rows, :] * w_ref[j:j + 1, :]
        acc = term if acc is None else acc + term
    return acc


def _conv5(xpad, w, *, act, out_dtype, name):
    SP, C = xpad.shape
    S = SP - 2 * CONV_HALO
    R = _row_tile(S, CONV_ROWS)
    tc = CONV_COLS
    half_blocks = (C // 2) // tc

    def body(x_ref, w_ref, o_ref):
        scale = jnp.where(pl.program_id(0) >= half_blocks, M_HEAD_DIM ** -0.5, 1.0)
        r0 = pl.multiple_of(pl.program_id(1) * R, R)
        y = _conv_taps(x_ref[pl.ds(r0, R + 2 * CONV_HALO), :], w_ref, R)
        if act:
            y = y * _sigmoid(y) * scale
        o_ref[...] = y.astype(out_dtype)

    return pl.pallas_call(
        body, name=name, grid=(C // tc, S // R),
        in_specs=[pl.BlockSpec((SP, tc), lambda c, i: (0, c)), pl.BlockSpec((CONV_WIDTH, tc), lambda c, i: (0, c))],
        out_specs=pl.BlockSpec((R, tc), lambda c, i: (i, c)),
        out_shape=jax.ShapeDtypeStruct((S, C), out_dtype),
        compiler_params=_params(),
    )(xpad, w)


def _conv_bwd_pre(xpad, w, dq2, dk2):
    SP, C = xpad.shape
    S = SP - 2 * CONV_HALO
    R = _row_tile(S, CONV_ROWS)
    tc = CONV_COLS
    half_blocks = (C // 2) // tc
    nrow = S // R

    def body(x_ref, w_ref, dqa_ref, dqb_ref, dka_ref, dkb_ref, o_ref, dw_ref, acc):
        is_k = pl.program_id(0) >= half_blocks
        i = pl.program_id(1)
        scale = jnp.where(is_k, M_HEAD_DIM ** -0.5, 1.0)

        @pl.when(i == 0)
        def _():
            acc[...] = jnp.zeros_like(acc)

        r0 = pl.multiple_of(i * R, R)
        xs = x_ref[pl.ds(r0, R + 2 * CONV_HALO), :]
        y = _conv_taps(xs, w_ref, R)
        sg = _sigmoid(y)
        dqv = dqa_ref[...] + dqb_ref[...]
        dkv = dka_ref[...] + dkb_ref[...]
        dpre = jnp.where(is_k, dkv, dqv) * scale * (sg * (1.0 + y * (1.0 - sg)))
        o_ref[...] = dpre
        total = xs.shape[0]
        for j in range(CONV_WIDTH):
            shift = (CONV_WIDTH // 2 - j) % total
            xj = (pltpu.roll(xs, shift, 0) if shift else xs)[CONV_HALO:CONV_HALO + R, :]
            acc[j:j + 1, :] += jnp.sum(dpre * xj, axis=0, keepdims=True)

        @pl.when(i == nrow - 1)
        def _():
            dw_ref[...] = acc[0:CONV_WIDTH, :]

    nqb = (C // 2) // tc
    qmap = lambda d: (lambda c, i: (d, i, jnp.minimum(c, nqb - 1)))
    kmap = lambda d: (lambda c, i: (d, i, jnp.maximum(c - nqb, 0)))
    gspec = lambda m: pl.BlockSpec((None, R, tc), m)
    return pl.pallas_call(
        body, name="conv_bwd_pre", grid=(C // tc, nrow),
        in_specs=[pl.BlockSpec((SP, tc), lambda c, i: (0, c)), pl.BlockSpec((CONV_WIDTH, tc), lambda c, i: (0, c)),
                  gspec(qmap(0)), gspec(qmap(1)), gspec(kmap(0)), gspec(kmap(1))],
        out_specs=[pl.BlockSpec((R, tc), lambda c, i: (i, c)), pl.BlockSpec((CONV_WIDTH, tc), lambda c, i: (0, c))],
        out_shape=[jax.ShapeDtypeStruct((S, C), F32), jax.ShapeDtypeStruct((CONV_WIDTH, C), F32)],
        scratch_shapes=[pltpu.VMEM((8, tc), F32)],
        compiler_params=_params(),
    )(xpad, w, dq2, dq2, dk2, dk2)


def _pad_rows(x):
    return jnp.pad(x, ((CONV_HALO, CONV_HALO), (0, 0)))


def _lane_cumsum(x, reverse):
    lane = lax.broadcasted_iota(jnp.int32, x.shape, 1)
    sh = 1
    while sh < LANES:
        if reverse:
            x = x + jnp.where(lane < LANES - sh, pltpu.roll(x, LANES - sh, 1), 0.0)
        else:
            x = x + jnp.where(lane >= sh, pltpu.roll(x, sh, 1), 0.0)
        sh *= 2
    return x


def _gate_prep(gates_t, bias):
    R = gates_t.shape[0]
    half, quarter = R // 2, R // 4

    def body(g_ref, b_ref, ig_ref, cum_ref):
        ig_ref[...] = g_ref[0:half, :] + b_ref[0:half, :]
        fg = g_ref[half:R, :] + b_ref[half:R, :]
        lf = jnp.minimum(fg, 0.0) - jnp.log(1.0 + jnp.exp(-jnp.abs(fg)))
        cum_ref[0:quarter, :] = _lane_cumsum(lf[0:quarter, :], False)
        cum_ref[quarter:half, :] = _lane_cumsum(lf[quarter:half, :], True)

    return pl.pallas_call(
        body, name="gate_prep",
        out_shape=[jax.ShapeDtypeStruct((half, LANES), F32), jax.ShapeDtypeStruct((half, LANES), F32)],
        compiler_params=_params(),
    )(gates_t, bias)


def _gate_bwd(gates_t, bias, di, dfc, dbl, nc):
    R = gates_t.shape[0]
    half, quarter = R // 2, R // 4

    def body(g_ref, b_ref, di_ref, df_ref, dbl_ref, dg_ref, db_ref):
        dg_ref[0:half, :] = di_ref[...]
        dfv = df_ref[...]
        within = jnp.concatenate([_lane_cumsum(dfv[0:quarter, :], True),
                                  _lane_cumsum(dfv[quarter:half, :], False)], axis=0)
        fg = g_ref[half:R, :] + b_ref[half:R, :]
        dg_ref[half:R, :] = (within + dbl_ref[...]) * _sigmoid(-fg)
        rows = jnp.broadcast_to(jnp.sum(dg_ref[...], axis=-1, keepdims=True), (R, LANES))
        gr = lax.broadcasted_iota(jnp.int32, (N_GATES, R), 0)
        gc = lax.broadcasted_iota(jnp.int32, (N_GATES, R), 1)
        db_ref[...] = lax.dot_general((gc // nc == gr).astype(F32), rows, (((1,), (0,)), ((), ())),
                                      precision=lax.Precision.HIGHEST, preferred_element_type=F32)

    return pl.pallas_call(
        body, name="gate_bwd",
        out_shape=[jax.ShapeDtypeStruct((R, LANES), F32), jax.ShapeDtypeStruct((N_GATES, LANES), F32)],
        compiler_params=_params(),
    )(gates_t, bias, di, dfc, dbl)


def _chunk_index(d, c, nc, reverse):
    j = (nc - 1 - c) if reverse else c
    return j + d * (nc - 1 - 2 * j)


def _mlstm_chunk(d, q, k, vb, brow, igrow, C, nvec, m_prev, eye):
    L = BLK
    r = lax.broadcasted_iota(jnp.int32, (L, L), 0)
    c = lax.broadcasted_iota(jnp.int32, (L, L), 1)
    mask = (r - c) * (1 - 2 * d) >= 0
    bcol = _row_to_col(brow, eye)
    igcol = _row_to_col(igrow, eye)
    log_d = jnp.where(mask, bcol - brow + igrow, -jnp.inf)
    log_inter = bcol + m_prev
    m_t = jnp.maximum(log_inter, jnp.max(log_d, axis=-1, keepdims=True))
    d_mat = jnp.exp(log_d - m_t)
    inter = jnp.exp(log_inter - m_t)
    s = _dot_nt(q, k) * d_mat
    sb = s.astype(BF16)
    cb = C.astype(BF16)
    num = _dot(sb, vb) + inter * _dot_nt(q, cb)
    den = jnp.sum(s, axis=-1, keepdims=True) + inter * jnp.sum(q.astype(F32) * nvec, axis=-1, keepdims=True)
    floor = jnp.exp(-m_t)
    denom = jnp.maximum(jnp.abs(den), floor)
    b_last = jnp.where(d == 0, brow[:, L - 1:L], brow[:, 0:1])
    return dict(bcol=bcol, igcol=igcol, m_t=m_t, d_mat=d_mat, inter=inter, sb=sb, cb=cb, num=num, den=den,
                floor=floor, denom=denom, b_last=b_last)


def _mlstm_fwd(qk, proj, ig3, b3):
    S = qk.shape[0]
    nc = S // BLK
    L, Dh, H = BLK, M_HEAD_DIM, M_HEADS

    def body(q_ref, k_ref, v_ref, ig_ref, b_ref, h_ref, cs_ref, ns_ref, ms_ref, C, nvec, m):
        d = pl.program_id(0)

        @pl.when(pl.program_id(2) == 0)
        def _():
            C[...] = jnp.zeros_like(C)
            nvec[...] = jnp.zeros_like(nvec)
            m[...] = jnp.zeros_like(m)

        eye = _eye_mask(L)
        q, k = q_ref[...], k_ref[...]
        vf = v_ref[...]
        brow, igrow = b_ref[...], ig_ref[...]
        m_prev = m[:, 0:1]
        cs_ref[...] = C[...].astype(BF16)
        ns_ref[...] = nvec[...]
        ms_ref[...] = m[...]
        f = _mlstm_chunk(d, q, k, vf.astype(BF16), brow, igrow, C[...], nvec[...], m_prev, eye)
        h_ref[...] = f["num"] / f["denom"]
        b_last = f["b_last"]
        log_w = b_last - brow + igrow
        m_new = jnp.maximum(b_last + m_prev, jnp.max(log_w, axis=-1, keepdims=True))
        w_col = jnp.exp(b_last - f["bcol"] + f["igcol"] - m_new)
        decay = jnp.exp(b_last + m_prev - m_new)
        C[...] = decay * C[...] + _dot_tn((w_col * vf).astype(BF16), k)
        nvec[...] = decay * nvec[...] + jnp.sum(w_col * k.astype(F32), axis=0, keepdims=True)
        m[...] = jnp.broadcast_to(m_new, (1, LANES))

    cidx = lambda d, c: _chunk_index(d, c, nc, False)
    gspec = pl.BlockSpec((None, 1, LANES), lambda d, h, c: ((d * H + h) * nc + cidx(d, c), 0, 0))
    st = lambda *blk: pl.BlockSpec((None, None, None) + blk, lambda d, h, c: (d, h, cidx(d, c), 0, 0))
    return pl.pallas_call(
        body, name="mlstm_fwd", grid=(2, H, nc),
        in_specs=[pl.BlockSpec((L, Dh), lambda d, h, c: (cidx(d, c), h)),
                  pl.BlockSpec((L, Dh), lambda d, h, c: (cidx(d, c), H + h)),
                  pl.BlockSpec((L, Dh), lambda d, h, c: (cidx(d, c), OFF_VM // Dh + h)), gspec, gspec],
        out_specs=[pl.BlockSpec((None, L, Dh), lambda d, h, c: (d, cidx(d, c), h)),
                   st(Dh, Dh), st(1, Dh), st(1, LANES)],
        out_shape=[jax.ShapeDtypeStruct((2, S, M_WIDTH), F32), jax.ShapeDtypeStruct((2, H, nc, Dh, Dh), BF16),
                   jax.ShapeDtypeStruct((2, H, nc, 1, Dh), F32), jax.ShapeDtypeStruct((2, H, nc, 1, LANES), F32)],
        scratch_shapes=[pltpu.VMEM((Dh, Dh), F32), pltpu.VMEM((1, Dh), F32), pltpu.VMEM((1, LANES), F32)],
        compiler_params=_params(),
    )(qk, qk, proj, ig3, b3)


def _mlstm_bwd(qk, proj, ig3, b3, cs, ns, ms, dhm):
    S = qk.shape[0]
    nc = S // BLK
    L, Dh, H = BLK, M_HEAD_DIM, M_HEADS

    def body(q_ref, k_ref, v_ref, ig_ref, b_ref, cs_ref, ns_ref, ms_ref, dh_ref,
             dq_ref, dk_ref, dv_ref, di_ref, df_ref, dbl_ref, R, rvec, mu):
        d = pl.program_id(0)

        @pl.when(pl.program_id(2) == 0)
        def _():
            R[...] = jnp.zeros_like(R)
            rvec[...] = jnp.zeros_like(rvec)
            mu[...] = jnp.full(mu.shape, NEG_BIG, F32)

        eye = _eye_mask(L)
        q, k = q_ref[...], k_ref[...]
        qf, kf = q.astype(F32), k.astype(F32)
        vb = v_ref[...].astype(BF16)
        brow, igrow = b_ref[...], ig_ref[...]
        nprev = ns_ref[...]
        f = _mlstm_chunk(d, q, k, vb, brow, igrow, cs_ref[...], nprev, ms_ref[:, 0:1], eye)
        dh = dh_ref[...]
        inv = 1.0 / f["denom"]
        hcur = f["num"] * inv
        dnum = dh * inv
        active = jnp.abs(f["den"]) >= f["floor"]
        dden = jnp.where(active, -jnp.sign(f["den"]), 0.0) * jnp.sum(dh * hcur, axis=-1, keepdims=True) * inv
        dnb = dnum.astype(BF16)
        dqk = ((_dot_nt(dnb, vb) + dden) * f["d_mat"]).astype(BF16)
        dq = _dot(dqk, k) + f["inter"] * (_dot(dnb, f["cb"]) + dden * nprev)
        mu_prev = mu[:, 0:1]
        a_col = jnp.exp(f["b_last"] - f["bcol"] + f["igcol"] + mu_prev)
        rb = R[...].astype(BF16)
        dv = _dot_tn(f["sb"], dnb) + a_col * _dot_nt(k, rb)
        dk_inter = a_col * (_dot(vb, rb) + rvec[...])
        dk = _dot_tn(dqk, q) + dk_inter
        dq_ref[...] = dq
        dk_ref[...] = dk
        dv_ref[...] = dv
        kdk = jnp.sum(kf * dk, axis=-1, keepdims=True)
        qdq = jnp.sum(qf * dq, axis=-1, keepdims=True)
        di_ref[...] = _col_to_row(kdk, eye)
        df_ref[...] = _col_to_row(qdq - kdk, eye)
        older = jnp.sum(jnp.sum(R[...] * cs_ref[...].astype(F32), axis=-1, keepdims=True), axis=0, keepdims=True)
        older = older + jnp.sum(rvec[...] * nprev, axis=-1, keepdims=True)
        dbl = (jnp.sum(jnp.sum(kf * dk_inter, axis=-1, keepdims=True), axis=0, keepdims=True)
               + jnp.exp(f["b_last"] + mu_prev + ms_ref[:, 0:1]) * older)
        dbl_ref[...] = jnp.broadcast_to(dbl, (1, LANES))
        lw = f["bcol"] - f["m_t"]
        mu_new = jnp.maximum(f["b_last"] + mu_prev, jnp.max(lw, axis=0, keepdims=True))
        wq = jnp.exp(lw - mu_new)
        decay = jnp.exp(f["b_last"] + mu_prev - mu_new)
        R[...] = decay * R[...] + _dot_tn((wq * dnum).astype(BF16), q)
        rvec[...] = decay * rvec[...] + jnp.sum(wq * dden * qf, axis=0, keepdims=True)
        mu[...] = jnp.broadcast_to(mu_new, (1, LANES))

    cidx = lambda d, c: _chunk_index(d, c, nc, True)
    gin = pl.BlockSpec((None, 1, LANES), lambda d, h, c: ((d * H + h) * nc + cidx(d, c), 0, 0))
    st = lambda *blk: pl.BlockSpec((None, None, None) + blk, lambda d, h, c: (d, h, cidx(d, c), 0, 0))
    per_dir = pl.BlockSpec((None, L, Dh), lambda d, h, c: (d, cidx(d, c), h))
    big = jax.ShapeDtypeStruct((2, S, M_WIDTH), F32)
    small = jax.ShapeDtypeStruct((2 * H * nc, 1, LANES), F32)
    return pl.pallas_call(
        body, name="mlstm_bwd", grid=(2, H, nc),
        in_specs=[pl.BlockSpec((L, Dh), lambda d, h, c: (cidx(d, c), h)),
                  pl.BlockSpec((L, Dh), lambda d, h, c: (cidx(d, c), H + h)),
                  pl.BlockSpec((L, Dh), lambda d, h, c: (cidx(d, c), OFF_VM // Dh + h)), gin, gin,
                  st(Dh, Dh), st(1, Dh), st(1, LANES),
                  pl.BlockSpec((L, Dh), lambda d, h, c: (cidx(d, c), h))],
        out_specs=[per_dir, per_dir, per_dir, gin, gin, gin],
        out_shape=[big, big, big, small, small, small],
        scratch_shapes=[pltpu.VMEM((Dh, Dh), F32), pltpu.VMEM((1, Dh), F32), pltpu.VMEM((1, LANES), F32)],
        compiler_params=_params(),
    )(qk, qk, proj, ig3, b3, cs, ns, ms, dhm)


def _mlstm_out_fwd(hs, proj, gamma):
    S = hs.shape[1]
    tm = _row_tile(S, 512)
    Dh, H = M_HEAD_DIM, M_HEADS

    def body(hf_ref, hb_ref, o_ref, g_ref, y_ref):
        hm = hf_ref[...] + hb_ref[...]
        r = lax.rsqrt(jnp.mean(hm * hm, axis=-1, keepdims=True) + EPS)
        y_ref[...] = (_sigmoid(o_ref[...]) * (hm * r * g_ref[...])).astype(BF16)

    hspec = lambda d: pl.BlockSpec((None, tm, Dh), lambda i, h: (d, i, h))
    return pl.pallas_call(
        body, name="mlstm_out_fwd", grid=(S // tm, H),
        in_specs=[hspec(0), hspec(1), pl.BlockSpec((tm, Dh), lambda i, h: (i, OFF_OM // Dh + h)),
                  pl.BlockSpec((1, Dh), lambda i, h: (0, h))],
        out_specs=pl.BlockSpec((tm, Dh), lambda i, h: (i, h)),
        out_shape=jax.ShapeDtypeStruct((S, M_WIDTH), BF16),
        compiler_params=_params(),
    )(hs, hs, proj, gamma.reshape(1, M_WIDTH))


def _mlstm_out_bwd(hs, proj, gamma, dycat):
    S = hs.shape[1]
    tm = _row_tile(S, 512)
    Dh, H = M_HEAD_DIM, M_HEADS
    D_OFF = ATT_WIDTH // Dh

    def body(hf_ref, hb_ref, o_ref, g_ref, dy_ref, dh_ref, do_ref, dg_ref):
        hm = hf_ref[...] + hb_ref[...]
        r = lax.rsqrt(jnp.mean(hm * hm, axis=-1, keepdims=True) + EPS)
        hh = hm * r
        so = _sigmoid(o_ref[...])
        dy = dy_ref[...]
        gam = g_ref[...]
        do_ref[...] = (dy * hh * gam * so * (1.0 - so)).astype(BF16)
        dyn = dy * so
        gd = dyn * gam
        dh_ref[...] = r * (gd - hh * jnp.mean(hh * gd, axis=-1, keepdims=True))
        part = jnp.sum(dyn * hh, axis=0, keepdims=True)

        @pl.when(pl.program_id(1) == 0)
        def _():
            dg_ref[...] = part

        @pl.when(pl.program_id(1) > 0)
        def _():
            dg_ref[...] += part

    hspec = lambda d: pl.BlockSpec((None, tm, Dh), lambda h, i: (d, i, h))
    out = pl.BlockSpec((tm, Dh), lambda h, i: (i, h))
    vec = pl.BlockSpec((1, Dh), lambda h, i: (0, h))
    dhm, do, dg = pl.pallas_call(
        body, name="mlstm_out_bwd", grid=(H, S // tm),
        in_specs=[hspec(0), hspec(1), pl.BlockSpec((tm, Dh), lambda h, i: (i, OFF_OM // Dh + h)), vec,
                  pl.BlockSpec((tm, Dh), lambda h, i: (i, D_OFF + h))],
        out_specs=[out, out, vec],
        out_shape=[jax.ShapeDtypeStruct((S, M_WIDTH), F32), jax.ShapeDtypeStruct((S, M_WIDTH), BF16),
                   jax.ShapeDtypeStruct((1, M_WIDTH), F32)],
        compiler_params=_params(),
    )(hs, hs, proj, gamma.reshape(1, M_WIDTH), dycat)
    return dhm, do, dg.reshape(M_WIDTH)


def _sum_cast(a2, name):
    _, S, C = a2.shape
    tm = _row_tile(S, 512)

    def body(a_ref, b_ref, o_ref):
        o_ref[...] = (a_ref[...] + b_ref[...]).astype(BF16)

    spec = lambda d: pl.BlockSpec((None, tm, C), lambda i: (d, i, 0))
    return pl.pallas_call(
        body, name=name, grid=(S // tm,),
        in_specs=[spec(0), spec(1)], out_specs=pl.BlockSpec((tm, C), lambda i: (i, 0)),
        out_shape=jax.ShapeDtypeStruct((S, C), BF16),
        compiler_params=_params(),
    )(a2, a2)


def _adamw(w, g, m, v):
    shape = w.shape
    C = shape[-1]
    R = w.size // C
    tm = _elementwise_rows(R, C)

    def body(w_ref, g_ref, m_ref, v_ref, d_ref, mo_ref, vo_ref):
        gv = g_ref[...]
        mn = ADAM_B1 * m_ref[...] + (1.0 - ADAM_B1) * gv
        vn = ADAM_B2 * v_ref[...] + (1.0 - ADAM_B2) * (gv * gv)
        m_hat = mn / (1.0 - ADAM_B1 ** ADAM_STEP)
        v_hat = vn / (1.0 - ADAM_B2 ** ADAM_STEP)
        d_ref[...] = -ADAM_LR * (m_hat / (jnp.sqrt(v_hat) + ADAM_EPS) + ADAM_WD * w_ref[...])
        mo_ref[...] = mn
        vo_ref[...] = vn

    row = pl.BlockSpec((tm, C), lambda i: (i, 0))
    sds = jax.ShapeDtypeStruct((R, C), F32)
    outs = pl.pallas_call(
        body, name="adamw", grid=(R // tm,),
        in_specs=[row] * 4, out_specs=[row] * 3, out_shape=[sds] * 3,
        compiler_params=_params(),
    )(*[t.reshape(R, C) for t in (w, g, m, v)])
    return tuple(o.reshape(shape) for o in outs)


CID_WEIGHT_GATHER, CID_PAIR_SWAP, CID_SCATTER = 0, 1, 2


def _other_chips(x, y):
    return [(1 - x, y), (x, 1 - y), (1 - x, 1 - y)]


def _handshake(peers):
    barrier = pltpu.get_barrier_semaphore()
    for peer in peers:
        pl.semaphore_signal(barrier, inc=1, device_id=peer, device_id_type=MESH)
    pl.semaphore_wait(barrier, len(peers))


def _sequencer_call(body, name, out_type, sem_counts, collective_id, operands):
    return pl.kernel(
        body, name=name, out_type=out_type,
        mesh=plsc.ScalarSubcoreMesh(axis_name="sequencer", num_cores=1),
        scratch_types=[pltpu.SemaphoreType.DMA((n,)) for n in sem_counts],
        compiler_params=pltpu.CompilerParams(collective_id=collective_id),
    )(*operands)


def _comm_weight_gather(shards):
    T = len(shards)

    def body(*refs):
        ins, outs = refs[:T], refs[T:2 * T]
        send, recv, local = refs[2 * T:]
        x, y, c = lax.axis_index("x"), lax.axis_index("y"), lax.axis_index("c")
        me = 2 * x + y
        chips = _other_chips(x, y)
        _handshake([(px, py, c) for px, py in chips])
        mine = []
        for t in range(T):
            cp = pltpu.make_async_copy(ins[t], outs[t].at[me], local.at[t])
            cp.start()
            mine.append(cp)
            for k, (px, py) in enumerate(chips):
                pltpu.make_async_remote_copy(
                    src_ref=ins[t], dst_ref=outs[t].at[me], send_sem=send.at[3 * t + k], recv_sem=recv.at[3 * t + k],
                    device_id=(px, py, c), device_id_type=MESH).start()
        for t in range(T):
            for k, (px, py) in enumerate(chips):
                cp = pltpu.make_async_remote_copy(
                    src_ref=ins[t], dst_ref=outs[t].at[2 * px + py], send_sem=send.at[3 * t + k],
                    recv_sem=recv.at[3 * t + k], device_id=(px, py, c), device_id_type=MESH)
                cp.wait_send()
                cp.wait_recv()
            mine[t].wait()

    return _sequencer_call(
        body, "comm_weight_gather", tuple(jax.ShapeDtypeStruct((4,) + s.shape, s.dtype) for s in shards),
        (3 * T, 3 * T, T), CID_WEIGHT_GATHER, shards)


def _comm_to_sibling(xs):
    T = len(xs)

    def body(*refs):
        ins, outs = refs[:T], refs[T:2 * T]
        send, recv = refs[2 * T:]
        x, y, c = lax.axis_index("x"), lax.axis_index("y"), lax.axis_index("c")
        _handshake([(x, y, 1 - c)])
        cps = [pltpu.make_async_remote_copy(
            src_ref=ins[t], dst_ref=outs[t], send_sem=send.at[t], recv_sem=recv.at[t],
            device_id=(x, y, 1 - c), device_id_type=MESH) for t in range(T)]
        for cp in cps:
            cp.start()
        for cp in cps:
            cp.wait_send()
            cp.wait_recv()

    return _sequencer_call(
        body, "comm_pair_swap", tuple(jax.ShapeDtypeStruct(s.shape, s.dtype) for s in xs),
        (T, T), CID_PAIR_SWAP, xs)


def _comm_scatter_to_owners(ps):
    T = len(ps)

    def body(*refs):
        ins, outs = refs[:T], refs[T:2 * T]
        send, recv = refs[2 * T:]
        x, y, c = lax.axis_index("x"), lax.axis_index("y"), lax.axis_index("c")
        chips = _other_chips(x, y)
        _handshake([(px, py, c) for px, py in chips])
        cps = []
        for t in range(T):
            for k, (px, py) in enumerate(chips):
                cp = pltpu.make_async_remote_copy(
                    src_ref=ins[t].at[2 * px + py], dst_ref=outs[t].at[k], send_sem=send.at[3 * t + k],
                    recv_sem=recv.at[3 * t + k], device_id=(px, py, c), device_id_type=MESH)
                cp.start()
                cps.append(cp)
        for cp in cps:
            cp.wait_send()
            cp.wait_recv()

    return _sequencer_call(
        body, "comm_scatter_to_owners", tuple(jax.ShapeDtypeStruct((3,) + p.shape[1:], p.dtype) for p in ps),
        (3 * T, 3 * T), CID_SCATTER, ps)


def _comm_all_to_all_small(pack):
    R = pack.shape[0]

    def body(in_ref, out_ref, send, recv, local):
        x, y, c = lax.axis_index("x"), lax.axis_index("y"), lax.axis_index("c")
        me = 4 * x + 2 * y + c
        mine = pltpu.make_async_copy(in_ref, out_ref.at[me], local.at[0])
        mine.start()
        peers = []
        for k in range(1, 8):
            fx, fy, fc = (k >> 2) & 1, (k >> 1) & 1, k & 1
            peers.append((x + fx * (1 - 2 * x), y + fy * (1 - 2 * y), c + fc * (1 - 2 * c)))
        for k, peer in enumerate(peers):
            pltpu.make_async_remote_copy(
                src_ref=in_ref, dst_ref=out_ref.at[me], send_sem=send.at[k], recv_sem=recv.at[k],
                device_id=peer, device_id_type=MESH).start()
        for k, (px, py, pc) in enumerate(peers):
            cp = pltpu.make_async_remote_copy(
                src_ref=in_ref, dst_ref=out_ref.at[4 * px + 2 * py + pc], send_sem=send.at[k], recv_sem=recv.at[k],
                device_id=(px, py, pc), device_id_type=MESH)
            cp.wait_send()
            cp.wait_recv()
        mine.wait()

    return pl.pallas_call(
        body, name="comm_all_to_all_small",
        in_specs=[ANY], out_specs=ANY,
        out_shape=jax.ShapeDtypeStruct((8, R, LANES), F32),
        scratch_shapes=[pltpu.SemaphoreType.DMA((7,)), pltpu.SemaphoreType.DMA((7,)), pltpu.SemaphoreType.DMA((1,))],
    )(pack)


def _tie(*trees):
    return lax.optimization_barrier(trees)


class _ReduceScatter:
    def __init__(self, grads, dh):
        c = lax.axis_index("c")
        split = [g.reshape(4, 2, g.shape[1] // 2, g.shape[2]) for g in grads]
        keep = [lax.dynamic_index_in_dim(s, c, 1, keepdims=False) for s in split]
        give = [lax.dynamic_index_in_dim(s, 1 - c, 1, keepdims=False) for s in split]
        self.keep, give, self.dh = _tie(keep, give, dh)
        self.got = _comm_to_sibling(give)

    def scatter(self, dh):
        chip = 2 * lax.axis_index("x") + lax.axis_index("y")
        keep, got, dh = _tie(self.keep, list(self.got), dh)
        pair = [_add_n([a, b], "pair_sum") for a, b in zip(keep, got)]
        pair, dh = _tie(pair, dh)
        self.own = [lax.dynamic_index_in_dim(p, chip, 0, keepdims=False) for p in pair]
        self.arrived = _comm_scatter_to_owners(pair)
        return dh

    def gather(self, dh):
        own, arrived, dh = _tie(self.own, list(self.arrived), dh)
        halves = [_add_n([o, a[0], a[1], a[2]], "chip_sum") for o, a in zip(own, arrived)]
        self.halves, dh = _tie(halves, dh)
        self.other = _comm_to_sibling(self.halves)
        return dh

    def settle(self, dh):
        self.other, dh = _tie(list(self.other), dh)
        return dh

    def result(self):
        south = lax.axis_index("c") == 0
        return [jnp.where(south, jnp.concatenate([mine, theirs]), jnp.concatenate([theirs, mine]))
                for mine, theirs in zip(self.halves, self.other)]


def _ffn_fwd(x, g_pre, g_post, wgu, wd):
    S, D = x.shape
    F = wd.shape[1]
    xn = _norm_fwd(x, g_pre)
    gu, h = _ffn_up(xn, wgu)
    f = _mm(h, wd, "nn", ja="r", jb="r", tm=_row_tile(S, ACT_ROWS), tn=MODEL_COLS, tk=F, name="ffn_down")
    x_out = _resid_norm_fwd(x, f, g_post, 0.5)
    return x_out, (x, xn, gu, h, f)


def _ffn_bwd(dx, saved, g_pre, g_post, wgu, wd, mid=None):
    x, xn, gu, h, f = saved
    S, D = x.shape
    J, F, _ = wd.shape
    tm, tk = _row_tile(S, ACT_ROWS), _row_tile(S, TOKEN_ROWS)
    df, dg_post = _norm_bwd(f, g_post, dx, 0.5, out_dtype=BF16)
    d_wd = _mm(h, df, "tn", ja="b", tm=F, tn=MODEL_COLS, tk=tk, name="ffn_dwd")
    dgu = _ffn_bwd_hidden(df, wd, gu)
    if mid is not None:
        dgu = mid(dgu)
    d_wgu = _mm(xn, dgu, "tn", jb="b", tm=MODEL_COLS // 2, tn=F, tk=tk, name="ffn_dwgu")
    dxn = _mm(dgu, wgu, "nt", ja="r", jb="r", tm=tm, tn=MODEL_COLS, tk=2 * F, name="ffn_dxn")
    dx_in, dg_pre = _norm_bwd(x, g_pre, dxn, 1.0, resid=dx)
    return dx_in, dg_pre, dg_post, d_wgu, d_wd


def _gates_layout(proj, b_gate, nc):
    gt = jnp.transpose(proj[:, OFF_G:OFF_G + N_GATES]).reshape(N_GATES * nc, LANES)
    bias = jnp.repeat(b_gate, nc).reshape(N_GATES * nc, 1)
    return gt, bias


def _mixer_fwd(x, p, cosf, sinf):
    S, D = x.shape
    nc = S // BLK
    hn = _norm_fwd(x, p["g_pre"])
    proj = _mm(hn, p["w_in"], "nn", tm=_row_tile(S, ACT_ROWS), tn=IN_PAD // 5, tk=D, name="in_proj")
    qr, kr, va = _rope_fwd(proj, cosf, sinf)
    y_att = _attn_fwd(qr, kr, va, p["sink"])
    xpad = _pad_rows(proj[:, OFF_QM:OFF_VM])
    qk = _conv5(xpad, p["conv_w"], act=True, out_dtype=BF16, name="conv_silu")
    gt, bias = _gates_layout(proj, p["b_gate"], nc)
    ig, cum = _gate_prep(gt, bias)
    ig3, b3 = ig.reshape(-1, 1, LANES), cum.reshape(-1, 1, LANES)
    hs, cs, ns, ms = _mlstm_fwd(qk, proj, ig3, b3)
    y_m = _mlstm_out_fwd(hs, proj, p["gamma"])
    ycat = jnp.concatenate([y_att, y_m], axis=-1)
    mix = _mm(ycat, p["w_out"], "nn", tm=_row_tile(S, ACT_ROWS), tn=MODEL_COLS, tk=D, name="out_proj")
    x_out = _resid_norm_fwd(x, mix, p["g_post"], 1.0)
    return x_out, (x, hn, proj, qr, kr, va, xpad, qk, gt, bias, ig3, b3, hs, cs, ns, ms, ycat, mix)


def _mixer_bwd(dx, saved, p, cosf, sinf):
    x, hn, proj, qr, kr, va, xpad, qk, gt, bias, ig3, b3, hs, cs, ns, ms, ycat, mix = saved
    S, D = x.shape
    nc = S // BLK
    tm, tk = _row_tile(S, ACT_ROWS), _row_tile(S, TOKEN_ROWS)
    dmix, dg_post = _norm_bwd(mix, p["g_post"], dx, 1.0, out_dtype=BF16)
    d_wout = _mm(ycat, dmix, "tn", tm=MODEL_COLS, tn=MODEL_COLS, tk=tk, name="dw_out")
    dycat = _mm(dmix, p["w_out"], "nt", tm=tm, tn=MODEL_COLS, tk=D, name="d_ycat")
    dhm, d_om, d_gamma = _mlstm_out_bwd(hs, proj, p["gamma"], dycat)
    dq2, dk2, dv2, di, dfc, dbl = _mlstm_bwd(qk, proj, ig3, b3, cs, ns, ms, dhm)
    d_vm = _sum_cast(dv2, "dv_sum")
    dgt, db = _gate_bwd(gt, bias, di.reshape(-1, LANES), dfc.reshape(-1, LANES), dbl.reshape(-1, LANES), nc)
    dpre, d_conv = _conv_bwd_pre(xpad, p["conv_w"], dq2, dk2)
    d_qkm = _conv5(_pad_rows(dpre), p["conv_w"][::-1], act=False, out_dtype=BF16, name="conv_bwd_x")
    d_gates = jnp.transpose(dgt.reshape(N_GATES, S)).astype(BF16)
    d_gates = jnp.pad(d_gates, ((0, 0), (0, IN_PAD - IN_WIDTH)))
    dqr, dkr, dva, d_sink = _attn_bwd(qr, kr, va, p["sink"], dycat)
    d_att = _rope_bwd(dqr, dkr, dva, cosf, sinf)
    dproj = jnp.concatenate([d_att, d_qkm, d_vm, d_om, d_gates], axis=-1)
    d_win = _mm(hn, dproj, "tn", tm=MODEL_COLS, tn=IN_PAD // 5, tk=tk, name="dw_in")
    dhn = _mm(dproj, p["w_in"], "nt", tm=tm, tn=MODEL_COLS, tk=IN_PAD // 5, name="d_hn")
    dx_in, dg_pre = _norm_bwd(x, p["g_pre"], dhn, 1.0, resid=dx)
    small = dict(g_pre=dg_pre, g_post=dg_post, gamma=d_gamma, conv=d_conv, b_gate=db[:, 0], sink=d_sink)
    return dx_in, small, d_win, d_wout


def _pad_lanes(v):
    v = v.reshape(-1)
    return jnp.pad(v, (0, (-v.shape[0]) % LANES))


def kernel(x, ffn1_norm_pre, ffn1_norm_post, ffn1_w_gate, ffn1_w_up, ffn1_w_down, mix_norm_pre, mix_norm_post, w_in, b_gate, conv_w, attn_sink, mlstm_norm, w_out, ffn2_norm_pre, ffn2_norm_post, ffn2_w_gate, ffn2_w_up, ffn2_w_down, loss_target, m_ffn1_norm_pre, m_ffn1_norm_post, m_ffn1_w_gate, m_ffn1_w_up, m_ffn1_w_down, m_mix_norm_pre, m_mix_norm_post, m_w_in, m_b_gate, m_conv_w, m_attn_sink, m_mlstm_norm, m_w_out, m_ffn2_norm_pre, m_ffn2_norm_post, m_ffn2_w_gate, m_ffn2_w_up, m_ffn2_w_down, v_ffn1_norm_pre, v_ffn1_norm_post, v_ffn1_w_gate, v_ffn1_w_up, v_ffn1_w_down, v_mix_norm_pre, v_mix_norm_post, v_w_in, v_b_gate, v_conv_w, v_attn_sink, v_mlstm_norm, v_w_out, v_ffn2_norm_pre, v_ffn2_norm_post, v_ffn2_w_gate, v_ffn2_w_up, v_ffn2_w_down):
    names = ["ffn1_norm_pre", "ffn1_norm_post", "ffn1_w_gate", "ffn1_w_up", "ffn1_w_down", "mix_norm_pre",
             "mix_norm_post", "w_in", "b_gate", "conv_w", "attn_sink", "mlstm_norm", "w_out", "ffn2_norm_pre",
             "ffn2_norm_post", "ffn2_w_gate", "ffn2_w_up", "ffn2_w_down"]
    w = dict(zip(names, [ffn1_norm_pre, ffn1_norm_post, ffn1_w_gate, ffn1_w_up, ffn1_w_down, mix_norm_pre,
                         mix_norm_post, w_in, b_gate, conv_w, attn_sink, mlstm_norm, w_out, ffn2_norm_pre,
                         ffn2_norm_post, ffn2_w_gate, ffn2_w_up, ffn2_w_down]))
    mom_m = dict(zip(names, [m_ffn1_norm_pre, m_ffn1_norm_post, m_ffn1_w_gate, m_ffn1_w_up, m_ffn1_w_down,
                             m_mix_norm_pre, m_mix_norm_post, m_w_in, m_b_gate, m_conv_w, m_attn_sink,
                             m_mlstm_norm, m_w_out, m_ffn2_norm_pre, m_ffn2_norm_post, m_ffn2_w_gate,
                             m_ffn2_w_up, m_ffn2_w_down]))
    mom_v = dict(zip(names, [v_ffn1_norm_pre, v_ffn1_norm_post, v_ffn1_w_gate, v_ffn1_w_up, v_ffn1_w_down,
                             v_mix_norm_pre, v_mix_norm_post, v_w_in, v_b_gate, v_conv_w, v_attn_sink,
                             v_mlstm_norm, v_w_out, v_ffn2_norm_pre, v_ffn2_norm_post, v_ffn2_w_gate,
                             v_ffn2_w_up, v_ffn2_w_down]))
    xs = x[0]
    target = loss_target[0]
    S, D = xs.shape
    depth = w_in.shape[0]
    F = ffn1_w_gate.shape[-1]
    in_shard = w_in.shape[-1]
    conv_shard = conv_w.shape[-1]
    chip = 2 * lax.axis_index("x") + lax.axis_index("y")
    cosf, sinf = _rope_tables(S)

    gathered = []
    for l in range(depth):
        first = [jnp.concatenate([w["ffn1_w_gate"][l], w["ffn1_w_up"][l]], axis=-1).astype(BF16),
                 w["ffn1_w_down"][l].astype(BF16)]
        rest = [w["w_in"][l].astype(BF16), w["w_out"][l].astype(BF16),
                jnp.concatenate([w["ffn2_w_gate"][l], w["ffn2_w_up"][l]], axis=-1).astype(BF16),
                w["ffn2_w_down"][l].astype(BF16),
                jnp.pad(w["conv_w"][l], ((0, 8 - CONV_WIDTH), (0, 0)))]
        gathered.append((list(_comm_weight_gather(first)), list(_comm_weight_gather(rest))))

    def mixer_weights(l, win, wout, convg):
        win_full = jnp.concatenate([win[j] for j in range(4)], axis=-1)
        conv_full = jnp.concatenate([convg[j] for j in range(4)], axis=-1)[:CONV_WIDTH]
        return dict(w_in=jnp.pad(win_full, ((0, 0), (0, IN_PAD - IN_WIDTH))), w_out=wout.reshape(D, D),
                    conv_w=conv_full, g_pre=w["mix_norm_pre"][l], g_post=w["mix_norm_post"][l],
                    b_gate=w["b_gate"][l], sink=w["attn_sink"][l], gamma=w["mlstm_norm"][l])

    h = xs
    saved = []
    layers = []
    for l in range(depth):
        (wgu1, wd1), h = _tie(gathered[l][0], h)
        h, s1 = _ffn_fwd(h, w["ffn1_norm_pre"][l], w["ffn1_norm_post"][l], wgu1, wd1)
        (win, wout, wgu2, wd2, convg), h = _tie(gathered[l][1], h)
        p = dict(wgu1=wgu1, wd1=wd1, wgu2=wgu2, wd2=wd2, mix=mixer_weights(l, win, wout, convg))
        layers.append(p)
        h, s2 = _mixer_fwd(h, p["mix"], cosf, sinf)
        h, s3 = _ffn_fwd(h, w["ffn2_norm_pre"][l], w["ffn2_norm_post"][l], p["wgu2"], p["wd2"])
        saved.append((s1, s2, s3))
    dh, loss_tile = _loss_grad(h, target)

    big = {}
    small_rows = []
    rs_a = rs_b = None
    for l in reversed(range(depth)):
        p = layers[l]
        s1, s2, s3 = saved[l]
        old_a, old_b = rs_a, rs_b
        dh, dg2_pre, dg2_post, d_wgu2, d_wd2 = _ffn_bwd(dh, s3, w["ffn2_norm_pre"][l], w["ffn2_norm_post"][l],
                                                       p["wgu2"], p["wd2"], mid=old_b.scatter if old_b else None)
        if old_a is not None:
            dh = old_a.gather(dh)
        dh, sm, d_win, d_wout = _mixer_bwd(dh, s2, p["mix"], cosf, sinf)
        if old_b is not None:
            dh = old_b.gather(dh)
        d_win4 = jnp.stack([d_win[:, j * in_shard:(j + 1) * in_shard] for j in range(4)])
        rs_a = _ReduceScatter([d_win4, d_wout.reshape(4, D // 4, D), d_wgu2, d_wd2], dh)
        dh, dg1_pre, dg1_post, d_wgu1, d_wd1 = _ffn_bwd(rs_a.dh, s1, w["ffn1_norm_pre"][l], w["ffn1_norm_post"][l],
                                                       p["wgu1"], p["wd1"], mid=rs_a.scatter)
        if old_a is not None:
            dh = old_b.settle(old_a.settle(dh))
        rs_b = _ReduceScatter([d_wgu1, d_wd1], dh)
        dh = rs_b.dh
        big[l] = (rs_a, rs_b)
        small_rows.append((l, [dg1_pre, dg1_post, sm["g_pre"], sm["g_post"], dg2_pre, dg2_post, sm["gamma"],
                               sm["conv"], sm["b_gate"], sm["sink"]]))
    dh = rs_b.gather(rs_b.scatter(rs_a.gather(dh)))
    big = {l: b.result() + a.result() for l, (a, b) in big.items()}

    small_rows.sort(key=lambda t: t[0])
    flat = [_pad_lanes(v) for _, vs in small_rows for v in vs] + [loss_tile[0]]
    sizes = [f.shape[0] for f in flat]
    pack = jnp.concatenate(flat)
    pack = jnp.pad(pack, (0, (-pack.shape[0]) % (8 * LANES))).reshape(-1, LANES)
    every = _comm_all_to_all_small(pack)
    total = _add_n([every[i] for i in range(8)], "small_sum").reshape(-1)
    pieces, off = [], 0
    for n in sizes:
        pieces.append(total[off:off + n])
        off += n
    loss = pieces[-1][0]
    per_layer = [pieces[10 * l:10 * l + 10] for l in range(depth)]

    def stack_small(i, shape):
        n = 1
        for s in shape:
            n *= s
        return jnp.stack([per_layer[l][i][:n].reshape(shape) for l in range(depth)])

    conv_full_grad = stack_small(7, (CONV_WIDTH, 4 * conv_shard))
    grads = {
        "ffn1_norm_pre": stack_small(0, (D,)), "ffn1_norm_post": stack_small(1, (D,)),
        "mix_norm_pre": stack_small(2, (D,)), "mix_norm_post": stack_small(3, (D,)),
        "ffn2_norm_pre": stack_small(4, (D,)), "ffn2_norm_post": stack_small(5, (D,)),
        "mlstm_norm": stack_small(6, (M_WIDTH,)),
        "conv_w": lax.dynamic_slice_in_dim(conv_full_grad, chip * conv_shard, conv_shard, 2),
        "b_gate": stack_small(8, (N_GATES,)), "attn_sink": stack_small(9, (ATT_HEADS,)),
    }
    gu1 = jnp.stack([big[l][0] for l in range(depth)])
    gu2 = jnp.stack([big[l][4] for l in range(depth)])
    grads["ffn1_w_gate"], grads["ffn1_w_up"] = gu1[:, :, :F], gu1[:, :, F:]
    grads["ffn2_w_gate"], grads["ffn2_w_up"] = gu2[:, :, :F], gu2[:, :, F:]
    grads["ffn1_w_down"] = jnp.stack([big[l][1] for l in range(depth)])
    grads["ffn2_w_down"] = jnp.stack([big[l][5] for l in range(depth)])
    grads["w_in"] = jnp.stack([big[l][2] for l in range(depth)])
    grads["w_out"] = jnp.stack([big[l][3] for l in range(depth)])

    deltas, new_m, new_v = {}, {}, {}
    for n in names:
        deltas[n], new_m[n], new_v[n] = _adamw(w[n], grads[n], mom_m[n], mom_v[n])
    grad_x = dh[None]
    return (loss, grad_x, *[grads[n] for n in names], *[deltas[n] for n in names],
            *[new_m[n] for n in names], *[new_v[n] for n in names])
```

```python
import jax
import jax.numpy as jnp
from jax import lax
from jax.experimental import pallas as pl
from jax.experimental.pallas import tpu as pltpu
from jax.experimental.pallas import tpu_sc as plsc

F32 = jnp.float32
BF16 = jnp.bfloat16
MESH = pl.DeviceIdType.MESH
ANY = pl.BlockSpec(memory_space=pl.ANY)

VMEM_LIMIT_BYTES = 56 * 1024 * 1024
LANES = 128

EPS = 1e-6
ATT_HEADS = 8
ATT_KV_HEADS = 2
ATT_GROUP = ATT_HEADS // ATT_KV_HEADS
ATT_HEAD_DIM = 128
ATT_WIDTH = ATT_HEADS * ATT_HEAD_DIM
KV_WIDTH = ATT_KV_HEADS * ATT_HEAD_DIM
BLK = 128
M_HEADS = 4
M_HEAD_DIM = 256
M_WIDTH = M_HEADS * M_HEAD_DIM
CONV_WIDTH = 5
CONV_HALO = 8
ROPE_THETA = 10000.0
N_GATES = 4 * M_HEADS
OFF_QA, OFF_KA, OFF_VA = 0, ATT_WIDTH, ATT_WIDTH + KV_WIDTH
OFF_QM = ATT_WIDTH + 2 * KV_WIDTH
OFF_KM = OFF_QM + M_WIDTH
OFF_VM = OFF_KM + M_WIDTH
OFF_OM = OFF_VM + M_WIDTH
OFF_G = OFF_OM + M_WIDTH
IN_WIDTH = OFF_G + N_GATES
IN_PAD = OFF_G + LANES
NEG_BIG = -1e30

ADAM_LR, ADAM_B1, ADAM_B2, ADAM_EPS, ADAM_WD, ADAM_STEP = 0.001, 0.9, 0.999, 1e-08, 0.01, 10


def _params(**kw):
    return pltpu.CompilerParams(vmem_limit_bytes=VMEM_LIMIT_BYTES, **kw)


def _dot(a, b):
    return lax.dot_general(a, b, (((1,), (0,)), ((), ())), preferred_element_type=F32)


def _dot_nt(a, b):
    return lax.dot_general(a, b, (((1,), (1,)), ((), ())), preferred_element_type=F32)


def _dot_tn(a, b):
    return lax.dot_general(a, b, (((0,), (0,)), ((), ())), preferred_element_type=F32)


def _sigmoid(x):
    return 1.0 / (1.0 + jnp.exp(-x))


def _eye_mask(n):
    r = lax.broadcasted_iota(jnp.int32, (n, n), 0)
    c = lax.broadcasted_iota(jnp.int32, (n, n), 1)
    return r == c


def _row_to_col(row, eye):
    n = eye.shape[0]
    return jnp.sum(jnp.where(eye, jnp.broadcast_to(row, (n, n)), 0.0), axis=1, keepdims=True)


def _col_to_row(col, eye):
    n = eye.shape[0]
    return jnp.sum(jnp.where(eye, jnp.broadcast_to(col, (n, n)), 0.0), axis=0, keepdims=True)


def _mm(a, b, kind, *, tm, tn, tk, name, out_dtype=F32, ja=None, jb=None):
    a2, b2 = a.shape[-2:], b.shape[-2:]
    if kind == "nn":
        (M, K), (_, N) = a2, b2
    elif kind == "nt":
        (M, K), (N, _) = a2, b2
    else:
        (K, M), (_, N) = a2, b2
    J = a.shape[0] if ja else (b.shape[0] if jb else 1)
    batch = "b" in (ja, jb)
    red = "r" in (ja, jb)
    nk = K // tk
    nr = nk * (J if red else 1)
    grid = ((J if batch else 1), M // tm, N // tn, nr)

    def lead(mode, g, r):
        return g if mode == "b" else r // nk

    def a_map(g, i, n, r):
        kk = r % nk
        idx = (i, kk) if kind != "tn" else (kk, i)
        return idx if ja is None else (lead(ja, g, r),) + idx

    def b_map(g, i, n, r):
        kk = r % nk
        idx = (kk, n) if kind != "nt" else (n, kk)
        return idx if jb is None else (lead(jb, g, r),) + idx

    def o_map(g, i, n, r):
        return (g, i, n) if batch else (i, n)

    a_blk = (tm, tk) if kind != "tn" else (tk, tm)
    b_blk = (tk, tn) if kind != "nt" else (tn, tk)
    dot = {"nn": _dot, "nt": _dot_nt, "tn": _dot_tn}[kind]

    def body(a_ref, b_ref, o_ref, *scratch):
        part = dot(a_ref[...], b_ref[...])
        if nr == 1:
            o_ref[...] = part.astype(out_dtype)
        else:
            acc = scratch[0]
            r = pl.program_id(3)

            @pl.when(r == 0)
            def _():
                acc[...] = part

            @pl.when(r > 0)
            def _():
                acc[...] += part

            @pl.when(r == nr - 1)
            def _():
                o_ref[...] = acc[...].astype(out_dtype)

    return pl.pallas_call(
        body, name=name, grid=grid,
        in_specs=[pl.BlockSpec(a_blk if ja is None else (None,) + a_blk, a_map),
                  pl.BlockSpec(b_blk if jb is None else (None,) + b_blk, b_map)],
        out_specs=pl.BlockSpec((None, tm, tn) if batch else (tm, tn), o_map),
        out_shape=jax.ShapeDtypeStruct((J, M, N) if batch else (M, N), out_dtype),
        scratch_shapes=[pltpu.VMEM((tm, tn), F32)] if nr > 1 else [],
        compiler_params=_params(),
    )(a, b)


def _row_tile(S, want):
    return min(S, want)


ELEMENTWISE_TILE_BYTES = 1 << 20
ACT_ROWS = 1024
TOKEN_ROWS = 2048
MODEL_COLS = 1024


def _elementwise_rows(R, C):
    for cand in (1024, 512, 256, 128, 64, 32, 16, 8):
        if R % cand == 0 and R > cand and cand * C * 4 <= ELEMENTWISE_TILE_BYTES:
            return cand
    return R if R * C * 4 <= ELEMENTWISE_TILE_BYTES or R % 8 else 8


def _ffn_up(xn, wgu):
    S, D = xn.shape
    J, _, F2 = wgu.shape
    F = F2 // 2
    tm = _row_tile(S, 256)

    def body(x_ref, w_ref, gu_ref, h_ref):
        gu = _dot(x_ref[...], w_ref[...])
        g, u = gu[:, :F], gu[:, F:]
        gu_ref[...] = gu.astype(BF16)
        h_ref[...] = (g * _sigmoid(g) * u).astype(BF16)

    return pl.pallas_call(
        body, name="ffn_up", grid=(J, S // tm),
        in_specs=[pl.BlockSpec((tm, D), lambda j, i: (i, 0)),
                  pl.BlockSpec((None, D, F2), lambda j, i: (j, 0, 0))],
        out_specs=[pl.BlockSpec((None, tm, F2), lambda j, i: (j, i, 0)),
                   pl.BlockSpec((None, tm, F), lambda j, i: (j, i, 0))],
        out_shape=[jax.ShapeDtypeStruct((J, S, F2), BF16), jax.ShapeDtypeStruct((J, S, F), BF16)],
        compiler_params=_params(),
    )(xn, wgu)


def _ffn_bwd_hidden(df, wd, gu):
    S, D = df.shape
    J, F, _ = wd.shape
    F2 = 2 * F
    tm = _row_tile(S, 256)

    def body(df_ref, w_ref, gu_ref, o_ref):
        dh = _dot_nt(df_ref[...], w_ref[...])
        g = gu_ref[:, :F].astype(F32)
        u = gu_ref[:, F:].astype(F32)
        sg = _sigmoid(g)
        o_ref[:, :F] = (dh * u * (sg * (1.0 + g * (1.0 - sg)))).astype(BF16)
        o_ref[:, F:] = (dh * (g * sg)).astype(BF16)

    return pl.pallas_call(
        body, name="ffn_bwd_hidden", grid=(J, S // tm),
        in_specs=[pl.BlockSpec((tm, D), lambda j, i: (i, 0)),
                  pl.BlockSpec((None, F, D), lambda j, i: (j, 0, 0)),
                  pl.BlockSpec((None, tm, F2), lambda j, i: (j, i, 0))],
        out_specs=pl.BlockSpec((None, tm, F2), lambda j, i: (j, i, 0)),
        out_shape=jax.ShapeDtypeStruct((J, S, F2), BF16),
        compiler_params=_params(),
    )(df, wd, gu)


def _norm_fwd(x, g):
    S, D = x.shape
    tm = _row_tile(S, 512)

    def body(x_ref, g_ref, o_ref):
        xv = x_ref[...]
        r = lax.rsqrt(jnp.mean(xv * xv, axis=-1, keepdims=True) + EPS)
        o_ref[...] = (xv * r * g_ref[...]).astype(BF16)

    return pl.pallas_call(
        body, name="norm_fwd", grid=(S // tm,),
        in_specs=[pl.BlockSpec((tm, D), lambda i: (i, 0)), pl.BlockSpec((1, D), lambda i: (0, 0))],
        out_specs=pl.BlockSpec((tm, D), lambda i: (i, 0)),
        out_shape=jax.ShapeDtypeStruct((S, D), BF16),
        compiler_params=_params(),
    )(x, g.reshape(1, D))


def _resid_norm_fwd(x, f, g, alpha):
    S, D = x.shape
    tm = _row_tile(S, 512)

    def body(x_ref, f_ref, g_ref, o_ref):
        fv = f_ref[...]
        r = lax.rsqrt(jnp.mean(fv * fv, axis=-1, keepdims=True) + EPS)
        o_ref[...] = x_ref[...] + alpha * (fv * r * g_ref[...])

    return pl.pallas_call(
        body, name="resid_norm_fwd", grid=(S // tm,),
        in_specs=[pl.BlockSpec((tm, D), lambda i: (i, 0)), pl.BlockSpec((tm, D), lambda i: (i, 0)),
                  pl.BlockSpec((1, D), lambda i: (0, 0))],
        out_specs=pl.BlockSpec((tm, D), lambda i: (i, 0)),
        out_shape=jax.ShapeDtypeStruct((S, D), F32),
        compiler_params=_params(),
    )(x, f, g.reshape(1, D))


def _norm_bwd(x, g, dy, alpha, resid=None, out_dtype=F32):
    S, D = x.shape
    tm = _row_tile(S, 256)
    has_resid = resid is not None

    def body(*refs):
        x_ref, g_ref, dy_ref = refs[:3]
        res_ref = refs[3] if has_resid else None
        dx_ref, dg_ref = refs[-2:]
        xv = x_ref[...]
        r = lax.rsqrt(jnp.mean(xv * xv, axis=-1, keepdims=True) + EPS)
        xh = xv * r
        dyv = dy_ref[...].astype(F32) * alpha
        gdy = dyv * g_ref[...]
        dx = r * (gdy - xh * jnp.mean(xh * gdy, axis=-1, keepdims=True))
        if has_resid:
            dx = dx + res_ref[...]
        dx_ref[...] = dx.astype(out_dtype)
        part = jnp.sum(dyv * xh, axis=0, keepdims=True)

        @pl.when(pl.program_id(0) == 0)
        def _():
            dg_ref[...] = part

        @pl.when(pl.program_id(0) > 0)
        def _():
            dg_ref[...] += part

    row = pl.BlockSpec((tm, D), lambda i: (i, 0))
    vec = pl.BlockSpec((1, D), lambda i: (0, 0))
    ins = [x, g.reshape(1, D), dy] + ([resid] if has_resid else [])
    dx, dg = pl.pallas_call(
        body, name="norm_bwd_res" if has_resid else "norm_bwd", grid=(S // tm,),
        in_specs=[row, vec, row] + ([row] if has_resid else []),
        out_specs=[row, vec],
        out_shape=[jax.ShapeDtypeStruct((S, D), out_dtype), jax.ShapeDtypeStruct((1, D), F32)],
        compiler_params=_params(),
    )(*ins)
    return dx, dg.reshape(D)


def _loss_grad(y, target):
    S, D = y.shape
    tm = _row_tile(S, 512)

    def body(y_ref, t_ref, dy_ref, l_ref):
        err = y_ref[...] - t_ref[...]
        dy_ref[...] = err * (1.0 / D)
        part = jnp.sum(jnp.sum(err * err, axis=-1, keepdims=True) * (0.5 / D), axis=0, keepdims=True)
        part = jnp.broadcast_to(part, (8, LANES))

        @pl.when(pl.program_id(0) == 0)
        def _():
            l_ref[...] = part

        @pl.when(pl.program_id(0) > 0)
        def _():
            l_ref[...] += part

    row = pl.BlockSpec((tm, D), lambda i: (i, 0))
    return pl.pallas_call(
        body, name="loss_grad", grid=(S // tm,),
        in_specs=[row, row],
        out_specs=[row, pl.BlockSpec((8, LANES), lambda i: (0, 0))],
        out_shape=[jax.ShapeDtypeStruct((S, D), F32), jax.ShapeDtypeStruct((8, LANES), F32)],
        compiler_params=_params(),
    )(y, target)


def _add_n(xs, name):
    shape = xs[0].shape
    C = shape[-1]
    R = 1
    for s in shape[:-1]:
        R *= s
    tm = _elementwise_rows(R, C)

    def body(*refs):
        acc = refs[0][...]
        for r in refs[1:-1]:
            acc = acc + r[...]
        refs[-1][...] = acc

    row = pl.BlockSpec((tm, C), lambda i: (i, 0))
    out = pl.pallas_call(
        body, name=name, grid=(R // tm,),
        in_specs=[row] * len(xs), out_specs=row,
        out_shape=jax.ShapeDtypeStruct((R, C), F32),
        compiler_params=_params(),
    )(*[x.reshape(R, C) for x in xs])
    return out.reshape(shape)


def _pair_sum(split, got):
    J, _, r, C = split.shape
    tm = _elementwise_rows(r, C)
    core = lax.axis_index("c").astype(jnp.int32).reshape(1)

    def body(core_ref, a_ref, b_ref, o_ref):
        o_ref[...] = a_ref[...] + b_ref[...]

    row = pl.BlockSpec((None, tm, C), lambda j, i, core_ref: (j, i, 0))
    return pl.pallas_call(
        body, name="pair_sum",
        grid_spec=pltpu.PrefetchScalarGridSpec(
            num_scalar_prefetch=1, grid=(J, r // tm),
            in_specs=[pl.BlockSpec((None, None, tm, C), lambda j, i, core_ref: (j, core_ref[0], i, 0)), row],
            out_specs=row),
        out_shape=jax.ShapeDtypeStruct((J, r, C), F32),
        compiler_params=_params(),
    )(core, split, got)


def _rope_tables(S):
    half = ATT_HEAD_DIM // 2
    inv_freq = ROPE_THETA ** (-jnp.arange(half, dtype=F32) / half)
    ang = jnp.arange(S, dtype=F32)[:, None] * inv_freq[None, :]
    cos, sin = jnp.cos(ang), jnp.sin(ang)
    return jnp.concatenate([cos, cos], axis=-1), jnp.concatenate([-sin, sin], axis=-1)


def _rotate(x, cosf, sinf):
    return x * cosf + pltpu.roll(x, ATT_HEAD_DIM // 2, 1) * sinf


def _rope_fwd(proj, cosf, sinf):
    S = proj.shape[0]
    tm = _row_tile(S, 512)

    def body(q_ref, k_ref, v_ref, c_ref, s_ref, qo_ref, ko_ref, vo_ref):
        c, s = c_ref[...], s_ref[...]
        for h in range(ATT_HEADS):
            sl = slice(h * ATT_HEAD_DIM, (h + 1) * ATT_HEAD_DIM)
            qo_ref[:, sl] = _rotate(q_ref[:, sl], c, s).astype(BF16)
        for h in range(ATT_KV_HEADS):
            sl = slice(h * ATT_HEAD_DIM, (h + 1) * ATT_HEAD_DIM)
            ko_ref[:, sl] = _rotate(k_ref[:, sl], c, s).astype(BF16)
        vo_ref[...] = v_ref[...].astype(BF16)

    tab = pl.BlockSpec((tm, ATT_HEAD_DIM), lambda i: (i, 0))
    return pl.pallas_call(
        body, name="rope_fwd", grid=(S // tm,),
        in_specs=[pl.BlockSpec((tm, ATT_WIDTH), lambda i: (i, 0)),
                  pl.BlockSpec((tm, KV_WIDTH), lambda i: (i, OFF_KA // KV_WIDTH)),
                  pl.BlockSpec((tm, KV_WIDTH), lambda i: (i, OFF_VA // KV_WIDTH)), tab, tab],
        out_specs=[pl.BlockSpec((tm, ATT_WIDTH), lambda i: (i, 0)),
                   pl.BlockSpec((tm, KV_WIDTH), lambda i: (i, 0)),
                   pl.BlockSpec((tm, KV_WIDTH), lambda i: (i, 0))],
        out_shape=[jax.ShapeDtypeStruct((S, ATT_WIDTH), BF16), jax.ShapeDtypeStruct((S, KV_WIDTH), BF16),
                   jax.ShapeDtypeStruct((S, KV_WIDTH), BF16)],
        compiler_params=_params(),
    )(proj, proj, proj, cosf, sinf)


def _rope_bwd(dq, dk, dv, cosf, sinf):
    S = dq.shape[0]
    tm = _row_tile(S, 512)
    W = ATT_WIDTH + 2 * KV_WIDTH

    def body(q_ref, k_ref, v_ref, c_ref, s_ref, o_ref):
        c, s = c_ref[...], -s_ref[...]
        for h in range(ATT_HEADS):
            sl = slice(h * ATT_HEAD_DIM, (h + 1) * ATT_HEAD_DIM)
            o_ref[:, sl] = _rotate(q_ref[:, sl], c, s).astype(BF16)
        for h in range(ATT_KV_HEADS):
            sl = slice(h * ATT_HEAD_DIM, (h + 1) * ATT_HEAD_DIM)
            o_ref[:, ATT_WIDTH + h * ATT_HEAD_DIM:ATT_WIDTH + (h + 1) * ATT_HEAD_DIM] = (
                _rotate(k_ref[:, sl], c, s).astype(BF16))
        o_ref[:, ATT_WIDTH + KV_WIDTH:] = v_ref[...].astype(BF16)

    tab = pl.BlockSpec((tm, ATT_HEAD_DIM), lambda i: (i, 0))
    return pl.pallas_call(
        body, name="rope_bwd", grid=(S // tm,),
        in_specs=[pl.BlockSpec((tm, ATT_WIDTH), lambda i: (i, 0)),
                  pl.BlockSpec((tm, KV_WIDTH), lambda i: (i, 0)),
                  pl.BlockSpec((tm, KV_WIDTH), lambda i: (i, 0)), tab, tab],
        out_specs=pl.BlockSpec((tm, W), lambda i: (i, 0)),
        out_shape=jax.ShapeDtypeStruct((S, W), BF16),
        compiler_params=_params(),
    )(dq, dk, dv, cosf, sinf)


def _attn_probs(q_ref, k_refs, sink_ref, n, nb):
    G, L = ATT_GROUP, BLK
    q4 = jnp.concatenate([q_ref[:, g * ATT_HEAD_DIM:(g + 1) * ATT_HEAD_DIM] for g in range(G)], axis=0)
    kcat = jnp.concatenate([r[...] for r in k_refs], axis=0)
    s = _dot_nt(q4, kcat) * (ATT_HEAD_DIM ** -0.5)
    row = lax.broadcasted_iota(jnp.int32, (G * L, 3 * L), 0) % L
    col = lax.broadcasted_iota(jnp.int32, (G * L, 3 * L), 1)
    kpos = (n - 1) * L + col
    mask = (jnp.abs(col - L - row) <= L) & (kpos >= 0) & (kpos < nb * L)
    s = jnp.where(mask, s, -jnp.inf)
    sink = jnp.concatenate([jnp.broadcast_to(sink_ref[:, g:g + 1], (L, 1)) for g in range(G)], axis=0)
    m = jnp.maximum(jnp.max(s, axis=-1, keepdims=True), sink)
    p = jnp.exp(s - m)
    es = jnp.exp(sink - m)
    inv = 1.0 / (jnp.sum(p, axis=-1, keepdims=True) + es)
    return q4, kcat, p * inv, es * inv


def _kv_specs(nb):
    w = ATT_HEAD_DIM
    return [pl.BlockSpec((BLK, w), lambda kh, n: (jnp.maximum(n - 1, 0), kh)),
            pl.BlockSpec((BLK, w), lambda kh, n: (n, kh)),
            pl.BlockSpec((BLK, w), lambda kh, n: (jnp.minimum(n + 1, nb - 1), kh))]


def _attn_fwd(qr, kr, va, sink):
    S = qr.shape[0]
    nb = S // BLK
    G = ATT_GROUP

    def body(q_ref, k0, k1, k2, v0, v1, v2, sink_ref, o_ref):
        n = pl.program_id(1)
        _, _, probs, _ = _attn_probs(q_ref, (k0, k1, k2), sink_ref, n, nb)
        vcat = jnp.concatenate([v0[...], v1[...], v2[...]], axis=0)
        out = _dot(probs.astype(BF16), vcat)
        for g in range(G):
            o_ref[:, g * ATT_HEAD_DIM:(g + 1) * ATT_HEAD_DIM] = out[g * BLK:(g + 1) * BLK, :].astype(BF16)

    qspec = pl.BlockSpec((BLK, G * ATT_HEAD_DIM), lambda kh, n: (n, kh))
    return pl.pallas_call(
        body, name="attn_fwd", grid=(ATT_KV_HEADS, nb),
        in_specs=[qspec] + _kv_specs(nb) + _kv_specs(nb) + [pl.BlockSpec((None, 1, G), lambda kh, n: (kh, 0, 0))],
        out_specs=qspec,
        out_shape=jax.ShapeDtypeStruct((S, ATT_WIDTH), BF16),
        compiler_params=_params(),
    )(qr, kr, kr, kr, va, va, va, sink.reshape(ATT_KV_HEADS, 1, G))


def _attn_bwd(qr, kr, va, sink, dycat):
    S = qr.shape[0]
    nb = S // BLK
    G, L, Dh = ATT_GROUP, BLK, ATT_HEAD_DIM
    SP = S + 2 * L

    def body(q_ref, k0, k1, k2, v0, v1, v2, sink_ref, do_ref, dq_ref, dk_ref, dv_ref, ds_ref):
        n = pl.program_id(1)

        @pl.when(n == 0)
        def _():
            dk_ref[...] = jnp.zeros_like(dk_ref)
            dv_ref[...] = jnp.zeros_like(dv_ref)
            ds_ref[...] = jnp.zeros_like(ds_ref)

        q4, kcat, probs, psink = _attn_probs(q_ref, (k0, k1, k2), sink_ref, n, nb)
        vcat = jnp.concatenate([v0[...], v1[...], v2[...]], axis=0)
        do4 = jnp.concatenate([do_ref[:, g * Dh:(g + 1) * Dh] for g in range(G)], axis=0)
        pb = probs.astype(BF16)
        dob = do4.astype(BF16)
        out = _dot(pb, vcat)
        delta = jnp.sum(do4 * out, axis=-1, keepdims=True)
        dp = _dot_nt(dob, vcat)
        dsc = (probs * (dp - delta) * (Dh ** -0.5)).astype(BF16)
        dq4 = _dot(dsc, kcat)
        for g in range(G):
            dq_ref[:, g * Dh:(g + 1) * Dh] = dq4[g * L:(g + 1) * L, :]
        rows = pl.ds(pl.multiple_of(n * L, L), 3 * L)
        dk_ref[rows, :] += _dot_tn(dsc, q4)
        dv_ref[rows, :] += _dot_tn(pb, dob)
        lane = lax.broadcasted_iota(jnp.int32, (8, LANES), 1)
        dsink = jnp.zeros((8, LANES), F32)
        for g in range(G):
            val = -jnp.sum(psink[g * L:(g + 1) * L] * delta[g * L:(g + 1) * L], axis=0, keepdims=True)
            dsink = dsink + jnp.where(lane == g, jnp.broadcast_to(val, (8, LANES)), 0.0)
        ds_ref[...] += dsink

    qspec = pl.BlockSpec((L, G * Dh), lambda kh, n: (n, kh))
    accspec = pl.BlockSpec((None, SP, Dh), lambda kh, n: (kh, 0, 0))
    dq, dkp, dvp, dsink = pl.pallas_call(
        body, name="attn_bwd", grid=(ATT_KV_HEADS, nb),
        in_specs=[qspec] + _kv_specs(nb) + _kv_specs(nb)
        + [pl.BlockSpec((None, 1, G), lambda kh, n: (kh, 0, 0)), qspec],
        out_specs=[qspec, accspec, accspec, pl.BlockSpec((None, 8, LANES), lambda kh, n: (kh, 0, 0))],
        out_shape=[jax.ShapeDtypeStruct((S, ATT_WIDTH), F32), jax.ShapeDtypeStruct((ATT_KV_HEADS, SP, Dh), F32),
                   jax.ShapeDtypeStruct((ATT_KV_HEADS, SP, Dh), F32),
                   jax.ShapeDtypeStruct((ATT_KV_HEADS, 8, LANES), F32)],
        compiler_params=_params(),
    )(qr, kr, kr, kr, va, va, va, sink.reshape(ATT_KV_HEADS, 1, G), dycat)
    unpad = lambda t: jnp.transpose(t[:, L:L + S, :], (1, 0, 2)).reshape(S, KV_WIDTH)
    return dq, unpad(dkp), unpad(dvp), dsink[:, 0, :G].reshape(ATT_HEADS)


CONV_COLS = 256
CONV_ROWS = 512


def _conv_taps(xs, w_ref, rows):
    total = xs.shape[0]
    acc = None
    for j in range(CONV_WIDTH):
        shift = (CONV_WIDTH // 2 - j) % total
        term = (pltpu.roll(xs, shift, 0) if shift else xs)[CONV_HALO:CONV_HALO + rows, :] * w_ref[j:j + 1, :]
        acc = term if acc is None else acc + term
    return acc


def _conv_window(x_ref, i, R, nrow):
    S = x_ref.shape[0]
    r0 = pl.multiple_of(i * R, R)
    top = x_ref[pl.ds(pl.multiple_of(jnp.maximum(r0 - CONV_HALO, 0), CONV_HALO), CONV_HALO), :]
    bot = x_ref[pl.ds(pl.multiple_of(jnp.minimum(r0 + R, S - CONV_HALO), CONV_HALO), CONV_HALO), :]
    top = jnp.where(i > 0, top, 0.0)
    bot = jnp.where(i < nrow - 1, bot, 0.0)
    return jnp.concatenate([top, x_ref[pl.ds(r0, R), :], bot], axis=0)


def _conv5(x, w, *, col0, act, out_dtype, name):
    S = x.shape[0]
    C = w.shape[1]
    R = _row_tile(S, CONV_ROWS)
    tc = CONV_COLS
    half_blocks = (C // 2) // tc
    nrow = S // R

    def body(x_ref, w_ref, o_ref):
        scale = jnp.where(pl.program_id(0) >= half_blocks, M_HEAD_DIM ** -0.5, 1.0)
        y = _conv_taps(_conv_window(x_ref, pl.program_id(1), R, nrow), w_ref, R)
        if act:
            y = y * _sigmoid(y) * scale
        o_ref[...] = y.astype(out_dtype)

    return pl.pallas_call(
        body, name=name, grid=(C // tc, nrow),
        in_specs=[pl.BlockSpec((S, tc), lambda c, i: (0, col0 // tc + c)),
                  pl.BlockSpec((CONV_WIDTH, tc), lambda c, i: (0, c))],
        out_specs=pl.BlockSpec((R, tc), lambda c, i: (i, c)),
        out_shape=jax.ShapeDtypeStruct((S, C), out_dtype),
        compiler_params=_params(),
    )(x, w)


def _conv_bwd_pre(x, col0, w, dq2, dk2):
    S = x.shape[0]
    C = w.shape[1]
    R = _row_tile(S, CONV_ROWS)
    tc = CONV_COLS
    half_blocks = (C // 2) // tc
    nrow = S // R

    def body(x_ref, w_ref, dqa_ref, dqb_ref, dka_ref, dkb_ref, o_ref, dw_ref, acc):
        is_k = pl.program_id(0) >= half_blocks
        i = pl.program_id(1)
        scale = jnp.where(is_k, M_HEAD_DIM ** -0.5, 1.0)

        @pl.when(i == 0)
        def _():
            acc[...] = jnp.zeros_like(acc)

        xs = _conv_window(x_ref, i, R, nrow)
        y = _conv_taps(xs, w_ref, R)
        sg = _sigmoid(y)
        dqv = dqa_ref[...] + dqb_ref[...]
        dkv = dka_ref[...] + dkb_ref[...]
        dpre = jnp.where(is_k, dkv, dqv) * scale * (sg * (1.0 + y * (1.0 - sg)))
        o_ref[...] = dpre
        total = xs.shape[0]
        for j in range(CONV_WIDTH):
            shift = (CONV_WIDTH // 2 - j) % total
            xj = (pltpu.roll(xs, shift, 0) if shift else xs)[CONV_HALO:CONV_HALO + R, :]
            acc[j:j + 1, :] += jnp.sum(dpre * xj, axis=0, keepdims=True)

        @pl.when(i == nrow - 1)
        def _():
            dw_ref[...] = acc[0:CONV_WIDTH, :]

    nqb = (C // 2) // tc
    qmap = lambda d: (lambda c, i: (d, i, jnp.minimum(c, nqb - 1)))
    kmap = lambda d: (lambda c, i: (d, i, jnp.maximum(c - nqb, 0)))
    gspec = lambda m: pl.BlockSpec((None, R, tc), m)
    return pl.pallas_call(
        body, name="conv_bwd_pre", grid=(C // tc, nrow),
        in_specs=[pl.BlockSpec((S, tc), lambda c, i: (0, col0 // tc + c)),
                  pl.BlockSpec((CONV_WIDTH, tc), lambda c, i: (0, c)),
                  gspec(qmap(0)), gspec(qmap(1)), gspec(kmap(0)), gspec(kmap(1))],
        out_specs=[pl.BlockSpec((R, tc), lambda c, i: (i, c)), pl.BlockSpec((CONV_WIDTH, tc), lambda c, i: (0, c))],
        out_shape=[jax.ShapeDtypeStruct((S, C), F32), jax.ShapeDtypeStruct((CONV_WIDTH, C), F32)],
        scratch_shapes=[pltpu.VMEM((8, tc), F32)],
        compiler_params=_params(),
    )(x, w, dq2, dq2, dk2, dk2)


def _lane_cumsum(x, reverse):
    lane = lax.broadcasted_iota(jnp.int32, x.shape, 1)
    sh = 1
    while sh < LANES:
        if reverse:
            x = x + jnp.where(lane < LANES - sh, pltpu.roll(x, LANES - sh, 1), 0.0)
        else:
            x = x + jnp.where(lane >= sh, pltpu.roll(x, sh, 1), 0.0)
        sh *= 2
    return x


def _gate_prep(gates_t, bias):
    R = gates_t.shape[0]
    half, quarter = R // 2, R // 4

    def body(g_ref, b_ref, ig_ref, cum_ref):
        ig_ref[...] = g_ref[0:half, :] + b_ref[0:half, :]
        fg = g_ref[half:R, :] + b_ref[half:R, :]
        lf = jnp.minimum(fg, 0.0) - jnp.log(1.0 + jnp.exp(-jnp.abs(fg)))
        cum_ref[0:quarter, :] = _lane_cumsum(lf[0:quarter, :], False)
        cum_ref[quarter:half, :] = _lane_cumsum(lf[quarter:half, :], True)

    return pl.pallas_call(
        body, name="gate_prep",
        out_shape=[jax.ShapeDtypeStruct((half, LANES), F32), jax.ShapeDtypeStruct((half, LANES), F32)],
        compiler_params=_params(),
    )(gates_t, bias)


def _gate_bwd(gates_t, bias, di, dfc, dbl, nc):
    R = gates_t.shape[0]
    half, quarter = R // 2, R // 4

    def body(g_ref, b_ref, di_ref, df_ref, dbl_ref, dg_ref, db_ref):
        dg_ref[0:half, :] = di_ref[...]
        dfv = df_ref[...]
        within = jnp.concatenate([_lane_cumsum(dfv[0:quarter, :], True),
                                  _lane_cumsum(dfv[quarter:half, :], False)], axis=0)
        fg = g_ref[half:R, :] + b_ref[half:R, :]
        dg_ref[half:R, :] = (within + dbl_ref[...]) * _sigmoid(-fg)
        rows = jnp.broadcast_to(jnp.sum(dg_ref[...], axis=-1, keepdims=True), (R, LANES))
        gr = lax.broadcasted_iota(jnp.int32, (N_GATES, R), 0)
        gc = lax.broadcasted_iota(jnp.int32, (N_GATES, R), 1)
        db_ref[...] = lax.dot_general((gc // nc == gr).astype(F32), rows, (((1,), (0,)), ((), ())),
                                      precision=lax.Precision.HIGHEST, preferred_element_type=F32)

    return pl.pallas_call(
        body, name="gate_bwd",
        out_shape=[jax.ShapeDtypeStruct((R, LANES), F32), jax.ShapeDtypeStruct((N_GATES, LANES), F32)],
        compiler_params=_params(),
    )(gates_t, bias, di, dfc, dbl)


def _chunk_index(d, c, nc, reverse):
    j = (nc - 1 - c) if reverse else c
    return j + d * (nc - 1 - 2 * j)


def _mlstm_chunk(d, q, k, vb, brow, igrow, C, nvec, m_prev, eye):
    L = BLK
    r = lax.broadcasted_iota(jnp.int32, (L, L), 0)
    c = lax.broadcasted_iota(jnp.int32, (L, L), 1)
    mask = (r - c) * (1 - 2 * d) >= 0
    bcol = _row_to_col(brow, eye)
    igcol = _row_to_col(igrow, eye)
    log_d = jnp.where(mask, bcol - brow + igrow, -jnp.inf)
    log_inter = bcol + m_prev
    m_t = jnp.maximum(log_inter, jnp.max(log_d, axis=-1, keepdims=True))
    d_mat = jnp.exp(log_d - m_t)
    inter = jnp.exp(log_inter - m_t)
    s = _dot_nt(q, k) * d_mat
    sb = s.astype(BF16)
    cb = C.astype(BF16)
    num = _dot(sb, vb) + inter * _dot_nt(q, cb)
    den = jnp.sum(s, axis=-1, keepdims=True) + inter * jnp.sum(q.astype(F32) * nvec, axis=-1, keepdims=True)
    floor = jnp.exp(-m_t)
    denom = jnp.maximum(jnp.abs(den), floor)
    b_last = jnp.where(d == 0, brow[:, L - 1:L], brow[:, 0:1])
    return dict(bcol=bcol, igcol=igcol, m_t=m_t, d_mat=d_mat, inter=inter, sb=sb, cb=cb, num=num, den=den,
                floor=floor, denom=denom, b_last=b_last)


def _head_cols(v0_ref, v1_ref, hd):
    ref = v0_ref if hd < M_HEADS // 2 else v1_ref
    lo = (hd % (M_HEADS // 2)) * M_HEAD_DIM
    return ref[:, lo:lo + M_HEAD_DIM]


def _mlstm_fwd(qk, proj, ig5, b5):
    S = qk.shape[0]
    nc = S // BLK
    L, Dh, H = BLK, M_HEAD_DIM, M_HEADS

    def body(q_ref, k_ref, v0_ref, v1_ref, ig_ref, b_ref, h_ref, cs_ref, ns_ref, ms_ref, C, nvec, m):
        d = pl.program_id(0)

        @pl.when(pl.program_id(1) == 0)
        def _():
            C[...] = jnp.zeros_like(C)
            nvec[...] = jnp.zeros_like(nvec)
            m[...] = jnp.zeros_like(m)

        eye = _eye_mask(L)
        for hd in range(H):
            cols = slice(hd * Dh, (hd + 1) * Dh)
            q, k = q_ref[:, cols], k_ref[:, cols]
            vf = _head_cols(v0_ref, v1_ref, hd)
            brow, igrow = b_ref[hd], ig_ref[hd]
            m_prev = m[hd, :, 0:1]
            cs_ref[hd] = C[hd].astype(BF16)
            ns_ref[hd] = nvec[hd]
            ms_ref[hd] = m[hd]
            f = _mlstm_chunk(d, q, k, vf.astype(BF16), brow, igrow, C[hd], nvec[hd], m_prev, eye)
            h_ref[:, cols] = f["num"] / f["denom"]
            b_last = f["b_last"]
            log_w = b_last - brow + igrow
            m_new = jnp.maximum(b_last + m_prev, jnp.max(log_w, axis=-1, keepdims=True))
            w_col = jnp.exp(b_last - f["bcol"] + f["igcol"] - m_new)
            decay = jnp.exp(b_last + m_prev - m_new)
            C[hd] = decay * C[hd] + _dot_tn((w_col * vf).astype(BF16), k)
            nvec[hd] = decay * nvec[hd] + jnp.sum(w_col * k.astype(F32), axis=0, keepdims=True)
            m[hd] = jnp.broadcast_to(m_new, (1, LANES))

    cidx = lambda d, c: _chunk_index(d, c, nc, False)
    gspec = pl.BlockSpec((None, H, None, 1, LANES), lambda d, c: (d, 0, cidx(d, c), 0, 0))
    st = lambda *blk: pl.BlockSpec((None, H, None) + blk, lambda d, c: (d, 0, cidx(d, c), 0, 0))
    half = M_WIDTH // 2
    return pl.pallas_call(
        body, name="mlstm_fwd", grid=(2, nc),
        in_specs=[pl.BlockSpec((L, M_WIDTH), lambda d, c: (cidx(d, c), 0)),
                  pl.BlockSpec((L, M_WIDTH), lambda d, c: (cidx(d, c), 1)),
                  pl.BlockSpec((L, half), lambda d, c: (cidx(d, c), OFF_VM // half)),
                  pl.BlockSpec((L, half), lambda d, c: (cidx(d, c), OFF_VM // half + 1)), gspec, gspec],
        out_specs=[pl.BlockSpec((None, L, M_WIDTH), lambda d, c: (d, cidx(d, c), 0)),
                   st(Dh, Dh), st(1, Dh), st(1, LANES)],
        out_shape=[jax.ShapeDtypeStruct((2, S, M_WIDTH), F32), jax.ShapeDtypeStruct((2, H, nc, Dh, Dh), BF16),
                   jax.ShapeDtypeStruct((2, H, nc, 1, Dh), F32), jax.ShapeDtypeStruct((2, H, nc, 1, LANES), F32)],
        scratch_shapes=[pltpu.VMEM((H, Dh, Dh), F32), pltpu.VMEM((H, 1, Dh), F32), pltpu.VMEM((H, 1, LANES), F32)],
        compiler_params=_params(),
    )(qk, qk, proj, proj, ig5, b5)


def _mlstm_bwd(qk, proj, ig5, b5, cs, ns, ms, dhm):
    S = qk.shape[0]
    nc = S // BLK
    L, Dh, H = BLK, M_HEAD_DIM, M_HEADS

    def body(q_ref, k_ref, v0_ref, v1_ref, ig_ref, b_ref, cs_ref, ns_ref, ms_ref, dh_ref,
             dq_ref, dk_ref, dv_ref, di_ref, df_ref, dbl_ref, R, rvec, mu):
        d = pl.program_id(0)

        @pl.when(pl.program_id(1) == 0)
        def _():
            R[...] = jnp.zeros_like(R)
            rvec[...] = jnp.zeros_like(rvec)
            mu[...] = jnp.full(mu.shape, NEG_BIG, F32)

        eye = _eye_mask(L)
        for hd in range(H):
            cols = slice(hd * Dh, (hd + 1) * Dh)
            q, k = q_ref[:, cols], k_ref[:, cols]
            qf, kf = q.astype(F32), k.astype(F32)
            vb = _head_cols(v0_ref, v1_ref, hd).astype(BF16)
            brow, igrow = b_ref[hd], ig_ref[hd]
            nprev = ns_ref[hd]
            m_prev = ms_ref[hd, :, 0:1]
            cprev = cs_ref[hd]
            f = _mlstm_chunk(d, q, k, vb, brow, igrow, cprev, nprev, m_prev, eye)
            dh = dh_ref[:, cols]
            inv = 1.0 / f["denom"]
            hcur = f["num"] * inv
            dnum = dh * inv
            active = jnp.abs(f["den"]) >= f["floor"]
            dden = (jnp.where(active, -jnp.sign(f["den"]), 0.0)
                    * jnp.sum(dh * hcur, axis=-1, keepdims=True) * inv)
            dnb = dnum.astype(BF16)
            dqk = ((_dot_nt(dnb, vb) + dden) * f["d_mat"]).astype(BF16)
            dq = _dot(dqk, k) + f["inter"] * (_dot(dnb, f["cb"]) + dden * nprev)
            mu_prev = mu[hd, :, 0:1]
            a_col = jnp.exp(f["b_last"] - f["bcol"] + f["igcol"] + mu_prev)
            rb = R[hd].astype(BF16)
            dv = _dot_tn(f["sb"], dnb) + a_col * _dot_nt(k, rb)
            dk_inter = a_col * (_dot(vb, rb) + rvec[hd])
            dk = _dot_tn(dqk, q) + dk_inter
            dq_ref[:, cols] = dq
            dk_ref[:, cols] = dk
            dv_ref[:, cols] = dv
            kdk = jnp.sum(kf * dk, axis=-1, keepdims=True)
            qdq = jnp.sum(qf * dq, axis=-1, keepdims=True)
            di_ref[hd] = _col_to_row(kdk, eye)
            df_ref[hd] = _col_to_row(qdq - kdk, eye)
            older = jnp.sum(jnp.sum(R[hd] * cprev.astype(F32), axis=-1, keepdims=True), axis=0, keepdims=True)
            older = older + jnp.sum(rvec[hd] * nprev, axis=-1, keepdims=True)
            dbl = (jnp.sum(jnp.sum(kf * dk_inter, axis=-1, keepdims=True), axis=0, keepdims=True)
                   + jnp.exp(f["b_last"] + mu_prev + m_prev) * older)
            dbl_ref[hd] = jnp.broadcast_to(dbl, (1, LANES))
            lw = f["bcol"] - f["m_t"]
            mu_new = jnp.maximum(f["b_last"] + mu_prev, jnp.max(lw, axis=0, keepdims=True))
            wq = jnp.exp(lw - mu_new)
            decay = jnp.exp(f["b_last"] + mu_prev - mu_new)
            R[hd] = decay * R[hd] + _dot_tn((wq * dnum).astype(BF16), q)
            rvec[hd] = decay * rvec[hd] + jnp.sum(wq * dden * qf, axis=0, keepdims=True)
            mu[hd] = jnp.broadcast_to(mu_new, (1, LANES))

    cidx = lambda d, c: _chunk_index(d, c, nc, True)
    gin = pl.BlockSpec((None, H, None, 1, LANES), lambda d, c: (d, 0, cidx(d, c), 0, 0))
    st = lambda *blk: pl.BlockSpec((None, H, None) + blk, lambda d, c: (d, 0, cidx(d, c), 0, 0))
    per_dir = pl.BlockSpec((None, L, M_WIDTH), lambda d, c: (d, cidx(d, c), 0))
    big = jax.ShapeDtypeStruct((2, S, M_WIDTH), F32)
    small = jax.ShapeDtypeStruct((2, H, nc, 1, LANES), F32)
    half = M_WIDTH // 2
    return pl.pallas_call(
        body, name="mlstm_bwd", grid=(2, nc),
        in_specs=[pl.BlockSpec((L, M_WIDTH), lambda d, c: (cidx(d, c), 0)),
                  pl.BlockSpec((L, M_WIDTH), lambda d, c: (cidx(d, c), 1)),
                  pl.BlockSpec((L, half), lambda d, c: (cidx(d, c), OFF_VM // half)),
                  pl.BlockSpec((L, half), lambda d, c: (cidx(d, c), OFF_VM // half + 1)), gin, gin,
                  st(Dh, Dh), st(1, Dh), st(1, LANES),
                  pl.BlockSpec((L, M_WIDTH), lambda d, c: (cidx(d, c), 0))],
        out_specs=[per_dir, per_dir, per_dir, gin, gin, gin],
        out_shape=[big, big, big, small, small, small],
        scratch_shapes=[pltpu.VMEM((H, Dh, Dh), F32), pltpu.VMEM((H, 1, Dh), F32), pltpu.VMEM((H, 1, LANES), F32)],
        compiler_params=_params(),
    )(qk, qk, proj, proj, ig5, b5, cs, ns, ms, dhm)


def _mlstm_out_fwd(hs, proj, gamma):
    S = hs.shape[1]
    tm = _row_tile(S, 512)
    Dh, H = M_HEAD_DIM, M_HEADS

    def body(hf_ref, hb_ref, o_ref, g_ref, y_ref):
        hm = hf_ref[...] + hb_ref[...]
        r = lax.rsqrt(jnp.mean(hm * hm, axis=-1, keepdims=True) + EPS)
        y_ref[...] = (_sigmoid(o_ref[...]) * (hm * r * g_ref[...])).astype(BF16)

    hspec = lambda d: pl.BlockSpec((None, tm, Dh), lambda i, h: (d, i, h))
    return pl.pallas_call(
        body, name="mlstm_out_fwd", grid=(S // tm, H),
        in_specs=[hspec(0), hspec(1), pl.BlockSpec((tm, Dh), lambda i, h: (i, OFF_OM // Dh + h)),
                  pl.BlockSpec((1, Dh), lambda i, h: (0, h))],
        out_specs=pl.BlockSpec((tm, Dh), lambda i, h: (i, h)),
        out_shape=jax.ShapeDtypeStruct((S, M_WIDTH), BF16),
        compiler_params=_params(),
    )(hs, hs, proj, gamma.reshape(1, M_WIDTH))


def _mlstm_out_bwd(hs, proj, gamma, dycat):
    S = hs.shape[1]
    tm = _row_tile(S, 512)
    Dh, H = M_HEAD_DIM, M_HEADS
    D_OFF = ATT_WIDTH // Dh

    def body(hf_ref, hb_ref, o_ref, g_ref, dy_ref, dh_ref, do_ref, dg_ref):
        hm = hf_ref[...] + hb_ref[...]
        r = lax.rsqrt(jnp.mean(hm * hm, axis=-1, keepdims=True) + EPS)
        hh = hm * r
        so = _sigmoid(o_ref[...])
        dy = dy_ref[...]
        gam = g_ref[...]
        do_ref[...] = (dy * hh * gam * so * (1.0 - so)).astype(BF16)
        dyn = dy * so
        gd = dyn * gam
        dh_ref[...] = r * (gd - hh * jnp.mean(hh * gd, axis=-1, keepdims=True))
        part = jnp.sum(dyn * hh, axis=0, keepdims=True)

        @pl.when(pl.program_id(1) == 0)
        def _():
            dg_ref[...] = part

        @pl.when(pl.program_id(1) > 0)
        def _():
            dg_ref[...] += part

    hspec = lambda d: pl.BlockSpec((None, tm, Dh), lambda h, i: (d, i, h))
    out = pl.BlockSpec((tm, Dh), lambda h, i: (i, h))
    vec = pl.BlockSpec((1, Dh), lambda h, i: (0, h))
    dhm, do, dg = pl.pallas_call(
        body, name="mlstm_out_bwd", grid=(H, S // tm),
        in_specs=[hspec(0), hspec(1), pl.BlockSpec((tm, Dh), lambda h, i: (i, OFF_OM // Dh + h)), vec,
                  pl.BlockSpec((tm, Dh), lambda h, i: (i, D_OFF + h))],
        out_specs=[out, out, vec],
        out_shape=[jax.ShapeDtypeStruct((S, M_WIDTH), F32), jax.ShapeDtypeStruct((S, M_WIDTH), BF16),
                   jax.ShapeDtypeStruct((1, M_WIDTH), F32)],
        compiler_params=_params(),
    )(hs, hs, proj, gamma.reshape(1, M_WIDTH), dycat)
    return dhm, do, dg.reshape(M_WIDTH)


def _sum_cast(a2, name):
    _, S, C = a2.shape
    tm = _row_tile(S, 512)

    def body(a_ref, b_ref, o_ref):
        o_ref[...] = (a_ref[...] + b_ref[...]).astype(BF16)

    spec = lambda d: pl.BlockSpec((None, tm, C), lambda i: (d, i, 0))
    return pl.pallas_call(
        body, name=name, grid=(S // tm,),
        in_specs=[spec(0), spec(1)], out_specs=pl.BlockSpec((tm, C), lambda i: (i, 0)),
        out_shape=jax.ShapeDtypeStruct((S, C), BF16),
        compiler_params=_params(),
    )(a2, a2)


def _adamw(w, g, m, v):
    shape = w.shape
    C = shape[-1]
    R = w.size // C
    tm = _elementwise_rows(R, C)

    def body(w_ref, g_ref, m_ref, v_ref, d_ref, mo_ref, vo_ref):
        gv = g_ref[...]
        mn = ADAM_B1 * m_ref[...] + (1.0 - ADAM_B1) * gv
        vn = ADAM_B2 * v_ref[...] + (1.0 - ADAM_B2) * (gv * gv)
        m_hat = mn / (1.0 - ADAM_B1 ** ADAM_STEP)
        v_hat = vn / (1.0 - ADAM_B2 ** ADAM_STEP)
        d_ref[...] = -ADAM_LR * (m_hat / (jnp.sqrt(v_hat) + ADAM_EPS) + ADAM_WD * w_ref[...])
        mo_ref[...] = mn
        vo_ref[...] = vn

    row = pl.BlockSpec((tm, C), lambda i: (i, 0))
    sds = jax.ShapeDtypeStruct((R, C), F32)
    outs = pl.pallas_call(
        body, name="adamw", grid=(R // tm,),
        in_specs=[row] * 4, out_specs=[row] * 3, out_shape=[sds] * 3,
        compiler_params=_params(),
    )(*[t.reshape(R, C) for t in (w, g, m, v)])
    return tuple(o.reshape(shape) for o in outs)


CID_WEIGHT_GATHER, CID_PAIR_SWAP, CID_SCATTER = 0, 1, 2


def _other_chips(x, y):
    return [(1 - x, y), (x, 1 - y), (1 - x, 1 - y)]


def _handshake(peers):
    barrier = pltpu.get_barrier_semaphore()
    for peer in peers:
        pl.semaphore_signal(barrier, inc=1, device_id=peer, device_id_type=MESH)
    pl.semaphore_wait(barrier, len(peers))


def _sequencer_call(body, name, out_type, sem_counts, collective_id, operands):
    return pl.kernel(
        body, name=name, out_type=out_type,
        mesh=plsc.ScalarSubcoreMesh(axis_name="sequencer", num_cores=1),
        scratch_types=[pltpu.SemaphoreType.DMA((n,)) for n in sem_counts],
        compiler_params=pltpu.CompilerParams(collective_id=collective_id),
    )(*operands)


def _comm_weight_gather(shards):
    T = len(shards)

    def body(*refs):
        ins, outs = refs[:T], refs[T:2 * T]
        send, recv, local = refs[2 * T:]
        x, y, c = lax.axis_index("x"), lax.axis_index("y"), lax.axis_index("c")
        me = 2 * x + y
        chips = _other_chips(x, y)
        _handshake([(px, py, c) for px, py in chips])
        mine = []
        for t in range(T):
            cp = pltpu.make_async_copy(ins[t], outs[t].at[me], local.at[t])
            cp.start()
            mine.append(cp)
            for k, (px, py) in enumerate(chips):
                pltpu.make_async_remote_copy(
                    src_ref=ins[t], dst_ref=outs[t].at[me], send_sem=send.at[3 * t + k], recv_sem=recv.at[3 * t + k],
                    device_id=(px, py, c), device_id_type=MESH).start()
        for t in range(T):
            for k, (px, py) in enumerate(chips):
                cp = pltpu.make_async_remote_copy(
                    src_ref=ins[t], dst_ref=outs[t].at[2 * px + py], send_sem=send.at[3 * t + k],
                    recv_sem=recv.at[3 * t + k], device_id=(px, py, c), device_id_type=MESH)
                cp.wait_send()
                cp.wait_recv()
            mine[t].wait()

    return _sequencer_call(
        body, "comm_weight_gather", tuple(jax.ShapeDtypeStruct((4,) + s.shape, s.dtype) for s in shards),
        (3 * T, 3 * T, T), CID_WEIGHT_GATHER, shards)


def _comm_to_sibling(xs, other_half=False):
    T = len(xs)

    def body(*refs):
        ins, outs = refs[:T], refs[T:2 * T]
        send, recv = refs[2 * T:]
        x, y, c = lax.axis_index("x"), lax.axis_index("y"), lax.axis_index("c")
        _handshake([(x, y, 1 - c)])
        cps = [pltpu.make_async_remote_copy(
            src_ref=ins[t].at[:, 1 - c] if other_half else ins[t], dst_ref=outs[t], send_sem=send.at[t],
            recv_sem=recv.at[t], device_id=(x, y, 1 - c), device_id_type=MESH) for t in range(T)]
        for cp in cps:
            cp.start()
        for cp in cps:
            cp.wait_send()
            cp.wait_recv()

    shapes = [s.shape[:1] + s.shape[2:] if other_half else s.shape for s in xs]
    return _sequencer_call(
        body, "comm_pair_swap", tuple(jax.ShapeDtypeStruct(sh, s.dtype) for sh, s in zip(shapes, xs)),
        (T, T), CID_PAIR_SWAP, xs)


def _comm_scatter_to_owners(ps):
    T = len(ps)

    def body(*refs):
        ins, outs = refs[:T], refs[T:2 * T]
        send, recv = refs[2 * T:]
        x, y, c = lax.axis_index("x"), lax.axis_index("y"), lax.axis_index("c")
        chips = _other_chips(x, y)
        _handshake([(px, py, c) for px, py in chips])
        cps = []
        for t in range(T):
            for k, (px, py) in enumerate(chips):
                cp = pltpu.make_async_remote_copy(
                    src_ref=ins[t].at[2 * px + py], dst_ref=outs[t].at[k], send_sem=send.at[3 * t + k],
                    recv_sem=recv.at[3 * t + k], device_id=(px, py, c), device_id_type=MESH)
                cp.start()
                cps.append(cp)
        for cp in cps:
            cp.wait_send()
            cp.wait_recv()

    return _sequencer_call(
        body, "comm_scatter_to_owners", tuple(jax.ShapeDtypeStruct((3,) + p.shape[1:], p.dtype) for p in ps),
        (3 * T, 3 * T), CID_SCATTER, ps)


def _comm_all_to_all_small(pack):
    R = pack.shape[0]

    def body(in_ref, out_ref, send, recv, local):
        x, y, c = lax.axis_index("x"), lax.axis_index("y"), lax.axis_index("c")
        me = 4 * x + 2 * y + c
        mine = pltpu.make_async_copy(in_ref, out_ref.at[me], local.at[0])
        mine.start()
        peers = []
        for k in range(1, 8):
            fx, fy, fc = (k >> 2) & 1, (k >> 1) & 1, k & 1
            peers.append((x + fx * (1 - 2 * x), y + fy * (1 - 2 * y), c + fc * (1 - 2 * c)))
        for k, peer in enumerate(peers):
            pltpu.make_async_remote_copy(
                src_ref=in_ref, dst_ref=out_ref.at[me], send_sem=send.at[k], recv_sem=recv.at[k],
                device_id=peer, device_id_type=MESH).start()
        for k, (px, py, pc) in enumerate(peers):
            cp = pltpu.make_async_remote_copy(
                src_ref=in_ref, dst_ref=out_ref.at[4 * px + 2 * py + pc], send_sem=send.at[k], recv_sem=recv.at[k],
                device_id=(px, py, pc), device_id_type=MESH)
            cp.wait_send()
            cp.wait_recv()
        mine.wait()

    return pl.pallas_call(
        body, name="comm_all_to_all_small",
        in_specs=[ANY], out_specs=ANY,
        out_shape=jax.ShapeDtypeStruct((8, R, LANES), F32),
        scratch_shapes=[pltpu.SemaphoreType.DMA((7,)), pltpu.SemaphoreType.DMA((7,)), pltpu.SemaphoreType.DMA((1,))],
    )(pack)


def _tie(*trees):
    return lax.optimization_barrier(trees)


class _ReduceScatter:
    def __init__(self, grads, dh):
        split = [g.reshape(4, 2, g.shape[1] // 2, g.shape[2]) for g in grads]
        self.split, self.dh = _tie(split, dh)
        self.got = _comm_to_sibling(self.split, other_half=True)

    def scatter(self, dh):
        chip = 2 * lax.axis_index("x") + lax.axis_index("y")
        split, got, dh = _tie(self.split, list(self.got), dh)
        pair = [_pair_sum(a, b) for a, b in zip(split, got)]
        pair, dh = _tie(pair, dh)
        self.own = [lax.dynamic_index_in_dim(p, chip, 0, keepdims=False) for p in pair]
        self.arrived = _comm_scatter_to_owners(pair)
        return dh

    def gather(self, dh):
        own, arrived, dh = _tie(self.own, list(self.arrived), dh)
        halves = [_add_n([o, a[0], a[1], a[2]], "chip_sum") for o, a in zip(own, arrived)]
        self.halves, dh = _tie(halves, dh)
        self.other = _comm_to_sibling(self.halves)
        return dh

    def settle(self, dh):
        self.other, dh = _tie(list(self.other), dh)
        return dh

    def result(self):
        south = lax.axis_index("c") == 0
        return [jnp.where(south, jnp.concatenate([mine, theirs]), jnp.concatenate([theirs, mine]))
                for mine, theirs in zip(self.halves, self.other)]


def _ffn_fwd(x, g_pre, g_post, wgu, wd):
    S, D = x.shape
    F = wd.shape[1]
    xn = _norm_fwd(x, g_pre)
    gu, h = _ffn_up(xn, wgu)
    f = _mm(h, wd, "nn", ja="r", jb="r", tm=_row_tile(S, ACT_ROWS), tn=MODEL_COLS, tk=F, name="ffn_down")
    x_out = _resid_norm_fwd(x, f, g_post, 0.5)
    return x_out, (x, xn, gu, h, f)


def _ffn_bwd(dx, saved, g_pre, g_post, wgu, wd, mid=None):
    x, xn, gu, h, f = saved
    S, D = x.shape
    J, F, _ = wd.shape
    tm, tk = _row_tile(S, ACT_ROWS), _row_tile(S, TOKEN_ROWS)
    df, dg_post = _norm_bwd(f, g_post, dx, 0.5, out_dtype=BF16)
    d_wd = _mm(h, df, "tn", ja="b", tm=F, tn=MODEL_COLS, tk=tk, name="ffn_dwd")
    dgu = _ffn_bwd_hidden(df, wd, gu)
    if mid is not None:
        dgu = mid(dgu)
    d_wgu = _mm(xn, dgu, "tn", jb="b", tm=MODEL_COLS // 2, tn=F, tk=tk, name="ffn_dwgu")
    dxn = _mm(dgu, wgu, "nt", ja="r", jb="r", tm=tm, tn=MODEL_COLS, tk=2 * F, name="ffn_dxn")
    dx_in, dg_pre = _norm_bwd(x, g_pre, dxn, 1.0, resid=dx)
    return dx_in, dg_pre, dg_post, d_wgu, d_wd


def _gates_layout(proj, b_gate, nc):
    gt = jnp.transpose(proj[:, OFF_G:OFF_G + N_GATES]).reshape(N_GATES * nc, LANES)
    bias = jnp.repeat(b_gate, nc).reshape(N_GATES * nc, 1)
    return gt, bias


def _mixer_fwd(x, p, cosf, sinf):
    S, D = x.shape
    nc = S // BLK
    hn = _norm_fwd(x, p["g_pre"])
    proj = _mm(hn, p["w_in"], "nn", tm=_row_tile(S, ACT_ROWS), tn=IN_PAD // 5, tk=D, name="in_proj")
    qr, kr, va = _rope_fwd(proj, cosf, sinf)
    y_att = _attn_fwd(qr, kr, va, p["sink"])
    qk = _conv5(proj, p["conv_w"], col0=OFF_QM, act=True, out_dtype=BF16, name="conv_silu")
    gt, bias = _gates_layout(proj, p["b_gate"], nc)
    ig, cum = _gate_prep(gt, bias)
    ig5, b5 = ig.reshape(2, M_HEADS, nc, 1, LANES), cum.reshape(2, M_HEADS, nc, 1, LANES)
    hs, cs, ns, ms = _mlstm_fwd(qk, proj, ig5, b5)
    y_m = _mlstm_out_fwd(hs, proj, p["gamma"])
    ycat = jnp.concatenate([y_att, y_m], axis=-1)
    mix = _mm(ycat, p["w_out"], "nn", tm=_row_tile(S, ACT_ROWS), tn=MODEL_COLS, tk=D, name="out_proj")
    x_out = _resid_norm_fwd(x, mix, p["g_post"], 1.0)
    return x_out, (x, hn, proj, qr, kr, va, qk, gt, bias, ig5, b5, hs, cs, ns, ms, ycat, mix)


def _mixer_bwd(dx, saved, p, cosf, sinf):
    x, hn, proj, qr, kr, va, qk, gt, bias, ig5, b5, hs, cs, ns, ms, ycat, mix = saved
    S, D = x.shape
    nc = S // BLK
    tm, tk = _row_tile(S, ACT_ROWS), _row_tile(S, TOKEN_ROWS)
    dmix, dg_post = _norm_bwd(mix, p["g_post"], dx, 1.0, out_dtype=BF16)
    d_wout = _mm(ycat, dmix, "tn", tm=MODEL_COLS, tn=MODEL_COLS, tk=tk, name="dw_out")
    dycat = _mm(dmix, p["w_out"], "nt", tm=tm, tn=MODEL_COLS, tk=D, name="d_ycat")
    dhm, d_om, d_gamma = _mlstm_out_bwd(hs, proj, p["gamma"], dycat)
    dq2, dk2, dv2, di, dfc, dbl = _mlstm_bwd(qk, proj, ig5, b5, cs, ns, ms, dhm)
    d_vm = _sum_cast(dv2, "dv_sum")
    dgt, db = _gate_bwd(gt, bias, di.reshape(-1, LANES), dfc.reshape(-1, LANES), dbl.reshape(-1, LANES), nc)
    dpre, d_conv = _conv_bwd_pre(proj, OFF_QM, p["conv_w"], dq2, dk2)
    d_qkm = _conv5(dpre, p["conv_w"][::-1], col0=0, act=False, out_dtype=BF16, name="conv_bwd_x")
    d_gates = jnp.transpose(dgt.reshape(N_GATES, S)).astype(BF16)
    d_gates = jnp.pad(d_gates, ((0, 0), (0, IN_PAD - IN_WIDTH)))
    dqr, dkr, dva, d_sink = _attn_bwd(qr, kr, va, p["sink"], dycat)
    d_att = _rope_bwd(dqr, dkr, dva, cosf, sinf)
    dproj = jnp.concatenate([d_att, d_qkm, d_vm, d_om, d_gates], axis=-1)
    d_win = _mm(hn, dproj, "tn", tm=MODEL_COLS, tn=IN_PAD // 5, tk=tk, name="dw_in")
    dhn = _mm(dproj, p["w_in"], "nt", tm=tm, tn=MODEL_COLS, tk=IN_PAD // 5, name="d_hn")
    dx_in, dg_pre = _norm_bwd(x, p["g_pre"], dhn, 1.0, resid=dx)
    small = dict(g_pre=dg_pre, g_post=dg_post, gamma=d_gamma, conv=d_conv, b_gate=db[:, 0], sink=d_sink)
    return dx_in, small, d_win, d_wout


def _pad_lanes(v):
    v = v.reshape(-1)
    return jnp.pad(v, (0, (-v.shape[0]) % LANES))


def kernel(x, ffn1_norm_pre, ffn1_norm_post, ffn1_w_gate, ffn1_w_up, ffn1_w_down, mix_norm_pre, mix_norm_post, w_in, b_gate, conv_w, attn_sink, mlstm_norm, w_out, ffn2_norm_pre, ffn2_norm_post, ffn2_w_gate, ffn2_w_up, ffn2_w_down, loss_target, m_ffn1_norm_pre, m_ffn1_norm_post, m_ffn1_w_gate, m_ffn1_w_up, m_ffn1_w_down, m_mix_norm_pre, m_mix_norm_post, m_w_in, m_b_gate, m_conv_w, m_attn_sink, m_mlstm_norm, m_w_out, m_ffn2_norm_pre, m_ffn2_norm_post, m_ffn2_w_gate, m_ffn2_w_up, m_ffn2_w_down, v_ffn1_norm_pre, v_ffn1_norm_post, v_ffn1_w_gate, v_ffn1_w_up, v_ffn1_w_down, v_mix_norm_pre, v_mix_norm_post, v_w_in, v_b_gate, v_conv_w, v_attn_sink, v_mlstm_norm, v_w_out, v_ffn2_norm_pre, v_ffn2_norm_post, v_ffn2_w_gate, v_ffn2_w_up, v_ffn2_w_down):
    names = ["ffn1_norm_pre", "ffn1_norm_post", "ffn1_w_gate", "ffn1_w_up", "ffn1_w_down", "mix_norm_pre",
             "mix_norm_post", "w_in", "b_gate", "conv_w", "attn_sink", "mlstm_norm", "w_out", "ffn2_norm_pre",
             "ffn2_norm_post", "ffn2_w_gate", "ffn2_w_up", "ffn2_w_down"]
    w = dict(zip(names, [ffn1_norm_pre, ffn1_norm_post, ffn1_w_gate, ffn1_w_up, ffn1_w_down, mix_norm_pre,
                         mix_norm_post, w_in, b_gate, conv_w, attn_sink, mlstm_norm, w_out, ffn2_norm_pre,
                         ffn2_norm_post, ffn2_w_gate, ffn2_w_up, ffn2_w_down]))
    mom_m = dict(zip(names, [m_ffn1_norm_pre, m_ffn1_norm_post, m_ffn1_w_gate, m_ffn1_w_up, m_ffn1_w_down,
                             m_mix_norm_pre, m_mix_norm_post, m_w_in, m_b_gate, m_conv_w, m_attn_sink,
                             m_mlstm_norm, m_w_out, m_ffn2_norm_pre, m_ffn2_norm_post, m_ffn2_w_gate,
                             m_ffn2_w_up, m_ffn2_w_down]))
    mom_v = dict(zip(names, [v_ffn1_norm_pre, v_ffn1_norm_post, v_ffn1_w_gate, v_ffn1_w_up, v_ffn1_w_down,
                             v_mix_norm_pre, v_mix_norm_post, v_w_in, v_b_gate, v_conv_w, v_attn_sink,
                             v_mlstm_norm, v_w_out, v_ffn2_norm_pre, v_ffn2_norm_post, v_ffn2_w_gate,
                             v_ffn2_w_up, v_ffn2_w_down]))
    xs = x[0]
    target = loss_target[0]
    S, D = xs.shape
    depth = w_in.shape[0]
    F = ffn1_w_gate.shape[-1]
    in_shard = w_in.shape[-1]
    conv_shard = conv_w.shape[-1]
    chip = 2 * lax.axis_index("x") + lax.axis_index("y")
    cosf, sinf = _rope_tables(S)

    gathered = []
    for l in range(depth):
        first = [jnp.concatenate([w["ffn1_w_gate"][l], w["ffn1_w_up"][l]], axis=-1).astype(BF16),
                 w["ffn1_w_down"][l].astype(BF16)]
        rest = [w["w_in"][l].astype(BF16), w["w_out"][l].astype(BF16),
                jnp.concatenate([w["ffn2_w_gate"][l], w["ffn2_w_up"][l]], axis=-1).astype(BF16),
                w["ffn2_w_down"][l].astype(BF16),
                jnp.pad(w["conv_w"][l], ((0, 8 - CONV_WIDTH), (0, 0)))]
        gathered.append((list(_comm_weight_gather(first)), list(_comm_weight_gather(rest))))

    def mixer_weights(l, win, wout, convg):
        win_full = jnp.concatenate([win[j] for j in range(4)], axis=-1)
        conv_full = jnp.concatenate([convg[j] for j in range(4)], axis=-1)[:CONV_WIDTH]
        return dict(w_in=jnp.pad(win_full, ((0, 0), (0, IN_PAD - IN_WIDTH))), w_out=wout.reshape(D, D),
                    conv_w=conv_full, g_pre=w["mix_norm_pre"][l], g_post=w["mix_norm_post"][l],
                    b_gate=w["b_gate"][l], sink=w["attn_sink"][l], gamma=w["mlstm_norm"][l])

    h = xs
    saved = []
    layers = []
    for l in range(depth):
        (wgu1, wd1), h = _tie(gathered[l][0], h)
        h, s1 = _ffn_fwd(h, w["ffn1_norm_pre"][l], w["ffn1_norm_post"][l], wgu1, wd1)
        (win, wout, wgu2, wd2, convg), h = _tie(gathered[l][1], h)
        p = dict(wgu1=wgu1, wd1=wd1, wgu2=wgu2, wd2=wd2, mix=mixer_weights(l, win, wout, convg))
        layers.append(p)
        h, s2 = _mixer_fwd(h, p["mix"], cosf, sinf)
        h, s3 = _ffn_fwd(h, w["ffn2_norm_pre"][l], w["ffn2_norm_post"][l], p["wgu2"], p["wd2"])
        saved.append((s1, s2, s3))
    dh, loss_tile = _loss_grad(h, target)

    big = {}
    small_rows = []
    rs_a = rs_b = None
    for l in reversed(range(depth)):
        p = layers[l]
        s1, s2, s3 = saved[l]
        old_a, old_b = rs_a, rs_b
        dh, dg2_pre, dg2_post, d_wgu2, d_wd2 = _ffn_bwd(dh, s3, w["ffn2_norm_pre"][l], w["ffn2_norm_post"][l],
                                                       p["wgu2"], p["wd2"], mid=old_b.scatter if old_b else None)
        if old_a is not None:
            dh = old_a.gather(dh)
        dh, sm, d_win, d_wout = _mixer_bwd(dh, s2, p["mix"], cosf, sinf)
        if old_b is not None:
            dh = old_b.gather(dh)
        d_win4 = jnp.stack([d_win[:, j * in_shard:(j + 1) * in_shard] for j in range(4)])
        rs_a = _ReduceScatter([d_win4, d_wout.reshape(4, D // 4, D), d_wgu2, d_wd2], dh)
        dh, dg1_pre, dg1_post, d_wgu1, d_wd1 = _ffn_bwd(rs_a.dh, s1, w["ffn1_norm_pre"][l], w["ffn1_norm_post"][l],
                                                       p["wgu1"], p["wd1"], mid=rs_a.scatter)
        if old_a is not None:
            dh = old_b.settle(old_a.settle(dh))
        rs_b = _ReduceScatter([d_wgu1, d_wd1], dh)
        dh = rs_b.dh
        big[l] = (rs_a, rs_b)
        small_rows.append((l, [dg1_pre, dg1_post, sm["g_pre"], sm["g_post"], dg2_pre, dg2_post, sm["gamma"],
                               sm["conv"], sm["b_gate"], sm["sink"]]))
    dh = rs_b.gather(rs_b.scatter(rs_a.gather(dh)))
    big = {l: b.result() + a.result() for l, (a, b) in big.items()}

    small_rows.sort(key=lambda t: t[0])
    flat = [_pad_lanes(v) for _, vs in small_rows for v in vs] + [loss_tile[0]]
    sizes = [f.shape[0] for f in flat]
    pack = jnp.concatenate(flat)
    pack = jnp.pad(pack, (0, (-pack.shape[0]) % (8 * LANES))).reshape(-1, LANES)
    every = _comm_all_to_all_small(pack)
    total = _add_n([every[i] for i in range(8)], "small_sum").reshape(-1)
    pieces, off = [], 0
    for n in sizes:
        pieces.append(total[off:off + n])
        off += n
    loss = pieces[-1][0]
    per_layer = [pieces[10 * l:10 * l + 10] for l in range(depth)]

    def stack_small(i, shape):
        n = 1
        for s in shape:
            n *= s
        return jnp.stack([per_layer[l][i][:n].reshape(shape) for l in range(depth)])

    conv_full_grad = stack_small(7, (CONV_WIDTH, 4 * conv_shard))
    grads = {
        "ffn1_norm_pre": stack_small(0, (D,)), "ffn1_norm_post": stack_small(1, (D,)),
        "mix_norm_pre": stack_small(2, (D,)), "mix_norm_post": stack_small(3, (D,)),
        "ffn2_norm_pre": stack_small(4, (D,)), "ffn2_norm_post": stack_small(5, (D,)),
        "mlstm_norm": stack_small(6, (M_WIDTH,)),
        "conv_w": lax.dynamic_slice_in_dim(conv_full_grad, chip * conv_shard, conv_shard, 2),
        "b_gate": stack_small(8, (N_GATES,)), "attn_sink": stack_small(9, (ATT_HEADS,)),
    }
    gu1 = jnp.stack([big[l][0] for l in range(depth)])
    gu2 = jnp.stack([big[l][4] for l in range(depth)])
    grads["ffn1_w_gate"], grads["ffn1_w_up"] = gu1[:, :, :F], gu1[:, :, F:]
    grads["ffn2_w_gate"], grads["ffn2_w_up"] = gu2[:, :, :F], gu2[:, :, F:]
    grads["ffn1_w_down"] = jnp.stack([big[l][1] for l in range(depth)])
    grads["ffn2_w_down"] = jnp.stack([big[l][5] for l in range(depth)])
    grads["w_in"] = jnp.stack([big[l][2] for l in range(depth)])
    grads["w_out"] = jnp.stack([big[l][3] for l in range(depth)])

    deltas, new_m, new_v = {}, {}, {}
    for n in names:
        deltas[n], new_m[n], new_v[n] = _adamw(w[n], grads[n], mom_m[n], mom_v[n])
    grad_x = dh[None]
    return (loss, grad_x, *[grads[n] for n in names], *[deltas[n] for n in names],
            *[new_m[n] for n in names], *[new_v[n] for n in names])
```

```python
import jax
import jax.numpy as jnp
from jax import lax
from jax.experimental import pallas as pl
from jax.experimental.pallas import tpu as pltpu
from jax.experimental.pallas import tpu_sc as plsc

F32 = jnp.float32
BF16 = jnp.bfloat16
MESH = pl.DeviceIdType.MESH
ANY = pl.BlockSpec(memory_space=pl.ANY)

VMEM_LIMIT_BYTES = 56 * 1024 * 1024
LANES = 128

EPS = 1e-6
ATT_HEADS = 8
ATT_KV_HEADS = 2
ATT_GROUP = ATT_HEADS // ATT_KV_HEADS
ATT_HEAD_DIM = 128
ATT_WIDTH = ATT_HEADS * ATT_HEAD_DIM
KV_WIDTH = ATT_KV_HEADS * ATT_HEAD_DIM
BLK = 128
M_HEADS = 4
M_HEAD_DIM = 256
M_WIDTH = M_HEADS * M_HEAD_DIM
CONV_WIDTH = 5
CONV_HALO = 8
ROPE_THETA = 10000.0
N_GATES = 4 * M_HEADS
OFF_QA, OFF_KA, OFF_VA = 0, ATT_WIDTH, ATT_WIDTH + KV_WIDTH
OFF_QM = ATT_WIDTH + 2 * KV_WIDTH
OFF_KM = OFF_QM + M_WIDTH
OFF_VM = OFF_KM + M_WIDTH
OFF_OM = OFF_VM + M_WIDTH
OFF_G = OFF_OM + M_WIDTH
IN_WIDTH = OFF_G + N_GATES
IN_PAD = OFF_G + LANES
NEG_BIG = -1e30

ADAM_LR, ADAM_B1, ADAM_B2, ADAM_EPS, ADAM_WD, ADAM_STEP = 0.001, 0.9, 0.999, 1e-08, 0.01, 10


def _params(**kw):
    return pltpu.CompilerParams(vmem_limit_bytes=VMEM_LIMIT_BYTES, **kw)


def _dot(a, b):
    return lax.dot_general(a, b, (((1,), (0,)), ((), ())), preferred_element_type=F32)


def _dot_nt(a, b):
    return lax.dot_general(a, b, (((1,), (1,)), ((), ())), preferred_element_type=F32)


def _dot_tn(a, b):
    return lax.dot_general(a, b, (((0,), (0,)), ((), ())), preferred_element_type=F32)


def _sigmoid(x):
    return 1.0 / (1.0 + jnp.exp(-x))


def _eye_mask(n):
    r = lax.broadcasted_iota(jnp.int32, (n, n), 0)
    c = lax.broadcasted_iota(jnp.int32, (n, n), 1)
    return r == c


def _row_to_col(row, eye):
    n = eye.shape[0]
    return jnp.sum(jnp.where(eye, jnp.broadcast_to(row, (n, n)), 0.0), axis=1, keepdims=True)


def _col_to_row(col, eye):
    n = eye.shape[0]
    return jnp.sum(jnp.where(eye, jnp.broadcast_to(col, (n, n)), 0.0), axis=0, keepdims=True)


def _mm(a, b, kind, *, tm, tn, tk, name, out_dtype=F32, ja=None, jb=None):
    a2, b2 = a.shape[-2:], b.shape[-2:]
    if kind == "nn":
        (M, K), (_, N) = a2, b2
    elif kind == "nt":
        (M, K), (N, _) = a2, b2
    else:
        (K, M), (_, N) = a2, b2
    J = a.shape[0] if ja else (b.shape[0] if jb else 1)
    batch = "b" in (ja, jb)
    red = "r" in (ja, jb)
    nk = K // tk
    nr = nk * (J if red else 1)
    grid = ((J if batch else 1), M // tm, N // tn, nr)

    def lead(mode, g, r):
        return g if mode == "b" else r // nk

    def a_map(g, i, n, r):
        kk = r % nk
        idx = (i, kk) if kind != "tn" else (kk, i)
        return idx if ja is None else (lead(ja, g, r),) + idx

    def b_map(g, i, n, r):
        kk = r % nk
        idx = (kk, n) if kind != "nt" else (n, kk)
        return idx if jb is None else (lead(jb, g, r),) + idx

    def o_map(g, i, n, r):
        return (g, i, n) if batch else (i, n)

    a_blk = (tm, tk) if kind != "tn" else (tk, tm)
    b_blk = (tk, tn) if kind != "nt" else (tn, tk)
    dot = {"nn": _dot, "nt": _dot_nt, "tn": _dot_tn}[kind]

    def body(a_ref, b_ref, o_ref, *scratch):
        part = dot(a_ref[...], b_ref[...])
        if nr == 1:
            o_ref[...] = part.astype(out_dtype)
        else:
            acc = scratch[0]
            r = pl.program_id(3)

            @pl.when(r == 0)
            def _():
                acc[...] = part

            @pl.when(r > 0)
            def _():
                acc[...] += part

            @pl.when(r == nr - 1)
            def _():
                o_ref[...] = acc[...].astype(out_dtype)

    return pl.pallas_call(
        body, name=name, grid=grid,
        in_specs=[pl.BlockSpec(a_blk if ja is None else (None,) + a_blk, a_map),
                  pl.BlockSpec(b_blk if jb is None else (None,) + b_blk, b_map)],
        out_specs=pl.BlockSpec((None, tm, tn) if batch else (tm, tn), o_map),
        out_shape=jax.ShapeDtypeStruct((J, M, N) if batch else (M, N), out_dtype),
        scratch_shapes=[pltpu.VMEM((tm, tn), F32)] if nr > 1 else [],
        compiler_params=_params(),
    )(a, b)


def _row_tile(S, want):
    return min(S, want)


ELEMENTWISE_TILE_BYTES = 1 << 20
ACT_ROWS = 1024
TOKEN_ROWS = 2048
MODEL_COLS = 1024


def _elementwise_rows(R, C):
    for cand in (1024, 512, 256, 128, 64, 32, 16, 8):
        if R % cand == 0 and R > cand and cand * C * 4 <= ELEMENTWISE_TILE_BYTES:
            return cand
    return R if R * C * 4 <= ELEMENTWISE_TILE_BYTES or R % 8 else 8


def _ffn_up(xn, wgu):
    S, D = xn.shape
    J, _, F2 = wgu.shape
    F = F2 // 2
    tm = _row_tile(S, 256)

    def body(x_ref, w_ref, gu_ref, h_ref):
        gu = _dot(x_ref[...], w_ref[...])
        g, u = gu[:, :F], gu[:, F:]
        gu_ref[...] = gu.astype(BF16)
        h_ref[...] = (g * _sigmoid(g) * u).astype(BF16)

    return pl.pallas_call(
        body, name="ffn_up", grid=(J, S // tm),
        in_specs=[pl.BlockSpec((tm, D), lambda j, i: (i, 0)),
                  pl.BlockSpec((None, D, F2), lambda j, i: (j, 0, 0))],
        out_specs=[pl.BlockSpec((None, tm, F2), lambda j, i: (j, i, 0)),
                   pl.BlockSpec((None, tm, F), lambda j, i: (j, i, 0))],
        out_shape=[jax.ShapeDtypeStruct((J, S, F2), BF16), jax.ShapeDtypeStruct((J, S, F), BF16)],
        compiler_params=_params(),
    )(xn, wgu)


def _ffn_bwd_hidden(df, wd, gu):
    S, D = df.shape
    J, F, _ = wd.shape
    F2 = 2 * F
    tm = _row_tile(S, 512)

    def body(df_ref, w_ref, gu_ref, o_ref):
        dh = _dot_nt(df_ref[...], w_ref[...])
        g = gu_ref[:, :F].astype(F32)
        u = gu_ref[:, F:].astype(F32)
        sg = _sigmoid(g)
        o_ref[:, :F] = (dh * u * (sg * (1.0 + g * (1.0 - sg)))).astype(BF16)
        o_ref[:, F:] = (dh * (g * sg)).astype(BF16)

    return pl.pallas_call(
        body, name="ffn_bwd_hidden", grid=(J, S // tm),
        in_specs=[pl.BlockSpec((tm, D), lambda j, i: (i, 0)),
                  pl.BlockSpec((None, F, D), lambda j, i: (j, 0, 0)),
                  pl.BlockSpec((None, tm, F2), lambda j, i: (j, i, 0))],
        out_specs=pl.BlockSpec((None, tm, F2), lambda j, i: (j, i, 0)),
        out_shape=jax.ShapeDtypeStruct((J, S, F2), BF16),
        compiler_params=_params(),
    )(df, wd, gu)


def _norm_fwd(x, g):
    S, D = x.shape
    tm = _row_tile(S, 512)

    def body(x_ref, g_ref, o_ref):
        xv = x_ref[...]
        r = lax.rsqrt(jnp.mean(xv * xv, axis=-1, keepdims=True) + EPS)
        o_ref[...] = (xv * r * g_ref[...]).astype(BF16)

    return pl.pallas_call(
        body, name="norm_fwd", grid=(S // tm,),
        in_specs=[pl.BlockSpec((tm, D), lambda i: (i, 0)), pl.BlockSpec((1, D), lambda i: (0, 0))],
        out_specs=pl.BlockSpec((tm, D), lambda i: (i, 0)),
        out_shape=jax.ShapeDtypeStruct((S, D), BF16),
        compiler_params=_params(),
    )(x, g.reshape(1, D))


def _resid_norm_fwd(x, f, g, alpha):
    S, D = x.shape
    tm = _row_tile(S, 512)

    def body(x_ref, f_ref, g_ref, o_ref):
        fv = f_ref[...]
        r = lax.rsqrt(jnp.mean(fv * fv, axis=-1, keepdims=True) + EPS)
        o_ref[...] = x_ref[...] + alpha * (fv * r * g_ref[...])

    return pl.pallas_call(
        body, name="resid_norm_fwd", grid=(S // tm,),
        in_specs=[pl.BlockSpec((tm, D), lambda i: (i, 0)), pl.BlockSpec((tm, D), lambda i: (i, 0)),
                  pl.BlockSpec((1, D), lambda i: (0, 0))],
        out_specs=pl.BlockSpec((tm, D), lambda i: (i, 0)),
        out_shape=jax.ShapeDtypeStruct((S, D), F32),
        compiler_params=_params(),
    )(x, f, g.reshape(1, D))


def _norm_bwd(x, g, dy, alpha, resid=None, out_dtype=F32):
    S, D = x.shape
    tm = _row_tile(S, 256)
    has_resid = resid is not None

    def body(*refs):
        x_ref, g_ref, dy_ref = refs[:3]
        res_ref = refs[3] if has_resid else None
        dx_ref, dg_ref = refs[-2:]
        xv = x_ref[...]
        r = lax.rsqrt(jnp.mean(xv * xv, axis=-1, keepdims=True) + EPS)
        xh = xv * r
        dyv = dy_ref[...].astype(F32) * alpha
        gdy = dyv * g_ref[...]
        dx = r * (gdy - xh * jnp.mean(xh * gdy, axis=-1, keepdims=True))
        if has_resid:
            dx = dx + res_ref[...]
        dx_ref[...] = dx.astype(out_dtype)
        part = jnp.sum(dyv * xh, axis=0, keepdims=True)

        @pl.when(pl.program_id(0) == 0)
        def _():
            dg_ref[...] = part

        @pl.when(pl.program_id(0) > 0)
        def _():
            dg_ref[...] += part

    row = pl.BlockSpec((tm, D), lambda i: (i, 0))
    vec = pl.BlockSpec((1, D), lambda i: (0, 0))
    ins = [x, g.reshape(1, D), dy] + ([resid] if has_resid else [])
    dx, dg = pl.pallas_call(
        body, name="norm_bwd_res" if has_resid else "norm_bwd", grid=(S // tm,),
        in_specs=[row, vec, row] + ([row] if has_resid else []),
        out_specs=[row, vec],
        out_shape=[jax.ShapeDtypeStruct((S, D), out_dtype), jax.ShapeDtypeStruct((1, D), F32)],
        compiler_params=_params(),
    )(*ins)
    return dx, dg.reshape(D)


def _loss_grad(y, target):
    S, D = y.shape
    tm = _row_tile(S, 512)

    def body(y_ref, t_ref, dy_ref, l_ref):
        err = y_ref[...] - t_ref[...]
        dy_ref[...] = err * (1.0 / D)
        part = jnp.sum(jnp.sum(err * err, axis=-1, keepdims=True) * (0.5 / D), axis=0, keepdims=True)
        part = jnp.broadcast_to(part, (8, LANES))

        @pl.when(pl.program_id(0) == 0)
        def _():
            l_ref[...] = part

        @pl.when(pl.program_id(0) > 0)
        def _():
            l_ref[...] += part

    row = pl.BlockSpec((tm, D), lambda i: (i, 0))
    return pl.pallas_call(
        body, name="loss_grad", grid=(S // tm,),
        in_specs=[row, row],
        out_specs=[row, pl.BlockSpec((8, LANES), lambda i: (0, 0))],
        out_shape=[jax.ShapeDtypeStruct((S, D), F32), jax.ShapeDtypeStruct((8, LANES), F32)],
        compiler_params=_params(),
    )(y, target)


def _add_n(xs, name):
    shape = xs[0].shape
    C = shape[-1]
    R = 1
    for s in shape[:-1]:
        R *= s
    tm = _elementwise_rows(R, C)

    def body(*refs):
        acc = refs[0][...]
        for r in refs[1:-1]:
            acc = acc + r[...]
        refs[-1][...] = acc

    row = pl.BlockSpec((tm, C), lambda i: (i, 0))
    out = pl.pallas_call(
        body, name=name, grid=(R // tm,),
        in_specs=[row] * len(xs), out_specs=row,
        out_shape=jax.ShapeDtypeStruct((R, C), F32),
        compiler_params=_params(),
    )(*[x.reshape(R, C) for x in xs])
    return out.reshape(shape)


def _pair_sum(split, got):
    J, _, r, C = split.shape
    tm = _elementwise_rows(r, C)
    core = lax.axis_index("c").astype(jnp.int32).reshape(1)

    def body(core_ref, a_ref, b_ref, o_ref):
        o_ref[...] = a_ref[...] + b_ref[...]

    row = pl.BlockSpec((None, tm, C), lambda j, i, core_ref: (j, i, 0))
    return pl.pallas_call(
        body, name="pair_sum",
        grid_spec=pltpu.PrefetchScalarGridSpec(
            num_scalar_prefetch=1, grid=(J, r // tm),
            in_specs=[pl.BlockSpec((None, None, tm, C), lambda j, i, core_ref: (j, core_ref[0], i, 0)), row],
            out_specs=row),
        out_shape=jax.ShapeDtypeStruct((J, r, C), F32),
        compiler_params=_params(),
    )(core, split, got)


def _chip_sum(pair, arrived):
    _, r, C = pair.shape
    tm = _elementwise_rows(r, C)
    chip = (2 * lax.axis_index("x") + lax.axis_index("y")).astype(jnp.int32).reshape(1)

    def body(chip_ref, own_ref, a_ref, b_ref, c_ref, o_ref):
        o_ref[...] = own_ref[...] + a_ref[...] + b_ref[...] + c_ref[...]

    part = lambda k: pl.BlockSpec((None, tm, C), lambda i, chip_ref: (k, i, 0))
    return pl.pallas_call(
        body, name="chip_sum",
        grid_spec=pltpu.PrefetchScalarGridSpec(
            num_scalar_prefetch=1, grid=(r // tm,),
            in_specs=[pl.BlockSpec((None, tm, C), lambda i, chip_ref: (chip_ref[0], i, 0)), part(0), part(1), part(2)],
            out_specs=pl.BlockSpec((tm, C), lambda i, chip_ref: (i, 0))),
        out_shape=jax.ShapeDtypeStruct((r, C), F32),
        compiler_params=_params(),
    )(chip, pair, arrived, arrived, arrived)


def _rope_tables(S):
    half = ATT_HEAD_DIM // 2
    inv_freq = ROPE_THETA ** (-jnp.arange(half, dtype=F32) / half)
    ang = jnp.arange(S, dtype=F32)[:, None] * inv_freq[None, :]
    cos, sin = jnp.cos(ang), jnp.sin(ang)
    return jnp.concatenate([cos, cos], axis=-1), jnp.concatenate([-sin, sin], axis=-1)


def _rotate(x, cosf, sinf):
    return x * cosf + pltpu.roll(x, ATT_HEAD_DIM // 2, 1) * sinf


def _rope_fwd(proj, cosf, sinf):
    S = proj.shape[0]
    tm = _row_tile(S, 512)

    def body(q_ref, k_ref, v_ref, c_ref, s_ref, qo_ref, ko_ref, vo_ref):
        c, s = c_ref[...], s_ref[...]
        for h in range(ATT_HEADS):
            sl = slice(h * ATT_HEAD_DIM, (h + 1) * ATT_HEAD_DIM)
            qo_ref[:, sl] = _rotate(q_ref[:, sl], c, s).astype(BF16)
        for h in range(ATT_KV_HEADS):
            sl = slice(h * ATT_HEAD_DIM, (h + 1) * ATT_HEAD_DIM)
            ko_ref[:, sl] = _rotate(k_ref[:, sl], c, s).astype(BF16)
        vo_ref[...] = v_ref[...].astype(BF16)

    tab = pl.BlockSpec((tm, ATT_HEAD_DIM), lambda i: (i, 0))
    return pl.pallas_call(
        body, name="rope_fwd", grid=(S // tm,),
        in_specs=[pl.BlockSpec((tm, ATT_WIDTH), lambda i: (i, 0)),
                  pl.BlockSpec((tm, KV_WIDTH), lambda i: (i, OFF_KA // KV_WIDTH)),
                  pl.BlockSpec((tm, KV_WIDTH), lambda i: (i, OFF_VA // KV_WIDTH)), tab, tab],
        out_specs=[pl.BlockSpec((tm, ATT_WIDTH), lambda i: (i, 0)),
                   pl.BlockSpec((tm, KV_WIDTH), lambda i: (i, 0)),
                   pl.BlockSpec((tm, KV_WIDTH), lambda i: (i, 0))],
        out_shape=[jax.ShapeDtypeStruct((S, ATT_WIDTH), BF16), jax.ShapeDtypeStruct((S, KV_WIDTH), BF16),
                   jax.ShapeDtypeStruct((S, KV_WIDTH), BF16)],
        compiler_params=_params(),
    )(proj, proj, proj, cosf, sinf)


def _rope_bwd(dq, dk, dv, cosf, sinf):
    S = dq.shape[0]
    tm = _row_tile(S, 512)
    W = ATT_WIDTH + 2 * KV_WIDTH

    def body(q_ref, k_ref, v_ref, c_ref, s_ref, o_ref):
        c, s = c_ref[...], -s_ref[...]
        for h in range(ATT_HEADS):
            sl = slice(h * ATT_HEAD_DIM, (h + 1) * ATT_HEAD_DIM)
            o_ref[:, sl] = _rotate(q_ref[:, sl], c, s).astype(BF16)
        for h in range(ATT_KV_HEADS):
            sl = slice(h * ATT_HEAD_DIM, (h + 1) * ATT_HEAD_DIM)
            o_ref[:, ATT_WIDTH + h * ATT_HEAD_DIM:ATT_WIDTH + (h + 1) * ATT_HEAD_DIM] = (
                _rotate(k_ref[:, sl], c, s).astype(BF16))
        o_ref[:, ATT_WIDTH + KV_WIDTH:] = v_ref[...].astype(BF16)

    tab = pl.BlockSpec((tm, ATT_HEAD_DIM), lambda i: (i, 0))
    return pl.pallas_call(
        body, name="rope_bwd", grid=(S // tm,),
        in_specs=[pl.BlockSpec((tm, ATT_WIDTH), lambda i: (i, 0)),
                  pl.BlockSpec((tm, KV_WIDTH), lambda i: (i, 0)),
                  pl.BlockSpec((tm, KV_WIDTH), lambda i: (i, 0)), tab, tab],
        out_specs=pl.BlockSpec((tm, W), lambda i: (i, 0)),
        out_shape=jax.ShapeDtypeStruct((S, W), BF16),
        compiler_params=_params(),
    )(dq, dk, dv, cosf, sinf)


def _attn_probs(q_ref, k_refs, sink_ref, kh, n, nb):
    G, L, Dh = ATT_GROUP, BLK, ATT_HEAD_DIM
    q4 = jnp.concatenate([q_ref[:, (kh * G + g) * Dh:(kh * G + g + 1) * Dh] for g in range(G)], axis=0)
    kcat = jnp.concatenate([r[:, kh * Dh:(kh + 1) * Dh] for r in k_refs], axis=0)
    s = _dot_nt(q4, kcat) * (Dh ** -0.5)
    row = lax.broadcasted_iota(jnp.int32, (G * L, 3 * L), 0) % L
    col = lax.broadcasted_iota(jnp.int32, (G * L, 3 * L), 1)
    kpos = (n - 1) * L + col
    mask = (jnp.abs(col - L - row) <= L) & (kpos >= 0) & (kpos < nb * L)
    s = jnp.where(mask, s, -jnp.inf)
    sink = jnp.concatenate([jnp.broadcast_to(sink_ref[kh, :, g:g + 1], (L, 1)) for g in range(G)], axis=0)
    m = jnp.maximum(jnp.max(s, axis=-1, keepdims=True), sink)
    p = jnp.exp(s - m)
    es = jnp.exp(sink - m)
    inv = 1.0 / (jnp.sum(p, axis=-1, keepdims=True) + es)
    return q4, kcat, p * inv, es * inv


def _kv_specs(nb):
    return [pl.BlockSpec((BLK, KV_WIDTH), lambda n: (jnp.maximum(n - 1, 0), 0)),
            pl.BlockSpec((BLK, KV_WIDTH), lambda n: (n, 0)),
            pl.BlockSpec((BLK, KV_WIDTH), lambda n: (jnp.minimum(n + 1, nb - 1), 0))]


def _attn_fwd(qr, kr, va, sink):
    S = qr.shape[0]
    nb = S // BLK
    G, Dh = ATT_GROUP, ATT_HEAD_DIM

    def body(q_ref, k0, k1, k2, v0, v1, v2, sink_ref, o_ref):
        n = pl.program_id(0)
        for kh in range(ATT_KV_HEADS):
            _, _, probs, _ = _attn_probs(q_ref, (k0, k1, k2), sink_ref, kh, n, nb)
            vcat = jnp.concatenate([r[:, kh * Dh:(kh + 1) * Dh] for r in (v0, v1, v2)], axis=0)
            out = _dot(probs.astype(BF16), vcat)
            for g in range(G):
                o_ref[:, (kh * G + g) * Dh:(kh * G + g + 1) * Dh] = out[g * BLK:(g + 1) * BLK, :].astype(BF16)

    qspec = pl.BlockSpec((BLK, ATT_WIDTH), lambda n: (n, 0))
    return pl.pallas_call(
        body, name="attn_fwd", grid=(nb,),
        in_specs=[qspec] + _kv_specs(nb) + _kv_specs(nb)
        + [pl.BlockSpec((ATT_KV_HEADS, 1, G), lambda n: (0, 0, 0))],
        out_specs=qspec,
        out_shape=jax.ShapeDtypeStruct((S, ATT_WIDTH), BF16),
        compiler_params=_params(),
    )(qr, kr, kr, kr, va, va, va, sink.reshape(ATT_KV_HEADS, 1, G))


def _attn_bwd(qr, kr, va, sink, dycat):
    S = qr.shape[0]
    nb = S // BLK
    G, L, Dh = ATT_GROUP, BLK, ATT_HEAD_DIM
    SP = S + 2 * L

    def body(q_ref, k0, k1, k2, v0, v1, v2, sink_ref, do_ref, dq_ref, dk_ref, dv_ref, ds_ref):
        n = pl.program_id(0)

        @pl.when(n == 0)
        def _():
            dk_ref[...] = jnp.zeros_like(dk_ref)
            dv_ref[...] = jnp.zeros_like(dv_ref)
            ds_ref[...] = jnp.zeros_like(ds_ref)

        rows = pl.ds(pl.multiple_of(n * L, L), 3 * L)
        lane = lax.broadcasted_iota(jnp.int32, (8, LANES), 1)
        for kh in range(ATT_KV_HEADS):
            q4, kcat, probs, psink = _attn_probs(q_ref, (k0, k1, k2), sink_ref, kh, n, nb)
            vcat = jnp.concatenate([r[:, kh * Dh:(kh + 1) * Dh] for r in (v0, v1, v2)], axis=0)
            do4 = jnp.concatenate([do_ref[:, (kh * G + g) * Dh:(kh * G + g + 1) * Dh] for g in range(G)], axis=0)
            pb = probs.astype(BF16)
            dob = do4.astype(BF16)
            out = _dot(pb, vcat)
            delta = jnp.sum(do4 * out, axis=-1, keepdims=True)
            dp = _dot_nt(dob, vcat)
            dsc = (probs * (dp - delta) * (Dh ** -0.5)).astype(BF16)
            dq4 = _dot(dsc, kcat)
            for g in range(G):
                dq_ref[:, (kh * G + g) * Dh:(kh * G + g + 1) * Dh] = dq4[g * L:(g + 1) * L, :]
            dk_ref[rows, kh * Dh:(kh + 1) * Dh] += _dot_tn(dsc, q4)
            dv_ref[rows, kh * Dh:(kh + 1) * Dh] += _dot_tn(pb, dob)
            dsink = jnp.zeros((8, LANES), F32)
            for g in range(G):
                val = -jnp.sum(psink[g * L:(g + 1) * L] * delta[g * L:(g + 1) * L], axis=0, keepdims=True)
                dsink = dsink + jnp.where(lane == g, jnp.broadcast_to(val, (8, LANES)), 0.0)
            ds_ref[kh] += dsink

    qspec = pl.BlockSpec((L, ATT_WIDTH), lambda n: (n, 0))
    accspec = pl.BlockSpec((SP, KV_WIDTH), lambda n: (0, 0))
    dq, dkp, dvp, dsink = pl.pallas_call(
        body, name="attn_bwd", grid=(nb,),
        in_specs=[qspec] + _kv_specs(nb) + _kv_specs(nb)
        + [pl.BlockSpec((ATT_KV_HEADS, 1, G), lambda n: (0, 0, 0)), qspec],
        out_specs=[qspec, accspec, accspec, pl.BlockSpec((ATT_KV_HEADS, 8, LANES), lambda n: (0, 0, 0))],
        out_shape=[jax.ShapeDtypeStruct((S, ATT_WIDTH), F32), jax.ShapeDtypeStruct((SP, KV_WIDTH), F32),
                   jax.ShapeDtypeStruct((SP, KV_WIDTH), F32),
                   jax.ShapeDtypeStruct((ATT_KV_HEADS, 8, LANES), F32)],
        compiler_params=_params(),
    )(qr, kr, kr, kr, va, va, va, sink.reshape(ATT_KV_HEADS, 1, G), dycat)
    return dq, dkp[L:L + S], dvp[L:L + S], dsink[:, 0, :G].reshape(ATT_HEADS)


CONV_COLS = 256
CONV_ROWS = 512


def _conv_taps(xs, w_ref, rows):
    total = xs.shape[0]
    acc = None
    for j in range(CONV_WIDTH):
        shift = (CONV_WIDTH // 2 - j) % total
        term = (pltpu.roll(xs, shift, 0) if shift else xs)[CONV_HALO:CONV_HALO + rows, :] * w_ref[j:j + 1, :]
        acc = term if acc is None else acc + term
    return acc


def _conv_window(x_ref, i, R, nrow):
    S = x_ref.shape[0]
    r0 = pl.multiple_of(i * R, R)
    top = x_ref[pl.ds(pl.multiple_of(jnp.maximum(r0 - CONV_HALO, 0), CONV_HALO), CONV_HALO), :]
    bot = x_ref[pl.ds(pl.multiple_of(jnp.minimum(r0 + R, S - CONV_HALO), CONV_HALO), CONV_HALO), :]
    top = jnp.where(i > 0, top, 0.0)
    bot = jnp.where(i < nrow - 1, bot, 0.0)
    return jnp.concatenate([top, x_ref[pl.ds(r0, R), :], bot], axis=0)


def _conv5(x, w, *, col0, act, out_dtype, name):
    S = x.shape[0]
    C = w.shape[1]
    R = _row_tile(S, CONV_ROWS)
    tc = CONV_COLS
    half_blocks = (C // 2) // tc
    nrow = S // R

    def body(x_ref, w_ref, o_ref):
        scale = jnp.where(pl.program_id(0) >= half_blocks, M_HEAD_DIM ** -0.5, 1.0)
        y = _conv_taps(_conv_window(x_ref, pl.program_id(1), R, nrow), w_ref, R)
        if act:
            y = y * _sigmoid(y) * scale
        o_ref[...] = y.astype(out_dtype)

    return pl.pallas_call(
        body, name=name, grid=(C // tc, nrow),
        in_specs=[pl.BlockSpec((S, tc), lambda c, i: (0, col0 // tc + c)),
                  pl.BlockSpec((CONV_WIDTH, tc), lambda c, i: (0, c))],
        out_specs=pl.BlockSpec((R, tc), lambda c, i: (i, c)),
        out_shape=jax.ShapeDtypeStruct((S, C), out_dtype),
        compiler_params=_params(),
    )(x, w)


def _conv_bwd_pre(x, col0, w, dq2, dk2):
    S = x.shape[0]
    C = w.shape[1]
    R = _row_tile(S, CONV_ROWS)
    tc = CONV_COLS
    half_blocks = (C // 2) // tc
    nrow = S // R

    def body(x_ref, w_ref, dqa_ref, dqb_ref, dka_ref, dkb_ref, o_ref, dw_ref, acc):
        is_k = pl.program_id(0) >= half_blocks
        i = pl.program_id(1)
        scale = jnp.where(is_k, M_HEAD_DIM ** -0.5, 1.0)

        @pl.when(i == 0)
        def _():
            acc[...] = jnp.zeros_like(acc)

        xs = _conv_window(x_ref, i, R, nrow)
        y = _conv_taps(xs, w_ref, R)
        sg = _sigmoid(y)
        dqv = dqa_ref[...] + dqb_ref[...]
        dkv = dka_ref[...] + dkb_ref[...]
        dpre = jnp.where(is_k, dkv, dqv) * scale * (sg * (1.0 + y * (1.0 - sg)))
        o_ref[...] = dpre
        total = xs.shape[0]
        for j in range(CONV_WIDTH):
            shift = (CONV_WIDTH // 2 - j) % total
            xj = (pltpu.roll(xs, shift, 0) if shift else xs)[CONV_HALO:CONV_HALO + R, :]
            acc[j:j + 1, :] += jnp.sum(dpre * xj, axis=0, keepdims=True)

        @pl.when(i == nrow - 1)
        def _():
            dw_ref[...] = acc[0:CONV_WIDTH, :]

    nqb = (C // 2) // tc
    qmap = lambda d: (lambda c, i: (d, i, jnp.minimum(c, nqb - 1)))
    kmap = lambda d: (lambda c, i: (d, i, jnp.maximum(c - nqb, 0)))
    gspec = lambda m: pl.BlockSpec((None, R, tc), m)
    return pl.pallas_call(
        body, name="conv_bwd_pre", grid=(C // tc, nrow),
        in_specs=[pl.BlockSpec((S, tc), lambda c, i: (0, col0 // tc + c)),
                  pl.BlockSpec((CONV_WIDTH, tc), lambda c, i: (0, c)),
                  gspec(qmap(0)), gspec(qmap(1)), gspec(kmap(0)), gspec(kmap(1))],
        out_specs=[pl.BlockSpec((R, tc), lambda c, i: (i, c)), pl.BlockSpec((CONV_WIDTH, tc), lambda c, i: (0, c))],
        out_shape=[jax.ShapeDtypeStruct((S, C), F32), jax.ShapeDtypeStruct((CONV_WIDTH, C), F32)],
        scratch_shapes=[pltpu.VMEM((8, tc), F32)],
        compiler_params=_params(),
    )(x, w, dq2, dq2, dk2, dk2)


def _lane_cumsum(x, reverse):
    lane = lax.broadcasted_iota(jnp.int32, x.shape, 1)
    sh = 1
    while sh < LANES:
        if reverse:
            x = x + jnp.where(lane < LANES - sh, pltpu.roll(x, LANES - sh, 1), 0.0)
        else:
            x = x + jnp.where(lane >= sh, pltpu.roll(x, sh, 1), 0.0)
        sh *= 2
    return x


def _gate_prep(gates_t, bias):
    R = gates_t.shape[0]
    half, quarter = R // 2, R // 4

    def body(g_ref, b_ref, ig_ref, cum_ref):
        ig_ref[...] = g_ref[0:half, :] + b_ref[0:half, :]
        fg = g_ref[half:R, :] + b_ref[half:R, :]
        lf = jnp.minimum(fg, 0.0) - jnp.log(1.0 + jnp.exp(-jnp.abs(fg)))
        cum_ref[0:quarter, :] = _lane_cumsum(lf[0:quarter, :], False)
        cum_ref[quarter:half, :] = _lane_cumsum(lf[quarter:half, :], True)

    return pl.pallas_call(
        body, name="gate_prep",
        out_shape=[jax.ShapeDtypeStruct((half, LANES), F32), jax.ShapeDtypeStruct((half, LANES), F32)],
        compiler_params=_params(),
    )(gates_t, bias)


def _gate_bwd(gates_t, bias, di, dfc, dbl, nc):
    R = gates_t.shape[0]
    half, quarter = R // 2, R // 4

    def body(g_ref, b_ref, di_ref, df_ref, dbl_ref, dg_ref, db_ref):
        dg_ref[0:half, :] = di_ref[...]
        dfv = df_ref[...]
        within = jnp.concatenate([_lane_cumsum(dfv[0:quarter, :], True),
                                  _lane_cumsum(dfv[quarter:half, :], False)], axis=0)
        fg = g_ref[half:R, :] + b_ref[half:R, :]
        dg_ref[half:R, :] = (within + dbl_ref[...]) * _sigmoid(-fg)
        rows = jnp.broadcast_to(jnp.sum(dg_ref[...], axis=-1, keepdims=True), (R, LANES))
        gr = lax.broadcasted_iota(jnp.int32, (N_GATES, R), 0)
        gc = lax.broadcasted_iota(jnp.int32, (N_GATES, R), 1)
        db_ref[...] = lax.dot_general((gc // nc == gr).astype(F32), rows, (((1,), (0,)), ((), ())),
                                      precision=lax.Precision.HIGHEST, preferred_element_type=F32)

    return pl.pallas_call(
        body, name="gate_bwd",
        out_shape=[jax.ShapeDtypeStruct((R, LANES), F32), jax.ShapeDtypeStruct((N_GATES, LANES), F32)],
        compiler_params=_params(),
    )(gates_t, bias, di, dfc, dbl)


def _chunk_index(d, c, nc, reverse):
    j = (nc - 1 - c) if reverse else c
    return j + d * (nc - 1 - 2 * j)


def _mlstm_chunk(d, q, k, vb, brow, igrow, C, nvec, m_prev, eye):
    L = BLK
    r = lax.broadcasted_iota(jnp.int32, (L, L), 0)
    c = lax.broadcasted_iota(jnp.int32, (L, L), 1)
    mask = (r - c) * (1 - 2 * d) >= 0
    bcol = _row_to_col(brow, eye)
    igcol = _row_to_col(igrow, eye)
    log_d = jnp.where(mask, bcol - brow + igrow, -jnp.inf)
    log_inter = bcol + m_prev
    m_t = jnp.maximum(log_inter, jnp.max(log_d, axis=-1, keepdims=True))
    d_mat = jnp.exp(log_d - m_t)
    inter = jnp.exp(log_inter - m_t)
    s = _dot_nt(q, k) * d_mat
    sb = s.astype(BF16)
    cb = C.astype(BF16)
    num = _dot(sb, vb) + inter * _dot_nt(q, cb)
    den = jnp.sum(s, axis=-1, keepdims=True) + inter * jnp.sum(q.astype(F32) * nvec, axis=-1, keepdims=True)
    floor = jnp.exp(-m_t)
    denom = jnp.maximum(jnp.abs(den), floor)
    b_last = jnp.where(d == 0, brow[:, L - 1:L], brow[:, 0:1])
    return dict(bcol=bcol, igcol=igcol, m_t=m_t, d_mat=d_mat, inter=inter, sb=sb, cb=cb, num=num, den=den,
                floor=floor, denom=denom, b_last=b_last)


def _head_cols(v0_ref, v1_ref, hd):
    ref = v0_ref if hd < M_HEADS // 2 else v1_ref
    lo = (hd % (M_HEADS // 2)) * M_HEAD_DIM
    return ref[:, lo:lo + M_HEAD_DIM]


def _mlstm_fwd(qk, proj, ig5, b5):
    S = qk.shape[0]
    nc = S // BLK
    L, Dh, H = BLK, M_HEAD_DIM, M_HEADS

    def body(q_ref, k_ref, v0_ref, v1_ref, ig_ref, b_ref, h_ref, cs_ref, ns_ref, ms_ref, C, nvec, m):
        d = pl.program_id(0)

        @pl.when(pl.program_id(1) == 0)
        def _():
            C[...] = jnp.zeros_like(C)
            nvec[...] = jnp.zeros_like(nvec)
            m[...] = jnp.zeros_like(m)

        eye = _eye_mask(L)
        for hd in range(H):
            cols = slice(hd * Dh, (hd + 1) * Dh)
            q, k = q_ref[:, cols], k_ref[:, cols]
            vf = _head_cols(v0_ref, v1_ref, hd)
            brow, igrow = b_ref[hd], ig_ref[hd]
            m_prev = m[hd, :, 0:1]
            cs_ref[hd] = C[hd].astype(BF16)
            ns_ref[hd] = nvec[hd]
            ms_ref[hd] = m[hd]
            f = _mlstm_chunk(d, q, k, vf.astype(BF16), brow, igrow, C[hd], nvec[hd], m_prev, eye)
            h_ref[:, cols] = f["num"] / f["denom"]
            b_last = f["b_last"]
            log_w = b_last - brow + igrow
            m_new = jnp.maximum(b_last + m_prev, jnp.max(log_w, axis=-1, keepdims=True))
            w_col = jnp.exp(b_last - f["bcol"] + f["igcol"] - m_new)
            decay = jnp.exp(b_last + m_prev - m_new)
            C[hd] = decay * C[hd] + _dot_tn((w_col * vf).astype(BF16), k)
            nvec[hd] = decay * nvec[hd] + jnp.sum(w_col * k.astype(F32), axis=0, keepdims=True)
            m[hd] = jnp.broadcast_to(m_new, (1, LANES))

    cidx = lambda d, c: _chunk_index(d, c, nc, False)
    gspec = pl.BlockSpec((None, H, None, 1, LANES), lambda d, c: (d, 0, cidx(d, c), 0, 0))
    st = lambda *blk: pl.BlockSpec((None, H, None) + blk, lambda d, c: (d, 0, cidx(d, c), 0, 0))
    half = M_WIDTH // 2
    return pl.pallas_call(
        body, name="mlstm_fwd", grid=(2, nc),
        in_specs=[pl.BlockSpec((L, M_WIDTH), lambda d, c: (cidx(d, c), 0)),
                  pl.BlockSpec((L, M_WIDTH), lambda d, c: (cidx(d, c), 1)),
                  pl.BlockSpec((L, half), lambda d, c: (cidx(d, c), OFF_VM // half)),
                  pl.BlockSpec((L, half), lambda d, c: (cidx(d, c), OFF_VM // half + 1)), gspec, gspec],
        out_specs=[pl.BlockSpec((None, L, M_WIDTH), lambda d, c: (d, cidx(d, c), 0)),
                   st(Dh, Dh), st(1, Dh), st(1, LANES)],
        out_shape=[jax.ShapeDtypeStruct((2, S, M_WIDTH), F32), jax.ShapeDtypeStruct((2, H, nc, Dh, Dh), BF16),
                   jax.ShapeDtypeStruct((2, H, nc, 1, Dh), F32), jax.ShapeDtypeStruct((2, H, nc, 1, LANES), F32)],
        scratch_shapes=[pltpu.VMEM((H, Dh, Dh), F32), pltpu.VMEM((H, 1, Dh), F32), pltpu.VMEM((H, 1, LANES), F32)],
        compiler_params=_params(),
    )(qk, qk, proj, proj, ig5, b5)


def _mlstm_bwd(qk, proj, ig5, b5, cs, ns, ms, dhm):
    S = qk.shape[0]
    nc = S // BLK
    L, Dh, H = BLK, M_HEAD_DIM, M_HEADS

    def body(q_ref, k_ref, v0_ref, v1_ref, ig_ref, b_ref, cs_ref, ns_ref, ms_ref, dh_ref,
             dq_ref, dk_ref, dv_ref, di_ref, df_ref, dbl_ref, R, rvec, mu):
        d = pl.program_id(0)

        @pl.when(pl.program_id(1) == 0)
        def _():
            R[...] = jnp.zeros_like(R)
            rvec[...] = jnp.zeros_like(rvec)
            mu[...] = jnp.full(mu.shape, NEG_BIG, F32)

        eye = _eye_mask(L)
        for hd in range(H):
            cols = slice(hd * Dh, (hd + 1) * Dh)
            q, k = q_ref[:, cols], k_ref[:, cols]
            qf, kf = q.astype(F32), k.astype(F32)
            vb = _head_cols(v0_ref, v1_ref, hd).astype(BF16)
            brow, igrow = b_ref[hd], ig_ref[hd]
            nprev = ns_ref[hd]
            m_prev = ms_ref[hd, :, 0:1]
            cprev = cs_ref[hd]
            f = _mlstm_chunk(d, q, k, vb, brow, igrow, cprev, nprev, m_prev, eye)
            dh = dh_ref[:, cols]
            inv = 1.0 / f["denom"]
            hcur = f["num"] * inv
            dnum = dh * inv
            active = jnp.abs(f["den"]) >= f["floor"]
            dden = (jnp.where(active, -jnp.sign(f["den"]), 0.0)
                    * jnp.sum(dh * hcur, axis=-1, keepdims=True) * inv)
            dnb = dnum.astype(BF16)
            dqk = ((_dot_nt(dnb, vb) + dden) * f["d_mat"]).astype(BF16)
            dq = _dot(dqk, k) + f["inter"] * (_dot(dnb, f["cb"]) + dden * nprev)
            mu_prev = mu[hd, :, 0:1]
            a_col = jnp.exp(f["b_last"] - f["bcol"] + f["igcol"] + mu_prev)
            rb = R[hd].astype(BF16)
            dv = _dot_tn(f["sb"], dnb) + a_col * _dot_nt(k, rb)
            dk_inter = a_col * (_dot(vb, rb) + rvec[hd])
            dk = _dot_tn(dqk, q) + dk_inter
            dq_ref[:, cols] = dq
            dk_ref[:, cols] = dk
            dv_ref[:, cols] = dv
            kdk = jnp.sum(kf * dk, axis=-1, keepdims=True)
            qdq = jnp.sum(qf * dq, axis=-1, keepdims=True)
            di_ref[hd] = _col_to_row(kdk, eye)
            df_ref[hd] = _col_to_row(qdq - kdk, eye)
            older = jnp.sum(jnp.sum(R[hd] * cprev.astype(F32), axis=-1, keepdims=True), axis=0, keepdims=True)
            older = older + jnp.sum(rvec[hd] * nprev, axis=-1, keepdims=True)
            dbl = (jnp.sum(jnp.sum(kf * dk_inter, axis=-1, keepdims=True), axis=0, keepdims=True)
                   + jnp.exp(f["b_last"] + mu_prev + m_prev) * older)
            dbl_ref[hd] = jnp.broadcast_to(dbl, (1, LANES))
            lw = f["bcol"] - f["m_t"]
            mu_new = jnp.maximum(f["b_last"] + mu_prev, jnp.max(lw, axis=0, keepdims=True))
            wq = jnp.exp(lw - mu_new)
            decay = jnp.exp(f["b_last"] + mu_prev - mu_new)
            R[hd] = decay * R[hd] + _dot_tn((wq * dnum).astype(BF16), q)
            rvec[hd] = decay * rvec[hd] + jnp.sum(wq * dden * qf, axis=0, keepdims=True)
            mu[hd] = jnp.broadcast_to(mu_new, (1, LANES))

    cidx = lambda d, c: _chunk_index(d, c, nc, True)
    gin = pl.BlockSpec((None, H, None, 1, LANES), lambda d, c: (d, 0, cidx(d, c), 0, 0))
    st = lambda *blk: pl.BlockSpec((None, H, None) + blk, lambda d, c: (d, 0, cidx(d, c), 0, 0))
    per_dir = pl.BlockSpec((None, L, M_WIDTH), lambda d, c: (d, cidx(d, c), 0))
    big = jax.ShapeDtypeStruct((2, S, M_WIDTH), F32)
    small = jax.ShapeDtypeStruct((2, H, nc, 1, LANES), F32)
    half = M_WIDTH // 2
    return pl.pallas_call(
        body, name="mlstm_bwd", grid=(2, nc),
        in_specs=[pl.BlockSpec((L, M_WIDTH), lambda d, c: (cidx(d, c), 0)),
                  pl.BlockSpec((L, M_WIDTH), lambda d, c: (cidx(d, c), 1)),
                  pl.BlockSpec((L, half), lambda d, c: (cidx(d, c), OFF_VM // half)),
                  pl.BlockSpec((L, half), lambda d, c: (cidx(d, c), OFF_VM // half + 1)), gin, gin,
                  st(Dh, Dh), st(1, Dh), st(1, LANES),
                  pl.BlockSpec((L, M_WIDTH), lambda d, c: (cidx(d, c), 0))],
        out_specs=[per_dir, per_dir, per_dir, gin, gin, gin],
        out_shape=[big, big, big, small, small, small],
        scratch_shapes=[pltpu.VMEM((H, Dh, Dh), F32), pltpu.VMEM((H, 1, Dh), F32), pltpu.VMEM((H, 1, LANES), F32)],
        compiler_params=_params(),
    )(qk, qk, proj, proj, ig5, b5, cs, ns, ms, dhm)


def _mlstm_out_fwd(hs, proj, gamma):
    S = hs.shape[1]
    tm = _row_tile(S, 512)
    Dh, H = M_HEAD_DIM, M_HEADS

    def body(hf_ref, hb_ref, o_ref, g_ref, y_ref):
        hm = hf_ref[...] + hb_ref[...]
        r = lax.rsqrt(jnp.mean(hm * hm, axis=-1, keepdims=True) + EPS)
        y_ref[...] = (_sigmoid(o_ref[...]) * (hm * r * g_ref[...])).astype(BF16)

    hspec = lambda d: pl.BlockSpec((None, tm, Dh), lambda i, h: (d, i, h))
    return pl.pallas_call(
        body, name="mlstm_out_fwd", grid=(S // tm, H),
        in_specs=[hspec(0), hspec(1), pl.BlockSpec((tm, Dh), lambda i, h: (i, OFF_OM // Dh + h)),
                  pl.BlockSpec((1, Dh), lambda i, h: (0, h))],
        out_specs=pl.BlockSpec((tm, Dh), lambda i, h: (i, h)),
        out_shape=jax.ShapeDtypeStruct((S, M_WIDTH), BF16),
        compiler_params=_params(),
    )(hs, hs, proj, gamma.reshape(1, M_WIDTH))


def _mlstm_out_bwd(hs, proj, gamma, dycat):
    S = hs.shape[1]
    tm = _row_tile(S, 512)
    Dh, H = M_HEAD_DIM, M_HEADS
    D_OFF = ATT_WIDTH // Dh

    def body(hf_ref, hb_ref, o_ref, g_ref, dy_ref, dh_ref, do_ref, dg_ref):
        hm = hf_ref[...] + hb_ref[...]
        r = lax.rsqrt(jnp.mean(hm * hm, axis=-1, keepdims=True) + EPS)
        hh = hm * r
        so = _sigmoid(o_ref[...])
        dy = dy_ref[...]
        gam = g_ref[...]
        do_ref[...] = (dy * hh * gam * so * (1.0 - so)).astype(BF16)
        dyn = dy * so
        gd = dyn * gam
        dh_ref[...] = r * (gd - hh * jnp.mean(hh * gd, axis=-1, keepdims=True))
        part = jnp.sum(dyn * hh, axis=0, keepdims=True)

        @pl.when(pl.program_id(1) == 0)
        def _():
            dg_ref[...] = part

        @pl.when(pl.program_id(1) > 0)
        def _():
            dg_ref[...] += part

    hspec = lambda d: pl.BlockSpec((None, tm, Dh), lambda h, i: (d, i, h))
    out = pl.BlockSpec((tm, Dh), lambda h, i: (i, h))
    vec = pl.BlockSpec((1, Dh), lambda h, i: (0, h))
    dhm, do, dg = pl.pallas_call(
        body, name="mlstm_out_bwd", grid=(H, S // tm),
        in_specs=[hspec(0), hspec(1), pl.BlockSpec((tm, Dh), lambda h, i: (i, OFF_OM // Dh + h)), vec,
                  pl.BlockSpec((tm, Dh), lambda h, i: (i, D_OFF + h))],
        out_specs=[out, out, vec],
        out_shape=[jax.ShapeDtypeStruct((S, M_WIDTH), F32), jax.ShapeDtypeStruct((S, M_WIDTH), BF16),
                   jax.ShapeDtypeStruct((1, M_WIDTH), F32)],
        compiler_params=_params(),
    )(hs, hs, proj, gamma.reshape(1, M_WIDTH), dycat)
    return dhm, do, dg.reshape(M_WIDTH)


def _sum_cast(a2, name):
    _, S, C = a2.shape
    tm = _row_tile(S, 512)

    def body(a_ref, b_ref, o_ref):
        o_ref[...] = (a_ref[...] + b_ref[...]).astype(BF16)

    spec = lambda d: pl.BlockSpec((None, tm, C), lambda i: (d, i, 0))
    return pl.pallas_call(
        body, name=name, grid=(S // tm,),
        in_specs=[spec(0), spec(1)], out_specs=pl.BlockSpec((tm, C), lambda i: (i, 0)),
        out_shape=jax.ShapeDtypeStruct((S, C), BF16),
        compiler_params=_params(),
    )(a2, a2)


def _adamw(w, g, m, v):
    shape = w.shape
    C = shape[-1]
    R = w.size // C
    tm = _elementwise_rows(R, C)

    def body(w_ref, g_ref, m_ref, v_ref, d_ref, mo_ref, vo_ref):
        gv = g_ref[...]
        mn = ADAM_B1 * m_ref[...] + (1.0 - ADAM_B1) * gv
        vn = ADAM_B2 * v_ref[...] + (1.0 - ADAM_B2) * (gv * gv)
        m_hat = mn / (1.0 - ADAM_B1 ** ADAM_STEP)
        v_hat = vn / (1.0 - ADAM_B2 ** ADAM_STEP)
        d_ref[...] = -ADAM_LR * (m_hat / (jnp.sqrt(v_hat) + ADAM_EPS) + ADAM_WD * w_ref[...])
        mo_ref[...] = mn
        vo_ref[...] = vn

    row = pl.BlockSpec((tm, C), lambda i: (i, 0))
    sds = jax.ShapeDtypeStruct((R, C), F32)
    outs = pl.pallas_call(
        body, name="adamw", grid=(R // tm,),
        in_specs=[row] * 4, out_specs=[row] * 3, out_shape=[sds] * 3,
        compiler_params=_params(),
    )(*[t.reshape(R, C) for t in (w, g, m, v)])
    return tuple(o.reshape(shape) for o in outs)


CID_WEIGHT_GATHER, CID_PAIR_SWAP, CID_SCATTER = 0, 1, 2


def _other_chips(x, y):
    return [(1 - x, y), (x, 1 - y), (1 - x, 1 - y)]


def _handshake(peers):
    barrier = pltpu.get_barrier_semaphore()
    for peer in peers:
        pl.semaphore_signal(barrier, inc=1, device_id=peer, device_id_type=MESH)
    pl.semaphore_wait(barrier, len(peers))


def _sequencer_call(body, name, out_type, sem_counts, collective_id, operands):
    return pl.kernel(
        body, name=name, out_type=out_type,
        mesh=plsc.ScalarSubcoreMesh(axis_name="sequencer", num_cores=1),
        scratch_types=[pltpu.SemaphoreType.DMA((n,)) for n in sem_counts],
        compiler_params=pltpu.CompilerParams(collective_id=collective_id),
    )(*operands)


def _comm_weight_gather(shards):
    T = len(shards)

    def body(*refs):
        ins, outs = refs[:T], refs[T:2 * T]
        send, recv, local = refs[2 * T:]
        x, y, c = lax.axis_index("x"), lax.axis_index("y"), lax.axis_index("c")
        me = 2 * x + y
        chips = _other_chips(x, y)
        _handshake([(px, py, c) for px, py in chips])
        mine = []
        for t in range(T):
            cp = pltpu.make_async_copy(ins[t], outs[t].at[me], local.at[t])
            cp.start()
            mine.append(cp)
            for k, (px, py) in enumerate(chips):
                pltpu.make_async_remote_copy(
                    src_ref=ins[t], dst_ref=outs[t].at[me], send_sem=send.at[3 * t + k], recv_sem=recv.at[3 * t + k],
                    device_id=(px, py, c), device_id_type=MESH).start()
        for t in range(T):
            for k, (px, py) in enumerate(chips):
                cp = pltpu.make_async_remote_copy(
                    src_ref=ins[t], dst_ref=outs[t].at[2 * px + py], send_sem=send.at[3 * t + k],
                    recv_sem=recv.at[3 * t + k], device_id=(px, py, c), device_id_type=MESH)
                cp.wait_send()
                cp.wait_recv()
            mine[t].wait()

    return _sequencer_call(
        body, "comm_weight_gather", tuple(jax.ShapeDtypeStruct((4,) + s.shape, s.dtype) for s in shards),
        (3 * T, 3 * T, T), CID_WEIGHT_GATHER, shards)


def _comm_to_sibling(xs, other_half=False):
    T = len(xs)

    def body(*refs):
        ins, outs = refs[:T], refs[T:2 * T]
        send, recv = refs[2 * T:]
        x, y, c = lax.axis_index("x"), lax.axis_index("y"), lax.axis_index("c")
        _handshake([(x, y, 1 - c)])
        cps = [pltpu.make_async_remote_copy(
            src_ref=ins[t].at[:, 1 - c] if other_half else ins[t], dst_ref=outs[t], send_sem=send.at[t],
            recv_sem=recv.at[t], device_id=(x, y, 1 - c), device_id_type=MESH) for t in range(T)]
        for cp in cps:
            cp.start()
        for cp in cps:
            cp.wait_send()
            cp.wait_recv()

    shapes = [s.shape[:1] + s.shape[2:] if other_half else s.shape for s in xs]
    return _sequencer_call(
        body, "comm_pair_swap", tuple(jax.ShapeDtypeStruct(sh, s.dtype) for sh, s in zip(shapes, xs)),
        (T, T), CID_PAIR_SWAP, xs)


def _comm_scatter_to_owners(ps):
    T = len(ps)

    def body(*refs):
        ins, outs = refs[:T], refs[T:2 * T]
        send, recv = refs[2 * T:]
        x, y, c = lax.axis_index("x"), lax.axis_index("y"), lax.axis_index("c")
        chips = _other_chips(x, y)
        _handshake([(px, py, c) for px, py in chips])
        cps = []
        for t in range(T):
            for k, (px, py) in enumerate(chips):
                cp = pltpu.make_async_remote_copy(
                    src_ref=ins[t].at[2 * px + py], dst_ref=outs[t].at[k], send_sem=send.at[3 * t + k],
                    recv_sem=recv.at[3 * t + k], device_id=(px, py, c), device_id_type=MESH)
                cp.start()
                cps.append(cp)
        for cp in cps:
            cp.wait_send()
            cp.wait_recv()

    return _sequencer_call(
        body, "comm_scatter_to_owners", tuple(jax.ShapeDtypeStruct((3,) + p.shape[1:], p.dtype) for p in ps),
        (3 * T, 3 * T), CID_SCATTER, ps)


def _comm_all_to_all_small(pack):
    R = pack.shape[0]

    def body(in_ref, out_ref, send, recv, local):
        x, y, c = lax.axis_index("x"), lax.axis_index("y"), lax.axis_index("c")
        me = 4 * x + 2 * y + c
        mine = pltpu.make_async_copy(in_ref, out_ref.at[me], local.at[0])
        mine.start()
        peers = []
        for k in range(1, 8):
            fx, fy, fc = (k >> 2) & 1, (k >> 1) & 1, k & 1
            peers.append((x + fx * (1 - 2 * x), y + fy * (1 - 2 * y), c + fc * (1 - 2 * c)))
        for k, peer in enumerate(peers):
            pltpu.make_async_remote_copy(
                src_ref=in_ref, dst_ref=out_ref.at[me], send_sem=send.at[k], recv_sem=recv.at[k],
                device_id=peer, device_id_type=MESH).start()
        for k, (px, py, pc) in enumerate(peers):
            cp = pltpu.make_async_remote_copy(
                src_ref=in_ref, dst_ref=out_ref.at[4 * px + 2 * py + pc], send_sem=send.at[k], recv_sem=recv.at[k],
                device_id=(px, py, pc), device_id_type=MESH)
            cp.wait_send()
            cp.wait_recv()
        mine.wait()

    return pl.pallas_call(
        body, name="comm_all_to_all_small",
        in_specs=[ANY], out_specs=ANY,
        out_shape=jax.ShapeDtypeStruct((8, R, LANES), F32),
        scratch_shapes=[pltpu.SemaphoreType.DMA((7,)), pltpu.SemaphoreType.DMA((7,)), pltpu.SemaphoreType.DMA((1,))],
    )(pack)


def _tie(*trees):
    return lax.optimization_barrier(trees)


class _ReduceScatter:
    def __init__(self, grads, dh):
        split = [g.reshape(4, 2, g.shape[1] // 2, g.shape[2]) for g in grads]
        self.split, self.dh = _tie(split, dh)
        self.got = _comm_to_sibling(self.split, other_half=True)

    def scatter(self, dh):
        split, got, dh = _tie(self.split, list(self.got), dh)
        pair = [_pair_sum(a, b) for a, b in zip(split, got)]
        self.pair, dh = _tie(pair, dh)
        self.arrived = _comm_scatter_to_owners(self.pair)
        return dh

    def gather(self, dh):
        pair, arrived, dh = _tie(self.pair, list(self.arrived), dh)
        halves = [_chip_sum(p, a) for p, a in zip(pair, arrived)]
        self.halves, dh = _tie(halves, dh)
        self.other = _comm_to_sibling(self.halves)
        return dh

    def settle(self, dh):
        self.other, dh = _tie(list(self.other), dh)
        return dh

    def result(self):
        south = lax.axis_index("c") == 0
        return [jnp.where(south, jnp.concatenate([mine, theirs]), jnp.concatenate([theirs, mine]))
                for mine, theirs in zip(self.halves, self.other)]


def _ffn_fwd(x, g_pre, g_post, wgu, wd):
    S, D = x.shape
    F = wd.shape[1]
    xn = _norm_fwd(x, g_pre)
    gu, h = _ffn_up(xn, wgu)
    f = _mm(h, wd, "nn", ja="r", jb="r", tm=_row_tile(S, ACT_ROWS), tn=MODEL_COLS, tk=F, name="ffn_down")
    x_out = _resid_norm_fwd(x, f, g_post, 0.5)
    return x_out, (x, xn, gu, h, f)


def _ffn_bwd(dx, saved, g_pre, g_post, wgu, wd, mid=None):
    x, xn, gu, h, f = saved
    S, D = x.shape
    J, F, _ = wd.shape
    tm, tk = _row_tile(S, ACT_ROWS), _row_tile(S, TOKEN_ROWS)
    df, dg_post = _norm_bwd(f, g_post, dx, 0.5, out_dtype=BF16)
    d_wd = _mm(h, df, "tn", ja="b", tm=F, tn=MODEL_COLS, tk=tk, name="ffn_dwd")
    dgu = _ffn_bwd_hidden(df, wd, gu)
    if mid is not None:
        dgu = mid(dgu)
    d_wgu = _mm(xn, dgu, "tn", jb="b", tm=MODEL_COLS, tn=F, tk=tk, name="ffn_dwgu")
    dxn = _mm(dgu, wgu, "nt", ja="r", jb="r", tm=tm, tn=MODEL_COLS, tk=2 * F, name="ffn_dxn")
    dx_in, dg_pre = _norm_bwd(x, g_pre, dxn, 1.0, resid=dx)
    return dx_in, dg_pre, dg_post, d_wgu, d_wd


def _gates_layout(proj, b_gate, nc):
    gt = jnp.transpose(proj[:, OFF_G:OFF_G + N_GATES]).reshape(N_GATES * nc, LANES)
    bias = jnp.repeat(b_gate, nc).reshape(N_GATES * nc, 1)
    return gt, bias


def _mixer_fwd(x, p, cosf, sinf):
    S, D = x.shape
    nc = S // BLK
    hn = _norm_fwd(x, p["g_pre"])
    proj = _mm(hn, p["w_in"], "nn", tm=_row_tile(S, ACT_ROWS), tn=IN_PAD // 5, tk=D, name="in_proj")
    qr, kr, va = _rope_fwd(proj, cosf, sinf)
    y_att = _attn_fwd(qr, kr, va, p["sink"])
    qk = _conv5(proj, p["conv_w"], col0=OFF_QM, act=True, out_dtype=BF16, name="conv_silu")
    gt, bias = _gates_layout(proj, p["b_gate"], nc)
    ig, cum = _gate_prep(gt, bias)
    ig5, b5 = ig.reshape(2, M_HEADS, nc, 1, LANES), cum.reshape(2, M_HEADS, nc, 1, LANES)
    hs, cs, ns, ms = _mlstm_fwd(qk, proj, ig5, b5)
    y_m = _mlstm_out_fwd(hs, proj, p["gamma"])
    ycat = jnp.concatenate([y_att, y_m], axis=-1)
    mix = _mm(ycat, p["w_out"], "nn", tm=_row_tile(S, ACT_ROWS), tn=MODEL_COLS, tk=D, name="out_proj")
    x_out = _resid_norm_fwd(x, mix, p["g_post"], 1.0)
    return x_out, (x, hn, proj, qr, kr, va, qk, gt, bias, ig5, b5, hs, cs, ns, ms, ycat, mix)


def _mixer_bwd(dx, saved, p, cosf, sinf):
    x, hn, proj, qr, kr, va, qk, gt, bias, ig5, b5, hs, cs, ns, ms, ycat, mix = saved
    S, D = x.shape
    nc = S // BLK
    tm, tk = _row_tile(S, ACT_ROWS), _row_tile(S, TOKEN_ROWS)
    dmix, dg_post = _norm_bwd(mix, p["g_post"], dx, 1.0, out_dtype=BF16)
    d_wout = _mm(ycat, dmix, "tn", tm=MODEL_COLS, tn=MODEL_COLS, tk=tk, name="dw_out")
    dycat = _mm(dmix, p["w_out"], "nt", tm=tm, tn=MODEL_COLS, tk=D, name="d_ycat")
    dhm, d_om, d_gamma = _mlstm_out_bwd(hs, proj, p["gamma"], dycat)
    dq2, dk2, dv2, di, dfc, dbl = _mlstm_bwd(qk, proj, ig5, b5, cs, ns, ms, dhm)
    d_vm = _sum_cast(dv2, "dv_sum")
    dgt, db = _gate_bwd(gt, bias, di.reshape(-1, LANES), dfc.reshape(-1, LANES), dbl.reshape(-1, LANES), nc)
    dpre, d_conv = _conv_bwd_pre(proj, OFF_QM, p["conv_w"], dq2, dk2)
    d_qkm = _conv5(dpre, p["conv_w"][::-1], col0=0, act=False, out_dtype=BF16, name="conv_bwd_x")
    d_gates = jnp.transpose(dgt.reshape(N_GATES, S)).astype(BF16)
    d_gates = jnp.pad(d_gates, ((0, 0), (0, IN_PAD - IN_WIDTH)))
    dqr, dkr, dva, d_sink = _attn_bwd(qr, kr, va, p["sink"], dycat)
    d_att = _rope_bwd(dqr, dkr, dva, cosf, sinf)
    dproj = jnp.concatenate([d_att, d_qkm, d_vm, d_om, d_gates], axis=-1)
    d_win = _mm(hn, dproj, "tn", tm=MODEL_COLS, tn=IN_PAD // 5, tk=tk, name="dw_in")
    dhn = _mm(dproj, p["w_in"], "nt", tm=tm, tn=MODEL_COLS, tk=IN_PAD // 5, name="d_hn")
    dx_in, dg_pre = _norm_bwd(x, p["g_pre"], dhn, 1.0, resid=dx)
    small = dict(g_pre=dg_pre, g_post=dg_post, gamma=d_gamma, conv=d_conv, b_gate=db[:, 0], sink=d_sink)
    return dx_in, small, d_win, d_wout


def _pad_lanes(v):
    v = v.reshape(-1)
    return jnp.pad(v, (0, (-v.shape[0]) % LANES))


def kernel(x, ffn1_norm_pre, ffn1_norm_post, ffn1_w_gate, ffn1_w_up, ffn1_w_down, mix_norm_pre, mix_norm_post, w_in, b_gate, conv_w, attn_sink, mlstm_norm, w_out, ffn2_norm_pre, ffn2_norm_post, ffn2_w_gate, ffn2_w_up, ffn2_w_down, loss_target, m_ffn1_norm_pre, m_ffn1_norm_post, m_ffn1_w_gate, m_ffn1_w_up, m_ffn1_w_down, m_mix_norm_pre, m_mix_norm_post, m_w_in, m_b_gate, m_conv_w, m_attn_sink, m_mlstm_norm, m_w_out, m_ffn2_norm_pre, m_ffn2_norm_post, m_ffn2_w_gate, m_ffn2_w_up, m_ffn2_w_down, v_ffn1_norm_pre, v_ffn1_norm_post, v_ffn1_w_gate, v_ffn1_w_up, v_ffn1_w_down, v_mix_norm_pre, v_mix_norm_post, v_w_in, v_b_gate, v_conv_w, v_attn_sink, v_mlstm_norm, v_w_out, v_ffn2_norm_pre, v_ffn2_norm_post, v_ffn2_w_gate, v_ffn2_w_up, v_ffn2_w_down):
    names = ["ffn1_norm_pre", "ffn1_norm_post", "ffn1_w_gate", "ffn1_w_up", "ffn1_w_down", "mix_norm_pre",
             "mix_norm_post", "w_in", "b_gate", "conv_w", "attn_sink", "mlstm_norm", "w_out", "ffn2_norm_pre",
             "ffn2_norm_post", "ffn2_w_gate", "ffn2_w_up", "ffn2_w_down"]
    w = dict(zip(names, [ffn1_norm_pre, ffn1_norm_post, ffn1_w_gate, ffn1_w_up, ffn1_w_down, mix_norm_pre,
                         mix_norm_post, w_in, b_gate, conv_w, attn_sink, mlstm_norm, w_out, ffn2_norm_pre,
                         ffn2_norm_post, ffn2_w_gate, ffn2_w_up, ffn2_w_down]))
    mom_m = dict(zip(names, [m_ffn1_norm_pre, m_ffn1_norm_post, m_ffn1_w_gate, m_ffn1_w_up, m_ffn1_w_down,
                             m_mix_norm_pre, m_mix_norm_post, m_w_in, m_b_gate, m_conv_w, m_attn_sink,
                             m_mlstm_norm, m_w_out, m_ffn2_norm_pre, m_ffn2_norm_post, m_ffn2_w_gate,
                             m_ffn2_w_up, m_ffn2_w_down]))
    mom_v = dict(zip(names, [v_ffn1_norm_pre, v_ffn1_norm_post, v_ffn1_w_gate, v_ffn1_w_up, v_ffn1_w_down,
                             v_mix_norm_pre, v_mix_norm_post, v_w_in, v_b_gate, v_conv_w, v_attn_sink,
                             v_mlstm_norm, v_w_out, v_ffn2_norm_pre, v_ffn2_norm_post, v_ffn2_w_gate,
                             v_ffn2_w_up, v_ffn2_w_down]))
    xs = x[0]
    target = loss_target[0]
    S, D = xs.shape
    depth = w_in.shape[0]
    F = ffn1_w_gate.shape[-1]
    in_shard = w_in.shape[-1]
    conv_shard = conv_w.shape[-1]
    chip = 2 * lax.axis_index("x") + lax.axis_index("y")
    cosf, sinf = _rope_tables(S)

    gathered = []
    for l in range(depth):
        first = [jnp.concatenate([w["ffn1_w_gate"][l], w["ffn1_w_up"][l]], axis=-1).astype(BF16),
                 w["ffn1_w_down"][l].astype(BF16)]
        rest = [w["w_in"][l].astype(BF16), w["w_out"][l].astype(BF16),
                jnp.concatenate([w["ffn2_w_gate"][l], w["ffn2_w_up"][l]], axis=-1).astype(BF16),
                w["ffn2_w_down"][l].astype(BF16),
                jnp.pad(w["conv_w"][l], ((0, 8 - CONV_WIDTH), (0, 0)))]
        gathered.append((list(_comm_weight_gather(first)), list(_comm_weight_gather(rest))))

    def mixer_weights(l, win, wout, convg):
        win_full = jnp.concatenate([win[j] for j in range(4)], axis=-1)
        conv_full = jnp.concatenate([convg[j] for j in range(4)], axis=-1)[:CONV_WIDTH]
        return dict(w_in=jnp.pad(win_full, ((0, 0), (0, IN_PAD - IN_WIDTH))), w_out=wout.reshape(D, D),
                    conv_w=conv_full, g_pre=w["mix_norm_pre"][l], g_post=w["mix_norm_post"][l],
                    b_gate=w["b_gate"][l], sink=w["attn_sink"][l], gamma=w["mlstm_norm"][l])

    h = xs
    saved = []
    layers = []
    for l in range(depth):
        (wgu1, wd1), h = _tie(gathered[l][0], h)
        h, s1 = _ffn_fwd(h, w["ffn1_norm_pre"][l], w["ffn1_norm_post"][l], wgu1, wd1)
        (win, wout, wgu2, wd2, convg), h = _tie(gathered[l][1], h)
        p = dict(wgu1=wgu1, wd1=wd1, wgu2=wgu2, wd2=wd2, mix=mixer_weights(l, win, wout, convg))
        layers.append(p)
        h, s2 = _mixer_fwd(h, p["mix"], cosf, sinf)
        h, s3 = _ffn_fwd(h, w["ffn2_norm_pre"][l], w["ffn2_norm_post"][l], p["wgu2"], p["wd2"])
        saved.append((s1, s2, s3))
    dh, loss_tile = _loss_grad(h, target)

    big = {}
    small_rows = []
    rs_a = rs_b = None
    for l in reversed(range(depth)):
        p = layers[l]
        s1, s2, s3 = saved[l]
        old_a, old_b = rs_a, rs_b
        dh, dg2_pre, dg2_post, d_wgu2, d_wd2 = _ffn_bwd(dh, s3, w["ffn2_norm_pre"][l], w["ffn2_norm_post"][l],
                                                       p["wgu2"], p["wd2"], mid=old_b.scatter if old_b else None)
        if old_a is not None:
            dh = old_a.gather(dh)
        dh, sm, d_win, d_wout = _mixer_bwd(dh, s2, p["mix"], cosf, sinf)
        if old_b is not None:
            dh = old_b.gather(dh)
        d_win4 = jnp.stack([d_win[:, j * in_shard:(j + 1) * in_shard] for j in range(4)])
        rs_a = _ReduceScatter([d_win4, d_wout.reshape(4, D // 4, D), d_wgu2, d_wd2], dh)
        dh, dg1_pre, dg1_post, d_wgu1, d_wd1 = _ffn_bwd(rs_a.dh, s1, w["ffn1_norm_pre"][l], w["ffn1_norm_post"][l],
                                                       p["wgu1"], p["wd1"], mid=rs_a.scatter)
        if old_a is not None:
            dh = old_b.settle(old_a.settle(dh))
        rs_b = _ReduceScatter([d_wgu1, d_wd1], dh)
        dh = rs_b.dh
        big[l] = (rs_a, rs_b)
        small_rows.append((l, [dg1_pre, dg1_post, sm["g_pre"], sm["g_post"], dg2_pre, dg2_post, sm["gamma"],
                               sm["conv"], sm["b_gate"], sm["sink"]]))
    dh = rs_b.gather(rs_b.scatter(rs_a.gather(dh)))
    big = {l: b.result() + a.result() for l, (a, b) in big.items()}

    small_rows.sort(key=lambda t: t[0])
    flat = [_pad_lanes(v) for _, vs in small_rows for v in vs] + [loss_tile[0]]
    sizes = [f.shape[0] for f in flat]
    pack = jnp.concatenate(flat)
    pack = jnp.pad(pack, (0, (-pack.shape[0]) % (8 * LANES))).reshape(-1, LANES)
    every = _comm_all_to_all_small(pack)
    total = _add_n([every[i] for i in range(8)], "small_sum").reshape(-1)
    pieces, off = [], 0
    for n in sizes:
        pieces.append(total[off:off + n])
        off += n
    loss = pieces[-1][0]
    per_layer = [pieces[10 * l:10 * l + 10] for l in range(depth)]

    def stack_small(i, shape):
        n = 1
        for s in shape:
            n *= s
        return jnp.stack([per_layer[l][i][:n].reshape(shape) for l in range(depth)])

    conv_full_grad = stack_small(7, (CONV_WIDTH, 4 * conv_shard))
    grads = {
        "ffn1_norm_pre": stack_small(0, (D,)), "ffn1_norm_post": stack_small(1, (D,)),
        "mix_norm_pre": stack_small(2, (D,)), "mix_norm_post": stack_small(3, (D,)),
        "ffn2_norm_pre": stack_small(4, (D,)), "ffn2_norm_post": stack_small(5, (D,)),
        "mlstm_norm": stack_small(6, (M_WIDTH,)),
        "conv_w": lax.dynamic_slice_in_dim(conv_full_grad, chip * conv_shard, conv_shard, 2),
        "b_gate": stack_small(8, (N_GATES,)), "attn_sink": stack_small(9, (ATT_HEADS,)),
    }
    gu1 = jnp.stack([big[l][0] for l in range(depth)])
    gu2 = jnp.stack([big[l][4] for l in range(depth)])
    grads["ffn1_w_gate"], grads["ffn1_w_up"] = gu1[:, :, :F], gu1[:, :, F:]
    grads["ffn2_w_gate"], grads["ffn2_w_up"] = gu2[:, :, :F], gu2[:, :, F:]
    grads["ffn1_w_down"] = jnp.stack([big[l][1] for l in range(depth)])
    grads["ffn2_w_down"] = jnp.stack([big[l][5] for l in range(depth)])
    grads["w_in"] = jnp.stack([big[l][2] for l in range(depth)])
    grads["w_out"] = jnp.stack([big[l][3] for l in range(depth)])

    deltas, new_m, new_v = {}, {}, {}
    for n in names:
        deltas[n], new_m[n], new_v[n] = _adamw(w[n], grads[n], mom_m[n], mom_v[n])
    grad_x = dh[None]
    return (loss, grad_x, *[grads[n] for n in names], *[deltas[n] for n in names],
            *[new_m[n] for n in names], *[new_v[n] for n in names])
```

```python
import jax
import jax.numpy as jnp
from jax import lax
from jax.experimental import pallas as pl
from jax.experimental.pallas import tpu as pltpu
from jax.experimental.pallas import tpu_sc as plsc

F32 = jnp.float32
BF16 = jnp.bfloat16
MESH = pl.DeviceIdType.MESH
ANY = pl.BlockSpec(memory_space=pl.ANY)

VMEM_LIMIT_BYTES = 56 * 1024 * 1024
LANES = 128

EPS = 1e-6
ATT_HEADS = 8
ATT_KV_HEADS = 2
ATT_GROUP = ATT_HEADS // ATT_KV_HEADS
ATT_HEAD_DIM = 128
ATT_WIDTH = ATT_HEADS * ATT_HEAD_DIM
KV_WIDTH = ATT_KV_HEADS * ATT_HEAD_DIM
BLK = 128
M_HEADS = 4
M_HEAD_DIM = 256
M_WIDTH = M_HEADS * M_HEAD_DIM
CONV_WIDTH = 5
CONV_HALO = 8
ROPE_THETA = 10000.0
N_GATES = 4 * M_HEADS
OFF_QA, OFF_KA, OFF_VA = 0, ATT_WIDTH, ATT_WIDTH + KV_WIDTH
OFF_QM = ATT_WIDTH + 2 * KV_WIDTH
OFF_KM = OFF_QM + M_WIDTH
OFF_VM = OFF_KM + M_WIDTH
OFF_OM = OFF_VM + M_WIDTH
OFF_G = OFF_OM + M_WIDTH
IN_WIDTH = OFF_G + N_GATES
IN_PAD = OFF_G + LANES
NEG_BIG = -1e30

ADAM_LR, ADAM_B1, ADAM_B2, ADAM_EPS, ADAM_WD, ADAM_STEP = 0.001, 0.9, 0.999, 1e-08, 0.01, 10


def _params(**kw):
    return pltpu.CompilerParams(vmem_limit_bytes=VMEM_LIMIT_BYTES, **kw)


def _dot(a, b):
    return lax.dot_general(a, b, (((1,), (0,)), ((), ())), preferred_element_type=F32)


def _dot_nt(a, b):
    return lax.dot_general(a, b, (((1,), (1,)), ((), ())), preferred_element_type=F32)


def _dot_tn(a, b):
    return lax.dot_general(a, b, (((0,), (0,)), ((), ())), preferred_element_type=F32)


def _sigmoid(x):
    return 1.0 / (1.0 + jnp.exp(-x))


def _eye_mask(n):
    r = lax.broadcasted_iota(jnp.int32, (n, n), 0)
    c = lax.broadcasted_iota(jnp.int32, (n, n), 1)
    return r == c


def _row_to_col(row, eye):
    n = eye.shape[0]
    return jnp.sum(jnp.where(eye, jnp.broadcast_to(row, (n, n)), 0.0), axis=1, keepdims=True)


def _col_to_row(col, eye):
    n = eye.shape[0]
    return jnp.sum(jnp.where(eye, jnp.broadcast_to(col, (n, n)), 0.0), axis=0, keepdims=True)


def _mm(a, b, kind, *, tm, tn, tk, name, out_dtype=F32, ja=None, jb=None):
    a2, b2 = a.shape[-2:], b.shape[-2:]
    if kind == "nn":
        (M, K), (_, N) = a2, b2
    elif kind == "nt":
        (M, K), (N, _) = a2, b2
    else:
        (K, M), (_, N) = a2, b2
    J = a.shape[0] if ja else (b.shape[0] if jb else 1)
    batch = "b" in (ja, jb)
    red = "r" in (ja, jb)
    nk = K // tk
    nr = nk * (J if red else 1)
    grid = ((J if batch else 1), M // tm, N // tn, nr)

    def lead(mode, g, r):
        return g if mode == "b" else r // nk

    def a_map(g, i, n, r):
        kk = r % nk
        idx = (i, kk) if kind != "tn" else (kk, i)
        return idx if ja is None else (lead(ja, g, r),) + idx

    def b_map(g, i, n, r):
        kk = r % nk
        idx = (kk, n) if kind != "nt" else (n, kk)
        return idx if jb is None else (lead(jb, g, r),) + idx

    def o_map(g, i, n, r):
        return (g, i, n) if batch else (i, n)

    a_blk = (tm, tk) if kind != "tn" else (tk, tm)
    b_blk = (tk, tn) if kind != "nt" else (tn, tk)
    dot = {"nn": _dot, "nt": _dot_nt, "tn": _dot_tn}[kind]

    def body(a_ref, b_ref, o_ref, *scratch):
        part = dot(a_ref[...], b_ref[...])
        if nr == 1:
            o_ref[...] = part.astype(out_dtype)
        else:
            acc = scratch[0]
            r = pl.program_id(3)

            @pl.when(r == 0)
            def _():
                acc[...] = part

            @pl.when(r > 0)
            def _():
                acc[...] += part

            @pl.when(r == nr - 1)
            def _():
                o_ref[...] = acc[...].astype(out_dtype)

    return pl.pallas_call(
        body, name=name, grid=grid,
        in_specs=[pl.BlockSpec(a_blk if ja is None else (None,) + a_blk, a_map),
                  pl.BlockSpec(b_blk if jb is None else (None,) + b_blk, b_map)],
        out_specs=pl.BlockSpec((None, tm, tn) if batch else (tm, tn), o_map),
        out_shape=jax.ShapeDtypeStruct((J, M, N) if batch else (M, N), out_dtype),
        scratch_shapes=[pltpu.VMEM((tm, tn), F32)] if nr > 1 else [],
        compiler_params=_params(),
    )(a, b)


def _row_tile(S, want):
    return min(S, want)


ELEMENTWISE_TILE_BYTES = 1 << 20
ACT_ROWS = 1024
TOKEN_ROWS = 2048
MODEL_COLS = 1024


def _elementwise_rows(R, C):
    for cand in (1024, 512, 256, 128, 64, 32, 16, 8):
        if R % cand == 0 and R > cand and cand * C * 4 <= ELEMENTWISE_TILE_BYTES:
            return cand
    return R if R * C * 4 <= ELEMENTWISE_TILE_BYTES or R % 8 else 8


def _ffn_up(xn, wgu):
    S, D = xn.shape
    J, _, F2 = wgu.shape
    F = F2 // 2
    tm = _row_tile(S, 256)

    def body(x_ref, w_ref, gu_ref, h_ref):
        gu = _dot(x_ref[...], w_ref[...])
        g, u = gu[:, :F], gu[:, F:]
        gu_ref[...] = gu.astype(BF16)
        h_ref[...] = (g * _sigmoid(g) * u).astype(BF16)

    return pl.pallas_call(
        body, name="ffn_up", grid=(J, S // tm),
        in_specs=[pl.BlockSpec((tm, D), lambda j, i: (i, 0)),
                  pl.BlockSpec((None, D, F2), lambda j, i: (j, 0, 0))],
        out_specs=[pl.BlockSpec((None, tm, F2), lambda j, i: (j, i, 0)),
                   pl.BlockSpec((None, tm, F), lambda j, i: (j, i, 0))],
        out_shape=[jax.ShapeDtypeStruct((J, S, F2), BF16), jax.ShapeDtypeStruct((J, S, F), BF16)],
        compiler_params=_params(),
    )(xn, wgu)


def _ffn_bwd_hidden(df, wd, gu):
    S, D = df.shape
    J, F, _ = wd.shape
    F2 = 2 * F
    tm = _row_tile(S, 512)

    def body(df_ref, w_ref, gu_ref, o_ref):
        dh = _dot_nt(df_ref[...], w_ref[...])
        g = gu_ref[:, :F].astype(F32)
        u = gu_ref[:, F:].astype(F32)
        sg = _sigmoid(g)
        o_ref[:, :F] = (dh * u * (sg * (1.0 + g * (1.0 - sg)))).astype(BF16)
        o_ref[:, F:] = (dh * (g * sg)).astype(BF16)

    return pl.pallas_call(
        body, name="ffn_bwd_hidden", grid=(J, S // tm),
        in_specs=[pl.BlockSpec((tm, D), lambda j, i: (i, 0)),
                  pl.BlockSpec((None, F, D), lambda j, i: (j, 0, 0)),
                  pl.BlockSpec((None, tm, F2), lambda j, i: (j, i, 0))],
        out_specs=pl.BlockSpec((None, tm, F2), lambda j, i: (j, i, 0)),
        out_shape=jax.ShapeDtypeStruct((J, S, F2), BF16),
        compiler_params=_params(),
    )(df, wd, gu)


def _norm_fwd(x, g):
    S, D = x.shape
    tm = _row_tile(S, 512)

    def body(x_ref, g_ref, o_ref):
        xv = x_ref[...]
        r = lax.rsqrt(jnp.mean(xv * xv, axis=-1, keepdims=True) + EPS)
        o_ref[...] = (xv * r * g_ref[...]).astype(BF16)

    return pl.pallas_call(
        body, name="norm_fwd", grid=(S // tm,),
        in_specs=[pl.BlockSpec((tm, D), lambda i: (i, 0)), pl.BlockSpec((1, D), lambda i: (0, 0))],
        out_specs=pl.BlockSpec((tm, D), lambda i: (i, 0)),
        out_shape=jax.ShapeDtypeStruct((S, D), BF16),
        compiler_params=_params(),
    )(x, g.reshape(1, D))


def _resid_norm_fwd(x, f, g, alpha):
    S, D = x.shape
    tm = _row_tile(S, 512)

    def body(x_ref, f_ref, g_ref, o_ref):
        fv = f_ref[...]
        r = lax.rsqrt(jnp.mean(fv * fv, axis=-1, keepdims=True) + EPS)
        o_ref[...] = x_ref[...] + alpha * (fv * r * g_ref[...])

    return pl.pallas_call(
        body, name="resid_norm_fwd", grid=(S // tm,),
        in_specs=[pl.BlockSpec((tm, D), lambda i: (i, 0)), pl.BlockSpec((tm, D), lambda i: (i, 0)),
                  pl.BlockSpec((1, D), lambda i: (0, 0))],
        out_specs=pl.BlockSpec((tm, D), lambda i: (i, 0)),
        out_shape=jax.ShapeDtypeStruct((S, D), F32),
        compiler_params=_params(),
    )(x, f, g.reshape(1, D))


def _norm_bwd(x, g, dy, alpha, resid=None, out_dtype=F32):
    S, D = x.shape
    tm = _row_tile(S, 256)
    has_resid = resid is not None

    def body(*refs):
        x_ref, g_ref, dy_ref = refs[:3]
        res_ref = refs[3] if has_resid else None
        dx_ref, dg_ref = refs[-2:]
        xv = x_ref[...]
        r = lax.rsqrt(jnp.mean(xv * xv, axis=-1, keepdims=True) + EPS)
        xh = xv * r
        dyv = dy_ref[...].astype(F32) * alpha
        gdy = dyv * g_ref[...]
        dx = r * (gdy - xh * jnp.mean(xh * gdy, axis=-1, keepdims=True))
        if has_resid:
            dx = dx + res_ref[...]
        dx_ref[...] = dx.astype(out_dtype)
        part = jnp.sum(dyv * xh, axis=0, keepdims=True)

        @pl.when(pl.program_id(0) == 0)
        def _():
            dg_ref[...] = part

        @pl.when(pl.program_id(0) > 0)
        def _():
            dg_ref[...] += part

    row = pl.BlockSpec((tm, D), lambda i: (i, 0))
    vec = pl.BlockSpec((1, D), lambda i: (0, 0))
    ins = [x, g.reshape(1, D), dy] + ([resid] if has_resid else [])
    dx, dg = pl.pallas_call(
        body, name="norm_bwd_res" if has_resid else "norm_bwd", grid=(S // tm,),
        in_specs=[row, vec, row] + ([row] if has_resid else []),
        out_specs=[row, vec],
        out_shape=[jax.ShapeDtypeStruct((S, D), out_dtype), jax.ShapeDtypeStruct((1, D), F32)],
        compiler_params=_params(),
    )(*ins)
    return dx, dg.reshape(D)


def _loss_grad(y, target):
    S, D = y.shape
    tm = _row_tile(S, 512)

    def body(y_ref, t_ref, dy_ref, l_ref):
        err = y_ref[...] - t_ref[...]
        dy_ref[...] = err * (1.0 / D)
        part = jnp.sum(jnp.sum(err * err, axis=-1, keepdims=True) * (0.5 / D), axis=0, keepdims=True)
        part = jnp.broadcast_to(part, (8, LANES))

        @pl.when(pl.program_id(0) == 0)
        def _():
            l_ref[...] = part

        @pl.when(pl.program_id(0) > 0)
        def _():
            l_ref[...] += part

    row = pl.BlockSpec((tm, D), lambda i: (i, 0))
    return pl.pallas_call(
        body, name="loss_grad", grid=(S // tm,),
        in_specs=[row, row],
        out_specs=[row, pl.BlockSpec((8, LANES), lambda i: (0, 0))],
        out_shape=[jax.ShapeDtypeStruct((S, D), F32), jax.ShapeDtypeStruct((8, LANES), F32)],
        compiler_params=_params(),
    )(y, target)


def _add_n(xs, name):
    shape = xs[0].shape
    C = shape[-1]
    R = 1
    for s in shape[:-1]:
        R *= s
    tm = _elementwise_rows(R, C)

    def body(*refs):
        acc = refs[0][...]
        for r in refs[1:-1]:
            acc = acc + r[...]
        refs[-1][...] = acc

    row = pl.BlockSpec((tm, C), lambda i: (i, 0))
    out = pl.pallas_call(
        body, name=name, grid=(R // tm,),
        in_specs=[row] * len(xs), out_specs=row,
        out_shape=jax.ShapeDtypeStruct((R, C), F32),
        compiler_params=_params(),
    )(*[x.reshape(R, C) for x in xs])
    return out.reshape(shape)


def _pair_sum(split, got):
    J, _, r, C = split.shape
    tm = _elementwise_rows(r, C)
    core = lax.axis_index("c").astype(jnp.int32).reshape(1)

    def body(core_ref, a_ref, b_ref, o_ref):
        o_ref[...] = a_ref[...] + b_ref[...]

    row = pl.BlockSpec((None, tm, C), lambda j, i, core_ref: (j, i, 0))
    return pl.pallas_call(
        body, name="pair_sum",
        grid_spec=pltpu.PrefetchScalarGridSpec(
            num_scalar_prefetch=1, grid=(J, r // tm),
            in_specs=[pl.BlockSpec((None, None, tm, C), lambda j, i, core_ref: (j, core_ref[0], i, 0)), row],
            out_specs=row),
        out_shape=jax.ShapeDtypeStruct((J, r, C), F32),
        compiler_params=_params(),
    )(core, split, got)


def _chip_sum(pair, arrived):
    _, r, C = pair.shape
    tm = _elementwise_rows(r, C)
    chip = (2 * lax.axis_index("x") + lax.axis_index("y")).astype(jnp.int32).reshape(1)

    def body(chip_ref, own_ref, a_ref, b_ref, c_ref, o_ref):
        o_ref[...] = own_ref[...] + a_ref[...] + b_ref[...] + c_ref[...]

    part = lambda k: pl.BlockSpec((None, tm, C), lambda i, chip_ref: (k, i, 0))
    return pl.pallas_call(
        body, name="chip_sum",
        grid_spec=pltpu.PrefetchScalarGridSpec(
            num_scalar_prefetch=1, grid=(r // tm,),
            in_specs=[pl.BlockSpec((None, tm, C), lambda i, chip_ref: (chip_ref[0], i, 0)), part(0), part(1), part(2)],
            out_specs=pl.BlockSpec((tm, C), lambda i, chip_ref: (i, 0))),
        out_shape=jax.ShapeDtypeStruct((r, C), F32),
        compiler_params=_params(),
    )(chip, pair, arrived, arrived, arrived)


def _rope_tables(S):
    half = ATT_HEAD_DIM // 2
    inv_freq = ROPE_THETA ** (-jnp.arange(half, dtype=F32) / half)
    ang = jnp.arange(S, dtype=F32)[:, None] * inv_freq[None, :]
    cos, sin = jnp.cos(ang), jnp.sin(ang)
    return jnp.concatenate([cos, cos], axis=-1), jnp.concatenate([-sin, sin], axis=-1)


def _rotate(x, cosf, sinf):
    return x * cosf + pltpu.roll(x, ATT_HEAD_DIM // 2, 1) * sinf


def _rope_fwd(proj, cosf, sinf):
    S = proj.shape[0]
    tm = _row_tile(S, 512)

    def body(q_ref, k_ref, v_ref, c_ref, s_ref, qo_ref, ko_ref, vo_ref):
        c, s = c_ref[...], s_ref[...]
        for h in range(ATT_HEADS):
            sl = slice(h * ATT_HEAD_DIM, (h + 1) * ATT_HEAD_DIM)
            qo_ref[:, sl] = _rotate(q_ref[:, sl], c, s).astype(BF16)
        for h in range(ATT_KV_HEADS):
            sl = slice(h * ATT_HEAD_DIM, (h + 1) * ATT_HEAD_DIM)
            ko_ref[:, sl] = _rotate(k_ref[:, sl], c, s).astype(BF16)
        vo_ref[...] = v_ref[...].astype(BF16)

    tab = pl.BlockSpec((tm, ATT_HEAD_DIM), lambda i: (i, 0))
    return pl.pallas_call(
        body, name="rope_fwd", grid=(S // tm,),
        in_specs=[pl.BlockSpec((tm, ATT_WIDTH), lambda i: (i, 0)),
                  pl.BlockSpec((tm, KV_WIDTH), lambda i: (i, OFF_KA // KV_WIDTH)),
                  pl.BlockSpec((tm, KV_WIDTH), lambda i: (i, OFF_VA // KV_WIDTH)), tab, tab],
        out_specs=[pl.BlockSpec((tm, ATT_WIDTH), lambda i: (i, 0)),
                   pl.BlockSpec((tm, KV_WIDTH), lambda i: (i, 0)),
                   pl.BlockSpec((tm, KV_WIDTH), lambda i: (i, 0))],
        out_shape=[jax.ShapeDtypeStruct((S, ATT_WIDTH), BF16), jax.ShapeDtypeStruct((S, KV_WIDTH), BF16),
                   jax.ShapeDtypeStruct((S, KV_WIDTH), BF16)],
        compiler_params=_params(),
    )(proj, proj, proj, cosf, sinf)


def _rope_bwd(dq, dk, dv, cosf, sinf):
    S = dq.shape[0]
    tm = _row_tile(S, 512)
    W = ATT_WIDTH + 2 * KV_WIDTH

    def body(q_ref, k_ref, v_ref, c_ref, s_ref, o_ref):
        c, s = c_ref[...], -s_ref[...]
        for h in range(ATT_HEADS):
            sl = slice(h * ATT_HEAD_DIM, (h + 1) * ATT_HEAD_DIM)
            o_ref[:, sl] = _rotate(q_ref[:, sl], c, s).astype(BF16)
        for h in range(ATT_KV_HEADS):
            sl = slice(h * ATT_HEAD_DIM, (h + 1) * ATT_HEAD_DIM)
            o_ref[:, ATT_WIDTH + h * ATT_HEAD_DIM:ATT_WIDTH + (h + 1) * ATT_HEAD_DIM] = (
                _rotate(k_ref[:, sl], c, s).astype(BF16))
        o_ref[:, ATT_WIDTH + KV_WIDTH:] = v_ref[...].astype(BF16)

    tab = pl.BlockSpec((tm, ATT_HEAD_DIM), lambda i: (i, 0))
    return pl.pallas_call(
        body, name="rope_bwd", grid=(S // tm,),
        in_specs=[pl.BlockSpec((tm, ATT_WIDTH), lambda i: (i, 0)),
                  pl.BlockSpec((tm, KV_WIDTH), lambda i: (i, 0)),
                  pl.BlockSpec((tm, KV_WIDTH), lambda i: (i, 0)), tab, tab],
        out_specs=pl.BlockSpec((tm, W), lambda i: (i, 0)),
        out_shape=jax.ShapeDtypeStruct((S, W), BF16),
        compiler_params=_params(),
    )(dq, dk, dv, cosf, sinf)


def _attn_probs(q_ref, k_refs, sink_ref, kh, n, nb):
    G, L, Dh = ATT_GROUP, BLK, ATT_HEAD_DIM
    q4 = jnp.concatenate([q_ref[:, (kh * G + g) * Dh:(kh * G + g + 1) * Dh] for g in range(G)], axis=0)
    kcat = jnp.concatenate([r[:, kh * Dh:(kh + 1) * Dh] for r in k_refs], axis=0)
    s = _dot_nt(q4, kcat) * (Dh ** -0.5)
    row = lax.broadcasted_iota(jnp.int32, (G * L, 3 * L), 0) % L
    col = lax.broadcasted_iota(jnp.int32, (G * L, 3 * L), 1)
    kpos = (n - 1) * L + col
    mask = (jnp.abs(col - L - row) <= L) & (kpos >= 0) & (kpos < nb * L)
    s = jnp.where(mask, s, -jnp.inf)
    sink = jnp.concatenate([jnp.broadcast_to(sink_ref[kh, :, g:g + 1], (L, 1)) for g in range(G)], axis=0)
    m = jnp.maximum(jnp.max(s, axis=-1, keepdims=True), sink)
    p = jnp.exp(s - m)
    es = jnp.exp(sink - m)
    inv = 1.0 / (jnp.sum(p, axis=-1, keepdims=True) + es)
    return q4, kcat, p * inv, es * inv


def _kv_specs(nb):
    return [pl.BlockSpec((BLK, KV_WIDTH), lambda n: (jnp.maximum(n - 1, 0), 0)),
            pl.BlockSpec((BLK, KV_WIDTH), lambda n: (n, 0)),
            pl.BlockSpec((BLK, KV_WIDTH), lambda n: (jnp.minimum(n + 1, nb - 1), 0))]


def _attn_fwd(qr, kr, va, sink):
    S = qr.shape[0]
    nb = S // BLK
    G, Dh = ATT_GROUP, ATT_HEAD_DIM

    def body(q_ref, k0, k1, k2, v0, v1, v2, sink_ref, o_ref):
        n = pl.program_id(0)
        for kh in range(ATT_KV_HEADS):
            _, _, probs, _ = _attn_probs(q_ref, (k0, k1, k2), sink_ref, kh, n, nb)
            vcat = jnp.concatenate([r[:, kh * Dh:(kh + 1) * Dh] for r in (v0, v1, v2)], axis=0)
            out = _dot(probs.astype(BF16), vcat)
            for g in range(G):
                o_ref[:, (kh * G + g) * Dh:(kh * G + g + 1) * Dh] = out[g * BLK:(g + 1) * BLK, :].astype(BF16)

    qspec = pl.BlockSpec((BLK, ATT_WIDTH), lambda n: (n, 0))
    return pl.pallas_call(
        body, name="attn_fwd", grid=(nb,),
        in_specs=[qspec] + _kv_specs(nb) + _kv_specs(nb)
        + [pl.BlockSpec((ATT_KV_HEADS, 1, G), lambda n: (0, 0, 0))],
        out_specs=qspec,
        out_shape=jax.ShapeDtypeStruct((S, ATT_WIDTH), BF16),
        compiler_params=_params(),
    )(qr, kr, kr, kr, va, va, va, sink.reshape(ATT_KV_HEADS, 1, G))


def _attn_bwd(qr, kr, va, sink, dycat):
    S = qr.shape[0]
    nb = S // BLK
    G, L, Dh = ATT_GROUP, BLK, ATT_HEAD_DIM
    SP = S + 2 * L

    def body(q_ref, k0, k1, k2, v0, v1, v2, sink_ref, do_ref, dq_ref, dk_ref, dv_ref, ds_ref):
        n = pl.program_id(0)

        @pl.when(n == 0)
        def _():
            dk_ref[...] = jnp.zeros_like(dk_ref)
            dv_ref[...] = jnp.zeros_like(dv_ref)
            ds_ref[...] = jnp.zeros_like(ds_ref)

        rows = pl.ds(pl.multiple_of(n * L, L), 3 * L)
        lane = lax.broadcasted_iota(jnp.int32, (8, LANES), 1)
        for kh in range(ATT_KV_HEADS):
            q4, kcat, probs, psink = _attn_probs(q_ref, (k0, k1, k2), sink_ref, kh, n, nb)
            vcat = jnp.concatenate([r[:, kh * Dh:(kh + 1) * Dh] for r in (v0, v1, v2)], axis=0)
            do4 = jnp.concatenate([do_ref[:, (kh * G + g) * Dh:(kh * G + g + 1) * Dh] for g in range(G)], axis=0)
            pb = probs.astype(BF16)
            dob = do4.astype(BF16)
            out = _dot(pb, vcat)
            delta = jnp.sum(do4 * out, axis=-1, keepdims=True)
            dp = _dot_nt(dob, vcat)
            dsc = (probs * (dp - delta) * (Dh ** -0.5)).astype(BF16)
            dq4 = _dot(dsc, kcat)
            for g in range(G):
                dq_ref[:, (kh * G + g) * Dh:(kh * G + g + 1) * Dh] = dq4[g * L:(g + 1) * L, :]
            dk_ref[rows, kh * Dh:(kh + 1) * Dh] += _dot_tn(dsc, q4)
            dv_ref[rows, kh * Dh:(kh + 1) * Dh] += _dot_tn(pb, dob)
            dsink = jnp.zeros((8, LANES), F32)
            for g in range(G):
                val = -jnp.sum(psink[g * L:(g + 1) * L] * delta[g * L:(g + 1) * L], axis=0, keepdims=True)
                dsink = dsink + jnp.where(lane == g, jnp.broadcast_to(val, (8, LANES)), 0.0)
            ds_ref[kh] += dsink

    qspec = pl.BlockSpec((L, ATT_WIDTH), lambda n: (n, 0))
    accspec = pl.BlockSpec((SP, KV_WIDTH), lambda n: (0, 0))
    dq, dkp, dvp, dsink = pl.pallas_call(
        body, name="attn_bwd", grid=(nb,),
        in_specs=[qspec] + _kv_specs(nb) + _kv_specs(nb)
        + [pl.BlockSpec((ATT_KV_HEADS, 1, G), lambda n: (0, 0, 0)), qspec],
        out_specs=[qspec, accspec, accspec, pl.BlockSpec((ATT_KV_HEADS, 8, LANES), lambda n: (0, 0, 0))],
        out_shape=[jax.ShapeDtypeStruct((S, ATT_WIDTH), F32), jax.ShapeDtypeStruct((SP, KV_WIDTH), F32),
                   jax.ShapeDtypeStruct((SP, KV_WIDTH), F32),
                   jax.ShapeDtypeStruct((ATT_KV_HEADS, 8, LANES), F32)],
        compiler_params=_params(),
    )(qr, kr, kr, kr, va, va, va, sink.reshape(ATT_KV_HEADS, 1, G), dycat)
    return dq, dkp[L:L + S], dvp[L:L + S], dsink[:, 0, :G].reshape(ATT_HEADS)


CONV_COLS = 256
CONV_ROWS = 512


def _conv_taps(xs, w_ref, rows):
    total = xs.shape[0]
    acc = None
    for j in range(CONV_WIDTH):
        shift = (CONV_WIDTH // 2 - j) % total
        term = (pltpu.roll(xs, shift, 0) if shift else xs)[CONV_HALO:CONV_HALO + rows, :] * w_ref[j:j + 1, :]
        acc = term if acc is None else acc + term
    return acc


def _conv_window(x_ref, i, R, nrow):
    S = x_ref.shape[0]
    r0 = pl.multiple_of(i * R, R)
    top = x_ref[pl.ds(pl.multiple_of(jnp.maximum(r0 - CONV_HALO, 0), CONV_HALO), CONV_HALO), :]
    bot = x_ref[pl.ds(pl.multiple_of(jnp.minimum(r0 + R, S - CONV_HALO), CONV_HALO), CONV_HALO), :]
    top = jnp.where(i > 0, top, 0.0)
    bot = jnp.where(i < nrow - 1, bot, 0.0)
    return jnp.concatenate([top, x_ref[pl.ds(r0, R), :], bot], axis=0)


def _conv5(x, w, *, col0, act, out_dtype, name):
    S = x.shape[0]
    C = w.shape[1]
    R = _row_tile(S, CONV_ROWS)
    tc = CONV_COLS
    half_blocks = (C // 2) // tc
    nrow = S // R

    def body(x_ref, w_ref, o_ref):
        scale = jnp.where(pl.program_id(0) >= half_blocks, M_HEAD_DIM ** -0.5, 1.0)
        y = _conv_taps(_conv_window(x_ref, pl.program_id(1), R, nrow), w_ref, R)
        if act:
            y = y * _sigmoid(y) * scale
        o_ref[...] = y.astype(out_dtype)

    return pl.pallas_call(
        body, name=name, grid=(C // tc, nrow),
        in_specs=[pl.BlockSpec((S, tc), lambda c, i: (0, col0 // tc + c)),
                  pl.BlockSpec((CONV_WIDTH, tc), lambda c, i: (0, c))],
        out_specs=pl.BlockSpec((R, tc), lambda c, i: (i, c)),
        out_shape=jax.ShapeDtypeStruct((S, C), out_dtype),
        compiler_params=_params(),
    )(x, w)


def _conv_bwd_pre(x, col0, w, dq2, dk2):
    S = x.shape[0]
    C = w.shape[1]
    R = _row_tile(S, CONV_ROWS)
    tc = CONV_COLS
    half_blocks = (C // 2) // tc
    nrow = S // R

    def body(x_ref, w_ref, dqa_ref, dqb_ref, dka_ref, dkb_ref, o_ref, dw_ref, acc):
        is_k = pl.program_id(0) >= half_blocks
        i = pl.program_id(1)
        scale = jnp.where(is_k, M_HEAD_DIM ** -0.5, 1.0)

        @pl.when(i == 0)
        def _():
            acc[...] = jnp.zeros_like(acc)

        xs = _conv_window(x_ref, i, R, nrow)
        y = _conv_taps(xs, w_ref, R)
        sg = _sigmoid(y)
        dqv = dqa_ref[...] + dqb_ref[...]
        dkv = dka_ref[...] + dkb_ref[...]
        dpre = jnp.where(is_k, dkv, dqv) * scale * (sg * (1.0 + y * (1.0 - sg)))
        o_ref[...] = dpre
        total = xs.shape[0]
        for j in range(CONV_WIDTH):
            shift = (CONV_WIDTH // 2 - j) % total
            xj = (pltpu.roll(xs, shift, 0) if shift else xs)[CONV_HALO:CONV_HALO + R, :]
            acc[j:j + 1, :] += jnp.sum(dpre * xj, axis=0, keepdims=True)

        @pl.when(i == nrow - 1)
        def _():
            dw_ref[...] = acc[0:CONV_WIDTH, :]

    nqb = (C // 2) // tc
    qmap = lambda d: (lambda c, i: (d, i, jnp.minimum(c, nqb - 1)))
    kmap = lambda d: (lambda c, i: (d, i, jnp.maximum(c - nqb, 0)))
    gspec = lambda m: pl.BlockSpec((None, R, tc), m)
    return pl.pallas_call(
        body, name="conv_bwd_pre", grid=(C // tc, nrow),
        in_specs=[pl.BlockSpec((S, tc), lambda c, i: (0, col0 // tc + c)),
                  pl.BlockSpec((CONV_WIDTH, tc), lambda c, i: (0, c)),
                  gspec(qmap(0)), gspec(qmap(1)), gspec(kmap(0)), gspec(kmap(1))],
        out_specs=[pl.BlockSpec((R, tc), lambda c, i: (i, c)), pl.BlockSpec((CONV_WIDTH, tc), lambda c, i: (0, c))],
        out_shape=[jax.ShapeDtypeStruct((S, C), F32), jax.ShapeDtypeStruct((CONV_WIDTH, C), F32)],
        scratch_shapes=[pltpu.VMEM((8, tc), F32)],
        compiler_params=_params(),
    )(x, w, dq2, dq2, dk2, dk2)


def _lane_cumsum(x, reverse):
    lane = lax.broadcasted_iota(jnp.int32, x.shape, 1)
    sh = 1
    while sh < LANES:
        if reverse:
            x = x + jnp.where(lane < LANES - sh, pltpu.roll(x, LANES - sh, 1), 0.0)
        else:
            x = x + jnp.where(lane >= sh, pltpu.roll(x, sh, 1), 0.0)
        sh *= 2
    return x


def _gate_prep(gates_t, bias):
    R = gates_t.shape[0]
    half, quarter = R // 2, R // 4

    def body(g_ref, b_ref, ig_ref, cum_ref):
        ig_ref[...] = g_ref[0:half, :] + b_ref[0:half, :]
        fg = g_ref[half:R, :] + b_ref[half:R, :]
        lf = jnp.minimum(fg, 0.0) - jnp.log(1.0 + jnp.exp(-jnp.abs(fg)))
        cum_ref[0:quarter, :] = _lane_cumsum(lf[0:quarter, :], False)
        cum_ref[quarter:half, :] = _lane_cumsum(lf[quarter:half, :], True)

    return pl.pallas_call(
        body, name="gate_prep",
        out_shape=[jax.ShapeDtypeStruct((half, LANES), F32), jax.ShapeDtypeStruct((half, LANES), F32)],
        compiler_params=_params(),
    )(gates_t, bias)


def _gate_bwd(gates_t, bias, di, dfc, dbl, nc):
    R = gates_t.shape[0]
    half, quarter = R // 2, R // 4

    def body(g_ref, b_ref, di_ref, df_ref, dbl_ref, dg_ref, db_ref):
        dg_ref[0:half, :] = di_ref[...]
        dfv = df_ref[...]
        within = jnp.concatenate([_lane_cumsum(dfv[0:quarter, :], True),
                                  _lane_cumsum(dfv[quarter:half, :], False)], axis=0)
        fg = g_ref[half:R, :] + b_ref[half:R, :]
        dg_ref[half:R, :] = (within + dbl_ref[...]) * _sigmoid(-fg)
        rows = jnp.broadcast_to(jnp.sum(dg_ref[...], axis=-1, keepdims=True), (R, LANES))
        gr = lax.broadcasted_iota(jnp.int32, (N_GATES, R), 0)
        gc = lax.broadcasted_iota(jnp.int32, (N_GATES, R), 1)
        db_ref[...] = lax.dot_general((gc // nc == gr).astype(F32), rows, (((1,), (0,)), ((), ())),
                                      precision=lax.Precision.HIGHEST, preferred_element_type=F32)

    return pl.pallas_call(
        body, name="gate_bwd",
        out_shape=[jax.ShapeDtypeStruct((R, LANES), F32), jax.ShapeDtypeStruct((N_GATES, LANES), F32)],
        compiler_params=_params(),
    )(gates_t, bias, di, dfc, dbl)


def _chunk_index(d, c, nc, reverse):
    j = (nc - 1 - c) if reverse else c
    return j + d * (nc - 1 - 2 * j)


def _mlstm_chunk(d, q, k, vb, brow, igrow, C, nvec, m_prev, eye):
    L = BLK
    r = lax.broadcasted_iota(jnp.int32, (L, L), 0)
    c = lax.broadcasted_iota(jnp.int32, (L, L), 1)
    mask = (r - c) * (1 - 2 * d) >= 0
    bcol = _row_to_col(brow, eye)
    igcol = _row_to_col(igrow, eye)
    log_d = jnp.where(mask, bcol - brow + igrow, -jnp.inf)
    log_inter = bcol + m_prev
    m_t = jnp.maximum(log_inter, jnp.max(log_d, axis=-1, keepdims=True))
    d_mat = jnp.exp(log_d - m_t)
    inter = jnp.exp(log_inter - m_t)
    s = _dot_nt(q, k) * d_mat
    sb = s.astype(BF16)
    cb = C.astype(BF16)
    num = _dot(sb, vb) + inter * _dot_nt(q, cb)
    den = jnp.sum(s, axis=-1, keepdims=True) + inter * jnp.sum(q.astype(F32) * nvec, axis=-1, keepdims=True)
    floor = jnp.exp(-m_t)
    denom = jnp.maximum(jnp.abs(den), floor)
    b_last = jnp.where(d == 0, brow[:, L - 1:L], brow[:, 0:1])
    return dict(bcol=bcol, igcol=igcol, m_t=m_t, d_mat=d_mat, inter=inter, sb=sb, cb=cb, num=num, den=den,
                floor=floor, denom=denom, b_last=b_last)


def _head_cols(v0_ref, v1_ref, hd):
    ref = v0_ref if hd < M_HEADS // 2 else v1_ref
    lo = (hd % (M_HEADS // 2)) * M_HEAD_DIM
    return ref[:, lo:lo + M_HEAD_DIM]


def _mlstm_fwd(qk, proj, ig5, b5):
    S = qk.shape[0]
    nc = S // BLK
    L, Dh, H = BLK, M_HEAD_DIM, M_HEADS

    def body(q_ref, k_ref, v0_ref, v1_ref, ig_ref, b_ref, h_ref, cs_ref, ns_ref, ms_ref, C, nvec, m):
        d = pl.program_id(0)

        @pl.when(pl.program_id(1) == 0)
        def _():
            C[...] = jnp.zeros_like(C)
            nvec[...] = jnp.zeros_like(nvec)
            m[...] = jnp.zeros_like(m)

        eye = _eye_mask(L)
        for hd in range(H):
            cols = slice(hd * Dh, (hd + 1) * Dh)
            q, k = q_ref[:, cols], k_ref[:, cols]
            vf = _head_cols(v0_ref, v1_ref, hd)
            brow, igrow = b_ref[hd], ig_ref[hd]
            m_prev = m[hd, :, 0:1]
            cs_ref[hd] = C[hd].astype(BF16)
            ns_ref[hd] = nvec[hd]
            ms_ref[hd] = m[hd]
            f = _mlstm_chunk(d, q, k, vf.astype(BF16), brow, igrow, C[hd], nvec[hd], m_prev, eye)
            h_ref[:, cols] = f["num"] / f["denom"]
            b_last = f["b_last"]
            log_w = b_last - brow + igrow
            m_new = jnp.maximum(b_last + m_prev, jnp.max(log_w, axis=-1, keepdims=True))
            w_col = jnp.exp(b_last - f["bcol"] + f["igcol"] - m_new)
            decay = jnp.exp(b_last + m_prev - m_new)
            C[hd] = decay * C[hd] + _dot_tn((w_col * vf).astype(BF16), k)
            nvec[hd] = decay * nvec[hd] + jnp.sum(w_col * k.astype(F32), axis=0, keepdims=True)
            m[hd] = jnp.broadcast_to(m_new, (1, LANES))

    cidx = lambda d, c: _chunk_index(d, c, nc, False)
    gspec = pl.BlockSpec((None, H, None, 1, LANES), lambda d, c: (d, 0, cidx(d, c), 0, 0))
    st = lambda *blk: pl.BlockSpec((None, H, None) + blk, lambda d, c: (d, 0, cidx(d, c), 0, 0))
    half = M_WIDTH // 2
    return pl.pallas_call(
        body, name="mlstm_fwd", grid=(2, nc),
        in_specs=[pl.BlockSpec((L, M_WIDTH), lambda d, c: (cidx(d, c), 0)),
                  pl.BlockSpec((L, M_WIDTH), lambda d, c: (cidx(d, c), 1)),
                  pl.BlockSpec((L, half), lambda d, c: (cidx(d, c), OFF_VM // half)),
                  pl.BlockSpec((L, half), lambda d, c: (cidx(d, c), OFF_VM // half + 1)), gspec, gspec],
        out_specs=[pl.BlockSpec((None, L, M_WIDTH), lambda d, c: (d, cidx(d, c), 0)),
                   st(Dh, Dh), st(1, Dh), st(1, LANES)],
        out_shape=[jax.ShapeDtypeStruct((2, S, M_WIDTH), F32), jax.ShapeDtypeStruct((2, H, nc, Dh, Dh), BF16),
                   jax.ShapeDtypeStruct((2, H, nc, 1, Dh), F32), jax.ShapeDtypeStruct((2, H, nc, 1, LANES), F32)],
        scratch_shapes=[pltpu.VMEM((H, Dh, Dh), F32), pltpu.VMEM((H, 1, Dh), F32), pltpu.VMEM((H, 1, LANES), F32)],
        compiler_params=_params(),
    )(qk, qk, proj, proj, ig5, b5)


def _mlstm_bwd(qk, proj, ig5, b5, cs, ns, ms, dhm):
    S = qk.shape[0]
    nc = S // BLK
    L, Dh, H = BLK, M_HEAD_DIM, M_HEADS

    def body(q_ref, k_ref, v0_ref, v1_ref, ig_ref, b_ref, cs_ref, ns_ref, ms_ref, dh_ref,
             dq_ref, dk_ref, dv_ref, di_ref, df_ref, dbl_ref, R, rvec, mu):
        d = pl.program_id(0)

        @pl.when(pl.program_id(1) == 0)
        def _():
            R[...] = jnp.zeros_like(R)
            rvec[...] = jnp.zeros_like(rvec)
            mu[...] = jnp.full(mu.shape, NEG_BIG, F32)

        eye = _eye_mask(L)
        for hd in range(H):
            cols = slice(hd * Dh, (hd + 1) * Dh)
            q, k = q_ref[:, cols], k_ref[:, cols]
            qf, kf = q.astype(F32), k.astype(F32)
            vb = _head_cols(v0_ref, v1_ref, hd).astype(BF16)
            brow, igrow = b_ref[hd], ig_ref[hd]
            nprev = ns_ref[hd]
            m_prev = ms_ref[hd, :, 0:1]
            cprev = cs_ref[hd]
            f = _mlstm_chunk(d, q, k, vb, brow, igrow, cprev, nprev, m_prev, eye)
            dh = dh_ref[:, cols]
            inv = 1.0 / f["denom"]
            hcur = f["num"] * inv
            dnum = dh * inv
            active = jnp.abs(f["den"]) >= f["floor"]
            dden = (jnp.where(active, -jnp.sign(f["den"]), 0.0)
                    * jnp.sum(dh * hcur, axis=-1, keepdims=True) * inv)
            dnb = dnum.astype(BF16)
            dqk = ((_dot_nt(dnb, vb) + dden) * f["d_mat"]).astype(BF16)
            dq = _dot(dqk, k) + f["inter"] * (_dot(dnb, f["cb"]) + dden * nprev)
            mu_prev = mu[hd, :, 0:1]
            a_col = jnp.exp(f["b_last"] - f["bcol"] + f["igcol"] + mu_prev)
            rb = R[hd].astype(BF16)
            dv = _dot_tn(f["sb"], dnb) + a_col * _dot_nt(k, rb)
            dk_inter = a_col * (_dot(vb, rb) + rvec[hd])
            dk = _dot_tn(dqk, q) + dk_inter
            dq_ref[:, cols] = dq
            dk_ref[:, cols] = dk
            dv_ref[:, cols] = dv
            kdk = jnp.sum(kf * dk, axis=-1, keepdims=True)
            qdq = jnp.sum(qf * dq, axis=-1, keepdims=True)
            di_ref[hd] = _col_to_row(kdk, eye)
            df_ref[hd] = _col_to_row(qdq - kdk, eye)
            older = jnp.sum(jnp.sum(R[hd] * cprev.astype(F32), axis=-1, keepdims=True), axis=0, keepdims=True)
            older = older + jnp.sum(rvec[hd] * nprev, axis=-1, keepdims=True)
            dbl = (jnp.sum(jnp.sum(kf * dk_inter, axis=-1, keepdims=True), axis=0, keepdims=True)
                   + jnp.exp(f["b_last"] + mu_prev + m_prev) * older)
            dbl_ref[hd] = jnp.broadcast_to(dbl, (1, LANES))
            lw = f["bcol"] - f["m_t"]
            mu_new = jnp.maximum(f["b_last"] + mu_prev, jnp.max(lw, axis=0, keepdims=True))
            wq = jnp.exp(lw - mu_new)
            decay = jnp.exp(f["b_last"] + mu_prev - mu_new)
            R[hd] = decay * R[hd] + _dot_tn((wq * dnum).astype(BF16), q)
            rvec[hd] = decay * rvec[hd] + jnp.sum(wq * dden * qf, axis=0, keepdims=True)
            mu[hd] = jnp.broadcast_to(mu_new, (1, LANES))

    cidx = lambda d, c: _chunk_index(d, c, nc, True)
    gin = pl.BlockSpec((None, H, None, 1, LANES), lambda d, c: (d, 0, cidx(d, c), 0, 0))
    st = lambda *blk: pl.BlockSpec((None, H, None) + blk, lambda d, c: (d, 0, cidx(d, c), 0, 0))
    per_dir = pl.BlockSpec((None, L, M_WIDTH), lambda d, c: (d, cidx(d, c), 0))
    big = jax.ShapeDtypeStruct((2, S, M_WIDTH), F32)
    small = jax.ShapeDtypeStruct((2, H, nc, 1, LANES), F32)
    half = M_WIDTH // 2
    return pl.pallas_call(
        body, name="mlstm_bwd", grid=(2, nc),
        in_specs=[pl.BlockSpec((L, M_WIDTH), lambda d, c: (cidx(d, c), 0)),
                  pl.BlockSpec((L, M_WIDTH), lambda d, c: (cidx(d, c), 1)),
                  pl.BlockSpec((L, half), lambda d, c: (cidx(d, c), OFF_VM // half)),
                  pl.BlockSpec((L, half), lambda d, c: (cidx(d, c), OFF_VM // half + 1)), gin, gin,
                  st(Dh, Dh), st(1, Dh), st(1, LANES),
                  pl.BlockSpec((L, M_WIDTH), lambda d, c: (cidx(d, c), 0))],
        out_specs=[per_dir, per_dir, per_dir, gin, gin, gin],
        out_shape=[big, big, big, small, small, small],
        scratch_shapes=[pltpu.VMEM((H, Dh, Dh), F32), pltpu.VMEM((H, 1, Dh), F32), pltpu.VMEM((H, 1, LANES), F32)],
        compiler_params=_params(),
    )(qk, qk, proj, proj, ig5, b5, cs, ns, ms, dhm)


def _mlstm_out_fwd(hs, proj, gamma):
    S = hs.shape[1]
    tm = _row_tile(S, 512)
    Dh, H = M_HEAD_DIM, M_HEADS

    def body(hf_ref, hb_ref, o_ref, g_ref, y_ref):
        hm = hf_ref[...] + hb_ref[...]
        r = lax.rsqrt(jnp.mean(hm * hm, axis=-1, keepdims=True) + EPS)
        y_ref[...] = (_sigmoid(o_ref[...]) * (hm * r * g_ref[...])).astype(BF16)

    hspec = lambda d: pl.BlockSpec((None, tm, Dh), lambda i, h: (d, i, h))
    return pl.pallas_call(
        body, name="mlstm_out_fwd", grid=(S // tm, H),
        in_specs=[hspec(0), hspec(1), pl.BlockSpec((tm, Dh), lambda i, h: (i, OFF_OM // Dh + h)),
                  pl.BlockSpec((1, Dh), lambda i, h: (0, h))],
        out_specs=pl.BlockSpec((tm, Dh), lambda i, h: (i, h)),
        out_shape=jax.ShapeDtypeStruct((S, M_WIDTH), BF16),
        compiler_params=_params(),
    )(hs, hs, proj, gamma.reshape(1, M_WIDTH))


def _mlstm_out_bwd(hs, proj, gamma, dycat):
    S = hs.shape[1]
    tm = _row_tile(S, 512)
    Dh, H = M_HEAD_DIM, M_HEADS
    D_OFF = ATT_WIDTH // Dh

    def body(hf_ref, hb_ref, o_ref, g_ref, dy_ref, dh_ref, do_ref, dg_ref):
        hm = hf_ref[...] + hb_ref[...]
        r = lax.rsqrt(jnp.mean(hm * hm, axis=-1, keepdims=True) + EPS)
        hh = hm * r
        so = _sigmoid(o_ref[...])
        dy = dy_ref[...]
        gam = g_ref[...]
        do_ref[...] = (dy * hh * gam * so * (1.0 - so)).astype(BF16)
        dyn = dy * so
        gd = dyn * gam
        dh_ref[...] = r * (gd - hh * jnp.mean(hh * gd, axis=-1, keepdims=True))
        part = jnp.sum(dyn * hh, axis=0, keepdims=True)

        @pl.when(pl.program_id(1) == 0)
        def _():
            dg_ref[...] = part

        @pl.when(pl.program_id(1) > 0)
        def _():
            dg_ref[...] += part

    hspec = lambda d: pl.BlockSpec((None, tm, Dh), lambda h, i: (d, i, h))
    out = pl.BlockSpec((tm, Dh), lambda h, i: (i, h))
    vec = pl.BlockSpec((1, Dh), lambda h, i: (0, h))
    dhm, do, dg = pl.pallas_call(
        body, name="mlstm_out_bwd", grid=(H, S // tm),
        in_specs=[hspec(0), hspec(1), pl.BlockSpec((tm, Dh), lambda h, i: (i, OFF_OM // Dh + h)), vec,
                  pl.BlockSpec((tm, Dh), lambda h, i: (i, D_OFF + h))],
        out_specs=[out, out, vec],
        out_shape=[jax.ShapeDtypeStruct((S, M_WIDTH), F32), jax.ShapeDtypeStruct((S, M_WIDTH), BF16),
                   jax.ShapeDtypeStruct((1, M_WIDTH), F32)],
        compiler_params=_params(),
    )(hs, hs, proj, gamma.reshape(1, M_WIDTH), dycat)
    return dhm, do, dg.reshape(M_WIDTH)


def _sum_cast(a2, name):
    _, S, C = a2.shape
    tm = _row_tile(S, 512)

    def body(a_ref, b_ref, o_ref):
        o_ref[...] = (a_ref[...] + b_ref[...]).astype(BF16)

    spec = lambda d: pl.BlockSpec((None, tm, C), lambda i: (d, i, 0))
    return pl.pallas_call(
        body, name=name, grid=(S // tm,),
        in_specs=[spec(0), spec(1)], out_specs=pl.BlockSpec((tm, C), lambda i: (i, 0)),
        out_shape=jax.ShapeDtypeStruct((S, C), BF16),
        compiler_params=_params(),
    )(a2, a2)


def _adamw(w, g, m, v):
    shape = w.shape
    C = shape[-1]
    R = w.size // C
    tm = _elementwise_rows(R, C)

    def body(w_ref, g_ref, m_ref, v_ref, d_ref, mo_ref, vo_ref):
        gv = g_ref[...]
        mn = ADAM_B1 * m_ref[...] + (1.0 - ADAM_B1) * gv
        vn = ADAM_B2 * v_ref[...] + (1.0 - ADAM_B2) * (gv * gv)
        m_hat = mn / (1.0 - ADAM_B1 ** ADAM_STEP)
        v_hat = vn / (1.0 - ADAM_B2 ** ADAM_STEP)
        d_ref[...] = -ADAM_LR * (m_hat / (jnp.sqrt(v_hat) + ADAM_EPS) + ADAM_WD * w_ref[...])
        mo_ref[...] = mn
        vo_ref[...] = vn

    row = pl.BlockSpec((tm, C), lambda i: (i, 0))
    sds = jax.ShapeDtypeStruct((R, C), F32)
    outs = pl.pallas_call(
        body, name="adamw", grid=(R // tm,),
        in_specs=[row] * 4, out_specs=[row] * 3, out_shape=[sds] * 3,
        compiler_params=_params(),
    )(*[t.reshape(R, C) for t in (w, g, m, v)])
    return tuple(o.reshape(shape) for o in outs)


CID_WEIGHT_GATHER, CID_PAIR_SWAP, CID_SCATTER = 0, 1, 2


def _other_chips(x, y):
    return [(1 - x, y), (x, 1 - y), (1 - x, 1 - y)]


def _handshake(peers):
    barrier = pltpu.get_barrier_semaphore()
    for peer in peers:
        pl.semaphore_signal(barrier, inc=1, device_id=peer, device_id_type=MESH)
    pl.semaphore_wait(barrier, len(peers))


def _sequencer_call(body, name, out_type, sem_counts, collective_id, operands):
    return pl.kernel(
        body, name=name, out_type=out_type,
        mesh=plsc.ScalarSubcoreMesh(axis_name="sequencer", num_cores=1),
        scratch_types=[pltpu.SemaphoreType.DMA((n,)) for n in sem_counts],
        compiler_params=pltpu.CompilerParams(collective_id=collective_id),
    )(*operands)


def _comm_weight_gather(groups):
    T = len(groups)
    flat = [(t, sum(p.shape[1] for p in g[:i]), part) for t, g in enumerate(groups) for i, part in enumerate(g)]
    N = len(flat)

    def body(*refs):
        ins, outs = refs[:N], refs[N:N + T]
        send, recv, local = refs[N + T:]
        x, y, c = lax.axis_index("x"), lax.axis_index("y"), lax.axis_index("c")
        me = 2 * x + y
        chips = _other_chips(x, y)
        _handshake([(px, py, c) for px, py in chips])

        def place(i, chip):
            t, off, part = flat[i]
            return outs[t].at[chip, :, pl.ds(off, part.shape[1])]

        mine = []
        for i in range(N):
            cp = pltpu.make_async_copy(ins[i], place(i, me), local.at[i])
            cp.start()
            mine.append(cp)
            for k, (px, py) in enumerate(chips):
                pltpu.make_async_remote_copy(
                    src_ref=ins[i], dst_ref=place(i, me), send_sem=send.at[3 * i + k], recv_sem=recv.at[3 * i + k],
                    device_id=(px, py, c), device_id_type=MESH).start()
        for i in range(N):
            for k, (px, py) in enumerate(chips):
                cp = pltpu.make_async_remote_copy(
                    src_ref=ins[i], dst_ref=place(i, 2 * px + py), send_sem=send.at[3 * i + k],
                    recv_sem=recv.at[3 * i + k], device_id=(px, py, c), device_id_type=MESH)
                cp.wait_send()
                cp.wait_recv()
            mine[i].wait()

    out_type = tuple(jax.ShapeDtypeStruct((4, g[0].shape[0], sum(p.shape[1] for p in g)), g[0].dtype) for g in groups)
    return _sequencer_call(
        body, "comm_weight_gather", out_type, (3 * N, 3 * N, N), CID_WEIGHT_GATHER, [part for _, _, part in flat])


def _comm_to_sibling(xs, other_half=False):
    T = len(xs)

    def body(*refs):
        ins, outs = refs[:T], refs[T:2 * T]
        send, recv = refs[2 * T:]
        x, y, c = lax.axis_index("x"), lax.axis_index("y"), lax.axis_index("c")
        _handshake([(x, y, 1 - c)])
        cps = [pltpu.make_async_remote_copy(
            src_ref=ins[t].at[:, 1 - c] if other_half else ins[t], dst_ref=outs[t], send_sem=send.at[t],
            recv_sem=recv.at[t], device_id=(x, y, 1 - c), device_id_type=MESH) for t in range(T)]
        for cp in cps:
            cp.start()
        for cp in cps:
            cp.wait_send()
            cp.wait_recv()

    shapes = [s.shape[:1] + s.shape[2:] if other_half else s.shape for s in xs]
    return _sequencer_call(
        body, "comm_pair_swap", tuple(jax.ShapeDtypeStruct(sh, s.dtype) for sh, s in zip(shapes, xs)),
        (T, T), CID_PAIR_SWAP, xs)


def _comm_scatter_to_owners(ps):
    T = len(ps)

    def body(*refs):
        ins, outs = refs[:T], refs[T:2 * T]
        send, recv = refs[2 * T:]
        x, y, c = lax.axis_index("x"), lax.axis_index("y"), lax.axis_index("c")
        chips = _other_chips(x, y)
        _handshake([(px, py, c) for px, py in chips])
        cps = []
        for t in range(T):
            for k, (px, py) in enumerate(chips):
                cp = pltpu.make_async_remote_copy(
                    src_ref=ins[t].at[2 * px + py], dst_ref=outs[t].at[k], send_sem=send.at[3 * t + k],
                    recv_sem=recv.at[3 * t + k], device_id=(px, py, c), device_id_type=MESH)
                cp.start()
                cps.append(cp)
        for cp in cps:
            cp.wait_send()
            cp.wait_recv()

    return _sequencer_call(
        body, "comm_scatter_to_owners", tuple(jax.ShapeDtypeStruct((3,) + p.shape[1:], p.dtype) for p in ps),
        (3 * T, 3 * T), CID_SCATTER, ps)


def _comm_all_to_all_small(pack):
    R = pack.shape[0]

    def body(in_ref, out_ref, send, recv, local):
        x, y, c = lax.axis_index("x"), lax.axis_index("y"), lax.axis_index("c")
        me = 4 * x + 2 * y + c
        mine = pltpu.make_async_copy(in_ref, out_ref.at[me], local.at[0])
        mine.start()
        peers = []
        for k in range(1, 8):
            fx, fy, fc = (k >> 2) & 1, (k >> 1) & 1, k & 1
            peers.append((x + fx * (1 - 2 * x), y + fy * (1 - 2 * y), c + fc * (1 - 2 * c)))
        for k, peer in enumerate(peers):
            pltpu.make_async_remote_copy(
                src_ref=in_ref, dst_ref=out_ref.at[me], send_sem=send.at[k], recv_sem=recv.at[k],
                device_id=peer, device_id_type=MESH).start()
        for k, (px, py, pc) in enumerate(peers):
            cp = pltpu.make_async_remote_copy(
                src_ref=in_ref, dst_ref=out_ref.at[4 * px + 2 * py + pc], send_sem=send.at[k], recv_sem=recv.at[k],
                device_id=(px, py, pc), device_id_type=MESH)
            cp.wait_send()
            cp.wait_recv()
        mine.wait()

    return pl.pallas_call(
        body, name="comm_all_to_all_small",
        in_specs=[ANY], out_specs=ANY,
        out_shape=jax.ShapeDtypeStruct((8, R, LANES), F32),
        scratch_shapes=[pltpu.SemaphoreType.DMA((7,)), pltpu.SemaphoreType.DMA((7,)), pltpu.SemaphoreType.DMA((1,))],
    )(pack)


def _tie(*trees):
    return lax.optimization_barrier(trees)


class _ReduceScatter:
    def __init__(self, grads, dh):
        split = [g.reshape(4, 2, g.shape[1] // 2, g.shape[2]) for g in grads]
        self.split, self.dh = _tie(split, dh)
        self.got = _comm_to_sibling(self.split, other_half=True)

    def scatter(self, dh):
        split, got, dh = _tie(self.split, list(self.got), dh)
        pair = [_pair_sum(a, b) for a, b in zip(split, got)]
        self.pair, dh = _tie(pair, dh)
        self.arrived = _comm_scatter_to_owners(self.pair)
        return dh

    def gather(self, dh):
        pair, arrived, dh = _tie(self.pair, list(self.arrived), dh)
        halves = [_chip_sum(p, a) for p, a in zip(pair, arrived)]
        self.halves, dh = _tie(halves, dh)
        self.other = _comm_to_sibling(self.halves)
        return dh

    def settle(self, dh):
        self.other, dh = _tie(list(self.other), dh)
        return dh

    def result(self):
        south = lax.axis_index("c") == 0
        return [jnp.where(south, jnp.concatenate([mine, theirs]), jnp.concatenate([theirs, mine]))
                for mine, theirs in zip(self.halves, self.other)]


def _ffn_fwd(x, g_pre, g_post, wgu, wd):
    S, D = x.shape
    F = wd.shape[1]
    xn = _norm_fwd(x, g_pre)
    gu, h = _ffn_up(xn, wgu)
    f = _mm(h, wd, "nn", ja="r", jb="r", tm=_row_tile(S, ACT_ROWS), tn=MODEL_COLS, tk=F, name="ffn_down")
    x_out = _resid_norm_fwd(x, f, g_post, 0.5)
    return x_out, (x, xn, gu, h, f)


def _ffn_bwd(dx, saved, g_pre, g_post, wgu, wd, mid=None, on_grads=None, late=None):
    x, xn, gu, h, f = saved
    S, D = x.shape
    J, F, _ = wd.shape
    tm, tk = _row_tile(S, ACT_ROWS), _row_tile(S, TOKEN_ROWS)
    df, dg_post = _norm_bwd(f, g_post, dx, 0.5, out_dtype=BF16)
    d_wd = _mm(h, df, "tn", ja="b", tm=F, tn=MODEL_COLS, tk=tk, name="ffn_dwd")
    dgu = _ffn_bwd_hidden(df, wd, gu)
    if mid is not None:
        dgu = mid(dgu)
    d_wgu = _mm(xn, dgu, "tn", jb="b", tm=MODEL_COLS, tn=F, tk=tk, name="ffn_dwgu")
    if on_grads is not None:
        dgu = on_grads(d_wgu, d_wd, dgu)
    dxn = _mm(dgu, wgu, "nt", ja="r", jb="r", tm=tm, tn=MODEL_COLS, tk=2 * F, name="ffn_dxn")
    if late is not None:
        dxn = late(dxn)
    dx_in, dg_pre = _norm_bwd(x, g_pre, dxn, 1.0, resid=dx)
    return dx_in, dg_pre, dg_post, d_wgu, d_wd


def _gates_layout(proj, b_gate, nc):
    gt = jnp.transpose(proj[:, OFF_G:OFF_G + N_GATES]).reshape(N_GATES * nc, LANES)
    bias = jnp.repeat(b_gate, nc).reshape(N_GATES * nc, 1)
    return gt, bias


def _mixer_fwd(x, p, cosf, sinf):
    S, D = x.shape
    nc = S // BLK
    hn = _norm_fwd(x, p["g_pre"])
    proj = _mm(hn, p["w_in"], "nn", tm=_row_tile(S, ACT_ROWS), tn=IN_PAD // 5, tk=D, name="in_proj")
    qr, kr, va = _rope_fwd(proj, cosf, sinf)
    y_att = _attn_fwd(qr, kr, va, p["sink"])
    qk = _conv5(proj, p["conv_w"], col0=OFF_QM, act=True, out_dtype=BF16, name="conv_silu")
    gt, bias = _gates_layout(proj, p["b_gate"], nc)
    ig, cum = _gate_prep(gt, bias)
    ig5, b5 = ig.reshape(2, M_HEADS, nc, 1, LANES), cum.reshape(2, M_HEADS, nc, 1, LANES)
    hs, cs, ns, ms = _mlstm_fwd(qk, proj, ig5, b5)
    y_m = _mlstm_out_fwd(hs, proj, p["gamma"])
    ycat = jnp.concatenate([y_att, y_m], axis=-1)
    mix = _mm(ycat, p["w_out"], "nn", tm=_row_tile(S, ACT_ROWS), tn=MODEL_COLS, tk=D, name="out_proj")
    x_out = _resid_norm_fwd(x, mix, p["g_post"], 1.0)
    return x_out, (x, hn, proj, qr, kr, va, qk, gt, bias, ig5, b5, hs, cs, ns, ms, ycat, mix)


def _mixer_bwd(dx, saved, p, cosf, sinf):
    x, hn, proj, qr, kr, va, qk, gt, bias, ig5, b5, hs, cs, ns, ms, ycat, mix = saved
    S, D = x.shape
    nc = S // BLK
    tm, tk = _row_tile(S, ACT_ROWS), _row_tile(S, TOKEN_ROWS)
    dmix, dg_post = _norm_bwd(mix, p["g_post"], dx, 1.0, out_dtype=BF16)
    d_wout = _mm(ycat, dmix, "tn", tm=MODEL_COLS, tn=MODEL_COLS, tk=tk, name="dw_out")
    dycat = _mm(dmix, p["w_out"], "nt", tm=tm, tn=MODEL_COLS, tk=D, name="d_ycat")
    dhm, d_om, d_gamma = _mlstm_out_bwd(hs, proj, p["gamma"], dycat)
    dq2, dk2, dv2, di, dfc, dbl = _mlstm_bwd(qk, proj, ig5, b5, cs, ns, ms, dhm)
    d_vm = _sum_cast(dv2, "dv_sum")
    dgt, db = _gate_bwd(gt, bias, di.reshape(-1, LANES), dfc.reshape(-1, LANES), dbl.reshape(-1, LANES), nc)
    dpre, d_conv = _conv_bwd_pre(proj, OFF_QM, p["conv_w"], dq2, dk2)
    d_qkm = _conv5(dpre, p["conv_w"][::-1], col0=0, act=False, out_dtype=BF16, name="conv_bwd_x")
    d_gates = jnp.transpose(dgt.reshape(N_GATES, S)).astype(BF16)
    d_gates = jnp.pad(d_gates, ((0, 0), (0, IN_PAD - IN_WIDTH)))
    dqr, dkr, dva, d_sink = _attn_bwd(qr, kr, va, p["sink"], dycat)
    d_att = _rope_bwd(dqr, dkr, dva, cosf, sinf)
    dproj = jnp.concatenate([d_att, d_qkm, d_vm, d_om, d_gates], axis=-1)
    d_win = _mm(hn, dproj, "tn", tm=MODEL_COLS, tn=IN_PAD // 5, tk=tk, name="dw_in")
    dhn = _mm(dproj, p["w_in"], "nt", tm=tm, tn=MODEL_COLS, tk=IN_PAD // 3, name="d_hn")
    dx_in, dg_pre = _norm_bwd(x, p["g_pre"], dhn, 1.0, resid=dx)
    small = dict(g_pre=dg_pre, g_post=dg_post, gamma=d_gamma, conv=d_conv, b_gate=db[:, 0], sink=d_sink)
    return dx_in, small, d_win, d_wout


def _pad_lanes(v):
    v = v.reshape(-1)
    return jnp.pad(v, (0, (-v.shape[0]) % LANES))


def kernel(x, ffn1_norm_pre, ffn1_norm_post, ffn1_w_gate, ffn1_w_up, ffn1_w_down, mix_norm_pre, mix_norm_post, w_in, b_gate, conv_w, attn_sink, mlstm_norm, w_out, ffn2_norm_pre, ffn2_norm_post, ffn2_w_gate, ffn2_w_up, ffn2_w_down, loss_target, m_ffn1_norm_pre, m_ffn1_norm_post, m_ffn1_w_gate, m_ffn1_w_up, m_ffn1_w_down, m_mix_norm_pre, m_mix_norm_post, m_w_in, m_b_gate, m_conv_w, m_attn_sink, m_mlstm_norm, m_w_out, m_ffn2_norm_pre, m_ffn2_norm_post, m_ffn2_w_gate, m_ffn2_w_up, m_ffn2_w_down, v_ffn1_norm_pre, v_ffn1_norm_post, v_ffn1_w_gate, v_ffn1_w_up, v_ffn1_w_down, v_mix_norm_pre, v_mix_norm_post, v_w_in, v_b_gate, v_conv_w, v_attn_sink, v_mlstm_norm, v_w_out, v_ffn2_norm_pre, v_ffn2_norm_post, v_ffn2_w_gate, v_ffn2_w_up, v_ffn2_w_down):
    names = ["ffn1_norm_pre", "ffn1_norm_post", "ffn1_w_gate", "ffn1_w_up", "ffn1_w_down", "mix_norm_pre",
             "mix_norm_post", "w_in", "b_gate", "conv_w", "attn_sink", "mlstm_norm", "w_out", "ffn2_norm_pre",
             "ffn2_norm_post", "ffn2_w_gate", "ffn2_w_up", "ffn2_w_down"]
    w = dict(zip(names, [ffn1_norm_pre, ffn1_norm_post, ffn1_w_gate, ffn1_w_up, ffn1_w_down, mix_norm_pre,
                         mix_norm_post, w_in, b_gate, conv_w, attn_sink, mlstm_norm, w_out, ffn2_norm_pre,
                         ffn2_norm_post, ffn2_w_gate, ffn2_w_up, ffn2_w_down]))
    mom_m = dict(zip(names, [m_ffn1_norm_pre, m_ffn1_norm_post, m_ffn1_w_gate, m_ffn1_w_up, m_ffn1_w_down,
                             m_mix_norm_pre, m_mix_norm_post, m_w_in, m_b_gate, m_conv_w, m_attn_sink,
                             m_mlstm_norm, m_w_out, m_ffn2_norm_pre, m_ffn2_norm_post, m_ffn2_w_gate,
                             m_ffn2_w_up, m_ffn2_w_down]))
    mom_v = dict(zip(names, [v_ffn1_norm_pre, v_ffn1_norm_post, v_ffn1_w_gate, v_ffn1_w_up, v_ffn1_w_down,
                             v_mix_norm_pre, v_mix_norm_post, v_w_in, v_b_gate, v_conv_w, v_attn_sink,
                             v_mlstm_norm, v_w_out, v_ffn2_norm_pre, v_ffn2_norm_post, v_ffn2_w_gate,
                             v_ffn2_w_up, v_ffn2_w_down]))
    xs = x[0]
    target = loss_target[0]
    S, D = xs.shape
    depth = w_in.shape[0]
    F = ffn1_w_gate.shape[-1]
    in_shard = w_in.shape[-1]
    conv_shard = conv_w.shape[-1]
    chip = 2 * lax.axis_index("x") + lax.axis_index("y")
    cosf, sinf = _rope_tables(S)

    gathered = []
    for l in range(depth):
        half = lambda name: w[name][l].astype(BF16)
        first = [[half("ffn1_w_gate"), half("ffn1_w_up")], [half("ffn1_w_down")]]
        rest = [[half("w_in")], [half("w_out")], [half("ffn2_w_gate"), half("ffn2_w_up")], [half("ffn2_w_down")],
                [jnp.pad(w["conv_w"][l], ((0, 8 - CONV_WIDTH), (0, 0)))]]
        gathered.append((list(_comm_weight_gather(first)), list(_comm_weight_gather(rest))))

    def mixer_weights(l, win, wout, convg):
        win_full = jnp.concatenate([win[j] for j in range(4)], axis=-1)
        conv_full = jnp.concatenate([convg[j] for j in range(4)], axis=-1)[:CONV_WIDTH]
        return dict(w_in=jnp.pad(win_full, ((0, 0), (0, IN_PAD - IN_WIDTH))), w_out=wout.reshape(D, D),
                    conv_w=conv_full, g_pre=w["mix_norm_pre"][l], g_post=w["mix_norm_post"][l],
                    b_gate=w["b_gate"][l], sink=w["attn_sink"][l], gamma=w["mlstm_norm"][l])

    h = xs
    saved = []
    layers = []
    for l in range(depth):
        (wgu1, wd1), h = _tie(gathered[l][0], h)
        h, s1 = _ffn_fwd(h, w["ffn1_norm_pre"][l], w["ffn1_norm_post"][l], wgu1, wd1)
        (win, wout, wgu2, wd2, convg), h = _tie(gathered[l][1], h)
        p = dict(wgu1=wgu1, wd1=wd1, wgu2=wgu2, wd2=wd2, mix=mixer_weights(l, win, wout, convg))
        layers.append(p)
        h, s2 = _mixer_fwd(h, p["mix"], cosf, sinf)
        h, s3 = _ffn_fwd(h, w["ffn2_norm_pre"][l], w["ffn2_norm_post"][l], p["wgu2"], p["wd2"])
        saved.append((s1, s2, s3))
    dh, loss_tile = _loss_grad(h, target)

    big = {}
    small_rows = []
    rs_a = rs_b = None
    for l in reversed(range(depth)):
        p = layers[l]
        s1, s2, s3 = saved[l]
        old_a, old_b = rs_a, rs_b
        dh, dg2_pre, dg2_post, d_wgu2, d_wd2 = _ffn_bwd(dh, s3, w["ffn2_norm_pre"][l], w["ffn2_norm_post"][l],
                                                       p["wgu2"], p["wd2"], mid=old_b.scatter if old_b else None)
        if old_a is not None:
            dh = old_a.gather(dh)
        dh, sm, d_win, d_wout = _mixer_bwd(dh, s2, p["mix"], cosf, sinf)
        if old_b is not None:
            dh = old_b.gather(dh)
        d_win4 = jnp.stack([d_win[:, j * in_shard:(j + 1) * in_shard] for j in range(4)])
        rs_a = _ReduceScatter([d_win4, d_wout.reshape(4, D // 4, D), d_wgu2, d_wd2], dh)
        if l > 0:
            dh, dg1_pre, dg1_post, d_wgu1, d_wd1 = _ffn_bwd(
                rs_a.dh, s1, w["ffn1_norm_pre"][l], w["ffn1_norm_post"][l], p["wgu1"], p["wd1"], mid=rs_a.scatter)
            if old_a is not None:
                dh = old_b.settle(old_a.settle(dh))
            rs_b = _ReduceScatter([d_wgu1, d_wd1], dh)
            dh = rs_b.dh
        else:
            last = []

            def start_last(d_wgu, d_wd, x):
                last.append(_ReduceScatter([d_wgu, d_wd], x))
                return last[0].dh

            dh, dg1_pre, dg1_post, d_wgu1, d_wd1 = _ffn_bwd(
                rs_a.dh, s1, w["ffn1_norm_pre"][l], w["ffn1_norm_post"][l], p["wgu1"], p["wd1"], mid=rs_a.scatter,
                on_grads=start_last, late=lambda x: last[0].scatter(x))
            if old_a is not None:
                dh = old_b.settle(old_a.settle(dh))
            rs_b = last[0]
        big[l] = (rs_a, rs_b)
        small_rows.append((l, [dg1_pre, dg1_post, sm["g_pre"], sm["g_post"], dg2_pre, dg2_post, sm["gamma"],
                               sm["conv"], sm["b_gate"], sm["sink"]]))
    dh = rs_a.gather(dh)
    big_a = {l: a.result() for l, (a, _) in big.items()}

    small_rows.sort(key=lambda t: t[0])
    flat = [_pad_lanes(v) for _, vs in small_rows for v in vs] + [loss_tile[0]]
    sizes = [f.shape[0] for f in flat]
    pack = jnp.concatenate(flat)
    pack = jnp.pad(pack, (0, (-pack.shape[0]) % (8 * LANES))).reshape(-1, LANES)
    every = _comm_all_to_all_small(pack)
    total = _add_n([every[i] for i in range(8)], "small_sum").reshape(-1)
    pieces, off = [], 0
    for n in sizes:
        pieces.append(total[off:off + n])
        off += n
    loss = pieces[-1][0]
    per_layer = [pieces[10 * l:10 * l + 10] for l in range(depth)]

    def stack_small(i, shape):
        n = 1
        for s in shape:
            n *= s
        return jnp.stack([per_layer[l][i][:n].reshape(shape) for l in range(depth)])

    conv_full_grad = stack_small(7, (CONV_WIDTH, 4 * conv_shard))
    grads = {
        "ffn1_norm_pre": stack_small(0, (D,)), "ffn1_norm_post": stack_small(1, (D,)),
        "mix_norm_pre": stack_small(2, (D,)), "mix_norm_post": stack_small(3, (D,)),
        "ffn2_norm_pre": stack_small(4, (D,)), "ffn2_norm_post": stack_small(5, (D,)),
        "mlstm_norm": stack_small(6, (M_WIDTH,)),
        "conv_w": lax.dynamic_slice_in_dim(conv_full_grad, chip * conv_shard, conv_shard, 2),
        "b_gate": stack_small(8, (N_GATES,)), "attn_sink": stack_small(9, (ATT_HEADS,)),
    }
    gu2 = jnp.stack([big_a[l][2] for l in range(depth)])
    grads["ffn2_w_gate"], grads["ffn2_w_up"] = gu2[:, :, :F], gu2[:, :, F:]
    grads["ffn2_w_down"] = jnp.stack([big_a[l][3] for l in range(depth)])
    grads["w_in"] = jnp.stack([big_a[l][0] for l in range(depth)])
    grads["w_out"] = jnp.stack([big_a[l][1] for l in range(depth)])

    deltas, new_m, new_v = {}, {}, {}
    last_group = ("ffn1_w_gate", "ffn1_w_up", "ffn1_w_down")
    for n in names:
        if n not in last_group:
            deltas[n], new_m[n], new_v[n] = _adamw(w[n], grads[n], mom_m[n], mom_v[n])
    deltas, new_m, new_v = big[0][1].gather((deltas, new_m, new_v))
    big_b = {l: b.result() for l, (_, b) in big.items()}
    gu1 = jnp.stack([big_b[l][0] for l in range(depth)])
    grads["ffn1_w_gate"], grads["ffn1_w_up"] = gu1[:, :, :F], gu1[:, :, F:]
    grads["ffn1_w_down"] = jnp.stack([big_b[l][1] for l in range(depth)])
    for n in last_group:
        deltas[n], new_m[n], new_v[n] = _adamw(w[n], grads[n], mom_m[n], mom_v[n])
    grad_x = dh[None]
    return (loss, grad_x, *[grads[n] for n in names], *[deltas[n] for n in names],
            *[new_m[n] for n in names], *[new_v[n] for n in names])
```

```python
import jax
import jax.numpy as jnp
from jax import lax
from jax.experimental import pallas as pl
from jax.experimental.pallas import tpu as pltpu
from jax.experimental.pallas import tpu_sc as plsc

F32 = jnp.float32
BF16 = jnp.bfloat16
MESH = pl.DeviceIdType.MESH
ANY = pl.BlockSpec(memory_space=pl.ANY)

VMEM_LIMIT_BYTES = 56 * 1024 * 1024
LANES = 128

EPS = 1e-6
ATT_HEADS = 8
ATT_KV_HEADS = 2
ATT_GROUP = ATT_HEADS // ATT_KV_HEADS
ATT_HEAD_DIM = 128
ATT_WIDTH = ATT_HEADS * ATT_HEAD_DIM
KV_WIDTH = ATT_KV_HEADS * ATT_HEAD_DIM
BLK = 128
M_HEADS = 4
M_HEAD_DIM = 256
M_WIDTH = M_HEADS * M_HEAD_DIM
CONV_WIDTH = 5
CONV_HALO = 8
ROPE_THETA = 10000.0
N_GATES = 4 * M_HEADS
OFF_QA, OFF_KA, OFF_VA = 0, ATT_WIDTH, ATT_WIDTH + KV_WIDTH
OFF_QM = ATT_WIDTH + 2 * KV_WIDTH
OFF_KM = OFF_QM + M_WIDTH
OFF_VM = OFF_KM + M_WIDTH
OFF_OM = OFF_VM + M_WIDTH
OFF_G = OFF_OM + M_WIDTH
IN_WIDTH = OFF_G + N_GATES
IN_PAD = OFF_G + LANES
NEG_BIG = -1e30

ADAM_LR, ADAM_B1, ADAM_B2, ADAM_EPS, ADAM_WD, ADAM_STEP = 0.001, 0.9, 0.999, 1e-08, 0.01, 10


def _params(**kw):
    return pltpu.CompilerParams(vmem_limit_bytes=VMEM_LIMIT_BYTES, **kw)


def _dot(a, b):
    return lax.dot_general(a, b, (((1,), (0,)), ((), ())), preferred_element_type=F32)


def _dot_nt(a, b):
    return lax.dot_general(a, b, (((1,), (1,)), ((), ())), preferred_element_type=F32)


def _dot_tn(a, b):
    return lax.dot_general(a, b, (((0,), (0,)), ((), ())), preferred_element_type=F32)


def _sigmoid(x):
    return 1.0 / (1.0 + jnp.exp(-x))


def _eye_mask(n):
    r = lax.broadcasted_iota(jnp.int32, (n, n), 0)
    c = lax.broadcasted_iota(jnp.int32, (n, n), 1)
    return r == c


def _row_to_col(row, eye):
    n = eye.shape[0]
    return jnp.sum(jnp.where(eye, jnp.broadcast_to(row, (n, n)), 0.0), axis=1, keepdims=True)


def _col_to_row(col, eye):
    n = eye.shape[0]
    return jnp.sum(jnp.where(eye, jnp.broadcast_to(col, (n, n)), 0.0), axis=0, keepdims=True)


def _mm(a, b, kind, *, tm, tn, tk, name, out_dtype=F32, ja=None, jb=None):
    a2, b2 = a.shape[-2:], b.shape[-2:]
    if kind == "nn":
        (M, K), (_, N) = a2, b2
    elif kind == "nt":
        (M, K), (N, _) = a2, b2
    else:
        (K, M), (_, N) = a2, b2
    J = a.shape[0] if ja else (b.shape[0] if jb else 1)
    batch = "b" in (ja, jb)
    red = "r" in (ja, jb)
    nk = K // tk
    nr = nk * (J if red else 1)
    grid = ((J if batch else 1), M // tm, N // tn, nr)

    def lead(mode, g, r):
        return g if mode == "b" else r // nk

    def a_map(g, i, n, r):
        kk = r % nk
        idx = (i, kk) if kind != "tn" else (kk, i)
        return idx if ja is None else (lead(ja, g, r),) + idx

    def b_map(g, i, n, r):
        kk = r % nk
        idx = (kk, n) if kind != "nt" else (n, kk)
        return idx if jb is None else (lead(jb, g, r),) + idx

    def o_map(g, i, n, r):
        return (g, i, n) if batch else (i, n)

    a_blk = (tm, tk) if kind != "tn" else (tk, tm)
    b_blk = (tk, tn) if kind != "nt" else (tn, tk)
    dot = {"nn": _dot, "nt": _dot_nt, "tn": _dot_tn}[kind]

    def body(a_ref, b_ref, o_ref, *scratch):
        part = dot(a_ref[...], b_ref[...])
        if nr == 1:
            o_ref[...] = part.astype(out_dtype)
        else:
            acc = scratch[0]
            r = pl.program_id(3)

            @pl.when(r == 0)
            def _():
                acc[...] = part

            @pl.when(r > 0)
            def _():
                acc[...] += part

            @pl.when(r == nr - 1)
            def _():
                o_ref[...] = acc[...].astype(out_dtype)

    return pl.pallas_call(
        body, name=name, grid=grid,
        in_specs=[pl.BlockSpec(a_blk if ja is None else (None,) + a_blk, a_map),
                  pl.BlockSpec(b_blk if jb is None else (None,) + b_blk, b_map)],
        out_specs=pl.BlockSpec((None, tm, tn) if batch else (tm, tn), o_map),
        out_shape=jax.ShapeDtypeStruct((J, M, N) if batch else (M, N), out_dtype),
        scratch_shapes=[pltpu.VMEM((tm, tn), F32)] if nr > 1 else [],
        compiler_params=_params(),
    )(a, b)


def _row_tile(S, want):
    return min(S, want)


ELEMENTWISE_TILE_BYTES = 1 << 20
ACT_ROWS = 1024
TOKEN_ROWS = 2048
MODEL_COLS = 1024


def _elementwise_rows(R, C):
    for cand in (1024, 512, 256, 128, 64, 32, 16, 8):
        if R % cand == 0 and R > cand and cand * C * 4 <= ELEMENTWISE_TILE_BYTES:
            return cand
    return R if R * C * 4 <= ELEMENTWISE_TILE_BYTES or R % 8 else 8


def _ffn_up(xn, wgu):
    S, D = xn.shape
    J, _, F2 = wgu.shape
    F = F2 // 2
    tm = _row_tile(S, 256)

    def body(x_ref, w_ref, gu_ref, h_ref):
        gu = _dot(x_ref[...], w_ref[...])
        g, u = gu[:, :F], gu[:, F:]
        gu_ref[...] = gu.astype(BF16)
        h_ref[...] = (g * _sigmoid(g) * u).astype(BF16)

    return pl.pallas_call(
        body, name="ffn_up", grid=(J, S // tm),
        in_specs=[pl.BlockSpec((tm, D), lambda j, i: (i, 0)),
                  pl.BlockSpec((None, D, F2), lambda j, i: (j, 0, 0))],
        out_specs=[pl.BlockSpec((None, tm, F2), lambda j, i: (j, i, 0)),
                   pl.BlockSpec((None, tm, F), lambda j, i: (j, i, 0))],
        out_shape=[jax.ShapeDtypeStruct((J, S, F2), BF16), jax.ShapeDtypeStruct((J, S, F), BF16)],
        compiler_params=_params(),
    )(xn, wgu)


def _ffn_bwd_hidden(df, wd, gu):
    S, D = df.shape
    J, F, _ = wd.shape
    F2 = 2 * F
    tm = _row_tile(S, 512)

    def body(df_ref, w_ref, gu_ref, o_ref):
        dh = _dot_nt(df_ref[...], w_ref[...])
        g = gu_ref[:, :F].astype(F32)
        u = gu_ref[:, F:].astype(F32)
        sg = _sigmoid(g)
        o_ref[:, :F] = (dh * u * (sg * (1.0 + g * (1.0 - sg)))).astype(BF16)
        o_ref[:, F:] = (dh * (g * sg)).astype(BF16)

    return pl.pallas_call(
        body, name="ffn_bwd_hidden", grid=(J, S // tm),
        in_specs=[pl.BlockSpec((tm, D), lambda j, i: (i, 0)),
                  pl.BlockSpec((None, F, D), lambda j, i: (j, 0, 0)),
                  pl.BlockSpec((None, tm, F2), lambda j, i: (j, i, 0))],
        out_specs=pl.BlockSpec((None, tm, F2), lambda j, i: (j, i, 0)),
        out_shape=jax.ShapeDtypeStruct((J, S, F2), BF16),
        compiler_params=_params(),
    )(df, wd, gu)


def _norm_fwd(x, g):
    S, D = x.shape
    tm = _row_tile(S, 512)

    def body(x_ref, g_ref, o_ref):
        xv = x_ref[...]
        r = lax.rsqrt(jnp.mean(xv * xv, axis=-1, keepdims=True) + EPS)
        o_ref[...] = (xv * r * g_ref[...]).astype(BF16)

    return pl.pallas_call(
        body, name="norm_fwd", grid=(S // tm,),
        in_specs=[pl.BlockSpec((tm, D), lambda i: (i, 0)), pl.BlockSpec((1, D), lambda i: (0, 0))],
        out_specs=pl.BlockSpec((tm, D), lambda i: (i, 0)),
        out_shape=jax.ShapeDtypeStruct((S, D), BF16),
        compiler_params=_params(),
    )(x, g.reshape(1, D))


def _resid_norm_fwd(x, f, g, alpha):
    S, D = x.shape
    tm = _row_tile(S, 512)

    def body(x_ref, f_ref, g_ref, o_ref):
        fv = f_ref[...]
        r = lax.rsqrt(jnp.mean(fv * fv, axis=-1, keepdims=True) + EPS)
        o_ref[...] = x_ref[...] + alpha * (fv * r * g_ref[...])

    return pl.pallas_call(
        body, name="resid_norm_fwd", grid=(S // tm,),
        in_specs=[pl.BlockSpec((tm, D), lambda i: (i, 0)), pl.BlockSpec((tm, D), lambda i: (i, 0)),
                  pl.BlockSpec((1, D), lambda i: (0, 0))],
        out_specs=pl.BlockSpec((tm, D), lambda i: (i, 0)),
        out_shape=jax.ShapeDtypeStruct((S, D), F32),
        compiler_params=_params(),
    )(x, f, g.reshape(1, D))


def _norm_bwd(x, g, dy, alpha, resid=None, out_dtype=F32):
    S, D = x.shape
    tm = _row_tile(S, 256)
    has_resid = resid is not None

    def body(*refs):
        x_ref, g_ref, dy_ref = refs[:3]
        res_ref = refs[3] if has_resid else None
        dx_ref, dg_ref = refs[-2:]
        xv = x_ref[...]
        r = lax.rsqrt(jnp.mean(xv * xv, axis=-1, keepdims=True) + EPS)
        xh = xv * r
        dyv = dy_ref[...].astype(F32) * alpha
        gdy = dyv * g_ref[...]
        dx = r * (gdy - xh * jnp.mean(xh * gdy, axis=-1, keepdims=True))
        if has_resid:
            dx = dx + res_ref[...]
        dx_ref[...] = dx.astype(out_dtype)
        part = jnp.sum(dyv * xh, axis=0, keepdims=True)

        @pl.when(pl.program_id(0) == 0)
        def _():
            dg_ref[...] = part

        @pl.when(pl.program_id(0) > 0)
        def _():
            dg_ref[...] += part

    row = pl.BlockSpec((tm, D), lambda i: (i, 0))
    vec = pl.BlockSpec((1, D), lambda i: (0, 0))
    ins = [x, g.reshape(1, D), dy] + ([resid] if has_resid else [])
    dx, dg = pl.pallas_call(
        body, name="norm_bwd_res" if has_resid else "norm_bwd", grid=(S // tm,),
        in_specs=[row, vec, row] + ([row] if has_resid else []),
        out_specs=[row, vec],
        out_shape=[jax.ShapeDtypeStruct((S, D), out_dtype), jax.ShapeDtypeStruct((1, D), F32)],
        compiler_params=_params(),
    )(*ins)
    return dx, dg.reshape(D)


def _loss_grad(y, target):
    S, D = y.shape
    tm = _row_tile(S, 512)

    def body(y_ref, t_ref, dy_ref, l_ref):
        err = y_ref[...] - t_ref[...]
        dy_ref[...] = err * (1.0 / D)
        part = jnp.sum(jnp.sum(err * err, axis=-1, keepdims=True) * (0.5 / D), axis=0, keepdims=True)
        part = jnp.broadcast_to(part, (8, LANES))

        @pl.when(pl.program_id(0) == 0)
        def _():
            l_ref[...] = part

        @pl.when(pl.program_id(0) > 0)
        def _():
            l_ref[...] += part

    row = pl.BlockSpec((tm, D), lambda i: (i, 0))
    return pl.pallas_call(
        body, name="loss_grad", grid=(S // tm,),
        in_specs=[row, row],
        out_specs=[row, pl.BlockSpec((8, LANES), lambda i: (0, 0))],
        out_shape=[jax.ShapeDtypeStruct((S, D), F32), jax.ShapeDtypeStruct((8, LANES), F32)],
        compiler_params=_params(),
    )(y, target)


def _add_n(xs, name):
    shape = xs[0].shape
    C = shape[-1]
    R = 1
    for s in shape[:-1]:
        R *= s
    tm = _elementwise_rows(R, C)

    def body(*refs):
        acc = refs[0][...]
        for r in refs[1:-1]:
            acc = acc + r[...]
        refs[-1][...] = acc

    row = pl.BlockSpec((tm, C), lambda i: (i, 0))
    out = pl.pallas_call(
        body, name=name, grid=(R // tm,),
        in_specs=[row] * len(xs), out_specs=row,
        out_shape=jax.ShapeDtypeStruct((R, C), F32),
        compiler_params=_params(),
    )(*[x.reshape(R, C) for x in xs])
    return out.reshape(shape)


def _pair_sum(split, got):
    J, _, r, C = split.shape
    tm = _elementwise_rows(r, C)
    core = lax.axis_index("c").astype(jnp.int32).reshape(1)

    def body(core_ref, a_ref, b_ref, o_ref):
        o_ref[...] = a_ref[...] + b_ref[...]

    row = pl.BlockSpec((None, tm, C), lambda j, i, core_ref: (j, i, 0))
    return pl.pallas_call(
        body, name="pair_sum",
        grid_spec=pltpu.PrefetchScalarGridSpec(
            num_scalar_prefetch=1, grid=(J, r // tm),
            in_specs=[pl.BlockSpec((None, None, tm, C), lambda j, i, core_ref: (j, core_ref[0], i, 0)), row],
            out_specs=row),
        out_shape=jax.ShapeDtypeStruct((J, r, C), F32),
        compiler_params=_params(),
    )(core, split, got)


def _chip_sum(pair, arrived):
    _, r, C = pair.shape
    tm = _elementwise_rows(r, C)
    chip = (2 * lax.axis_index("x") + lax.axis_index("y")).astype(jnp.int32).reshape(1)

    def body(chip_ref, own_ref, a_ref, b_ref, c_ref, o_ref):
        o_ref[...] = own_ref[...] + a_ref[...] + b_ref[...] + c_ref[...]

    part = lambda k: pl.BlockSpec((None, tm, C), lambda i, chip_ref: (k, i, 0))
    return pl.pallas_call(
        body, name="chip_sum",
        grid_spec=pltpu.PrefetchScalarGridSpec(
            num_scalar_prefetch=1, grid=(r // tm,),
            in_specs=[pl.BlockSpec((None, tm, C), lambda i, chip_ref: (chip_ref[0], i, 0)), part(0), part(1), part(2)],
            out_specs=pl.BlockSpec((tm, C), lambda i, chip_ref: (i, 0))),
        out_shape=jax.ShapeDtypeStruct((r, C), F32),
        compiler_params=_params(),
    )(chip, pair, arrived, arrived, arrived)


def _rope_tables(S):
    half = ATT_HEAD_DIM // 2
    inv_freq = ROPE_THETA ** (-jnp.arange(half, dtype=F32) / half)
    ang = jnp.arange(S, dtype=F32)[:, None] * inv_freq[None, :]
    cos, sin = jnp.cos(ang), jnp.sin(ang)
    return jnp.concatenate([cos, cos], axis=-1), jnp.concatenate([-sin, sin], axis=-1)


def _rotate(x, cosf, sinf):
    return x * cosf + pltpu.roll(x, ATT_HEAD_DIM // 2, 1) * sinf


def _rope_fwd(proj, cosf, sinf):
    S = proj.shape[0]
    tm = _row_tile(S, 512)

    def body(q_ref, k_ref, v_ref, c_ref, s_ref, qo_ref, ko_ref, vo_ref):
        c, s = c_ref[...], s_ref[...]
        for h in range(ATT_HEADS):
            sl = slice(h * ATT_HEAD_DIM, (h + 1) * ATT_HEAD_DIM)
            qo_ref[:, sl] = _rotate(q_ref[:, sl], c, s).astype(BF16)
        for h in range(ATT_KV_HEADS):
            sl = slice(h * ATT_HEAD_DIM, (h + 1) * ATT_HEAD_DIM)
            ko_ref[:, sl] = _rotate(k_ref[:, sl], c, s).astype(BF16)
        vo_ref[...] = v_ref[...].astype(BF16)

    tab = pl.BlockSpec((tm, ATT_HEAD_DIM), lambda i: (i, 0))
    return pl.pallas_call(
        body, name="rope_fwd", grid=(S // tm,),
        in_specs=[pl.BlockSpec((tm, ATT_WIDTH), lambda i: (i, 0)),
                  pl.BlockSpec((tm, KV_WIDTH), lambda i: (i, OFF_KA // KV_WIDTH)),
                  pl.BlockSpec((tm, KV_WIDTH), lambda i: (i, OFF_VA // KV_WIDTH)), tab, tab],
        out_specs=[pl.BlockSpec((tm, ATT_WIDTH), lambda i: (i, 0)),
                   pl.BlockSpec((tm, KV_WIDTH), lambda i: (i, 0)),
                   pl.BlockSpec((tm, KV_WIDTH), lambda i: (i, 0))],
        out_shape=[jax.ShapeDtypeStruct((S, ATT_WIDTH), BF16), jax.ShapeDtypeStruct((S, KV_WIDTH), BF16),
                   jax.ShapeDtypeStruct((S, KV_WIDTH), BF16)],
        compiler_params=_params(),
    )(proj, proj, proj, cosf, sinf)


def _rope_bwd(dq, dk, dv, cosf, sinf):
    S = dq.shape[0]
    tm = _row_tile(S, 512)
    W = ATT_WIDTH + 2 * KV_WIDTH

    def body(q_ref, k_ref, v_ref, c_ref, s_ref, o_ref):
        c, s = c_ref[...], -s_ref[...]
        for h in range(ATT_HEADS):
            sl = slice(h * ATT_HEAD_DIM, (h + 1) * ATT_HEAD_DIM)
            o_ref[:, sl] = _rotate(q_ref[:, sl], c, s).astype(BF16)
        for h in range(ATT_KV_HEADS):
            sl = slice(h * ATT_HEAD_DIM, (h + 1) * ATT_HEAD_DIM)
            o_ref[:, ATT_WIDTH + h * ATT_HEAD_DIM:ATT_WIDTH + (h + 1) * ATT_HEAD_DIM] = (
                _rotate(k_ref[:, sl], c, s).astype(BF16))
        o_ref[:, ATT_WIDTH + KV_WIDTH:] = v_ref[...].astype(BF16)

    tab = pl.BlockSpec((tm, ATT_HEAD_DIM), lambda i: (i, 0))
    return pl.pallas_call(
        body, name="rope_bwd", grid=(S // tm,),
        in_specs=[pl.BlockSpec((tm, ATT_WIDTH), lambda i: (i, 0)),
                  pl.BlockSpec((tm, KV_WIDTH), lambda i: (i, 0)),
                  pl.BlockSpec((tm, KV_WIDTH), lambda i: (i, 0)), tab, tab],
        out_specs=pl.BlockSpec((tm, W), lambda i: (i, 0)),
        out_shape=jax.ShapeDtypeStruct((S, W), BF16),
        compiler_params=_params(),
    )(dq, dk, dv, cosf, sinf)


def _attn_probs(q_ref, k_refs, sink_ref, kh, n, nb):
    G, L, Dh = ATT_GROUP, BLK, ATT_HEAD_DIM
    q4 = jnp.concatenate([q_ref[:, (kh * G + g) * Dh:(kh * G + g + 1) * Dh] for g in range(G)], axis=0)
    kcat = jnp.concatenate([r[:, kh * Dh:(kh + 1) * Dh] for r in k_refs], axis=0)
    s = _dot_nt(q4, kcat) * (Dh ** -0.5)
    row = lax.broadcasted_iota(jnp.int32, (G * L, 3 * L), 0) % L
    col = lax.broadcasted_iota(jnp.int32, (G * L, 3 * L), 1)
    kpos = (n - 1) * L + col
    mask = (jnp.abs(col - L - row) <= L) & (kpos >= 0) & (kpos < nb * L)
    s = jnp.where(mask, s, -jnp.inf)
    sink = jnp.concatenate([jnp.broadcast_to(sink_ref[kh, :, g:g + 1], (L, 1)) for g in range(G)], axis=0)
    m = jnp.maximum(jnp.max(s, axis=-1, keepdims=True), sink)
    p = jnp.exp(s - m)
    es = jnp.exp(sink - m)
    inv = 1.0 / (jnp.sum(p, axis=-1, keepdims=True) + es)
    return q4, kcat, p * inv, es * inv


def _kv_specs(nb):
    return [pl.BlockSpec((BLK, KV_WIDTH), lambda n: (jnp.maximum(n - 1, 0), 0)),
            pl.BlockSpec((BLK, KV_WIDTH), lambda n: (n, 0)),
            pl.BlockSpec((BLK, KV_WIDTH), lambda n: (jnp.minimum(n + 1, nb - 1), 0))]


def _attn_fwd(qr, kr, va, sink):
    S = qr.shape[0]
    nb = S // BLK
    G, Dh = ATT_GROUP, ATT_HEAD_DIM

    def body(q_ref, k0, k1, k2, v0, v1, v2, sink_ref, o_ref):
        n = pl.program_id(0)
        for kh in range(ATT_KV_HEADS):
            _, _, probs, _ = _attn_probs(q_ref, (k0, k1, k2), sink_ref, kh, n, nb)
            vcat = jnp.concatenate([r[:, kh * Dh:(kh + 1) * Dh] for r in (v0, v1, v2)], axis=0)
            out = _dot(probs.astype(BF16), vcat)
            for g in range(G):
                o_ref[:, (kh * G + g) * Dh:(kh * G + g + 1) * Dh] = out[g * BLK:(g + 1) * BLK, :].astype(BF16)

    qspec = pl.BlockSpec((BLK, ATT_WIDTH), lambda n: (n, 0))
    return pl.pallas_call(
        body, name="attn_fwd", grid=(nb,),
        in_specs=[qspec] + _kv_specs(nb) + _kv_specs(nb)
        + [pl.BlockSpec((ATT_KV_HEADS, 1, G), lambda n: (0, 0, 0))],
        out_specs=qspec,
        out_shape=jax.ShapeDtypeStruct((S, ATT_WIDTH), BF16),
        compiler_params=_params(),
    )(qr, kr, kr, kr, va, va, va, sink.reshape(ATT_KV_HEADS, 1, G))


def _attn_bwd(qr, kr, va, sink, dycat):
    S = qr.shape[0]
    nb = S // BLK
    G, L, Dh = ATT_GROUP, BLK, ATT_HEAD_DIM
    SP = S + 2 * L

    def body(q_ref, k0, k1, k2, v0, v1, v2, sink_ref, do_ref, dq_ref, dk_ref, dv_ref, ds_ref):
        n = pl.program_id(0)

        @pl.when(n == 0)
        def _():
            dk_ref[...] = jnp.zeros_like(dk_ref)
            dv_ref[...] = jnp.zeros_like(dv_ref)
            ds_ref[...] = jnp.zeros_like(ds_ref)

        rows = pl.ds(pl.multiple_of(n * L, L), 3 * L)
        lane = lax.broadcasted_iota(jnp.int32, (8, LANES), 1)
        for kh in range(ATT_KV_HEADS):
            q4, kcat, probs, psink = _attn_probs(q_ref, (k0, k1, k2), sink_ref, kh, n, nb)
            vcat = jnp.concatenate([r[:, kh * Dh:(kh + 1) * Dh] for r in (v0, v1, v2)], axis=0)
            do4 = jnp.concatenate([do_ref[:, (kh * G + g) * Dh:(kh * G + g + 1) * Dh] for g in range(G)], axis=0)
            pb = probs.astype(BF16)
            dob = do4.astype(BF16)
            out = _dot(pb, vcat)
            delta = jnp.sum(do4 * out, axis=-1, keepdims=True)
            dp = _dot_nt(dob, vcat)
            dsc = (probs * (dp - delta) * (Dh ** -0.5)).astype(BF16)
            dq4 = _dot(dsc, kcat)
            for g in range(G):
                dq_ref[:, (kh * G + g) * Dh:(kh * G + g + 1) * Dh] = dq4[g * L:(g + 1) * L, :]
            dk_ref[rows, kh * Dh:(kh + 1) * Dh] += _dot_tn(dsc, q4)
            dv_ref[rows, kh * Dh:(kh + 1) * Dh] += _dot_tn(pb, dob)
            dsink = jnp.zeros((8, LANES), F32)
            for g in range(G):
                val = -jnp.sum(psink[g * L:(g + 1) * L] * delta[g * L:(g + 1) * L], axis=0, keepdims=True)
                dsink = dsink + jnp.where(lane == g, jnp.broadcast_to(val, (8, LANES)), 0.0)
            ds_ref[kh] += dsink

    qspec = pl.BlockSpec((L, ATT_WIDTH), lambda n: (n, 0))
    accspec = pl.BlockSpec((SP, KV_WIDTH), lambda n: (0, 0))
    dq, dkp, dvp, dsink = pl.pallas_call(
        body, name="attn_bwd", grid=(nb,),
        in_specs=[qspec] + _kv_specs(nb) + _kv_specs(nb)
        + [pl.BlockSpec((ATT_KV_HEADS, 1, G), lambda n: (0, 0, 0)), qspec],
        out_specs=[qspec, accspec, accspec, pl.BlockSpec((ATT_KV_HEADS, 8, LANES), lambda n: (0, 0, 0))],
        out_shape=[jax.ShapeDtypeStruct((S, ATT_WIDTH), F32), jax.ShapeDtypeStruct((SP, KV_WIDTH), F32),
                   jax.ShapeDtypeStruct((SP, KV_WIDTH), F32),
                   jax.ShapeDtypeStruct((ATT_KV_HEADS, 8, LANES), F32)],
        compiler_params=_params(),
    )(qr, kr, kr, kr, va, va, va, sink.reshape(ATT_KV_HEADS, 1, G), dycat)
    return dq, dkp[L:L + S], dvp[L:L + S], dsink[:, 0, :G].reshape(ATT_HEADS)


CONV_COLS = 256
CONV_ROWS = 512


def _conv_taps(xs, w_ref, rows):
    total = xs.shape[0]
    acc = None
    for j in range(CONV_WIDTH):
        shift = (CONV_WIDTH // 2 - j) % total
        term = (pltpu.roll(xs, shift, 0) if shift else xs)[CONV_HALO:CONV_HALO + rows, :] * w_ref[j:j + 1, :]
        acc = term if acc is None else acc + term
    return acc


def _conv_window(x_ref, i, R, nrow):
    S = x_ref.shape[0]
    r0 = pl.multiple_of(i * R, R)
    top = x_ref[pl.ds(pl.multiple_of(jnp.maximum(r0 - CONV_HALO, 0), CONV_HALO), CONV_HALO), :]
    bot = x_ref[pl.ds(pl.multiple_of(jnp.minimum(r0 + R, S - CONV_HALO), CONV_HALO), CONV_HALO), :]
    top = jnp.where(i > 0, top, 0.0)
    bot = jnp.where(i < nrow - 1, bot, 0.0)
    return jnp.concatenate([top, x_ref[pl.ds(r0, R), :], bot], axis=0)


def _conv5(x, w, *, col0, act, out_dtype, name):
    S = x.shape[0]
    C = w.shape[1]
    R = _row_tile(S, CONV_ROWS)
    tc = CONV_COLS
    half_blocks = (C // 2) // tc
    nrow = S // R

    def body(x_ref, w_ref, o_ref):
        scale = jnp.where(pl.program_id(0) >= half_blocks, M_HEAD_DIM ** -0.5, 1.0)
        y = _conv_taps(_conv_window(x_ref, pl.program_id(1), R, nrow), w_ref, R)
        if act:
            y = y * _sigmoid(y) * scale
        o_ref[...] = y.astype(out_dtype)

    return pl.pallas_call(
        body, name=name, grid=(C // tc, nrow),
        in_specs=[pl.BlockSpec((S, tc), lambda c, i: (0, col0 // tc + c)),
                  pl.BlockSpec((CONV_WIDTH, tc), lambda c, i: (0, c))],
        out_specs=pl.BlockSpec((R, tc), lambda c, i: (i, c)),
        out_shape=jax.ShapeDtypeStruct((S, C), out_dtype),
        compiler_params=_params(),
    )(x, w)


def _conv_bwd_pre(x, col0, w, dq2, dk2):
    S = x.shape[0]
    C = w.shape[1]
    R = _row_tile(S, CONV_ROWS)
    tc = CONV_COLS
    half_blocks = (C // 2) // tc
    nrow = S // R

    def body(x_ref, w_ref, dqa_ref, dqb_ref, dka_ref, dkb_ref, o_ref, dw_ref, acc):
        is_k = pl.program_id(0) >= half_blocks
        i = pl.program_id(1)
        scale = jnp.where(is_k, M_HEAD_DIM ** -0.5, 1.0)

        @pl.when(i == 0)
        def _():
            acc[...] = jnp.zeros_like(acc)

        xs = _conv_window(x_ref, i, R, nrow)
        y = _conv_taps(xs, w_ref, R)
        sg = _sigmoid(y)
        dqv = dqa_ref[...] + dqb_ref[...]
        dkv = dka_ref[...] + dkb_ref[...]
        dpre = jnp.where(is_k, dkv, dqv) * scale * (sg * (1.0 + y * (1.0 - sg)))
        o_ref[...] = dpre
        total = xs.shape[0]
        for j in range(CONV_WIDTH):
            shift = (CONV_WIDTH // 2 - j) % total
            xj = (pltpu.roll(xs, shift, 0) if shift else xs)[CONV_HALO:CONV_HALO + R, :]
            acc[j:j + 1, :] += jnp.sum(dpre * xj, axis=0, keepdims=True)

        @pl.when(i == nrow - 1)
        def _():
            dw_ref[...] = acc[0:CONV_WIDTH, :]

    nqb = (C // 2) // tc
    qmap = lambda d: (lambda c, i: (d, i, jnp.minimum(c, nqb - 1)))
    kmap = lambda d: (lambda c, i: (d, i, jnp.maximum(c - nqb, 0)))
    gspec = lambda m: pl.BlockSpec((None, R, tc), m)
    return pl.pallas_call(
        body, name="conv_bwd_pre", grid=(C // tc, nrow),
        in_specs=[pl.BlockSpec((S, tc), lambda c, i: (0, col0 // tc + c)),
                  pl.BlockSpec((CONV_WIDTH, tc), lambda c, i: (0, c)),
                  gspec(qmap(0)), gspec(qmap(1)), gspec(kmap(0)), gspec(kmap(1))],
        out_specs=[pl.BlockSpec((R, tc), lambda c, i: (i, c)), pl.BlockSpec((CONV_WIDTH, tc), lambda c, i: (0, c))],
        out_shape=[jax.ShapeDtypeStruct((S, C), F32), jax.ShapeDtypeStruct((CONV_WIDTH, C), F32)],
        scratch_shapes=[pltpu.VMEM((8, tc), F32)],
        compiler_params=_params(),
    )(x, w, dq2, dq2, dk2, dk2)


def _lane_cumsum(x, reverse):
    lane = lax.broadcasted_iota(jnp.int32, x.shape, 1)
    sh = 1
    while sh < LANES:
        if reverse:
            x = x + jnp.where(lane < LANES - sh, pltpu.roll(x, LANES - sh, 1), 0.0)
        else:
            x = x + jnp.where(lane >= sh, pltpu.roll(x, sh, 1), 0.0)
        sh *= 2
    return x


def _gate_prep(gates_t, bias):
    R = gates_t.shape[0]
    half, quarter = R // 2, R // 4

    def body(g_ref, b_ref, ig_ref, cum_ref):
        ig_ref[...] = g_ref[0:half, :] + b_ref[0:half, :]
        fg = g_ref[half:R, :] + b_ref[half:R, :]
        lf = jnp.minimum(fg, 0.0) - jnp.log(1.0 + jnp.exp(-jnp.abs(fg)))
        cum_ref[0:quarter, :] = _lane_cumsum(lf[0:quarter, :], False)
        cum_ref[quarter:half, :] = _lane_cumsum(lf[quarter:half, :], True)

    return pl.pallas_call(
        body, name="gate_prep",
        out_shape=[jax.ShapeDtypeStruct((half, LANES), F32), jax.ShapeDtypeStruct((half, LANES), F32)],
        compiler_params=_params(),
    )(gates_t, bias)


def _gate_bwd(gates_t, bias, di, dfc, dbl, nc):
    R = gates_t.shape[0]
    half, quarter = R // 2, R // 4

    def body(g_ref, b_ref, di_ref, df_ref, dbl_ref, dg_ref, db_ref):
        dg_ref[0:half, :] = di_ref[...]
        dfv = df_ref[...]
        within = jnp.concatenate([_lane_cumsum(dfv[0:quarter, :], True),
                                  _lane_cumsum(dfv[quarter:half, :], False)], axis=0)
        fg = g_ref[half:R, :] + b_ref[half:R, :]
        dg_ref[half:R, :] = (within + dbl_ref[...]) * _sigmoid(-fg)
        rows = jnp.broadcast_to(jnp.sum(dg_ref[...], axis=-1, keepdims=True), (R, LANES))
        gr = lax.broadcasted_iota(jnp.int32, (N_GATES, R), 0)
        gc = lax.broadcasted_iota(jnp.int32, (N_GATES, R), 1)
        db_ref[...] = lax.dot_general((gc // nc == gr).astype(F32), rows, (((1,), (0,)), ((), ())),
                                      precision=lax.Precision.HIGHEST, preferred_element_type=F32)

    return pl.pallas_call(
        body, name="gate_bwd",
        out_shape=[jax.ShapeDtypeStruct((R, LANES), F32), jax.ShapeDtypeStruct((N_GATES, LANES), F32)],
        compiler_params=_params(),
    )(gates_t, bias, di, dfc, dbl)


def _chunk_index(d, c, nc, reverse):
    j = (nc - 1 - c) if reverse else c
    return j + d * (nc - 1 - 2 * j)


def _mlstm_chunk(d, q, k, vb, brow, igrow, C, nvec, m_prev, eye):
    L = BLK
    r = lax.broadcasted_iota(jnp.int32, (L, L), 0)
    c = lax.broadcasted_iota(jnp.int32, (L, L), 1)
    mask = (r - c) * (1 - 2 * d) >= 0
    bcol = _row_to_col(brow, eye)
    igcol = _row_to_col(igrow, eye)
    log_d = jnp.where(mask, bcol - brow + igrow, -jnp.inf)
    log_inter = bcol + m_prev
    m_t = jnp.maximum(log_inter, jnp.max(log_d, axis=-1, keepdims=True))
    d_mat = jnp.exp(log_d - m_t)
    inter = jnp.exp(log_inter - m_t)
    s = _dot_nt(q, k) * d_mat
    sb = s.astype(BF16)
    cb = C.astype(BF16)
    num = _dot(sb, vb) + inter * _dot_nt(q, cb)
    den = jnp.sum(s, axis=-1, keepdims=True) + inter * jnp.sum(q.astype(F32) * nvec, axis=-1, keepdims=True)
    floor = jnp.exp(-m_t)
    denom = jnp.maximum(jnp.abs(den), floor)
    b_last = jnp.where(d == 0, brow[:, L - 1:L], brow[:, 0:1])
    return dict(bcol=bcol, igcol=igcol, m_t=m_t, d_mat=d_mat, inter=inter, sb=sb, cb=cb, num=num, den=den,
                floor=floor, denom=denom, b_last=b_last)


def _head_cols(v0_ref, v1_ref, hd):
    ref = v0_ref if hd < M_HEADS // 2 else v1_ref
    lo = (hd % (M_HEADS // 2)) * M_HEAD_DIM
    return ref[:, lo:lo + M_HEAD_DIM]


def _mlstm_fwd(qk, proj, ig5, b5):
    S = qk.shape[0]
    nc = S // BLK
    L, Dh, H = BLK, M_HEAD_DIM, M_HEADS

    def body(q_ref, k_ref, v0_ref, v1_ref, ig_ref, b_ref, h_ref, cs_ref, ns_ref, ms_ref, C, nvec, m):
        d = pl.program_id(0)

        @pl.when(pl.program_id(1) == 0)
        def _():
            C[...] = jnp.zeros_like(C)
            nvec[...] = jnp.zeros_like(nvec)
            m[...] = jnp.zeros_like(m)

        eye = _eye_mask(L)
        for hd in range(H):
            cols = slice(hd * Dh, (hd + 1) * Dh)
            q, k = q_ref[:, cols], k_ref[:, cols]
            vf = _head_cols(v0_ref, v1_ref, hd)
            brow, igrow = b_ref[hd], ig_ref[hd]
            m_prev = m[hd, :, 0:1]
            cs_ref[hd] = C[hd].astype(BF16)
            ns_ref[hd] = nvec[hd]
            ms_ref[hd] = m[hd]
            f = _mlstm_chunk(d, q, k, vf.astype(BF16), brow, igrow, C[hd], nvec[hd], m_prev, eye)
            h_ref[:, cols] = f["num"] / f["denom"]
            b_last = f["b_last"]
            log_w = b_last - brow + igrow
            m_new = jnp.maximum(b_last + m_prev, jnp.max(log_w, axis=-1, keepdims=True))
            w_col = jnp.exp(b_last - f["bcol"] + f["igcol"] - m_new)
            decay = jnp.exp(b_last + m_prev - m_new)
            C[hd] = decay * C[hd] + _dot_tn((w_col * vf).astype(BF16), k)
            nvec[hd] = decay * nvec[hd] + jnp.sum(w_col * k.astype(F32), axis=0, keepdims=True)
            m[hd] = jnp.broadcast_to(m_new, (1, LANES))

    cidx = lambda d, c: _chunk_index(d, c, nc, False)
    gspec = pl.BlockSpec((None, H, None, 1, LANES), lambda d, c: (d, 0, cidx(d, c), 0, 0))
    st = lambda *blk: pl.BlockSpec((None, H, None) + blk, lambda d, c: (d, 0, cidx(d, c), 0, 0))
    half = M_WIDTH // 2
    return pl.pallas_call(
        body, name="mlstm_fwd", grid=(2, nc),
        in_specs=[pl.BlockSpec((L, M_WIDTH), lambda d, c: (cidx(d, c), 0)),
                  pl.BlockSpec((L, M_WIDTH), lambda d, c: (cidx(d, c), 1)),
                  pl.BlockSpec((L, half), lambda d, c: (cidx(d, c), OFF_VM // half)),
                  pl.BlockSpec((L, half), lambda d, c: (cidx(d, c), OFF_VM // half + 1)), gspec, gspec],
        out_specs=[pl.BlockSpec((None, L, M_WIDTH), lambda d, c: (d, cidx(d, c), 0)),
                   st(Dh, Dh), st(1, Dh), st(1, LANES)],
        out_shape=[jax.ShapeDtypeStruct((2, S, M_WIDTH), F32), jax.ShapeDtypeStruct((2, H, nc, Dh, Dh), BF16),
                   jax.ShapeDtypeStruct((2, H, nc, 1, Dh), F32), jax.ShapeDtypeStruct((2, H, nc, 1, LANES), F32)],
        scratch_shapes=[pltpu.VMEM((H, Dh, Dh), F32), pltpu.VMEM((H, 1, Dh), F32), pltpu.VMEM((H, 1, LANES), F32)],
        compiler_params=_params(),
    )(qk, qk, proj, proj, ig5, b5)


def _mlstm_bwd(qk, proj, ig5, b5, cs, ns, ms, dhm):
    S = qk.shape[0]
    nc = S // BLK
    L, Dh, H = BLK, M_HEAD_DIM, M_HEADS

    def body(q_ref, k_ref, v0_ref, v1_ref, ig_ref, b_ref, cs_ref, ns_ref, ms_ref, dh_ref,
             dq_ref, dk_ref, dv_ref, di_ref, df_ref, dbl_ref, R, rvec, mu):
        d = pl.program_id(0)

        @pl.when(pl.program_id(1) == 0)
        def _():
            R[...] = jnp.zeros_like(R)
            rvec[...] = jnp.zeros_like(rvec)
            mu[...] = jnp.full(mu.shape, NEG_BIG, F32)

        eye = _eye_mask(L)
        for hd in range(H):
            cols = slice(hd * Dh, (hd + 1) * Dh)
            q, k = q_ref[:, cols], k_ref[:, cols]
            qf, kf = q.astype(F32), k.astype(F32)
            vb = _head_cols(v0_ref, v1_ref, hd).astype(BF16)
            brow, igrow = b_ref[hd], ig_ref[hd]
            nprev = ns_ref[hd]
            m_prev = ms_ref[hd, :, 0:1]
            cprev = cs_ref[hd]
            f = _mlstm_chunk(d, q, k, vb, brow, igrow, cprev, nprev, m_prev, eye)
            dh = dh_ref[:, cols]
            inv = 1.0 / f["denom"]
            hcur = f["num"] * inv
            dnum = dh * inv
            active = jnp.abs(f["den"]) >= f["floor"]
            dden = (jnp.where(active, -jnp.sign(f["den"]), 0.0)
                    * jnp.sum(dh * hcur, axis=-1, keepdims=True) * inv)
            dnb = dnum.astype(BF16)
            dqk = ((_dot_nt(dnb, vb) + dden) * f["d_mat"]).astype(BF16)
            dq = _dot(dqk, k) + f["inter"] * (_dot(dnb, f["cb"]) + dden * nprev)
            mu_prev = mu[hd, :, 0:1]
            a_col = jnp.exp(f["b_last"] - f["bcol"] + f["igcol"] + mu_prev)
            rb = R[hd].astype(BF16)
            dv = _dot_tn(f["sb"], dnb) + a_col * _dot_nt(k, rb)
            dk_inter = a_col * (_dot(vb, rb) + rvec[hd])
            dk = _dot_tn(dqk, q) + dk_inter
            dq_ref[:, cols] = dq
            dk_ref[:, cols] = dk
            dv_ref[:, cols] = dv
            kdk = jnp.sum(kf * dk, axis=-1, keepdims=True)
            qdq = jnp.sum(qf * dq, axis=-1, keepdims=True)
            di_ref[hd] = _col_to_row(kdk, eye)
            df_ref[hd] = _col_to_row(qdq - kdk, eye)
            older = jnp.sum(jnp.sum(R[hd] * cprev.astype(F32), axis=-1, keepdims=True), axis=0, keepdims=True)
            older = older + jnp.sum(rvec[hd] * nprev, axis=-1, keepdims=True)
            dbl = (jnp.sum(jnp.sum(kf * dk_inter, axis=-1, keepdims=True), axis=0, keepdims=True)
                   + jnp.exp(f["b_last"] + mu_prev + m_prev) * older)
            dbl_ref[hd] = jnp.broadcast_to(dbl, (1, LANES))
            lw = f["bcol"] - f["m_t"]
            mu_new = jnp.maximum(f["b_last"] + mu_prev, jnp.max(lw, axis=0, keepdims=True))
            wq = jnp.exp(lw - mu_new)
            decay = jnp.exp(f["b_last"] + mu_prev - mu_new)
            R[hd] = decay * R[hd] + _dot_tn((wq * dnum).astype(BF16), q)
            rvec[hd] = decay * rvec[hd] + jnp.sum(wq * dden * qf, axis=0, keepdims=True)
            mu[hd] = jnp.broadcast_to(mu_new, (1, LANES))

    cidx = lambda d, c: _chunk_index(d, c, nc, True)
    gin = pl.BlockSpec((None, H, None, 1, LANES), lambda d, c: (d, 0, cidx(d, c), 0, 0))
    st = lambda *blk: pl.BlockSpec((None, H, None) + blk, lambda d, c: (d, 0, cidx(d, c), 0, 0))
    per_dir = pl.BlockSpec((None, L, M_WIDTH), lambda d, c: (d, cidx(d, c), 0))
    big = jax.ShapeDtypeStruct((2, S, M_WIDTH), F32)
    small = jax.ShapeDtypeStruct((2, H, nc, 1, LANES), F32)
    half = M_WIDTH // 2
    return pl.pallas_call(
        body, name="mlstm_bwd", grid=(2, nc),
        in_specs=[pl.BlockSpec((L, M_WIDTH), lambda d, c: (cidx(d, c), 0)),
                  pl.BlockSpec((L, M_WIDTH), lambda d, c: (cidx(d, c), 1)),
                  pl.BlockSpec((L, half), lambda d, c: (cidx(d, c), OFF_VM // half)),
                  pl.BlockSpec((L, half), lambda d, c: (cidx(d, c), OFF_VM // half + 1)), gin, gin,
                  st(Dh, Dh), st(1, Dh), st(1, LANES),
                  pl.BlockSpec((L, M_WIDTH), lambda d, c: (cidx(d, c), 0))],
        out_specs=[per_dir, per_dir, per_dir, gin, gin, gin],
        out_shape=[big, big, big, small, small, small],
        scratch_shapes=[pltpu.VMEM((H, Dh, Dh), F32), pltpu.VMEM((H, 1, Dh), F32), pltpu.VMEM((H, 1, LANES), F32)],
        compiler_params=_params(),
    )(qk, qk, proj, proj, ig5, b5, cs, ns, ms, dhm)


def _mlstm_out_fwd(hs, proj, gamma):
    S = hs.shape[1]
    tm = _row_tile(S, 512)
    Dh, H = M_HEAD_DIM, M_HEADS

    def body(hf_ref, hb_ref, o_ref, g_ref, y_ref):
        hm = hf_ref[...] + hb_ref[...]
        r = lax.rsqrt(jnp.mean(hm * hm, axis=-1, keepdims=True) + EPS)
        y_ref[...] = (_sigmoid(o_ref[...]) * (hm * r * g_ref[...])).astype(BF16)

    hspec = lambda d: pl.BlockSpec((None, tm, Dh), lambda i, h: (d, i, h))
    return pl.pallas_call(
        body, name="mlstm_out_fwd", grid=(S // tm, H),
        in_specs=[hspec(0), hspec(1), pl.BlockSpec((tm, Dh), lambda i, h: (i, OFF_OM // Dh + h)),
                  pl.BlockSpec((1, Dh), lambda i, h: (0, h))],
        out_specs=pl.BlockSpec((tm, Dh), lambda i, h: (i, h)),
        out_shape=jax.ShapeDtypeStruct((S, M_WIDTH), BF16),
        compiler_params=_params(),
    )(hs, hs, proj, gamma.reshape(1, M_WIDTH))


def _mlstm_out_bwd(hs, proj, gamma, dycat):
    S = hs.shape[1]
    tm = _row_tile(S, 512)
    Dh, H = M_HEAD_DIM, M_HEADS
    D_OFF = ATT_WIDTH // Dh

    def body(hf_ref, hb_ref, o_ref, g_ref, dy_ref, dh_ref, do_ref, dg_ref):
        hm = hf_ref[...] + hb_ref[...]
        r = lax.rsqrt(jnp.mean(hm * hm, axis=-1, keepdims=True) + EPS)
        hh = hm * r
        so = _sigmoid(o_ref[...])
        dy = dy_ref[...]
        gam = g_ref[...]
        do_ref[...] = (dy * hh * gam * so * (1.0 - so)).astype(BF16)
        dyn = dy * so
        gd = dyn * gam
        dh_ref[...] = r * (gd - hh * jnp.mean(hh * gd, axis=-1, keepdims=True))
        part = jnp.sum(dyn * hh, axis=0, keepdims=True)

        @pl.when(pl.program_id(1) == 0)
        def _():
            dg_ref[...] = part

        @pl.when(pl.program_id(1) > 0)
        def _():
            dg_ref[...] += part

    hspec = lambda d: pl.BlockSpec((None, tm, Dh), lambda h, i: (d, i, h))
    out = pl.BlockSpec((tm, Dh), lambda h, i: (i, h))
    vec = pl.BlockSpec((1, Dh), lambda h, i: (0, h))
    dhm, do, dg = pl.pallas_call(
        body, name="mlstm_out_bwd", grid=(H, S // tm),
        in_specs=[hspec(0), hspec(1), pl.BlockSpec((tm, Dh), lambda h, i: (i, OFF_OM // Dh + h)), vec,
                  pl.BlockSpec((tm, Dh), lambda h, i: (i, D_OFF + h))],
        out_specs=[out, out, vec],
        out_shape=[jax.ShapeDtypeStruct((S, M_WIDTH), F32), jax.ShapeDtypeStruct((S, M_WIDTH), BF16),
                   jax.ShapeDtypeStruct((1, M_WIDTH), F32)],
        compiler_params=_params(),
    )(hs, hs, proj, gamma.reshape(1, M_WIDTH), dycat)
    return dhm, do, dg.reshape(M_WIDTH)


def _sum_cast(a2, name):
    _, S, C = a2.shape
    tm = _row_tile(S, 512)

    def body(a_ref, b_ref, o_ref):
        o_ref[...] = (a_ref[...] + b_ref[...]).astype(BF16)

    spec = lambda d: pl.BlockSpec((None, tm, C), lambda i: (d, i, 0))
    return pl.pallas_call(
        body, name=name, grid=(S // tm,),
        in_specs=[spec(0), spec(1)], out_specs=pl.BlockSpec((tm, C), lambda i: (i, 0)),
        out_shape=jax.ShapeDtypeStruct((S, C), BF16),
        compiler_params=_params(),
    )(a2, a2)


def _adamw(w, g, m, v):
    shape = w.shape
    C = shape[-1]
    R = w.size // C
    tm = _elementwise_rows(R, C)

    def body(w_ref, g_ref, m_ref, v_ref, d_ref, mo_ref, vo_ref):
        gv = g_ref[...]
        mn = ADAM_B1 * m_ref[...] + (1.0 - ADAM_B1) * gv
        vn = ADAM_B2 * v_ref[...] + (1.0 - ADAM_B2) * (gv * gv)
        m_hat = mn / (1.0 - ADAM_B1 ** ADAM_STEP)
        v_hat = vn / (1.0 - ADAM_B2 ** ADAM_STEP)
        d_ref[...] = -ADAM_LR * (m_hat / (jnp.sqrt(v_hat) + ADAM_EPS) + ADAM_WD * w_ref[...])
        mo_ref[...] = mn
        vo_ref[...] = vn

    row = pl.BlockSpec((tm, C), lambda i: (i, 0))
    sds = jax.ShapeDtypeStruct((R, C), F32)
    outs = pl.pallas_call(
        body, name="adamw", grid=(R // tm,),
        in_specs=[row] * 4, out_specs=[row] * 3, out_shape=[sds] * 3,
        compiler_params=_params(),
    )(*[t.reshape(R, C) for t in (w, g, m, v)])
    return tuple(o.reshape(shape) for o in outs)


CID_WEIGHT_GATHER, CID_PAIR_SWAP, CID_SCATTER = 0, 1, 2


def _other_chips(x, y):
    return [(1 - x, y), (x, 1 - y), (1 - x, 1 - y)]


def _handshake(peers):
    barrier = pltpu.get_barrier_semaphore()
    for peer in peers:
        pl.semaphore_signal(barrier, inc=1, device_id=peer, device_id_type=MESH)
    pl.semaphore_wait(barrier, len(peers))


def _sequencer_call(body, name, out_type, sem_counts, collective_id, operands):
    return pl.kernel(
        body, name=name, out_type=out_type,
        mesh=plsc.ScalarSubcoreMesh(axis_name="sequencer", num_cores=1),
        scratch_types=[pltpu.SemaphoreType.DMA((n,)) for n in sem_counts],
        compiler_params=pltpu.CompilerParams(collective_id=collective_id),
    )(*operands)


def _comm_weight_gather(groups):
    T = len(groups)
    flat = [(t, sum(p.shape[1] for p in g[:i]), part) for t, g in enumerate(groups) for i, part in enumerate(g)]
    N = len(flat)

    def body(*refs):
        ins, outs = refs[:N], refs[N:N + T]
        send, recv, local = refs[N + T:]
        x, y, c = lax.axis_index("x"), lax.axis_index("y"), lax.axis_index("c")
        me = 2 * x + y
        chips = _other_chips(x, y)
        _handshake([(px, py, c) for px, py in chips])

        def place(i, chip):
            t, off, part = flat[i]
            return outs[t].at[chip, :, pl.ds(off, part.shape[1])]

        mine = []
        for i in range(N):
            cp = pltpu.make_async_copy(ins[i], place(i, me), local.at[i])
            cp.start()
            mine.append(cp)
            for k, (px, py) in enumerate(chips):
                pltpu.make_async_remote_copy(
                    src_ref=ins[i], dst_ref=place(i, me), send_sem=send.at[3 * i + k], recv_sem=recv.at[3 * i + k],
                    device_id=(px, py, c), device_id_type=MESH).start()
        for i in range(N):
            for k, (px, py) in enumerate(chips):
                cp = pltpu.make_async_remote_copy(
                    src_ref=ins[i], dst_ref=place(i, 2 * px + py), send_sem=send.at[3 * i + k],
                    recv_sem=recv.at[3 * i + k], device_id=(px, py, c), device_id_type=MESH)
                cp.wait_send()
                cp.wait_recv()
            mine[i].wait()

    out_type = tuple(jax.ShapeDtypeStruct((4, g[0].shape[0], sum(p.shape[1] for p in g)), g[0].dtype) for g in groups)
    return _sequencer_call(
        body, "comm_weight_gather", out_type, (3 * N, 3 * N, N), CID_WEIGHT_GATHER, [part for _, _, part in flat])


def _comm_to_sibling(xs, other_half=False):
    T = len(xs)

    def body(*refs):
        ins, outs = refs[:T], refs[T:2 * T]
        send, recv = refs[2 * T:]
        x, y, c = lax.axis_index("x"), lax.axis_index("y"), lax.axis_index("c")
        _handshake([(x, y, 1 - c)])
        cps = [pltpu.make_async_remote_copy(
            src_ref=ins[t].at[:, 1 - c] if other_half else ins[t], dst_ref=outs[t], send_sem=send.at[t],
            recv_sem=recv.at[t], device_id=(x, y, 1 - c), device_id_type=MESH) for t in range(T)]
        for cp in cps:
            cp.start()
        for cp in cps:
            cp.wait_send()
            cp.wait_recv()

    shapes = [s.shape[:1] + s.shape[2:] if other_half else s.shape for s in xs]
    return _sequencer_call(
        body, "comm_pair_swap", tuple(jax.ShapeDtypeStruct(sh, s.dtype) for sh, s in zip(shapes, xs)),
        (T, T), CID_PAIR_SWAP, xs)


def _comm_scatter_to_owners(ps):
    T = len(ps)

    def body(*refs):
        ins, outs = refs[:T], refs[T:2 * T]
        send, recv = refs[2 * T:]
        x, y, c = lax.axis_index("x"), lax.axis_index("y"), lax.axis_index("c")
        chips = _other_chips(x, y)
        _handshake([(px, py, c) for px, py in chips])
        cps = []
        for t in range(T):
            for k, (px, py) in enumerate(chips):
                cp = pltpu.make_async_remote_copy(
                    src_ref=ins[t].at[2 * px + py], dst_ref=outs[t].at[k], send_sem=send.at[3 * t + k],
                    recv_sem=recv.at[3 * t + k], device_id=(px, py, c), device_id_type=MESH)
                cp.start()
                cps.append(cp)
        for cp in cps:
            cp.wait_send()
            cp.wait_recv()

    return _sequencer_call(
        body, "comm_scatter_to_owners", tuple(jax.ShapeDtypeStruct((3,) + p.shape[1:], p.dtype) for p in ps),
        (3 * T, 3 * T), CID_SCATTER, ps)


def _comm_all_to_all_small(pack):
    R = pack.shape[0]

    def body(in_ref, out_ref, send, recv, local):
        x, y, c = lax.axis_index("x"), lax.axis_index("y"), lax.axis_index("c")
        me = 4 * x + 2 * y + c
        mine = pltpu.make_async_copy(in_ref, out_ref.at[me], local.at[0])
        mine.start()
        peers = []
        for k in range(1, 8):
            fx, fy, fc = (k >> 2) & 1, (k >> 1) & 1, k & 1
            peers.append((x + fx * (1 - 2 * x), y + fy * (1 - 2 * y), c + fc * (1 - 2 * c)))
        for k, peer in enumerate(peers):
            pltpu.make_async_remote_copy(
                src_ref=in_ref, dst_ref=out_ref.at[me], send_sem=send.at[k], recv_sem=recv.at[k],
                device_id=peer, device_id_type=MESH).start()
        for k, (px, py, pc) in enumerate(peers):
            cp = pltpu.make_async_remote_copy(
                src_ref=in_ref, dst_ref=out_ref.at[4 * px + 2 * py + pc], send_sem=send.at[k], recv_sem=recv.at[k],
                device_id=(px, py, pc), device_id_type=MESH)
            cp.wait_send()
            cp.wait_recv()
        mine.wait()

    return pl.pallas_call(
        body, name="comm_all_to_all_small",
        in_specs=[ANY], out_specs=ANY,
        out_shape=jax.ShapeDtypeStruct((8, R, LANES), F32),
        scratch_shapes=[pltpu.SemaphoreType.DMA((7,)), pltpu.SemaphoreType.DMA((7,)), pltpu.SemaphoreType.DMA((1,))],
    )(pack)


def _tie(*trees):
    return lax.optimization_barrier(trees)


class _ReduceScatter:
    def __init__(self, grads, dh):
        split = [g.reshape(4, 2, g.shape[1] // 2, g.shape[2]) for g in grads]
        self.split, self.dh = _tie(split, dh)
        self.got = _comm_to_sibling(self.split, other_half=True)

    def scatter(self, dh):
        split, got, dh = _tie(self.split, list(self.got), dh)
        pair = [_pair_sum(a, b) for a, b in zip(split, got)]
        self.pair, dh = _tie(pair, dh)
        self.arrived = _comm_scatter_to_owners(self.pair)
        return dh

    def gather(self, dh):
        pair, arrived, dh = _tie(self.pair, list(self.arrived), dh)
        halves = [_chip_sum(p, a) for p, a in zip(pair, arrived)]
        self.halves, dh = _tie(halves, dh)
        self.other = _comm_to_sibling(self.halves)
        return dh

    def settle(self, dh):
        self.other, dh = _tie(list(self.other), dh)
        return dh

    def result(self):
        south = lax.axis_index("c") == 0
        return [jnp.where(south, jnp.concatenate([mine, theirs]), jnp.concatenate([theirs, mine]))
                for mine, theirs in zip(self.halves, self.other)]


def _ffn_fwd(x, g_pre, g_post, wgu, wd):
    S, D = x.shape
    F = wd.shape[1]
    xn = _norm_fwd(x, g_pre)
    gu, h = _ffn_up(xn, wgu)
    wd, h = _tie(wd, h)
    f = _mm(h, wd, "nn", ja="r", jb="r", tm=_row_tile(S, ACT_ROWS), tn=MODEL_COLS, tk=F, name="ffn_down")
    x_out = _resid_norm_fwd(x, f, g_post, 0.5)
    return x_out, (x, xn, gu, h, f), wd


def _ffn_bwd(dx, saved, g_pre, g_post, wgu, wd, mid=None, on_grads=None, late=None):
    x, xn, gu, h, f = saved
    S, D = x.shape
    J, F, _ = wd.shape
    tm, tk = _row_tile(S, ACT_ROWS), _row_tile(S, TOKEN_ROWS)
    df, dg_post = _norm_bwd(f, g_post, dx, 0.5, out_dtype=BF16)
    d_wd = _mm(h, df, "tn", ja="b", tm=F, tn=MODEL_COLS, tk=tk, name="ffn_dwd")
    dgu = _ffn_bwd_hidden(df, wd, gu)
    if mid is not None:
        dgu = mid(dgu)
    d_wgu = _mm(xn, dgu, "tn", jb="b", tm=MODEL_COLS, tn=F, tk=tk, name="ffn_dwgu")
    if on_grads is not None:
        dgu = on_grads(d_wgu, d_wd, dgu)
    dxn = _mm(dgu, wgu, "nt", ja="r", jb="r", tm=tm, tn=MODEL_COLS, tk=2 * F, name="ffn_dxn")
    if late is not None:
        dxn = late(dxn)
    dx_in, dg_pre = _norm_bwd(x, g_pre, dxn, 1.0, resid=dx)
    return dx_in, dg_pre, dg_post, d_wgu, d_wd


def _gates_layout(proj, b_gate, nc):
    gt = jnp.transpose(proj[:, OFF_G:OFF_G + N_GATES]).reshape(N_GATES * nc, LANES)
    bias = jnp.repeat(b_gate, nc).reshape(N_GATES * nc, 1)
    return gt, bias


def _mixer_fwd(x, p, cosf, sinf):
    S, D = x.shape
    nc = S // BLK
    hn = _norm_fwd(x, p["g_pre"])
    proj = _mm(hn, p["w_in"], "nn", tm=_row_tile(S, ACT_ROWS), tn=IN_PAD // 5, tk=D, name="in_proj")
    qr, kr, va = _rope_fwd(proj, cosf, sinf)
    y_att = _attn_fwd(qr, kr, va, p["sink"])
    qk = _conv5(proj, p["conv_w"], col0=OFF_QM, act=True, out_dtype=BF16, name="conv_silu")
    gt, bias = _gates_layout(proj, p["b_gate"], nc)
    ig, cum = _gate_prep(gt, bias)
    ig5, b5 = ig.reshape(2, M_HEADS, nc, 1, LANES), cum.reshape(2, M_HEADS, nc, 1, LANES)
    hs, cs, ns, ms = _mlstm_fwd(qk, proj, ig5, b5)
    y_m = _mlstm_out_fwd(hs, proj, p["gamma"])
    ycat = jnp.concatenate([y_att, y_m], axis=-1)
    w_out, ycat = _tie(p["w_out"], ycat)
    p["w_out"] = w_out
    mix = _mm(ycat, w_out, "nn", tm=_row_tile(S, ACT_ROWS), tn=MODEL_COLS, tk=D, name="out_proj")
    x_out = _resid_norm_fwd(x, mix, p["g_post"], 1.0)
    return x_out, (x, hn, proj, qr, kr, va, qk, gt, bias, ig5, b5, hs, cs, ns, ms, ycat, mix)


def _mixer_bwd(dx, saved, p, cosf, sinf, mid=None):
    x, hn, proj, qr, kr, va, qk, gt, bias, ig5, b5, hs, cs, ns, ms, ycat, mix = saved
    S, D = x.shape
    nc = S // BLK
    tm, tk = _row_tile(S, ACT_ROWS), _row_tile(S, TOKEN_ROWS)
    dmix, dg_post = _norm_bwd(mix, p["g_post"], dx, 1.0, out_dtype=BF16)
    d_wout = _mm(ycat, dmix, "tn", tm=MODEL_COLS, tn=MODEL_COLS, tk=tk, name="dw_out")
    dycat = _mm(dmix, p["w_out"], "nt", tm=tm, tn=MODEL_COLS, tk=D, name="d_ycat")
    if mid is not None:
        dycat = mid(dycat)
    dhm, d_om, d_gamma = _mlstm_out_bwd(hs, proj, p["gamma"], dycat)
    dq2, dk2, dv2, di, dfc, dbl = _mlstm_bwd(qk, proj, ig5, b5, cs, ns, ms, dhm)
    d_vm = _sum_cast(dv2, "dv_sum")
    dgt, db = _gate_bwd(gt, bias, di.reshape(-1, LANES), dfc.reshape(-1, LANES), dbl.reshape(-1, LANES), nc)
    dpre, d_conv = _conv_bwd_pre(proj, OFF_QM, p["conv_w"], dq2, dk2)
    d_qkm = _conv5(dpre, p["conv_w"][::-1], col0=0, act=False, out_dtype=BF16, name="conv_bwd_x")
    d_gates = jnp.transpose(dgt.reshape(N_GATES, S)).astype(BF16)
    d_gates = jnp.pad(d_gates, ((0, 0), (0, IN_PAD - IN_WIDTH)))
    dqr, dkr, dva, d_sink = _attn_bwd(qr, kr, va, p["sink"], dycat)
    d_att = _rope_bwd(dqr, dkr, dva, cosf, sinf)
    dproj = jnp.concatenate([d_att, d_qkm, d_vm, d_om, d_gates], axis=-1)
    d_win = _mm(hn, dproj, "tn", tm=MODEL_COLS, tn=IN_PAD // 5, tk=tk, name="dw_in")
    dhn = _mm(dproj, p["w_in"], "nt", tm=tm, tn=MODEL_COLS, tk=IN_PAD // 3, name="d_hn")
    dx_in, dg_pre = _norm_bwd(x, p["g_pre"], dhn, 1.0, resid=dx)
    small = dict(g_pre=dg_pre, g_post=dg_post, gamma=d_gamma, conv=d_conv, b_gate=db[:, 0], sink=d_sink)
    return dx_in, small, d_win, d_wout


def _pad_lanes(v):
    v = v.reshape(-1)
    return jnp.pad(v, (0, (-v.shape[0]) % LANES))


def kernel(x, ffn1_norm_pre, ffn1_norm_post, ffn1_w_gate, ffn1_w_up, ffn1_w_down, mix_norm_pre, mix_norm_post, w_in, b_gate, conv_w, attn_sink, mlstm_norm, w_out, ffn2_norm_pre, ffn2_norm_post, ffn2_w_gate, ffn2_w_up, ffn2_w_down, loss_target, m_ffn1_norm_pre, m_ffn1_norm_post, m_ffn1_w_gate, m_ffn1_w_up, m_ffn1_w_down, m_mix_norm_pre, m_mix_norm_post, m_w_in, m_b_gate, m_conv_w, m_attn_sink, m_mlstm_norm, m_w_out, m_ffn2_norm_pre, m_ffn2_norm_post, m_ffn2_w_gate, m_ffn2_w_up, m_ffn2_w_down, v_ffn1_norm_pre, v_ffn1_norm_post, v_ffn1_w_gate, v_ffn1_w_up, v_ffn1_w_down, v_mix_norm_pre, v_mix_norm_post, v_w_in, v_b_gate, v_conv_w, v_attn_sink, v_mlstm_norm, v_w_out, v_ffn2_norm_pre, v_ffn2_norm_post, v_ffn2_w_gate, v_ffn2_w_up, v_ffn2_w_down):
    names = ["ffn1_norm_pre", "ffn1_norm_post", "ffn1_w_gate", "ffn1_w_up", "ffn1_w_down", "mix_norm_pre",
             "mix_norm_post", "w_in", "b_gate", "conv_w", "attn_sink", "mlstm_norm", "w_out", "ffn2_norm_pre",
             "ffn2_norm_post", "ffn2_w_gate", "ffn2_w_up", "ffn2_w_down"]
    w = dict(zip(names, [ffn1_norm_pre, ffn1_norm_post, ffn1_w_gate, ffn1_w_up, ffn1_w_down, mix_norm_pre,
                         mix_norm_post, w_in, b_gate, conv_w, attn_sink, mlstm_norm, w_out, ffn2_norm_pre,
                         ffn2_norm_post, ffn2_w_gate, ffn2_w_up, ffn2_w_down]))
    mom_m = dict(zip(names, [m_ffn1_norm_pre, m_ffn1_norm_post, m_ffn1_w_gate, m_ffn1_w_up, m_ffn1_w_down,
                             m_mix_norm_pre, m_mix_norm_post, m_w_in, m_b_gate, m_conv_w, m_attn_sink,
                             m_mlstm_norm, m_w_out, m_ffn2_norm_pre, m_ffn2_norm_post, m_ffn2_w_gate,
                             m_ffn2_w_up, m_ffn2_w_down]))
    mom_v = dict(zip(names, [v_ffn1_norm_pre, v_ffn1_norm_post, v_ffn1_w_gate, v_ffn1_w_up, v_ffn1_w_down,
                             v_mix_norm_pre, v_mix_norm_post, v_w_in, v_b_gate, v_conv_w, v_attn_sink,
                             v_mlstm_norm, v_w_out, v_ffn2_norm_pre, v_ffn2_norm_post, v_ffn2_w_gate,
                             v_ffn2_w_up, v_ffn2_w_down]))
    xs = x[0]
    target = loss_target[0]
    S, D = xs.shape
    depth = w_in.shape[0]
    F = ffn1_w_gate.shape[-1]
    in_shard = w_in.shape[-1]
    conv_shard = conv_w.shape[-1]
    chip = 2 * lax.axis_index("x") + lax.axis_index("y")
    cosf, sinf = _rope_tables(S)

    gathered = []
    for l in range(depth):
        half = lambda name: w[name][l].astype(BF16)
        parts = dict(wgu1=[half("ffn1_w_gate"), half("ffn1_w_up")], wd1=[half("ffn1_w_down")],
                     win=[half("w_in")], conv=[jnp.pad(w["conv_w"][l], ((0, 8 - CONV_WIDTH), (0, 0)))],
                     wout=[half("w_out")], wgu2=[half("ffn2_w_gate"), half("ffn2_w_up")], wd2=[half("ffn2_w_down")])
        kernels = ([["wgu1"], ["wd1"], ["win", "conv"], ["wout"], ["wgu2"], ["wd2"]] if l == 0
                   else [["wgu1", "wd1"], ["win", "conv", "wout", "wgu2", "wd2"]])
        got = {}
        for names_k in kernels:
            got.update(zip(names_k, _comm_weight_gather([parts[n] for n in names_k])))
        gathered.append(got)

    def mixer_weights(l, win, wout, convg):
        win_full = jnp.concatenate([win[j] for j in range(4)], axis=-1)
        conv_full = jnp.concatenate([convg[j] for j in range(4)], axis=-1)[:CONV_WIDTH]
        return dict(w_in=jnp.pad(win_full, ((0, 0), (0, IN_PAD - IN_WIDTH))), w_out=wout.reshape(D, D),
                    conv_w=conv_full, g_pre=w["mix_norm_pre"][l], g_post=w["mix_norm_post"][l],
                    b_gate=w["b_gate"][l], sink=w["attn_sink"][l], gamma=w["mlstm_norm"][l])

    h = xs
    saved = []
    layers = []
    for l in range(depth):
        g = gathered[l]
        wgu1, h = _tie(g["wgu1"], h)
        h, s1, wd1 = _ffn_fwd(h, w["ffn1_norm_pre"][l], w["ffn1_norm_post"][l], wgu1, g["wd1"])
        (win, convg), h = _tie((g["win"], g["conv"]), h)
        mix_p = mixer_weights(l, win, g["wout"], convg)
        h, s2 = _mixer_fwd(h, mix_p, cosf, sinf)
        wgu2, h = _tie(g["wgu2"], h)
        h, s3, wd2 = _ffn_fwd(h, w["ffn2_norm_pre"][l], w["ffn2_norm_post"][l], wgu2, g["wd2"])
        layers.append(dict(wgu1=wgu1, wd1=wd1, wgu2=wgu2, wd2=wd2, mix=mix_p))
        saved.append((s1, s2, s3))
    dh, loss_tile = _loss_grad(h, target)

    big = {}
    small_rows = []
    old = None
    for l in reversed(range(depth)):
        p = layers[l]
        s1, s2, s3 = saved[l]
        dh, dg2_pre, dg2_post, d_wgu2, d_wd2 = _ffn_bwd(dh, s3, w["ffn2_norm_pre"][l], w["ffn2_norm_post"][l],
                                                       p["wgu2"], p["wd2"], mid=old["b"].scatter if old else None)
        if old is not None:
            dh = old["a"].settle(old["b"].gather(dh))
        rs_c = _ReduceScatter([d_wgu2, d_wd2], dh)
        dh, sm, d_win, d_wout = _mixer_bwd(rs_c.dh, s2, p["mix"], cosf, sinf, mid=rs_c.scatter)
        dh = rs_c.gather(dh)
        if old is not None:
            dh = old["b"].settle(dh)
        d_win4 = jnp.stack([d_win[:, j * in_shard:(j + 1) * in_shard] for j in range(4)])
        rs_a = _ReduceScatter([d_win4, d_wout.reshape(4, D // 4, D)], dh)
        if l > 0:
            dh, dg1_pre, dg1_post, d_wgu1, d_wd1 = _ffn_bwd(
                rs_a.dh, s1, w["ffn1_norm_pre"][l], w["ffn1_norm_post"][l], p["wgu1"], p["wd1"], mid=rs_a.scatter)
            dh = rs_c.settle(rs_a.gather(dh))
            rs_b = _ReduceScatter([d_wgu1, d_wd1], dh)
            dh = rs_b.dh
        else:
            last = []

            def start_last(d_wgu, d_wd, x):
                last.append(_ReduceScatter([d_wgu, d_wd], x))
                return last[0].dh

            dh, dg1_pre, dg1_post, d_wgu1, d_wd1 = _ffn_bwd(
                rs_a.dh, s1, w["ffn1_norm_pre"][l], w["ffn1_norm_post"][l], p["wgu1"], p["wd1"], mid=rs_a.scatter,
                on_grads=start_last, late=lambda x: last[0].scatter(x))
            dh = rs_c.settle(rs_a.gather(dh))
            rs_b = last[0]
        old = dict(a=rs_a, b=rs_b, c=rs_c)
        big[l] = old
        small_rows.append((l, [dg1_pre, dg1_post, sm["g_pre"], sm["g_post"], dg2_pre, dg2_post, sm["gamma"],
                               sm["conv"], sm["b_gate"], sm["sink"]]))
    big_a = {l: g["a"].result() + g["c"].result() for l, g in big.items()}

    small_rows.sort(key=lambda t: t[0])
    flat = [_pad_lanes(v) for _, vs in small_rows for v in vs] + [loss_tile[0]]
    sizes = [f.shape[0] for f in flat]
    pack = jnp.concatenate(flat)
    pack = jnp.pad(pack, (0, (-pack.shape[0]) % (8 * LANES))).reshape(-1, LANES)
    every = _comm_all_to_all_small(pack)
    total = _add_n([every[i] for i in range(8)], "small_sum").reshape(-1)
    pieces, off = [], 0
    for n in sizes:
        pieces.append(total[off:off + n])
        off += n
    loss = pieces[-1][0]
    per_layer = [pieces[10 * l:10 * l + 10] for l in range(depth)]

    def stack_small(i, shape):
        n = 1
        for s in shape:
            n *= s
        return jnp.stack([per_layer[l][i][:n].reshape(shape) for l in range(depth)])

    conv_full_grad = stack_small(7, (CONV_WIDTH, 4 * conv_shard))
    grads = {
        "ffn1_norm_pre": stack_small(0, (D,)), "ffn1_norm_post": stack_small(1, (D,)),
        "mix_norm_pre": stack_small(2, (D,)), "mix_norm_post": stack_small(3, (D,)),
        "ffn2_norm_pre": stack_small(4, (D,)), "ffn2_norm_post": stack_small(5, (D,)),
        "mlstm_norm": stack_small(6, (M_WIDTH,)),
        "conv_w": lax.dynamic_slice_in_dim(conv_full_grad, chip * conv_shard, conv_shard, 2),
        "b_gate": stack_small(8, (N_GATES,)), "attn_sink": stack_small(9, (ATT_HEADS,)),
    }
    gu2 = jnp.stack([big_a[l][2] for l in range(depth)])
    grads["ffn2_w_gate"], grads["ffn2_w_up"] = gu2[:, :, :F], gu2[:, :, F:]
    grads["ffn2_w_down"] = jnp.stack([big_a[l][3] for l in range(depth)])
    grads["w_in"] = jnp.stack([big_a[l][0] for l in range(depth)])
    grads["w_out"] = jnp.stack([big_a[l][1] for l in range(depth)])

    deltas, new_m, new_v = {}, {}, {}
    last_group = ("ffn1_w_gate", "ffn1_w_up", "ffn1_w_down")
    for n in names:
        if n not in last_group:
            deltas[n], new_m[n], new_v[n] = _adamw(w[n], grads[n], mom_m[n], mom_v[n])
    deltas, new_m, new_v = big[0]["b"].gather((deltas, new_m, new_v))
    big_b = {l: g["b"].result() for l, g in big.items()}
    gu1 = jnp.stack([big_b[l][0] for l in range(depth)])
    grads["ffn1_w_gate"], grads["ffn1_w_up"] = gu1[:, :, :F], gu1[:, :, F:]
    grads["ffn1_w_down"] = jnp.stack([big_b[l][1] for l in range(depth)])
    for n in last_group:
        deltas[n], new_m[n], new_v[n] = _adamw(w[n], grads[n], mom_m[n], mom_v[n])
    grad_x = dh[None]
    return (loss, grad_x, *[grads[n] for n in names], *[deltas[n] for n in names],
            *[new_m[n] for n in names], *[new_v[n] for n in names])
```

```python
import jax
import jax.numpy as jnp
from jax import lax
from jax.experimental import pallas as pl
from jax.experimental.pallas import tpu as pltpu
from jax.experimental.pallas import tpu_sc as plsc

F32 = jnp.float32
BF16 = jnp.bfloat16
MESH = pl.DeviceIdType.MESH
ANY = pl.BlockSpec(memory_space=pl.ANY)

VMEM_LIMIT_BYTES = 56 * 1024 * 1024
LANES = 128

EPS = 1e-6
ATT_HEADS = 8
ATT_KV_HEADS = 2
ATT_GROUP = ATT_HEADS // ATT_KV_HEADS
ATT_HEAD_DIM = 128
ATT_WIDTH = ATT_HEADS * ATT_HEAD_DIM
KV_WIDTH = ATT_KV_HEADS * ATT_HEAD_DIM
BLK = 128
M_HEADS = 4
M_HEAD_DIM = 256
M_WIDTH = M_HEADS * M_HEAD_DIM
CONV_WIDTH = 5
CONV_HALO = 8
ROPE_THETA = 10000.0
N_GATES = 4 * M_HEADS
OFF_QA, OFF_KA, OFF_VA = 0, ATT_WIDTH, ATT_WIDTH + KV_WIDTH
OFF_QM = ATT_WIDTH + 2 * KV_WIDTH
OFF_KM = OFF_QM + M_WIDTH
OFF_VM = OFF_KM + M_WIDTH
OFF_OM = OFF_VM + M_WIDTH
OFF_G = OFF_OM + M_WIDTH
IN_WIDTH = OFF_G + N_GATES
IN_PAD = OFF_G + LANES
NEG_BIG = -1e30

ADAM_LR, ADAM_B1, ADAM_B2, ADAM_EPS, ADAM_WD, ADAM_STEP = 0.001, 0.9, 0.999, 1e-08, 0.01, 10


def _params(**kw):
    return pltpu.CompilerParams(vmem_limit_bytes=VMEM_LIMIT_BYTES, **kw)


def _dot(a, b):
    return lax.dot_general(a, b, (((1,), (0,)), ((), ())), preferred_element_type=F32)


def _dot_nt(a, b):
    return lax.dot_general(a, b, (((1,), (1,)), ((), ())), preferred_element_type=F32)


def _dot_tn(a, b):
    return lax.dot_general(a, b, (((0,), (0,)), ((), ())), preferred_element_type=F32)


def _sigmoid(x):
    return 1.0 / (1.0 + jnp.exp(-x))


def _eye_mask(n):
    r = lax.broadcasted_iota(jnp.int32, (n, n), 0)
    c = lax.broadcasted_iota(jnp.int32, (n, n), 1)
    return r == c


def _row_to_col(row, eye):
    n = eye.shape[0]
    return jnp.sum(jnp.where(eye, jnp.broadcast_to(row, (n, n)), 0.0), axis=1, keepdims=True)


def _col_to_row(col, eye):
    n = eye.shape[0]
    return jnp.sum(jnp.where(eye, jnp.broadcast_to(col, (n, n)), 0.0), axis=0, keepdims=True)


def _mm(a, b, kind, *, tm, tn, tk, name, out_dtype=F32, ja=None, jb=None):
    a2, b2 = a.shape[-2:], b.shape[-2:]
    if kind == "nn":
        (M, K), (_, N) = a2, b2
    elif kind == "nt":
        (M, K), (N, _) = a2, b2
    else:
        (K, M), (_, N) = a2, b2
    J = a.shape[0] if ja else (b.shape[0] if jb else 1)
    batch = "b" in (ja, jb)
    red = "r" in (ja, jb)
    nk = K // tk
    nr = nk * (J if red else 1)
    grid = ((J if batch else 1), M // tm, N // tn, nr)

    def lead(mode, g, r):
        return g if mode == "b" else r // nk

    def a_map(g, i, n, r):
        kk = r % nk
        idx = (i, kk) if kind != "tn" else (kk, i)
        return idx if ja is None else (lead(ja, g, r),) + idx

    def b_map(g, i, n, r):
        kk = r % nk
        idx = (kk, n) if kind != "nt" else (n, kk)
        return idx if jb is None else (lead(jb, g, r),) + idx

    def o_map(g, i, n, r):
        return (g, i, n) if batch else (i, n)

    a_blk = (tm, tk) if kind != "tn" else (tk, tm)
    b_blk = (tk, tn) if kind != "nt" else (tn, tk)
    dot = {"nn": _dot, "nt": _dot_nt, "tn": _dot_tn}[kind]

    def body(a_ref, b_ref, o_ref, *scratch):
        part = dot(a_ref[...], b_ref[...])
        if nr == 1:
            o_ref[...] = part.astype(out_dtype)
        else:
            acc = scratch[0]
            r = pl.program_id(3)

            @pl.when(r == 0)
            def _():
                acc[...] = part

            @pl.when(r > 0)
            def _():
                acc[...] += part

            @pl.when(r == nr - 1)
            def _():
                o_ref[...] = acc[...].astype(out_dtype)

    return pl.pallas_call(
        body, name=name, grid=grid,
        in_specs=[pl.BlockSpec(a_blk if ja is None else (None,) + a_blk, a_map),
                  pl.BlockSpec(b_blk if jb is None else (None,) + b_blk, b_map)],
        out_specs=pl.BlockSpec((None, tm, tn) if batch else (tm, tn), o_map),
        out_shape=jax.ShapeDtypeStruct((J, M, N) if batch else (M, N), out_dtype),
        scratch_shapes=[pltpu.VMEM((tm, tn), F32)] if nr > 1 else [],
        compiler_params=_params(),
    )(a, b)


def _row_tile(S, want):
    return min(S, want)


ELEMENTWISE_TILE_BYTES = 1 << 20
ACT_ROWS = 1024
TOKEN_ROWS = 2048
MODEL_COLS = 1024


def _elementwise_rows(R, C):
    for cand in (1024, 512, 256, 128, 64, 32, 16, 8):
        if R % cand == 0 and R > cand and cand * C * 4 <= ELEMENTWISE_TILE_BYTES:
            return cand
    return R if R * C * 4 <= ELEMENTWISE_TILE_BYTES or R % 8 else 8


def _ffn_up(xn, wgu):
    S, D = xn.shape
    J, _, F2 = wgu.shape
    F = F2 // 2
    tm = _row_tile(S, 256)

    def body(x_ref, w_ref, gu_ref, h_ref):
        gu = _dot(x_ref[...], w_ref[...])
        g, u = gu[:, :F], gu[:, F:]
        gu_ref[...] = gu.astype(BF16)
        h_ref[...] = (g * _sigmoid(g) * u).astype(BF16)

    return pl.pallas_call(
        body, name="ffn_up", grid=(J, S // tm),
        in_specs=[pl.BlockSpec((tm, D), lambda j, i: (i, 0)),
                  pl.BlockSpec((None, D, F2), lambda j, i: (j, 0, 0))],
        out_specs=[pl.BlockSpec((None, tm, F2), lambda j, i: (j, i, 0)),
                   pl.BlockSpec((None, tm, F), lambda j, i: (j, i, 0))],
        out_shape=[jax.ShapeDtypeStruct((J, S, F2), BF16), jax.ShapeDtypeStruct((J, S, F), BF16)],
        compiler_params=_params(),
    )(xn, wgu)


def _ffn_bwd_hidden(df, wd, gu):
    S, D = df.shape
    J, F, _ = wd.shape
    F2 = 2 * F
    tm = _row_tile(S, 512)

    def body(df_ref, w_ref, gu_ref, o_ref):
        dh = _dot_nt(df_ref[...], w_ref[...])
        g = gu_ref[:, :F]
        u = gu_ref[:, F:]
        sg = _sigmoid(g.astype(F32)).astype(BF16)
        dhb = dh.astype(BF16)
        o_ref[:, :F] = dhb * u * (sg * (1.0 + g * (1.0 - sg)))
        o_ref[:, F:] = dhb * (g * sg)

    return pl.pallas_call(
        body, name="ffn_bwd_hidden", grid=(J, S // tm),
        in_specs=[pl.BlockSpec((tm, D), lambda j, i: (i, 0)),
                  pl.BlockSpec((None, F, D), lambda j, i: (j, 0, 0)),
                  pl.BlockSpec((None, tm, F2), lambda j, i: (j, i, 0))],
        out_specs=pl.BlockSpec((None, tm, F2), lambda j, i: (j, i, 0)),
        out_shape=jax.ShapeDtypeStruct((J, S, F2), BF16),
        compiler_params=_params(),
    )(df, wd, gu)


def _norm_fwd(x, g):
    S, D = x.shape
    tm = _row_tile(S, 512)

    def body(x_ref, g_ref, o_ref):
        xv = x_ref[...]
        r = lax.rsqrt(jnp.mean(xv * xv, axis=-1, keepdims=True) + EPS)
        o_ref[...] = (xv * r * g_ref[...]).astype(BF16)

    return pl.pallas_call(
        body, name="norm_fwd", grid=(S // tm,),
        in_specs=[pl.BlockSpec((tm, D), lambda i: (i, 0)), pl.BlockSpec((1, D), lambda i: (0, 0))],
        out_specs=pl.BlockSpec((tm, D), lambda i: (i, 0)),
        out_shape=jax.ShapeDtypeStruct((S, D), BF16),
        compiler_params=_params(),
    )(x, g.reshape(1, D))


def _resid_norm_fwd(x, f, g, alpha):
    S, D = x.shape
    tm = _row_tile(S, 512)

    def body(x_ref, f_ref, g_ref, o_ref):
        fv = f_ref[...]
        r = lax.rsqrt(jnp.mean(fv * fv, axis=-1, keepdims=True) + EPS)
        o_ref[...] = x_ref[...] + alpha * (fv * r * g_ref[...])

    return pl.pallas_call(
        body, name="resid_norm_fwd", grid=(S // tm,),
        in_specs=[pl.BlockSpec((tm, D), lambda i: (i, 0)), pl.BlockSpec((tm, D), lambda i: (i, 0)),
                  pl.BlockSpec((1, D), lambda i: (0, 0))],
        out_specs=pl.BlockSpec((tm, D), lambda i: (i, 0)),
        out_shape=jax.ShapeDtypeStruct((S, D), F32),
        compiler_params=_params(),
    )(x, f, g.reshape(1, D))


def _norm_bwd(x, g, dy, alpha, resid=None, out_dtype=F32):
    S, D = x.shape
    tm = _row_tile(S, 256)
    has_resid = resid is not None

    def body(*refs):
        x_ref, g_ref, dy_ref = refs[:3]
        res_ref = refs[3] if has_resid else None
        dx_ref, dg_ref = refs[-2:]
        xv = x_ref[...]
        r = lax.rsqrt(jnp.mean(xv * xv, axis=-1, keepdims=True) + EPS)
        xh = xv * r
        dyv = dy_ref[...].astype(F32) * alpha
        gdy = dyv * g_ref[...]
        dx = r * (gdy - xh * jnp.mean(xh * gdy, axis=-1, keepdims=True))
        if has_resid:
            dx = dx + res_ref[...]
        dx_ref[...] = dx.astype(out_dtype)
        part = jnp.sum(dyv * xh, axis=0, keepdims=True)

        @pl.when(pl.program_id(0) == 0)
        def _():
            dg_ref[...] = part

        @pl.when(pl.program_id(0) > 0)
        def _():
            dg_ref[...] += part

    row = pl.BlockSpec((tm, D), lambda i: (i, 0))
    vec = pl.BlockSpec((1, D), lambda i: (0, 0))
    ins = [x, g.reshape(1, D), dy] + ([resid] if has_resid else [])
    dx, dg = pl.pallas_call(
        body, name="norm_bwd_res" if has_resid else "norm_bwd", grid=(S // tm,),
        in_specs=[row, vec, row] + ([row] if has_resid else []),
        out_specs=[row, vec],
        out_shape=[jax.ShapeDtypeStruct((S, D), out_dtype), jax.ShapeDtypeStruct((1, D), F32)],
        compiler_params=_params(),
    )(*ins)
    return dx, dg.reshape(D)


def _loss_grad(y, target):
    S, D = y.shape
    tm = _row_tile(S, 512)

    def body(y_ref, t_ref, dy_ref, l_ref):
        err = y_ref[...] - t_ref[...]
        dy_ref[...] = err * (1.0 / D)
        part = jnp.sum(jnp.sum(err * err, axis=-1, keepdims=True) * (0.5 / D), axis=0, keepdims=True)
        part = jnp.broadcast_to(part, (8, LANES))

        @pl.when(pl.program_id(0) == 0)
        def _():
            l_ref[...] = part

        @pl.when(pl.program_id(0) > 0)
        def _():
            l_ref[...] += part

    row = pl.BlockSpec((tm, D), lambda i: (i, 0))
    return pl.pallas_call(
        body, name="loss_grad", grid=(S // tm,),
        in_specs=[row, row],
        out_specs=[row, pl.BlockSpec((8, LANES), lambda i: (0, 0))],
        out_shape=[jax.ShapeDtypeStruct((S, D), F32), jax.ShapeDtypeStruct((8, LANES), F32)],
        compiler_params=_params(),
    )(y, target)


def _add_n(xs, name):
    shape = xs[0].shape
    C = shape[-1]
    R = 1
    for s in shape[:-1]:
        R *= s
    tm = _elementwise_rows(R, C)

    def body(*refs):
        acc = refs[0][...]
        for r in refs[1:-1]:
            acc = acc + r[...]
        refs[-1][...] = acc

    row = pl.BlockSpec((tm, C), lambda i: (i, 0))
    out = pl.pallas_call(
        body, name=name, grid=(R // tm,),
        in_specs=[row] * len(xs), out_specs=row,
        out_shape=jax.ShapeDtypeStruct((R, C), F32),
        compiler_params=_params(),
    )(*[x.reshape(R, C) for x in xs])
    return out.reshape(shape)


def _pair_sum(split, got):
    J, _, r, C = split.shape
    tm = _elementwise_rows(r, C)
    core = lax.axis_index("c").astype(jnp.int32).reshape(1)

    def body(core_ref, a_ref, b_ref, o_ref):
        o_ref[...] = a_ref[...] + b_ref[...]

    row = pl.BlockSpec((None, tm, C), lambda j, i, core_ref: (j, i, 0))
    return pl.pallas_call(
        body, name="pair_sum",
        grid_spec=pltpu.PrefetchScalarGridSpec(
            num_scalar_prefetch=1, grid=(J, r // tm),
            in_specs=[pl.BlockSpec((None, None, tm, C), lambda j, i, core_ref: (j, core_ref[0], i, 0)), row],
            out_specs=row),
        out_shape=jax.ShapeDtypeStruct((J, r, C), F32),
        compiler_params=_params(),
    )(core, split, got)


def _chip_sum(pair, arrived):
    _, r, C = pair.shape
    tm = _elementwise_rows(r, C)
    chip = (2 * lax.axis_index("x") + lax.axis_index("y")).astype(jnp.int32).reshape(1)

    def body(chip_ref, own_ref, a_ref, b_ref, c_ref, o_ref):
        o_ref[...] = own_ref[...] + a_ref[...] + b_ref[...] + c_ref[...]

    part = lambda k: pl.BlockSpec((None, tm, C), lambda i, chip_ref: (k, i, 0))
    return pl.pallas_call(
        body, name="chip_sum",
        grid_spec=pltpu.PrefetchScalarGridSpec(
            num_scalar_prefetch=1, grid=(r // tm,),
            in_specs=[pl.BlockSpec((None, tm, C), lambda i, chip_ref: (chip_ref[0], i, 0)), part(0), part(1), part(2)],
            out_specs=pl.BlockSpec((tm, C), lambda i, chip_ref: (i, 0))),
        out_shape=jax.ShapeDtypeStruct((r, C), F32),
        compiler_params=_params(),
    )(chip, pair, arrived, arrived, arrived)


def _rope_tables(S):
    half = ATT_HEAD_DIM // 2
    inv_freq = ROPE_THETA ** (-jnp.arange(half, dtype=F32) / half)
    ang = jnp.arange(S, dtype=F32)[:, None] * inv_freq[None, :]
    cos, sin = jnp.cos(ang), jnp.sin(ang)
    return jnp.concatenate([cos, cos], axis=-1), jnp.concatenate([-sin, sin], axis=-1)


def _rotate(x, cosf, sinf):
    return x * cosf + pltpu.roll(x, ATT_HEAD_DIM // 2, 1) * sinf


def _rope_fwd(proj, cosf, sinf):
    S = proj.shape[0]
    tm = _row_tile(S, 512)

    def body(q_ref, k_ref, v_ref, c_ref, s_ref, qo_ref, ko_ref, vo_ref):
        c, s = c_ref[...], s_ref[...]
        for h in range(ATT_HEADS):
            sl = slice(h * ATT_HEAD_DIM, (h + 1) * ATT_HEAD_DIM)
            qo_ref[:, sl] = _rotate(q_ref[:, sl], c, s).astype(BF16)
        for h in range(ATT_KV_HEADS):
            sl = slice(h * ATT_HEAD_DIM, (h + 1) * ATT_HEAD_DIM)
            ko_ref[:, sl] = _rotate(k_ref[:, sl], c, s).astype(BF16)
        vo_ref[...] = v_ref[...].astype(BF16)

    tab = pl.BlockSpec((tm, ATT_HEAD_DIM), lambda i: (i, 0))
    return pl.pallas_call(
        body, name="rope_fwd", grid=(S // tm,),
        in_specs=[pl.BlockSpec((tm, ATT_WIDTH), lambda i: (i, 0)),
                  pl.BlockSpec((tm, KV_WIDTH), lambda i: (i, OFF_KA // KV_WIDTH)),
                  pl.BlockSpec((tm, KV_WIDTH), lambda i: (i, OFF_VA // KV_WIDTH)), tab, tab],
        out_specs=[pl.BlockSpec((tm, ATT_WIDTH), lambda i: (i, 0)),
                   pl.BlockSpec((tm, KV_WIDTH), lambda i: (i, 0)),
                   pl.BlockSpec((tm, KV_WIDTH), lambda i: (i, 0))],
        out_shape=[jax.ShapeDtypeStruct((S, ATT_WIDTH), BF16), jax.ShapeDtypeStruct((S, KV_WIDTH), BF16),
                   jax.ShapeDtypeStruct((S, KV_WIDTH), BF16)],
        compiler_params=_params(),
    )(proj, proj, proj, cosf, sinf)


def _rope_bwd(dq, dk, dv, cosf, sinf):
    S = dq.shape[0]
    tm = _row_tile(S, 512)
    W = ATT_WIDTH + 2 * KV_WIDTH

    def body(q_ref, k_ref, v_ref, c_ref, s_ref, o_ref):
        c, s = c_ref[...], -s_ref[...]
        for h in range(ATT_HEADS):
            sl = slice(h * ATT_HEAD_DIM, (h + 1) * ATT_HEAD_DIM)
            o_ref[:, sl] = _rotate(q_ref[:, sl], c, s).astype(BF16)
        for h in range(ATT_KV_HEADS):
            sl = slice(h * ATT_HEAD_DIM, (h + 1) * ATT_HEAD_DIM)
            o_ref[:, ATT_WIDTH + h * ATT_HEAD_DIM:ATT_WIDTH + (h + 1) * ATT_HEAD_DIM] = (
                _rotate(k_ref[:, sl], c, s).astype(BF16))
        o_ref[:, ATT_WIDTH + KV_WIDTH:] = v_ref[...].astype(BF16)

    tab = pl.BlockSpec((tm, ATT_HEAD_DIM), lambda i: (i, 0))
    return pl.pallas_call(
        body, name="rope_bwd", grid=(S // tm,),
        in_specs=[pl.BlockSpec((tm, ATT_WIDTH), lambda i: (i, 0)),
                  pl.BlockSpec((tm, KV_WIDTH), lambda i: (i, 0)),
                  pl.BlockSpec((tm, KV_WIDTH), lambda i: (i, 0)), tab, tab],
        out_specs=pl.BlockSpec((tm, W), lambda i: (i, 0)),
        out_shape=jax.ShapeDtypeStruct((S, W), BF16),
        compiler_params=_params(),
    )(dq, dk, dv, cosf, sinf)


def _attn_probs(q_ref, k_refs, sink_ref, kh, n, nb):
    G, L, Dh = ATT_GROUP, BLK, ATT_HEAD_DIM
    q4 = jnp.concatenate([q_ref[:, (kh * G + g) * Dh:(kh * G + g + 1) * Dh] for g in range(G)], axis=0)
    kcat = jnp.concatenate([r[:, kh * Dh:(kh + 1) * Dh] for r in k_refs], axis=0)
    s = _dot_nt(q4, kcat) * (Dh ** -0.5)
    row = lax.broadcasted_iota(jnp.int32, (G * L, 3 * L), 0) % L
    col = lax.broadcasted_iota(jnp.int32, (G * L, 3 * L), 1)
    kpos = (n - 1) * L + col
    mask = (jnp.abs(col - L - row) <= L) & (kpos >= 0) & (kpos < nb * L)
    s = jnp.where(mask, s, -jnp.inf)
    sink = jnp.concatenate([jnp.broadcast_to(sink_ref[kh, :, g:g + 1], (L, 1)) for g in range(G)], axis=0)
    m = jnp.maximum(jnp.max(s, axis=-1, keepdims=True), sink)
    p = jnp.exp(s - m)
    es = jnp.exp(sink - m)
    inv = 1.0 / (jnp.sum(p, axis=-1, keepdims=True) + es)
    return q4, kcat, p * inv, es * inv


def _kv_specs(nb):
    return [pl.BlockSpec((BLK, KV_WIDTH), lambda n: (jnp.maximum(n - 1, 0), 0)),
            pl.BlockSpec((BLK, KV_WIDTH), lambda n: (n, 0)),
            pl.BlockSpec((BLK, KV_WIDTH), lambda n: (jnp.minimum(n + 1, nb - 1), 0))]


def _attn_fwd(qr, kr, va, sink):
    S = qr.shape[0]
    nb = S // BLK
    G, Dh = ATT_GROUP, ATT_HEAD_DIM

    def body(q_ref, k0, k1, k2, v0, v1, v2, sink_ref, o_ref):
        n = pl.program_id(0)
        for kh in range(ATT_KV_HEADS):
            _, _, probs, _ = _attn_probs(q_ref, (k0, k1, k2), sink_ref, kh, n, nb)
            vcat = jnp.concatenate([r[:, kh * Dh:(kh + 1) * Dh] for r in (v0, v1, v2)], axis=0)
            out = _dot(probs.astype(BF16), vcat)
            for g in range(G):
                o_ref[:, (kh * G + g) * Dh:(kh * G + g + 1) * Dh] = out[g * BLK:(g + 1) * BLK, :].astype(BF16)

    qspec = pl.BlockSpec((BLK, ATT_WIDTH), lambda n: (n, 0))
    return pl.pallas_call(
        body, name="attn_fwd", grid=(nb,),
        in_specs=[qspec] + _kv_specs(nb) + _kv_specs(nb)
        + [pl.BlockSpec((ATT_KV_HEADS, 1, G), lambda n: (0, 0, 0))],
        out_specs=qspec,
        out_shape=jax.ShapeDtypeStruct((S, ATT_WIDTH), BF16),
        compiler_params=_params(),
    )(qr, kr, kr, kr, va, va, va, sink.reshape(ATT_KV_HEADS, 1, G))


def _attn_bwd(qr, kr, va, sink, dycat):
    S = qr.shape[0]
    nb = S // BLK
    G, L, Dh = ATT_GROUP, BLK, ATT_HEAD_DIM
    SP = S + 2 * L

    def body(q_ref, k0, k1, k2, v0, v1, v2, sink_ref, do_ref, dq_ref, dk_ref, dv_ref, ds_ref):
        n = pl.program_id(0)

        @pl.when(n == 0)
        def _():
            dk_ref[...] = jnp.zeros_like(dk_ref)
            dv_ref[...] = jnp.zeros_like(dv_ref)
            ds_ref[...] = jnp.zeros_like(ds_ref)

        rows = pl.ds(pl.multiple_of(n * L, L), 3 * L)
        lane = lax.broadcasted_iota(jnp.int32, (8, LANES), 1)
        for kh in range(ATT_KV_HEADS):
            q4, kcat, probs, psink = _attn_probs(q_ref, (k0, k1, k2), sink_ref, kh, n, nb)
            vcat = jnp.concatenate([r[:, kh * Dh:(kh + 1) * Dh] for r in (v0, v1, v2)], axis=0)
            do4 = jnp.concatenate([do_ref[:, (kh * G + g) * Dh:(kh * G + g + 1) * Dh] for g in range(G)], axis=0)
            pb = probs.astype(BF16)
            dob = do4.astype(BF16)
            out = _dot(pb, vcat)
            delta = jnp.sum(do4 * out, axis=-1, keepdims=True)
            dp = _dot_nt(dob, vcat)
            dsc = (probs * (dp - delta) * (Dh ** -0.5)).astype(BF16)
            dq4 = _dot(dsc, kcat)
            for g in range(G):
                dq_ref[:, (kh * G + g) * Dh:(kh * G + g + 1) * Dh] = dq4[g * L:(g + 1) * L, :]
            dk_ref[rows, kh * Dh:(kh + 1) * Dh] += _dot_tn(dsc, q4)
            dv_ref[rows, kh * Dh:(kh + 1) * Dh] += _dot_tn(pb, dob)
            dsink = jnp.zeros((8, LANES), F32)
            for g in range(G):
                val = -jnp.sum(psink[g * L:(g + 1) * L] * delta[g * L:(g + 1) * L], axis=0, keepdims=True)
                dsink = dsink + jnp.where(lane == g, jnp.broadcast_to(val, (8, LANES)), 0.0)
            ds_ref[kh] += dsink

    qspec = pl.BlockSpec((L, ATT_WIDTH), lambda n: (n, 0))
    accspec = pl.BlockSpec((SP, KV_WIDTH), lambda n: (0, 0))
    dq, dkp, dvp, dsink = pl.pallas_call(
        body, name="attn_bwd", grid=(nb,),
        in_specs=[qspec] + _kv_specs(nb) + _kv_specs(nb)
        + [pl.BlockSpec((ATT_KV_HEADS, 1, G), lambda n: (0, 0, 0)), qspec],
        out_specs=[qspec, accspec, accspec, pl.BlockSpec((ATT_KV_HEADS, 8, LANES), lambda n: (0, 0, 0))],
        out_shape=[jax.ShapeDtypeStruct((S, ATT_WIDTH), F32), jax.ShapeDtypeStruct((SP, KV_WIDTH), F32),
                   jax.ShapeDtypeStruct((SP, KV_WIDTH), F32),
                   jax.ShapeDtypeStruct((ATT_KV_HEADS, 8, LANES), F32)],
        compiler_params=_params(),
    )(qr, kr, kr, kr, va, va, va, sink.reshape(ATT_KV_HEADS, 1, G), dycat)
    return dq, dkp[L:L + S], dvp[L:L + S], dsink[:, 0, :G].reshape(ATT_HEADS)


CONV_COLS = 256
CONV_ROWS = 512


def _conv_shifts(xs, rows):
    total = xs.shape[0]
    out = []
    for j in range(CONV_WIDTH):
        shift = (CONV_WIDTH // 2 - j) % total
        out.append((pltpu.roll(xs, shift, 0) if shift else xs)[CONV_HALO:CONV_HALO + rows, :])
    return out


def _conv_taps(xs, w_ref, rows):
    acc = None
    for j, xj in enumerate(_conv_shifts(xs, rows)):
        term = xj * w_ref[j:j + 1, :]
        acc = term if acc is None else acc + term
    return acc


def _conv_window(x_ref, i, R, nrow):
    S = x_ref.shape[0]
    r0 = pl.multiple_of(i * R, R)
    top = x_ref[pl.ds(pl.multiple_of(jnp.maximum(r0 - CONV_HALO, 0), CONV_HALO), CONV_HALO), :]
    bot = x_ref[pl.ds(pl.multiple_of(jnp.minimum(r0 + R, S - CONV_HALO), CONV_HALO), CONV_HALO), :]
    top = jnp.where(i > 0, top, 0.0)
    bot = jnp.where(i < nrow - 1, bot, 0.0)
    return jnp.concatenate([top, x_ref[pl.ds(r0, R), :], bot], axis=0)


def _conv5(x, w, *, col0, act, out_dtype, name):
    S = x.shape[0]
    C = w.shape[1]
    R = _row_tile(S, CONV_ROWS)
    tc = CONV_COLS
    half_blocks = (C // 2) // tc
    nrow = S // R

    def body(x_ref, w_ref, o_ref):
        scale = jnp.where(pl.program_id(0) >= half_blocks, M_HEAD_DIM ** -0.5, 1.0)
        y = _conv_taps(_conv_window(x_ref, pl.program_id(1), R, nrow), w_ref, R)
        if act:
            y = y * _sigmoid(y) * scale
        o_ref[...] = y.astype(out_dtype)

    return pl.pallas_call(
        body, name=name, grid=(C // tc, nrow),
        in_specs=[pl.BlockSpec((S, tc), lambda c, i: (0, col0 // tc + c)),
                  pl.BlockSpec((CONV_WIDTH, tc), lambda c, i: (0, c))],
        out_specs=pl.BlockSpec((R, tc), lambda c, i: (i, c)),
        out_shape=jax.ShapeDtypeStruct((S, C), out_dtype),
        compiler_params=_params(),
    )(x, w)


def _conv_bwd_pre(x, col0, w, dq2, dk2):
    S = x.shape[0]
    C = w.shape[1]
    R = _row_tile(S, CONV_ROWS)
    tc = CONV_COLS
    half_blocks = (C // 2) // tc
    nrow = S // R

    def body(x_ref, w_ref, dqa_ref, dqb_ref, dka_ref, dkb_ref, o_ref, dw_ref, acc):
        is_k = pl.program_id(0) >= half_blocks
        i = pl.program_id(1)
        scale = jnp.where(is_k, M_HEAD_DIM ** -0.5, 1.0)

        @pl.when(i == 0)
        def _():
            acc[...] = jnp.zeros_like(acc)

        shifted = _conv_shifts(_conv_window(x_ref, i, R, nrow), R)
        y = shifted[0] * w_ref[0:1, :]
        for j in range(1, CONV_WIDTH):
            y = y + shifted[j] * w_ref[j:j + 1, :]
        sg = _sigmoid(y)
        dqv = dqa_ref[...] + dqb_ref[...]
        dkv = dka_ref[...] + dkb_ref[...]
        dpre = jnp.where(is_k, dkv, dqv) * scale * (sg * (1.0 + y * (1.0 - sg)))
        o_ref[...] = dpre
        for j in range(CONV_WIDTH):
            acc[j:j + 1, :] += jnp.sum(dpre * shifted[j], axis=0, keepdims=True)

        @pl.when(i == nrow - 1)
        def _():
            dw_ref[...] = acc[0:CONV_WIDTH, :]

    nqb = (C // 2) // tc
    qmap = lambda d: (lambda c, i: (d, i, jnp.minimum(c, nqb - 1)))
    kmap = lambda d: (lambda c, i: (d, i, jnp.maximum(c - nqb, 0)))
    gspec = lambda m: pl.BlockSpec((None, R, tc), m)
    return pl.pallas_call(
        body, name="conv_bwd_pre", grid=(C // tc, nrow),
        in_specs=[pl.BlockSpec((S, tc), lambda c, i: (0, col0 // tc + c)),
                  pl.BlockSpec((CONV_WIDTH, tc), lambda c, i: (0, c)),
                  gspec(qmap(0)), gspec(qmap(1)), gspec(kmap(0)), gspec(kmap(1))],
        out_specs=[pl.BlockSpec((R, tc), lambda c, i: (i, c)), pl.BlockSpec((CONV_WIDTH, tc), lambda c, i: (0, c))],
        out_shape=[jax.ShapeDtypeStruct((S, C), F32), jax.ShapeDtypeStruct((CONV_WIDTH, C), F32)],
        scratch_shapes=[pltpu.VMEM((8, tc), F32)],
        compiler_params=_params(),
    )(x, w, dq2, dq2, dk2, dk2)


def _lane_cumsum(x, reverse):
    lane = lax.broadcasted_iota(jnp.int32, x.shape, 1)
    sh = 1
    while sh < LANES:
        if reverse:
            x = x + jnp.where(lane < LANES - sh, pltpu.roll(x, LANES - sh, 1), 0.0)
        else:
            x = x + jnp.where(lane >= sh, pltpu.roll(x, sh, 1), 0.0)
        sh *= 2
    return x


def _gate_prep(gates_t, bias):
    R = gates_t.shape[0]
    half, quarter = R // 2, R // 4

    def body(g_ref, b_ref, ig_ref, cum_ref):
        ig_ref[...] = g_ref[0:half, :] + b_ref[0:half, :]
        fg = g_ref[half:R, :] + b_ref[half:R, :]
        lf = jnp.minimum(fg, 0.0) - jnp.log(1.0 + jnp.exp(-jnp.abs(fg)))
        cum_ref[0:quarter, :] = _lane_cumsum(lf[0:quarter, :], False)
        cum_ref[quarter:half, :] = _lane_cumsum(lf[quarter:half, :], True)

    return pl.pallas_call(
        body, name="gate_prep",
        out_shape=[jax.ShapeDtypeStruct((half, LANES), F32), jax.ShapeDtypeStruct((half, LANES), F32)],
        compiler_params=_params(),
    )(gates_t, bias)


def _gate_bwd(gates_t, bias, di, dfc, dbl, nc):
    R = gates_t.shape[0]
    half, quarter = R // 2, R // 4

    def body(g_ref, b_ref, di_ref, df_ref, dbl_ref, dg_ref, db_ref):
        dg_ref[0:half, :] = di_ref[...]
        dfv = df_ref[...]
        within = jnp.concatenate([_lane_cumsum(dfv[0:quarter, :], True),
                                  _lane_cumsum(dfv[quarter:half, :], False)], axis=0)
        fg = g_ref[half:R, :] + b_ref[half:R, :]
        dg_ref[half:R, :] = (within + dbl_ref[...]) * _sigmoid(-fg)
        rows = jnp.broadcast_to(jnp.sum(dg_ref[...], axis=-1, keepdims=True), (R, LANES))
        gr = lax.broadcasted_iota(jnp.int32, (N_GATES, R), 0)
        gc = lax.broadcasted_iota(jnp.int32, (N_GATES, R), 1)
        db_ref[...] = lax.dot_general((gc // nc == gr).astype(F32), rows, (((1,), (0,)), ((), ())),
                                      precision=lax.Precision.HIGHEST, preferred_element_type=F32)

    return pl.pallas_call(
        body, name="gate_bwd",
        out_shape=[jax.ShapeDtypeStruct((R, LANES), F32), jax.ShapeDtypeStruct((N_GATES, LANES), F32)],
        compiler_params=_params(),
    )(gates_t, bias, di, dfc, dbl)


def _chunk_index(d, c, nc, reverse):
    j = (nc - 1 - c) if reverse else c
    return j + d * (nc - 1 - 2 * j)


def _mlstm_chunk(d, q, k, vb, brow, igrow, C, nvec, m_prev, eye):
    L = BLK
    r = lax.broadcasted_iota(jnp.int32, (L, L), 0)
    c = lax.broadcasted_iota(jnp.int32, (L, L), 1)
    mask = (r - c) * (1 - 2 * d) >= 0
    bcol = _row_to_col(brow, eye)
    igcol = _row_to_col(igrow, eye)
    log_d = jnp.where(mask, bcol - brow + igrow, -jnp.inf)
    log_inter = bcol + m_prev
    m_t = jnp.maximum(log_inter, jnp.max(log_d, axis=-1, keepdims=True))
    d_mat = jnp.exp(log_d - m_t)
    inter = jnp.exp(log_inter - m_t)
    s = _dot_nt(q, k) * d_mat
    sb = s.astype(BF16)
    cb = C.astype(BF16)
    num = _dot(sb, vb) + inter * _dot_nt(q, cb)
    den = jnp.sum(s, axis=-1, keepdims=True) + inter * jnp.sum(q.astype(F32) * nvec, axis=-1, keepdims=True)
    floor = jnp.exp(-m_t)
    denom = jnp.maximum(jnp.abs(den), floor)
    b_last = jnp.where(d == 0, brow[:, L - 1:L], brow[:, 0:1])
    return dict(bcol=bcol, igcol=igcol, m_t=m_t, d_mat=d_mat, inter=inter, sb=sb, cb=cb, num=num, den=den,
                floor=floor, denom=denom, b_last=b_last)


def _head_cols(v0_ref, v1_ref, hd):
    ref = v0_ref if hd < M_HEADS // 2 else v1_ref
    lo = (hd % (M_HEADS // 2)) * M_HEAD_DIM
    return ref[:, lo:lo + M_HEAD_DIM]


def _mlstm_fwd(qk, proj, ig5, b5):
    S = qk.shape[0]
    nc = S // BLK
    L, Dh, H = BLK, M_HEAD_DIM, M_HEADS

    def body(q_ref, k_ref, v0_ref, v1_ref, ig_ref, b_ref, h_ref, cs_ref, ns_ref, ms_ref, C, nvec, m):
        d = pl.program_id(0)

        @pl.when(pl.program_id(1) == 0)
        def _():
            C[...] = jnp.zeros_like(C)
            nvec[...] = jnp.zeros_like(nvec)
            m[...] = jnp.zeros_like(m)

        eye = _eye_mask(L)
        for hd in range(H):
            cols = slice(hd * Dh, (hd + 1) * Dh)
            q, k = q_ref[:, cols], k_ref[:, cols]
            vf = _head_cols(v0_ref, v1_ref, hd)
            brow, igrow = b_ref[hd], ig_ref[hd]
            m_prev = m[hd, :, 0:1]
            cs_ref[hd] = C[hd].astype(BF16)
            ns_ref[hd] = nvec[hd]
            ms_ref[hd] = m[hd]
            f = _mlstm_chunk(d, q, k, vf.astype(BF16), brow, igrow, C[hd], nvec[hd], m_prev, eye)
            h_ref[:, cols] = f["num"] / f["denom"]
            b_last = f["b_last"]
            log_w = b_last - brow + igrow
            m_new = jnp.maximum(b_last + m_prev, jnp.max(log_w, axis=-1, keepdims=True))
            w_col = jnp.exp(b_last - f["bcol"] + f["igcol"] - m_new)
            decay = jnp.exp(b_last + m_prev - m_new)
            C[hd] = decay * C[hd] + _dot_tn((w_col * vf).astype(BF16), k)
            nvec[hd] = decay * nvec[hd] + jnp.sum(w_col * k.astype(F32), axis=0, keepdims=True)
            m[hd] = jnp.broadcast_to(m_new, (1, LANES))

    cidx = lambda d, c: _chunk_index(d, c, nc, False)
    gspec = pl.BlockSpec((None, H, None, 1, LANES), lambda d, c: (d, 0, cidx(d, c), 0, 0))
    st = lambda *blk: pl.BlockSpec((None, H, None) + blk, lambda d, c: (d, 0, cidx(d, c), 0, 0))
    half = M_WIDTH // 2
    return pl.pallas_call(
        body, name="mlstm_fwd", grid=(2, nc),
        in_specs=[pl.BlockSpec((L, M_WIDTH), lambda d, c: (cidx(d, c), 0)),
                  pl.BlockSpec((L, M_WIDTH), lambda d, c: (cidx(d, c), 1)),
                  pl.BlockSpec((L, half), lambda d, c: (cidx(d, c), OFF_VM // half)),
                  pl.BlockSpec((L, half), lambda d, c: (cidx(d, c), OFF_VM // half + 1)), gspec, gspec],
        out_specs=[pl.BlockSpec((None, L, M_WIDTH), lambda d, c: (d, cidx(d, c), 0)),
                   st(Dh, Dh), st(1, Dh), st(1, LANES)],
        out_shape=[jax.ShapeDtypeStruct((2, S, M_WIDTH), F32), jax.ShapeDtypeStruct((2, H, nc, Dh, Dh), BF16),
                   jax.ShapeDtypeStruct((2, H, nc, 1, Dh), F32), jax.ShapeDtypeStruct((2, H, nc, 1, LANES), F32)],
        scratch_shapes=[pltpu.VMEM((H, Dh, Dh), F32), pltpu.VMEM((H, 1, Dh), F32), pltpu.VMEM((H, 1, LANES), F32)],
        compiler_params=_params(),
    )(qk, qk, proj, proj, ig5, b5)


def _mlstm_bwd(qk, proj, ig5, b5, cs, ns, ms, dhm):
    S = qk.shape[0]
    nc = S // BLK
    L, Dh, H = BLK, M_HEAD_DIM, M_HEADS

    def body(q_ref, k_ref, v0_ref, v1_ref, ig_ref, b_ref, cs_ref, ns_ref, ms_ref, dh_ref,
             dq_ref, dk_ref, dv_ref, di_ref, df_ref, dbl_ref, R, rvec, mu):
        d = pl.program_id(0)

        @pl.when(pl.program_id(1) == 0)
        def _():
            R[...] = jnp.zeros_like(R)
            rvec[...] = jnp.zeros_like(rvec)
            mu[...] = jnp.full(mu.shape, NEG_BIG, F32)

        eye = _eye_mask(L)
        for hd in range(H):
            cols = slice(hd * Dh, (hd + 1) * Dh)
            q, k = q_ref[:, cols], k_ref[:, cols]
            qf, kf = q.astype(F32), k.astype(F32)
            vb = _head_cols(v0_ref, v1_ref, hd).astype(BF16)
            brow, igrow = b_ref[hd], ig_ref[hd]
            nprev = ns_ref[hd]
            m_prev = ms_ref[hd, :, 0:1]
            cprev = cs_ref[hd]
            f = _mlstm_chunk(d, q, k, vb, brow, igrow, cprev, nprev, m_prev, eye)
            dh = dh_ref[:, cols]
            inv = 1.0 / f["denom"]
            hcur = f["num"] * inv
            dnum = dh * inv
            active = jnp.abs(f["den"]) >= f["floor"]
            dden = (jnp.where(active, -jnp.sign(f["den"]), 0.0)
                    * jnp.sum(dh * hcur, axis=-1, keepdims=True) * inv)
            dnb = dnum.astype(BF16)
            dqk = ((_dot_nt(dnb, vb) + dden) * f["d_mat"]).astype(BF16)
            dq = _dot(dqk, k) + f["inter"] * (_dot(dnb, f["cb"]) + dden * nprev)
            mu_prev = mu[hd, :, 0:1]
            a_col = jnp.exp(f["b_last"] - f["bcol"] + f["igcol"] + mu_prev)
            rb = R[hd].astype(BF16)
            dv = _dot_tn(f["sb"], dnb) + a_col * _dot_nt(k, rb)
            dk_inter = a_col * (_dot(vb, rb) + rvec[hd])
            dk = _dot_tn(dqk, q) + dk_inter
            dq_ref[:, cols] = dq
            dk_ref[:, cols] = dk
            dv_ref[:, cols] = dv
            kdk = jnp.sum(kf * dk, axis=-1, keepdims=True)
            qdq = jnp.sum(qf * dq, axis=-1, keepdims=True)
            di_ref[hd] = _col_to_row(kdk, eye)
            df_ref[hd] = _col_to_row(qdq - kdk, eye)
            older = jnp.sum(jnp.sum(R[hd] * cprev.astype(F32), axis=-1, keepdims=True), axis=0, keepdims=True)
            older = older + jnp.sum(rvec[hd] * nprev, axis=-1, keepdims=True)
            dbl = (jnp.sum(jnp.sum(kf * dk_inter, axis=-1, keepdims=True), axis=0, keepdims=True)
                   + jnp.exp(f["b_last"] + mu_prev + m_prev) * older)
            dbl_ref[hd] = jnp.broadcast_to(dbl, (1, LANES))
            lw = f["bcol"] - f["m_t"]
            mu_new = jnp.maximum(f["b_last"] + mu_prev, jnp.max(lw, axis=0, keepdims=True))
            wq = jnp.exp(lw - mu_new)
            decay = jnp.exp(f["b_last"] + mu_prev - mu_new)
            R[hd] = decay * R[hd] + _dot_tn((wq * dnum).astype(BF16), q)
            rvec[hd] = decay * rvec[hd] + jnp.sum(wq * dden * qf, axis=0, keepdims=True)
            mu[hd] = jnp.broadcast_to(mu_new, (1, LANES))

    cidx = lambda d, c: _chunk_index(d, c, nc, True)
    gin = pl.BlockSpec((None, H, None, 1, LANES), lambda d, c: (d, 0, cidx(d, c), 0, 0))
    st = lambda *blk: pl.BlockSpec((None, H, None) + blk, lambda d, c: (d, 0, cidx(d, c), 0, 0))
    per_dir = pl.BlockSpec((None, L, M_WIDTH), lambda d, c: (d, cidx(d, c), 0))
    big = jax.ShapeDtypeStruct((2, S, M_WIDTH), F32)
    small = jax.ShapeDtypeStruct((2, H, nc, 1, LANES), F32)
    half = M_WIDTH // 2
    return pl.pallas_call(
        body, name="mlstm_bwd", grid=(2, nc),
        in_specs=[pl.BlockSpec((L, M_WIDTH), lambda d, c: (cidx(d, c), 0)),
                  pl.BlockSpec((L, M_WIDTH), lambda d, c: (cidx(d, c), 1)),
                  pl.BlockSpec((L, half), lambda d, c: (cidx(d, c), OFF_VM // half)),
                  pl.BlockSpec((L, half), lambda d, c: (cidx(d, c), OFF_VM // half + 1)), gin, gin,
                  st(Dh, Dh), st(1, Dh), st(1, LANES),
                  pl.BlockSpec((L, M_WIDTH), lambda d, c: (cidx(d, c), 0))],
        out_specs=[per_dir, per_dir, per_dir, gin, gin, gin],
        out_shape=[big, big, big, small, small, small],
        scratch_shapes=[pltpu.VMEM((H, Dh, Dh), F32), pltpu.VMEM((H, 1, Dh), F32), pltpu.VMEM((H, 1, LANES), F32)],
        compiler_params=_params(),
    )(qk, qk, proj, proj, ig5, b5, cs, ns, ms, dhm)


def _mlstm_out_fwd(hs, proj, gamma):
    S = hs.shape[1]
    tm = _row_tile(S, 512)
    Dh, H = M_HEAD_DIM, M_HEADS

    def body(hf_ref, hb_ref, o_ref, g_ref, y_ref):
        hm = hf_ref[...] + hb_ref[...]
        r = lax.rsqrt(jnp.mean(hm * hm, axis=-1, keepdims=True) + EPS)
        y_ref[...] = (_sigmoid(o_ref[...]) * (hm * r * g_ref[...])).astype(BF16)

    hspec = lambda d: pl.BlockSpec((None, tm, Dh), lambda i, h: (d, i, h))
    return pl.pallas_call(
        body, name="mlstm_out_fwd", grid=(S // tm, H),
        in_specs=[hspec(0), hspec(1), pl.BlockSpec((tm, Dh), lambda i, h: (i, OFF_OM // Dh + h)),
                  pl.BlockSpec((1, Dh), lambda i, h: (0, h))],
        out_specs=pl.BlockSpec((tm, Dh), lambda i, h: (i, h)),
        out_shape=jax.ShapeDtypeStruct((S, M_WIDTH), BF16),
        compiler_params=_params(),
    )(hs, hs, proj, gamma.reshape(1, M_WIDTH))


def _mlstm_out_bwd(hs, proj, gamma, dycat):
    S = hs.shape[1]
    tm = _row_tile(S, 512)
    Dh, H = M_HEAD_DIM, M_HEADS
    D_OFF = ATT_WIDTH // Dh

    def body(hf_ref, hb_ref, o_ref, g_ref, dy_ref, dh_ref, do_ref, dg_ref):
        hm = hf_ref[...] + hb_ref[...]
        r = lax.rsqrt(jnp.mean(hm * hm, axis=-1, keepdims=True) + EPS)
        hh = hm * r
        so = _sigmoid(o_ref[...])
        dy = dy_ref[...]
        gam = g_ref[...]
        do_ref[...] = (dy * hh * gam * so * (1.0 - so)).astype(BF16)
        dyn = dy * so
        gd = dyn * gam
        dh_ref[...] = r * (gd - hh * jnp.mean(hh * gd, axis=-1, keepdims=True))
        part = jnp.sum(dyn * hh, axis=0, keepdims=True)

        @pl.when(pl.program_id(1) == 0)
        def _():
            dg_ref[...] = part

        @pl.when(pl.program_id(1) > 0)
        def _():
            dg_ref[...] += part

    hspec = lambda d: pl.BlockSpec((None, tm, Dh), lambda h, i: (d, i, h))
    out = pl.BlockSpec((tm, Dh), lambda h, i: (i, h))
    vec = pl.BlockSpec((1, Dh), lambda h, i: (0, h))
    dhm, do, dg = pl.pallas_call(
        body, name="mlstm_out_bwd", grid=(H, S // tm),
        in_specs=[hspec(0), hspec(1), pl.BlockSpec((tm, Dh), lambda h, i: (i, OFF_OM // Dh + h)), vec,
                  pl.BlockSpec((tm, Dh), lambda h, i: (i, D_OFF + h))],
        out_specs=[out, out, vec],
        out_shape=[jax.ShapeDtypeStruct((S, M_WIDTH), F32), jax.ShapeDtypeStruct((S, M_WIDTH), BF16),
                   jax.ShapeDtypeStruct((1, M_WIDTH), F32)],
        compiler_params=_params(),
    )(hs, hs, proj, gamma.reshape(1, M_WIDTH), dycat)
    return dhm, do, dg.reshape(M_WIDTH)


def _sum_cast(a2, name):
    _, S, C = a2.shape
    tm = _row_tile(S, 512)

    def body(a_ref, b_ref, o_ref):
        o_ref[...] = (a_ref[...] + b_ref[...]).astype(BF16)

    spec = lambda d: pl.BlockSpec((None, tm, C), lambda i: (d, i, 0))
    return pl.pallas_call(
        body, name=name, grid=(S // tm,),
        in_specs=[spec(0), spec(1)], out_specs=pl.BlockSpec((tm, C), lambda i: (i, 0)),
        out_shape=jax.ShapeDtypeStruct((S, C), BF16),
        compiler_params=_params(),
    )(a2, a2)


def _adamw(w, g, m, v):
    shape = w.shape
    C = shape[-1]
    R = w.size // C
    tm = _elementwise_rows(R, C)

    def body(w_ref, g_ref, m_ref, v_ref, d_ref, mo_ref, vo_ref):
        gv = g_ref[...]
        mn = ADAM_B1 * m_ref[...] + (1.0 - ADAM_B1) * gv
        vn = ADAM_B2 * v_ref[...] + (1.0 - ADAM_B2) * (gv * gv)
        m_hat = mn / (1.0 - ADAM_B1 ** ADAM_STEP)
        v_hat = vn / (1.0 - ADAM_B2 ** ADAM_STEP)
        d_ref[...] = -ADAM_LR * (m_hat / (jnp.sqrt(v_hat) + ADAM_EPS) + ADAM_WD * w_ref[...])
        mo_ref[...] = mn
        vo_ref[...] = vn

    row = pl.BlockSpec((tm, C), lambda i: (i, 0))
    sds = jax.ShapeDtypeStruct((R, C), F32)
    outs = pl.pallas_call(
        body, name="adamw", grid=(R // tm,),
        in_specs=[row] * 4, out_specs=[row] * 3, out_shape=[sds] * 3,
        compiler_params=_params(),
    )(*[t.reshape(R, C) for t in (w, g, m, v)])
    return tuple(o.reshape(shape) for o in outs)


CID_WEIGHT_GATHER, CID_PAIR_SWAP, CID_SCATTER = 0, 1, 2


def _other_chips(x, y):
    return [(1 - x, y), (x, 1 - y), (1 - x, 1 - y)]


def _handshake(peers):
    barrier = pltpu.get_barrier_semaphore()
    for peer in peers:
        pl.semaphore_signal(barrier, inc=1, device_id=peer, device_id_type=MESH)
    pl.semaphore_wait(barrier, len(peers))


def _sequencer_call(body, name, out_type, sem_counts, collective_id, operands):
    return pl.kernel(
        body, name=name, out_type=out_type,
        mesh=plsc.ScalarSubcoreMesh(axis_name="sequencer", num_cores=1),
        scratch_types=[pltpu.SemaphoreType.DMA((n,)) for n in sem_counts],
        compiler_params=pltpu.CompilerParams(collective_id=collective_id),
    )(*operands)


def _comm_weight_gather(groups):
    T = len(groups)
    flat = [(t, sum(p.shape[1] for p in g[:i]), part) for t, g in enumerate(groups) for i, part in enumerate(g)]
    N = len(flat)

    def body(*refs):
        ins, outs = refs[:N], refs[N:N + T]
        send, recv, local = refs[N + T:]
        x, y, c = lax.axis_index("x"), lax.axis_index("y"), lax.axis_index("c")
        me = 2 * x + y
        chips = _other_chips(x, y)
        _handshake([(px, py, c) for px, py in chips])

        def place(i, chip):
            t, off, part = flat[i]
            return outs[t].at[chip, :, pl.ds(off, part.shape[1])]

        mine = []
        for i in range(N):
            cp = pltpu.make_async_copy(ins[i], place(i, me), local.at[i])
            cp.start()
            mine.append(cp)
            for k, (px, py) in enumerate(chips):
                pltpu.make_async_remote_copy(
                    src_ref=ins[i], dst_ref=place(i, me), send_sem=send.at[3 * i + k], recv_sem=recv.at[3 * i + k],
                    device_id=(px, py, c), device_id_type=MESH).start()
        for i in range(N):
            for k, (px, py) in enumerate(chips):
                cp = pltpu.make_async_remote_copy(
                    src_ref=ins[i], dst_ref=place(i, 2 * px + py), send_sem=send.at[3 * i + k],
                    recv_sem=recv.at[3 * i + k], device_id=(px, py, c), device_id_type=MESH)
                cp.wait_send()
                cp.wait_recv()
            mine[i].wait()

    out_type = tuple(jax.ShapeDtypeStruct((4, g[0].shape[0], sum(p.shape[1] for p in g)), g[0].dtype) for g in groups)
    return _sequencer_call(
        body, "comm_weight_gather", out_type, (3 * N, 3 * N, N), CID_WEIGHT_GATHER, [part for _, _, part in flat])


def _comm_to_sibling(xs, other_half=False):
    T = len(xs)

    def body(*refs):
        ins, outs = refs[:T], refs[T:2 * T]
        send, recv = refs[2 * T:]
        x, y, c = lax.axis_index("x"), lax.axis_index("y"), lax.axis_index("c")
        _handshake([(x, y, 1 - c)])
        cps = [pltpu.make_async_remote_copy(
            src_ref=ins[t].at[:, 1 - c] if other_half else ins[t], dst_ref=outs[t], send_sem=send.at[t],
            recv_sem=recv.at[t], device_id=(x, y, 1 - c), device_id_type=MESH) for t in range(T)]
        for cp in cps:
            cp.start()
        for cp in cps:
            cp.wait_send()
            cp.wait_recv()

    shapes = [s.shape[:1] + s.shape[2:] if other_half else s.shape for s in xs]
    return _sequencer_call(
        body, "comm_pair_swap", tuple(jax.ShapeDtypeStruct(sh, s.dtype) for sh, s in zip(shapes, xs)),
        (T, T), CID_PAIR_SWAP, xs)


def _comm_scatter_to_owners(ps):
    T = len(ps)

    def body(*refs):
        ins, outs = refs[:T], refs[T:2 * T]
        send, recv = refs[2 * T:]
        x, y, c = lax.axis_index("x"), lax.axis_index("y"), lax.axis_index("c")
        chips = _other_chips(x, y)
        _handshake([(px, py, c) for px, py in chips])
        cps = []
        for t in range(T):
            for k, (px, py) in enumerate(chips):
                cp = pltpu.make_async_remote_copy(
                    src_ref=ins[t].at[2 * px + py], dst_ref=outs[t].at[k], send_sem=send.at[3 * t + k],
                    recv_sem=recv.at[3 * t + k], device_id=(px, py, c), device_id_type=MESH)
                cp.start()
                cps.append(cp)
        for cp in cps:
            cp.wait_send()
            cp.wait_recv()

    return _sequencer_call(
        body, "comm_scatter_to_owners", tuple(jax.ShapeDtypeStruct((3,) + p.shape[1:], p.dtype) for p in ps),
        (3 * T, 3 * T), CID_SCATTER, ps)


def _comm_all_to_all_small(pack):
    R = pack.shape[0]

    def body(in_ref, out_ref, send, recv, local):
        x, y, c = lax.axis_index("x"), lax.axis_index("y"), lax.axis_index("c")
        me = 4 * x + 2 * y + c
        mine = pltpu.make_async_copy(in_ref, out_ref.at[me], local.at[0])
        mine.start()
        peers = []
        for k in range(1, 8):
            fx, fy, fc = (k >> 2) & 1, (k >> 1) & 1, k & 1
            peers.append((x + fx * (1 - 2 * x), y + fy * (1 - 2 * y), c + fc * (1 - 2 * c)))
        for k, peer in enumerate(peers):
            pltpu.make_async_remote_copy(
                src_ref=in_ref, dst_ref=out_ref.at[me], send_sem=send.at[k], recv_sem=recv.at[k],
                device_id=peer, device_id_type=MESH).start()
        for k, (px, py, pc) in enumerate(peers):
            cp = pltpu.make_async_remote_copy(
                src_ref=in_ref, dst_ref=out_ref.at[4 * px + 2 * py + pc], send_sem=send.at[k], recv_sem=recv.at[k],
                device_id=(px, py, pc), device_id_type=MESH)
            cp.wait_send()
            cp.wait_recv()
        mine.wait()

    return pl.pallas_call(
        body, name="comm_all_to_all_small",
        in_specs=[ANY], out_specs=ANY,
        out_shape=jax.ShapeDtypeStruct((8, R, LANES), F32),
        scratch_shapes=[pltpu.SemaphoreType.DMA((7,)), pltpu.SemaphoreType.DMA((7,)), pltpu.SemaphoreType.DMA((1,))],
    )(pack)


def _tie(*trees):
    return lax.optimization_barrier(trees)


class _ReduceScatter:
    def __init__(self, grads, dh):
        split = [g.reshape(4, 2, g.shape[1] // 2, g.shape[2]) for g in grads]
        self.split, self.dh = _tie(split, dh)
        self.got = _comm_to_sibling(self.split, other_half=True)

    def scatter(self, dh):
        split, got, dh = _tie(self.split, list(self.got), dh)
        pair = [_pair_sum(a, b) for a, b in zip(split, got)]
        self.pair, dh = _tie(pair, dh)
        self.arrived = _comm_scatter_to_owners(self.pair)
        return dh

    def gather(self, dh):
        pair, arrived, dh = _tie(self.pair, list(self.arrived), dh)
        halves = [_chip_sum(p, a) for p, a in zip(pair, arrived)]
        self.halves, dh = _tie(halves, dh)
        self.other = _comm_to_sibling(self.halves)
        return dh

    def settle(self, dh):
        self.other, dh = _tie(list(self.other), dh)
        return dh

    def result(self):
        south = lax.axis_index("c") == 0
        return [jnp.where(south, jnp.concatenate([mine, theirs]), jnp.concatenate([theirs, mine]))
                for mine, theirs in zip(self.halves, self.other)]


def _ffn_fwd(x, g_pre, g_post, wgu, wd):
    S, D = x.shape
    F = wd.shape[1]
    xn = _norm_fwd(x, g_pre)
    gu, h = _ffn_up(xn, wgu)
    wd, h = _tie(wd, h)
    f = _mm(h, wd, "nn", ja="r", jb="r", tm=_row_tile(S, ACT_ROWS), tn=D, tk=F, name="ffn_down")
    x_out = _resid_norm_fwd(x, f, g_post, 0.5)
    return x_out, (x, xn, gu, h, f), wd


def _ffn_bwd(dx, saved, g_pre, g_post, wgu, wd, mid=None, on_grads=None, late=None):
    x, xn, gu, h, f = saved
    S, D = x.shape
    J, F, _ = wd.shape
    tm, tk = _row_tile(S, ACT_ROWS), _row_tile(S, TOKEN_ROWS)
    df, dg_post = _norm_bwd(f, g_post, dx, 0.5, out_dtype=BF16)
    d_wd = _mm(h, df, "tn", ja="b", tm=F, tn=MODEL_COLS, tk=tk, name="ffn_dwd")
    dgu = _ffn_bwd_hidden(df, wd, gu)
    if mid is not None:
        dgu = mid(dgu)
    d_wgu = _mm(xn, dgu, "tn", jb="b", tm=MODEL_COLS, tn=F, tk=tk, name="ffn_dwgu")
    if on_grads is not None:
        dgu = on_grads(d_wgu, d_wd, dgu)
    dxn = _mm(dgu, wgu, "nt", ja="r", jb="r", tm=tm, tn=MODEL_COLS, tk=2 * F, name="ffn_dxn")
    if late is not None:
        dxn = late(dxn)
    dx_in, dg_pre = _norm_bwd(x, g_pre, dxn, 1.0, resid=dx)
    return dx_in, dg_pre, dg_post, d_wgu, d_wd


def _gates_layout(proj, b_gate, nc):
    gt = jnp.transpose(proj[:, OFF_G:OFF_G + N_GATES]).reshape(N_GATES * nc, LANES)
    bias = jnp.repeat(b_gate, nc).reshape(N_GATES * nc, 1)
    return gt, bias


def _mixer_fwd(x, p, cosf, sinf):
    S, D = x.shape
    nc = S // BLK
    hn = _norm_fwd(x, p["g_pre"])
    proj = _mm(hn, p["w_in"], "nn", tm=_row_tile(S, ACT_ROWS), tn=IN_PAD // 5, tk=D, name="in_proj")
    qr, kr, va = _rope_fwd(proj, cosf, sinf)
    y_att = _attn_fwd(qr, kr, va, p["sink"])
    qk = _conv5(proj, p["conv_w"], col0=OFF_QM, act=True, out_dtype=BF16, name="conv_silu")
    gt, bias = _gates_layout(proj, p["b_gate"], nc)
    ig, cum = _gate_prep(gt, bias)
    ig5, b5 = ig.reshape(2, M_HEADS, nc, 1, LANES), cum.reshape(2, M_HEADS, nc, 1, LANES)
    hs, cs, ns, ms = _mlstm_fwd(qk, proj, ig5, b5)
    y_m = _mlstm_out_fwd(hs, proj, p["gamma"])
    ycat = jnp.concatenate([y_att, y_m], axis=-1)
    w_out, ycat = _tie(p["w_out"], ycat)
    p["w_out"] = w_out
    mix = _mm(ycat, w_out, "nn", tm=_row_tile(S, ACT_ROWS), tn=MODEL_COLS, tk=D, name="out_proj")
    x_out = _resid_norm_fwd(x, mix, p["g_post"], 1.0)
    return x_out, (x, hn, proj, qr, kr, va, qk, gt, bias, ig5, b5, hs, cs, ns, ms, ycat, mix)


def _mixer_bwd(dx, saved, p, cosf, sinf, mid=None):
    x, hn, proj, qr, kr, va, qk, gt, bias, ig5, b5, hs, cs, ns, ms, ycat, mix = saved
    S, D = x.shape
    nc = S // BLK
    tm, tk = _row_tile(S, ACT_ROWS), _row_tile(S, TOKEN_ROWS)
    dmix, dg_post = _norm_bwd(mix, p["g_post"], dx, 1.0, out_dtype=BF16)
    d_wout = _mm(ycat, dmix, "tn", tm=MODEL_COLS, tn=MODEL_COLS, tk=tk, name="dw_out")
    dycat = _mm(dmix, p["w_out"], "nt", tm=tm, tn=MODEL_COLS, tk=D, name="d_ycat")
    if mid is not None:
        dycat = mid(dycat)
    dhm, d_om, d_gamma = _mlstm_out_bwd(hs, proj, p["gamma"], dycat)
    dq2, dk2, dv2, di, dfc, dbl = _mlstm_bwd(qk, proj, ig5, b5, cs, ns, ms, dhm)
    d_vm = _sum_cast(dv2, "dv_sum")
    dgt, db = _gate_bwd(gt, bias, di.reshape(-1, LANES), dfc.reshape(-1, LANES), dbl.reshape(-1, LANES), nc)
    dpre, d_conv = _conv_bwd_pre(proj, OFF_QM, p["conv_w"], dq2, dk2)
    d_qkm = _conv5(dpre, p["conv_w"][::-1], col0=0, act=False, out_dtype=BF16, name="conv_bwd_x")
    d_gates = jnp.transpose(dgt.reshape(N_GATES, S)).astype(BF16)
    d_gates = jnp.pad(d_gates, ((0, 0), (0, IN_PAD - IN_WIDTH)))
    dqr, dkr, dva, d_sink = _attn_bwd(qr, kr, va, p["sink"], dycat)
    d_att = _rope_bwd(dqr, dkr, dva, cosf, sinf)
    dproj = jnp.concatenate([d_att, d_qkm, d_vm, d_om, d_gates], axis=-1)
    d_win = _mm(hn, dproj, "tn", tm=MODEL_COLS, tn=IN_PAD // 5, tk=tk, name="dw_in")
    dhn = _mm(dproj, p["w_in"], "nt", tm=tm, tn=MODEL_COLS, tk=IN_PAD // 3, name="d_hn")
    dx_in, dg_pre = _norm_bwd(x, p["g_pre"], dhn, 1.0, resid=dx)
    small = dict(g_pre=dg_pre, g_post=dg_post, gamma=d_gamma, conv=d_conv, b_gate=db[:, 0], sink=d_sink)
    return dx_in, small, d_win, d_wout


def _pad_lanes(v):
    v = v.reshape(-1)
    return jnp.pad(v, (0, (-v.shape[0]) % LANES))


def kernel(x, ffn1_norm_pre, ffn1_norm_post, ffn1_w_gate, ffn1_w_up, ffn1_w_down, mix_norm_pre, mix_norm_post, w_in, b_gate, conv_w, attn_sink, mlstm_norm, w_out, ffn2_norm_pre, ffn2_norm_post, ffn2_w_gate, ffn2_w_up, ffn2_w_down, loss_target, m_ffn1_norm_pre, m_ffn1_norm_post, m_ffn1_w_gate, m_ffn1_w_up, m_ffn1_w_down, m_mix_norm_pre, m_mix_norm_post, m_w_in, m_b_gate, m_conv_w, m_attn_sink, m_mlstm_norm, m_w_out, m_ffn2_norm_pre, m_ffn2_norm_post, m_ffn2_w_gate, m_ffn2_w_up, m_ffn2_w_down, v_ffn1_norm_pre, v_ffn1_norm_post, v_ffn1_w_gate, v_ffn1_w_up, v_ffn1_w_down, v_mix_norm_pre, v_mix_norm_post, v_w_in, v_b_gate, v_conv_w, v_attn_sink, v_mlstm_norm, v_w_out, v_ffn2_norm_pre, v_ffn2_norm_post, v_ffn2_w_gate, v_ffn2_w_up, v_ffn2_w_down):
    names = ["ffn1_norm_pre", "ffn1_norm_post", "ffn1_w_gate", "ffn1_w_up", "ffn1_w_down", "mix_norm_pre",
             "mix_norm_post", "w_in", "b_gate", "conv_w", "attn_sink", "mlstm_norm", "w_out", "ffn2_norm_pre",
             "ffn2_norm_post", "ffn2_w_gate", "ffn2_w_up", "ffn2_w_down"]
    w = dict(zip(names, [ffn1_norm_pre, ffn1_norm_post, ffn1_w_gate, ffn1_w_up, ffn1_w_down, mix_norm_pre,
                         mix_norm_post, w_in, b_gate, conv_w, attn_sink, mlstm_norm, w_out, ffn2_norm_pre,
                         ffn2_norm_post, ffn2_w_gate, ffn2_w_up, ffn2_w_down]))
    mom_m = dict(zip(names, [m_ffn1_norm_pre, m_ffn1_norm_post, m_ffn1_w_gate, m_ffn1_w_up, m_ffn1_w_down,
                             m_mix_norm_pre, m_mix_norm_post, m_w_in, m_b_gate, m_conv_w, m_attn_sink,
                             m_mlstm_norm, m_w_out, m_ffn2_norm_pre, m_ffn2_norm_post, m_ffn2_w_gate,
                             m_ffn2_w_up, m_ffn2_w_down]))
    mom_v = dict(zip(names, [v_ffn1_norm_pre, v_ffn1_norm_post, v_ffn1_w_gate, v_ffn1_w_up, v_ffn1_w_down,
                             v_mix_norm_pre, v_mix_norm_post, v_w_in, v_b_gate, v_conv_w, v_attn_sink,
                             v_mlstm_norm, v_w_out, v_ffn2_norm_pre, v_ffn2_norm_post, v_ffn2_w_gate,
                             v_ffn2_w_up, v_ffn2_w_down]))
    xs = x[0]
    target = loss_target[0]
    S, D = xs.shape
    depth = w_in.shape[0]
    F = ffn1_w_gate.shape[-1]
    in_shard = w_in.shape[-1]
    conv_shard = conv_w.shape[-1]
    chip = 2 * lax.axis_index("x") + lax.axis_index("y")
    cosf, sinf = _rope_tables(S)

    gathered = []
    for l in range(depth):
        half = lambda name: w[name][l].astype(BF16)
        parts = dict(wgu1=[half("ffn1_w_gate"), half("ffn1_w_up")], wd1=[half("ffn1_w_down")],
                     win=[half("w_in")], conv=[jnp.pad(w["conv_w"][l], ((0, 8 - CONV_WIDTH), (0, 0)))],
                     wout=[half("w_out")], wgu2=[half("ffn2_w_gate"), half("ffn2_w_up")], wd2=[half("ffn2_w_down")])
        kernels = ([["wgu1"], ["wd1"], ["win", "conv"], ["wout"], ["wgu2"], ["wd2"]] if l == 0
                   else [["wgu1", "wd1"], ["win", "conv", "wout", "wgu2", "wd2"]])
        got = {}
        for names_k in kernels:
            got.update(zip(names_k, _comm_weight_gather([parts[n] for n in names_k])))
        gathered.append(got)

    def mixer_weights(l, win, wout, convg):
        win_full = jnp.concatenate([win[j] for j in range(4)], axis=-1)
        conv_full = jnp.concatenate([convg[j] for j in range(4)], axis=-1)[:CONV_WIDTH]
        return dict(w_in=jnp.pad(win_full, ((0, 0), (0, IN_PAD - IN_WIDTH))), w_out=wout.reshape(D, D),
                    conv_w=conv_full, g_pre=w["mix_norm_pre"][l], g_post=w["mix_norm_post"][l],
                    b_gate=w["b_gate"][l], sink=w["attn_sink"][l], gamma=w["mlstm_norm"][l])

    h = xs
    saved = []
    layers = []
    for l in range(depth):
        g = gathered[l]
        wgu1, h = _tie(g["wgu1"], h)
        h, s1, wd1 = _ffn_fwd(h, w["ffn1_norm_pre"][l], w["ffn1_norm_post"][l], wgu1, g["wd1"])
        (win, convg), h = _tie((g["win"], g["conv"]), h)
        mix_p = mixer_weights(l, win, g["wout"], convg)
        h, s2 = _mixer_fwd(h, mix_p, cosf, sinf)
        wgu2, h = _tie(g["wgu2"], h)
        h, s3, wd2 = _ffn_fwd(h, w["ffn2_norm_pre"][l], w["ffn2_norm_post"][l], wgu2, g["wd2"])
        layers.append(dict(wgu1=wgu1, wd1=wd1, wgu2=wgu2, wd2=wd2, mix=mix_p))
        saved.append((s1, s2, s3))
    dh, loss_tile = _loss_grad(h, target)

    big = {}
    small_rows = []
    old = None
    for l in reversed(range(depth)):
        p = layers[l]
        s1, s2, s3 = saved[l]
        dh, dg2_pre, dg2_post, d_wgu2, d_wd2 = _ffn_bwd(dh, s3, w["ffn2_norm_pre"][l], w["ffn2_norm_post"][l],
                                                       p["wgu2"], p["wd2"], mid=old["b"].scatter if old else None)
        if old is not None:
            dh = old["a"].settle(old["b"].gather(dh))
        rs_c = _ReduceScatter([d_wgu2, d_wd2], dh)
        dh, sm, d_win, d_wout = _mixer_bwd(rs_c.dh, s2, p["mix"], cosf, sinf, mid=rs_c.scatter)
        dh = rs_c.gather(dh)
        if old is not None:
            dh = old["b"].settle(dh)
        d_win4 = jnp.stack([d_win[:, j * in_shard:(j + 1) * in_shard] for j in range(4)])
        rs_a = _ReduceScatter([d_win4, d_wout.reshape(4, D // 4, D)], dh)
        if l > 0:
            dh, dg1_pre, dg1_post, d_wgu1, d_wd1 = _ffn_bwd(
                rs_a.dh, s1, w["ffn1_norm_pre"][l], w["ffn1_norm_post"][l], p["wgu1"], p["wd1"], mid=rs_a.scatter)
            dh = rs_c.settle(rs_a.gather(dh))
            rs_b = _ReduceScatter([d_wgu1, d_wd1], dh)
            dh = rs_b.dh
        else:
            last = []

            def start_last(d_wgu, d_wd, x):
                last.append(_ReduceScatter([d_wgu, d_wd], x))
                return last[0].dh

            dh, dg1_pre, dg1_post, d_wgu1, d_wd1 = _ffn_bwd(
                rs_a.dh, s1, w["ffn1_norm_pre"][l], w["ffn1_norm_post"][l], p["wgu1"], p["wd1"], mid=rs_a.scatter,
                on_grads=start_last, late=lambda x: last[0].scatter(x))
            dh = rs_c.settle(rs_a.gather(dh))
            rs_b = last[0]
        old = dict(a=rs_a, b=rs_b, c=rs_c)
        big[l] = old
        small_rows.append((l, [dg1_pre, dg1_post, sm["g_pre"], sm["g_post"], dg2_pre, dg2_post, sm["gamma"],
                               sm["conv"], sm["b_gate"], sm["sink"]]))
    big_a = {l: g["a"].result() + g["c"].result() for l, g in big.items()}

    small_rows.sort(key=lambda t: t[0])
    flat = [_pad_lanes(v) for _, vs in small_rows for v in vs] + [loss_tile[0]]
    sizes = [f.shape[0] for f in flat]
    pack = jnp.concatenate(flat)
    pack = jnp.pad(pack, (0, (-pack.shape[0]) % (8 * LANES))).reshape(-1, LANES)
    every = _comm_all_to_all_small(pack)
    total = _add_n([every[i] for i in range(8)], "small_sum").reshape(-1)
    pieces, off = [], 0
    for n in sizes:
        pieces.append(total[off:off + n])
        off += n
    loss = pieces[-1][0]
    per_layer = [pieces[10 * l:10 * l + 10] for l in range(depth)]

    def stack_small(i, shape):
        n = 1
        for s in shape:
            n *= s
        return jnp.stack([per_layer[l][i][:n].reshape(shape) for l in range(depth)])

    conv_full_grad = stack_small(7, (CONV_WIDTH, 4 * conv_shard))
    grads = {
        "ffn1_norm_pre": stack_small(0, (D,)), "ffn1_norm_post": stack_small(1, (D,)),
        "mix_norm_pre": stack_small(2, (D,)), "mix_norm_post": stack_small(3, (D,)),
        "ffn2_norm_pre": stack_small(4, (D,)), "ffn2_norm_post": stack_small(5, (D,)),
        "mlstm_norm": stack_small(6, (M_WIDTH,)),
        "conv_w": lax.dynamic_slice_in_dim(conv_full_grad, chip * conv_shard, conv_shard, 2),
        "b_gate": stack_small(8, (N_GATES,)), "attn_sink": stack_small(9, (ATT_HEADS,)),
    }
    gu2 = jnp.stack([big_a[l][2] for l in range(depth)])
    grads["ffn2_w_gate"], grads["ffn2_w_up"] = gu2[:, :, :F], gu2[:, :, F:]
    grads["ffn2_w_down"] = jnp.stack([big_a[l][3] for l in range(depth)])
    grads["w_in"] = jnp.stack([big_a[l][0] for l in range(depth)])
    grads["w_out"] = jnp.stack([big_a[l][1] for l in range(depth)])

    deltas, new_m, new_v = {}, {}, {}
    last_group = ("ffn1_w_gate", "ffn1_w_up", "ffn1_w_down")
    for n in names:
        if n not in last_group:
            deltas[n], new_m[n], new_v[n] = _adamw(w[n], grads[n], mom_m[n], mom_v[n])
    deltas, new_m, new_v = big[0]["b"].gather((deltas, new_m, new_v))
    big_b = {l: g["b"].result() for l, g in big.items()}
    gu1 = jnp.stack([big_b[l][0] for l in range(depth)])
    grads["ffn1_w_gate"], grads["ffn1_w_up"] = gu1[:, :, :F], gu1[:, :, F:]
    grads["ffn1_w_down"] = jnp.stack([big_b[l][1] for l in range(depth)])
    for n in last_group:
        deltas[n], new_m[n], new_v[n] = _adamw(w[n], grads[n], mom_m[n], mom_v[n])
    grad_x = dh[None]
    return (loss, grad_x, *[grads[n] for n in names], *[deltas[n] for n in names],
            *[new_m[n] for n in names], *[new_v[n] for n in names])
```

```python
import jax
import jax.numpy as jnp
from jax import lax
from jax.experimental import pallas as pl
from jax.experimental.pallas import tpu as pltpu
from jax.experimental.pallas import tpu_sc as plsc

F32 = jnp.float32
BF16 = jnp.bfloat16
MESH = pl.DeviceIdType.MESH
ANY = pl.BlockSpec(memory_space=pl.ANY)

VMEM_LIMIT_BYTES = 56 * 1024 * 1024
LANES = 128

EPS = 1e-6
ATT_HEADS = 8
ATT_KV_HEADS = 2
ATT_GROUP = ATT_HEADS // ATT_KV_HEADS
ATT_HEAD_DIM = 128
ATT_WIDTH = ATT_HEADS * ATT_HEAD_DIM
KV_WIDTH = ATT_KV_HEADS * ATT_HEAD_DIM
BLK = 128
M_HEADS = 4
M_HEAD_DIM = 256
M_WIDTH = M_HEADS * M_HEAD_DIM
CONV_WIDTH = 5
CONV_HALO = 8
ROPE_THETA = 10000.0
N_GATES = 4 * M_HEADS
OFF_QA, OFF_KA, OFF_VA = 0, ATT_WIDTH, ATT_WIDTH + KV_WIDTH
OFF_QM = ATT_WIDTH + 2 * KV_WIDTH
OFF_KM = OFF_QM + M_WIDTH
OFF_VM = OFF_KM + M_WIDTH
OFF_OM = OFF_VM + M_WIDTH
OFF_G = OFF_OM + M_WIDTH
IN_WIDTH = OFF_G + N_GATES
IN_PAD = OFF_G + LANES
NEG_BIG = -1e30

ADAM_LR, ADAM_B1, ADAM_B2, ADAM_EPS, ADAM_WD, ADAM_STEP = 0.001, 0.9, 0.999, 1e-08, 0.01, 10


def _params(**kw):
    return pltpu.CompilerParams(vmem_limit_bytes=VMEM_LIMIT_BYTES, **kw)


def _dot(a, b):
    return lax.dot_general(a, b, (((1,), (0,)), ((), ())), preferred_element_type=F32)


def _dot_nt(a, b):
    return lax.dot_general(a, b, (((1,), (1,)), ((), ())), preferred_element_type=F32)


def _dot_tn(a, b):
    return lax.dot_general(a, b, (((0,), (0,)), ((), ())), preferred_element_type=F32)


def _sigmoid(x):
    return 0.5 * jnp.tanh(0.5 * x) + 0.5


def _eye_mask(n):
    r = lax.broadcasted_iota(jnp.int32, (n, n), 0)
    c = lax.broadcasted_iota(jnp.int32, (n, n), 1)
    return r == c


def _row_to_col(row, eye):
    n = eye.shape[0]
    return jnp.sum(jnp.where(eye, jnp.broadcast_to(row, (n, n)), 0.0), axis=1, keepdims=True)


def _col_to_row(col, eye):
    n = eye.shape[0]
    return jnp.sum(jnp.where(eye, jnp.broadcast_to(col, (n, n)), 0.0), axis=0, keepdims=True)


def _mm(a, b, kind, *, tm, tn, tk, name, out_dtype=F32, ja=None, jb=None):
    a2, b2 = a.shape[-2:], b.shape[-2:]
    if kind == "nn":
        (M, K), (_, N) = a2, b2
    elif kind == "nt":
        (M, K), (N, _) = a2, b2
    else:
        (K, M), (_, N) = a2, b2
    J = a.shape[0] if ja else (b.shape[0] if jb else 1)
    batch = "b" in (ja, jb)
    red = "r" in (ja, jb)
    nk = K // tk
    nr = nk * (J if red else 1)
    grid = ((J if batch else 1), M // tm, N // tn, nr)

    def lead(mode, g, r):
        return g if mode == "b" else r // nk

    def a_map(g, i, n, r):
        kk = r % nk
        idx = (i, kk) if kind != "tn" else (kk, i)
        return idx if ja is None else (lead(ja, g, r),) + idx

    def b_map(g, i, n, r):
        kk = r % nk
        idx = (kk, n) if kind != "nt" else (n, kk)
        return idx if jb is None else (lead(jb, g, r),) + idx

    def o_map(g, i, n, r):
        return (g, i, n) if batch else (i, n)

    a_blk = (tm, tk) if kind != "tn" else (tk, tm)
    b_blk = (tk, tn) if kind != "nt" else (tn, tk)
    dot = {"nn": _dot, "nt": _dot_nt, "tn": _dot_tn}[kind]

    def body(a_ref, b_ref, o_ref, *scratch):
        part = dot(a_ref[...], b_ref[...])
        if nr == 1:
            o_ref[...] = part.astype(out_dtype)
        else:
            acc = scratch[0]
            r = pl.program_id(3)

            @pl.when(r == 0)
            def _():
                acc[...] = part

            @pl.when(r > 0)
            def _():
                acc[...] += part

            @pl.when(r == nr - 1)
            def _():
                o_ref[...] = acc[...].astype(out_dtype)

    return pl.pallas_call(
        body, name=name, grid=grid,
        in_specs=[pl.BlockSpec(a_blk if ja is None else (None,) + a_blk, a_map),
                  pl.BlockSpec(b_blk if jb is None else (None,) + b_blk, b_map)],
        out_specs=pl.BlockSpec((None, tm, tn) if batch else (tm, tn), o_map),
        out_shape=jax.ShapeDtypeStruct((J, M, N) if batch else (M, N), out_dtype),
        scratch_shapes=[pltpu.VMEM((tm, tn), F32)] if nr > 1 else [],
        compiler_params=_params(),
    )(a, b)


def _row_tile(S, want):
    return min(S, want)


ELEMENTWISE_TILE_BYTES = 1 << 20
ACT_ROWS = 1024
TOKEN_ROWS = 2048
MODEL_COLS = 1024


def _elementwise_rows(R, C):
    for cand in (1024, 512, 256, 128, 64, 32, 16, 8):
        if R % cand == 0 and R > cand and cand * C * 4 <= ELEMENTWISE_TILE_BYTES:
            return cand
    return R if R * C * 4 <= ELEMENTWISE_TILE_BYTES or R % 8 else 8


def _ffn_up(xn, wgu):
    S, D = xn.shape
    J, _, F2 = wgu.shape
    F = F2 // 2
    tm = _row_tile(S, 256)

    def body(x_ref, w_ref, gu_ref, h_ref):
        gu = _dot(x_ref[...], w_ref[...])
        g, u = gu[:, :F], gu[:, F:]
        gu_ref[...] = gu.astype(BF16)
        h_ref[...] = (g * _sigmoid(g) * u).astype(BF16)

    return pl.pallas_call(
        body, name="ffn_up", grid=(J, S // tm),
        in_specs=[pl.BlockSpec((tm, D), lambda j, i: (i, 0)),
                  pl.BlockSpec((None, D, F2), lambda j, i: (j, 0, 0))],
        out_specs=[pl.BlockSpec((None, tm, F2), lambda j, i: (j, i, 0)),
                   pl.BlockSpec((None, tm, F), lambda j, i: (j, i, 0))],
        out_shape=[jax.ShapeDtypeStruct((J, S, F2), BF16), jax.ShapeDtypeStruct((J, S, F), BF16)],
        compiler_params=_params(),
    )(xn, wgu)


def _ffn_bwd_hidden(df, wd, gu):
    S, D = df.shape
    J, F, _ = wd.shape
    F2 = 2 * F
    tm = _row_tile(S, 512)

    def body(df_ref, w_ref, gu_ref, o_ref):
        dh = _dot_nt(df_ref[...], w_ref[...])
        g = gu_ref[:, :F].astype(F32)
        u = gu_ref[:, F:].astype(F32)
        sg = _sigmoid(g)
        o_ref[:, :F] = (dh * u * (sg * (1.0 + g * (1.0 - sg)))).astype(BF16)
        o_ref[:, F:] = (dh * (g * sg)).astype(BF16)

    return pl.pallas_call(
        body, name="ffn_bwd_hidden", grid=(J, S // tm),
        in_specs=[pl.BlockSpec((tm, D), lambda j, i: (i, 0)),
                  pl.BlockSpec((None, F, D), lambda j, i: (j, 0, 0)),
                  pl.BlockSpec((None, tm, F2), lambda j, i: (j, i, 0))],
        out_specs=pl.BlockSpec((None, tm, F2), lambda j, i: (j, i, 0)),
        out_shape=jax.ShapeDtypeStruct((J, S, F2), BF16),
        compiler_params=_params(),
    )(df, wd, gu)


def _norm_fwd(x, g):
    S, D = x.shape
    tm = _row_tile(S, 512)

    def body(x_ref, g_ref, o_ref):
        xv = x_ref[...]
        r = lax.rsqrt(jnp.mean(xv * xv, axis=-1, keepdims=True) + EPS)
        o_ref[...] = (xv * r * g_ref[...]).astype(BF16)

    return pl.pallas_call(
        body, name="norm_fwd", grid=(S // tm,),
        in_specs=[pl.BlockSpec((tm, D), lambda i: (i, 0)), pl.BlockSpec((1, D), lambda i: (0, 0))],
        out_specs=pl.BlockSpec((tm, D), lambda i: (i, 0)),
        out_shape=jax.ShapeDtypeStruct((S, D), BF16),
        compiler_params=_params(),
    )(x, g.reshape(1, D))


def _resid_norm_fwd(x, f, g, alpha):
    S, D = x.shape
    tm = _row_tile(S, 512)

    def body(x_ref, f_ref, g_ref, o_ref):
        fv = f_ref[...]
        r = lax.rsqrt(jnp.mean(fv * fv, axis=-1, keepdims=True) + EPS)
        o_ref[...] = x_ref[...] + alpha * (fv * r * g_ref[...])

    return pl.pallas_call(
        body, name="resid_norm_fwd", grid=(S // tm,),
        in_specs=[pl.BlockSpec((tm, D), lambda i: (i, 0)), pl.BlockSpec((tm, D), lambda i: (i, 0)),
                  pl.BlockSpec((1, D), lambda i: (0, 0))],
        out_specs=pl.BlockSpec((tm, D), lambda i: (i, 0)),
        out_shape=jax.ShapeDtypeStruct((S, D), F32),
        compiler_params=_params(),
    )(x, f, g.reshape(1, D))


def _norm_bwd(x, g, dy, alpha, resid=None, out_dtype=F32):
    S, D = x.shape
    tm = _row_tile(S, 256)
    has_resid = resid is not None

    def body(*refs):
        x_ref, g_ref, dy_ref = refs[:3]
        res_ref = refs[3] if has_resid else None
        dx_ref, dg_ref = refs[-2:]
        xv = x_ref[...]
        r = lax.rsqrt(jnp.mean(xv * xv, axis=-1, keepdims=True) + EPS)
        xh = xv * r
        dyv = dy_ref[...].astype(F32) * alpha
        gdy = dyv * g_ref[...]
        dx = r * (gdy - xh * jnp.mean(xh * gdy, axis=-1, keepdims=True))
        if has_resid:
            dx = dx + res_ref[...]
        dx_ref[...] = dx.astype(out_dtype)
        part = jnp.sum(dyv * xh, axis=0, keepdims=True)

        @pl.when(pl.program_id(0) == 0)
        def _():
            dg_ref[...] = part

        @pl.when(pl.program_id(0) > 0)
        def _():
            dg_ref[...] += part

    row = pl.BlockSpec((tm, D), lambda i: (i, 0))
    vec = pl.BlockSpec((1, D), lambda i: (0, 0))
    ins = [x, g.reshape(1, D), dy] + ([resid] if has_resid else [])
    dx, dg = pl.pallas_call(
        body, name="norm_bwd_res" if has_resid else "norm_bwd", grid=(S // tm,),
        in_specs=[row, vec, row] + ([row] if has_resid else []),
        out_specs=[row, vec],
        out_shape=[jax.ShapeDtypeStruct((S, D), out_dtype), jax.ShapeDtypeStruct((1, D), F32)],
        compiler_params=_params(),
    )(*ins)
    return dx, dg.reshape(D)


def _loss_grad(y, target):
    S, D = y.shape
    tm = _row_tile(S, 512)

    def body(y_ref, t_ref, dy_ref, l_ref):
        err = y_ref[...] - t_ref[...]
        dy_ref[...] = err * (1.0 / D)
        part = jnp.sum(jnp.sum(err * err, axis=-1, keepdims=True) * (0.5 / D), axis=0, keepdims=True)
        part = jnp.broadcast_to(part, (8, LANES))

        @pl.when(pl.program_id(0) == 0)
        def _():
            l_ref[...] = part

        @pl.when(pl.program_id(0) > 0)
        def _():
            l_ref[...] += part

    row = pl.BlockSpec((tm, D), lambda i: (i, 0))
    return pl.pallas_call(
        body, name="loss_grad", grid=(S // tm,),
        in_specs=[row, row],
        out_specs=[row, pl.BlockSpec((8, LANES), lambda i: (0, 0))],
        out_shape=[jax.ShapeDtypeStruct((S, D), F32), jax.ShapeDtypeStruct((8, LANES), F32)],
        compiler_params=_params(),
    )(y, target)


def _add_n(xs, name):
    shape = xs[0].shape
    C = shape[-1]
    R = 1
    for s in shape[:-1]:
        R *= s
    tm = _elementwise_rows(R, C)

    def body(*refs):
        acc = refs[0][...]
        for r in refs[1:-1]:
            acc = acc + r[...]
        refs[-1][...] = acc

    row = pl.BlockSpec((tm, C), lambda i: (i, 0))
    out = pl.pallas_call(
        body, name=name, grid=(R // tm,),
        in_specs=[row] * len(xs), out_specs=row,
        out_shape=jax.ShapeDtypeStruct((R, C), F32),
        compiler_params=_params(),
    )(*[x.reshape(R, C) for x in xs])
    return out.reshape(shape)


def _pair_sum(split, got):
    J, _, r, C = split.shape
    tm = _elementwise_rows(r, C)
    core = lax.axis_index("c").astype(jnp.int32).reshape(1)

    def body(core_ref, a_ref, b_ref, o_ref):
        o_ref[...] = a_ref[...] + b_ref[...]

    row = pl.BlockSpec((None, tm, C), lambda j, i, core_ref: (j, i, 0))
    return pl.pallas_call(
        body, name="pair_sum",
        grid_spec=pltpu.PrefetchScalarGridSpec(
            num_scalar_prefetch=1, grid=(J, r // tm),
            in_specs=[pl.BlockSpec((None, None, tm, C), lambda j, i, core_ref: (j, core_ref[0], i, 0)), row],
            out_specs=row),
        out_shape=jax.ShapeDtypeStruct((J, r, C), F32),
        compiler_params=_params(),
    )(core, split, got)


def _chip_sum(pair, arrived):
    _, r, C = pair.shape
    tm = _elementwise_rows(r, C)
    chip = (2 * lax.axis_index("x") + lax.axis_index("y")).astype(jnp.int32).reshape(1)

    def body(chip_ref, own_ref, a_ref, b_ref, c_ref, o_ref):
        o_ref[...] = own_ref[...] + a_ref[...] + b_ref[...] + c_ref[...]

    part = lambda k: pl.BlockSpec((None, tm, C), lambda i, chip_ref: (k, i, 0))
    return pl.pallas_call(
        body, name="chip_sum",
        grid_spec=pltpu.PrefetchScalarGridSpec(
            num_scalar_prefetch=1, grid=(r // tm,),
            in_specs=[pl.BlockSpec((None, tm, C), lambda i, chip_ref: (chip_ref[0], i, 0)), part(0), part(1), part(2)],
            out_specs=pl.BlockSpec((tm, C), lambda i, chip_ref: (i, 0))),
        out_shape=jax.ShapeDtypeStruct((r, C), F32),
        compiler_params=_params(),
    )(chip, pair, arrived, arrived, arrived)


def _rope_tables(S):
    half = ATT_HEAD_DIM // 2
    inv_freq = ROPE_THETA ** (-jnp.arange(half, dtype=F32) / half)
    ang = jnp.arange(S, dtype=F32)[:, None] * inv_freq[None, :]
    cos, sin = jnp.cos(ang), jnp.sin(ang)
    return jnp.concatenate([cos, cos], axis=-1), jnp.concatenate([-sin, sin], axis=-1)


def _rotate(x, cosf, sinf):
    return x * cosf + pltpu.roll(x, ATT_HEAD_DIM // 2, 1) * sinf


def _rope_fwd(proj, cosf, sinf):
    S = proj.shape[0]
    tm = _row_tile(S, 512)

    def body(q_ref, k_ref, v_ref, c_ref, s_ref, qo_ref, ko_ref, vo_ref):
        c, s = c_ref[...], s_ref[...]
        for h in range(ATT_HEADS):
            sl = slice(h * ATT_HEAD_DIM, (h + 1) * ATT_HEAD_DIM)
            qo_ref[:, sl] = _rotate(q_ref[:, sl], c, s).astype(BF16)
        for h in range(ATT_KV_HEADS):
            sl = slice(h * ATT_HEAD_DIM, (h + 1) * ATT_HEAD_DIM)
            ko_ref[:, sl] = _rotate(k_ref[:, sl], c, s).astype(BF16)
        vo_ref[...] = v_ref[...].astype(BF16)

    tab = pl.BlockSpec((tm, ATT_HEAD_DIM), lambda i: (i, 0))
    return pl.pallas_call(
        body, name="rope_fwd", grid=(S // tm,),
        in_specs=[pl.BlockSpec((tm, ATT_WIDTH), lambda i: (i, 0)),
                  pl.BlockSpec((tm, KV_WIDTH), lambda i: (i, OFF_KA // KV_WIDTH)),
                  pl.BlockSpec((tm, KV_WIDTH), lambda i: (i, OFF_VA // KV_WIDTH)), tab, tab],
        out_specs=[pl.BlockSpec((tm, ATT_WIDTH), lambda i: (i, 0)),
                   pl.BlockSpec((tm, KV_WIDTH), lambda i: (i, 0)),
                   pl.BlockSpec((tm, KV_WIDTH), lambda i: (i, 0))],
        out_shape=[jax.ShapeDtypeStruct((S, ATT_WIDTH), BF16), jax.ShapeDtypeStruct((S, KV_WIDTH), BF16),
                   jax.ShapeDtypeStruct((S, KV_WIDTH), BF16)],
        compiler_params=_params(),
    )(proj, proj, proj, cosf, sinf)


def _rope_bwd(dq, dk, dv, cosf, sinf):
    S = dq.shape[0]
    tm = _row_tile(S, 512)
    W = ATT_WIDTH + 2 * KV_WIDTH

    def body(q_ref, k_ref, v_ref, c_ref, s_ref, o_ref):
        c, s = c_ref[...], -s_ref[...]
        for h in range(ATT_HEADS):
            sl = slice(h * ATT_HEAD_DIM, (h + 1) * ATT_HEAD_DIM)
            o_ref[:, sl] = _rotate(q_ref[:, sl], c, s).astype(BF16)
        for h in range(ATT_KV_HEADS):
            sl = slice(h * ATT_HEAD_DIM, (h + 1) * ATT_HEAD_DIM)
            o_ref[:, ATT_WIDTH + h * ATT_HEAD_DIM:ATT_WIDTH + (h + 1) * ATT_HEAD_DIM] = (
                _rotate(k_ref[:, sl], c, s).astype(BF16))
        o_ref[:, ATT_WIDTH + KV_WIDTH:] = v_ref[...].astype(BF16)

    tab = pl.BlockSpec((tm, ATT_HEAD_DIM), lambda i: (i, 0))
    return pl.pallas_call(
        body, name="rope_bwd", grid=(S // tm,),
        in_specs=[pl.BlockSpec((tm, ATT_WIDTH), lambda i: (i, 0)),
                  pl.BlockSpec((tm, KV_WIDTH), lambda i: (i, 0)),
                  pl.BlockSpec((tm, KV_WIDTH), lambda i: (i, 0)), tab, tab],
        out_specs=pl.BlockSpec((tm, W), lambda i: (i, 0)),
        out_shape=jax.ShapeDtypeStruct((S, W), BF16),
        compiler_params=_params(),
    )(dq, dk, dv, cosf, sinf)


def _attn_probs(q_ref, k_refs, sink_ref, kh, n, nb):
    G, L, Dh = ATT_GROUP, BLK, ATT_HEAD_DIM
    q4 = jnp.concatenate([q_ref[:, (kh * G + g) * Dh:(kh * G + g + 1) * Dh] for g in range(G)], axis=0)
    kcat = jnp.concatenate([r[:, kh * Dh:(kh + 1) * Dh] for r in k_refs], axis=0)
    s = _dot_nt(q4, kcat) * (Dh ** -0.5)
    row = lax.broadcasted_iota(jnp.int32, (G * L, 3 * L), 0) % L
    col = lax.broadcasted_iota(jnp.int32, (G * L, 3 * L), 1)
    kpos = (n - 1) * L + col
    mask = (jnp.abs(col - L - row) <= L) & (kpos >= 0) & (kpos < nb * L)
    s = jnp.where(mask, s, -jnp.inf)
    sink = jnp.concatenate([jnp.broadcast_to(sink_ref[kh, :, g:g + 1], (L, 1)) for g in range(G)], axis=0)
    m = jnp.maximum(jnp.max(s, axis=-1, keepdims=True), sink)
    p = jnp.exp(s - m)
    es = jnp.exp(sink - m)
    inv = 1.0 / (jnp.sum(p, axis=-1, keepdims=True) + es)
    return q4, kcat, p * inv, es * inv


def _kv_specs(nb):
    return [pl.BlockSpec((BLK, KV_WIDTH), lambda n: (jnp.maximum(n - 1, 0), 0)),
            pl.BlockSpec((BLK, KV_WIDTH), lambda n: (n, 0)),
            pl.BlockSpec((BLK, KV_WIDTH), lambda n: (jnp.minimum(n + 1, nb - 1), 0))]


def _attn_fwd(qr, kr, va, sink):
    S = qr.shape[0]
    nb = S // BLK
    G, Dh = ATT_GROUP, ATT_HEAD_DIM

    def body(q_ref, k0, k1, k2, v0, v1, v2, sink_ref, o_ref):
        n = pl.program_id(0)
        for kh in range(ATT_KV_HEADS):
            _, _, probs, _ = _attn_probs(q_ref, (k0, k1, k2), sink_ref, kh, n, nb)
            vcat = jnp.concatenate([r[:, kh * Dh:(kh + 1) * Dh] for r in (v0, v1, v2)], axis=0)
            out = _dot(probs.astype(BF16), vcat)
            for g in range(G):
                o_ref[:, (kh * G + g) * Dh:(kh * G + g + 1) * Dh] = out[g * BLK:(g + 1) * BLK, :].astype(BF16)

    qspec = pl.BlockSpec((BLK, ATT_WIDTH), lambda n: (n, 0))
    return pl.pallas_call(
        body, name="attn_fwd", grid=(nb,),
        in_specs=[qspec] + _kv_specs(nb) + _kv_specs(nb)
        + [pl.BlockSpec((ATT_KV_HEADS, 1, G), lambda n: (0, 0, 0))],
        out_specs=qspec,
        out_shape=jax.ShapeDtypeStruct((S, ATT_WIDTH), BF16),
        compiler_params=_params(),
    )(qr, kr, kr, kr, va, va, va, sink.reshape(ATT_KV_HEADS, 1, G))


def _attn_bwd(qr, kr, va, sink, dycat):
    S = qr.shape[0]
    nb = S // BLK
    G, L, Dh = ATT_GROUP, BLK, ATT_HEAD_DIM
    SP = S + 2 * L

    def body(q_ref, k0, k1, k2, v0, v1, v2, sink_ref, do_ref, dq_ref, dk_ref, dv_ref, ds_ref):
        n = pl.program_id(0)

        @pl.when(n == 0)
        def _():
            dk_ref[...] = jnp.zeros_like(dk_ref)
            dv_ref[...] = jnp.zeros_like(dv_ref)
            ds_ref[...] = jnp.zeros_like(ds_ref)

        rows = pl.ds(pl.multiple_of(n * L, L), 3 * L)
        lane = lax.broadcasted_iota(jnp.int32, (8, LANES), 1)
        for kh in range(ATT_KV_HEADS):
            q4, kcat, probs, psink = _attn_probs(q_ref, (k0, k1, k2), sink_ref, kh, n, nb)
            vcat = jnp.concatenate([r[:, kh * Dh:(kh + 1) * Dh] for r in (v0, v1, v2)], axis=0)
            do4 = jnp.concatenate([do_ref[:, (kh * G + g) * Dh:(kh * G + g + 1) * Dh] for g in range(G)], axis=0)
            pb = probs.astype(BF16)
            dob = do4.astype(BF16)
            out = _dot(pb, vcat)
            delta = jnp.sum(do4 * out, axis=-1, keepdims=True)
            dp = _dot_nt(dob, vcat)
            dsc = (probs * (dp - delta) * (Dh ** -0.5)).astype(BF16)
            dq4 = _dot(dsc, kcat)
            for g in range(G):
                dq_ref[:, (kh * G + g) * Dh:(kh * G + g + 1) * Dh] = dq4[g * L:(g + 1) * L, :]
            dk_ref[rows, kh * Dh:(kh + 1) * Dh] += _dot_tn(dsc, q4)
            dv_ref[rows, kh * Dh:(kh + 1) * Dh] += _dot_tn(pb, dob)
            dsink = jnp.zeros((8, LANES), F32)
            for g in range(G):
                val = -jnp.sum(psink[g * L:(g + 1) * L] * delta[g * L:(g + 1) * L], axis=0, keepdims=True)
                dsink = dsink + jnp.where(lane == g, jnp.broadcast_to(val, (8, LANES)), 0.0)
            ds_ref[kh] += dsink

    qspec = pl.BlockSpec((L, ATT_WIDTH), lambda n: (n, 0))
    accspec = pl.BlockSpec((SP, KV_WIDTH), lambda n: (0, 0))
    dq, dkp, dvp, dsink = pl.pallas_call(
        body, name="attn_bwd", grid=(nb,),
        in_specs=[qspec] + _kv_specs(nb) + _kv_specs(nb)
        + [pl.BlockSpec((ATT_KV_HEADS, 1, G), lambda n: (0, 0, 0)), qspec],
        out_specs=[qspec, accspec, accspec, pl.BlockSpec((ATT_KV_HEADS, 8, LANES), lambda n: (0, 0, 0))],
        out_shape=[jax.ShapeDtypeStruct((S, ATT_WIDTH), F32), jax.ShapeDtypeStruct((SP, KV_WIDTH), F32),
                   jax.ShapeDtypeStruct((SP, KV_WIDTH), F32),
                   jax.ShapeDtypeStruct((ATT_KV_HEADS, 8, LANES), F32)],
        compiler_params=_params(),
    )(qr, kr, kr, kr, va, va, va, sink.reshape(ATT_KV_HEADS, 1, G), dycat)
    return dq, dkp[L:L + S], dvp[L:L + S], dsink[:, 0, :G].reshape(ATT_HEADS)


CONV_COLS = 256
CONV_ROWS = 512


def _conv_shifts(xs, rows):
    total = xs.shape[0]
    out = []
    for j in range(CONV_WIDTH):
        shift = (CONV_WIDTH // 2 - j) % total
        out.append((pltpu.roll(xs, shift, 0) if shift else xs)[CONV_HALO:CONV_HALO + rows, :])
    return out


def _conv_taps(xs, w_ref, rows):
    acc = None
    for j, xj in enumerate(_conv_shifts(xs, rows)):
        term = xj * w_ref[j:j + 1, :]
        acc = term if acc is None else acc + term
    return acc


def _conv_window(x_ref, i, R, nrow):
    S = x_ref.shape[0]
    r0 = pl.multiple_of(i * R, R)
    top = x_ref[pl.ds(pl.multiple_of(jnp.maximum(r0 - CONV_HALO, 0), CONV_HALO), CONV_HALO), :]
    bot = x_ref[pl.ds(pl.multiple_of(jnp.minimum(r0 + R, S - CONV_HALO), CONV_HALO), CONV_HALO), :]
    top = jnp.where(i > 0, top, 0.0)
    bot = jnp.where(i < nrow - 1, bot, 0.0)
    return jnp.concatenate([top, x_ref[pl.ds(r0, R), :], bot], axis=0)


def _conv5(x, w, *, col0, act, out_dtype, name):
    S = x.shape[0]
    C = w.shape[1]
    R = _row_tile(S, CONV_ROWS)
    tc = CONV_COLS
    half_blocks = (C // 2) // tc
    nrow = S // R

    def body(x_ref, w_ref, o_ref):
        scale = jnp.where(pl.program_id(0) >= half_blocks, M_HEAD_DIM ** -0.5, 1.0)
        y = _conv_taps(_conv_window(x_ref, pl.program_id(1), R, nrow), w_ref, R)
        if act:
            y = y * _sigmoid(y) * scale
        o_ref[...] = y.astype(out_dtype)

    return pl.pallas_call(
        body, name=name, grid=(C // tc, nrow),
        in_specs=[pl.BlockSpec((S, tc), lambda c, i: (0, col0 // tc + c)),
                  pl.BlockSpec((CONV_WIDTH, tc), lambda c, i: (0, c))],
        out_specs=pl.BlockSpec((R, tc), lambda c, i: (i, c)),
        out_shape=jax.ShapeDtypeStruct((S, C), out_dtype),
        compiler_params=_params(),
    )(x, w)


def _conv_bwd_pre(x, col0, w, dq2, dk2):
    S = x.shape[0]
    C = w.shape[1]
    R = _row_tile(S, CONV_ROWS)
    tc = CONV_COLS
    half_blocks = (C // 2) // tc
    nrow = S // R

    def body(x_ref, w_ref, dqa_ref, dqb_ref, dka_ref, dkb_ref, o_ref, dw_ref, acc):
        is_k = pl.program_id(0) >= half_blocks
        i = pl.program_id(1)
        scale = jnp.where(is_k, M_HEAD_DIM ** -0.5, 1.0)

        @pl.when(i == 0)
        def _():
            acc[...] = jnp.zeros_like(acc)

        shifted = _conv_shifts(_conv_window(x_ref, i, R, nrow), R)
        y = shifted[0] * w_ref[0:1, :]
        for j in range(1, CONV_WIDTH):
            y = y + shifted[j] * w_ref[j:j + 1, :]
        sg = _sigmoid(y)
        dqv = dqa_ref[...] + dqb_ref[...]
        dkv = dka_ref[...] + dkb_ref[...]
        dpre = jnp.where(is_k, dkv, dqv) * scale * (sg * (1.0 + y * (1.0 - sg)))
        o_ref[...] = dpre
        for j in range(CONV_WIDTH):
            acc[j:j + 1, :] += jnp.sum(dpre * shifted[j], axis=0, keepdims=True)

        @pl.when(i == nrow - 1)
        def _():
            dw_ref[...] = acc[0:CONV_WIDTH, :]

    nqb = (C // 2) // tc
    qmap = lambda d: (lambda c, i: (d, i, jnp.minimum(c, nqb - 1)))
    kmap = lambda d: (lambda c, i: (d, i, jnp.maximum(c - nqb, 0)))
    gspec = lambda m: pl.BlockSpec((None, R, tc), m)
    return pl.pallas_call(
        body, name="conv_bwd_pre", grid=(C // tc, nrow),
        in_specs=[pl.BlockSpec((S, tc), lambda c, i: (0, col0 // tc + c)),
                  pl.BlockSpec((CONV_WIDTH, tc), lambda c, i: (0, c)),
                  gspec(qmap(0)), gspec(qmap(1)), gspec(kmap(0)), gspec(kmap(1))],
        out_specs=[pl.BlockSpec((R, tc), lambda c, i: (i, c)), pl.BlockSpec((CONV_WIDTH, tc), lambda c, i: (0, c))],
        out_shape=[jax.ShapeDtypeStruct((S, C), F32), jax.ShapeDtypeStruct((CONV_WIDTH, C), F32)],
        scratch_shapes=[pltpu.VMEM((8, tc), F32)],
        compiler_params=_params(),
    )(x, w, dq2, dq2, dk2, dk2)


def _lane_cumsum(x, reverse):
    lane = lax.broadcasted_iota(jnp.int32, x.shape, 1)
    sh = 1
    while sh < LANES:
        if reverse:
            x = x + jnp.where(lane < LANES - sh, pltpu.roll(x, LANES - sh, 1), 0.0)
        else:
            x = x + jnp.where(lane >= sh, pltpu.roll(x, sh, 1), 0.0)
        sh *= 2
    return x


def _gate_prep(gates_t, bias):
    R = gates_t.shape[0]
    half, quarter = R // 2, R // 4

    def body(g_ref, b_ref, ig_ref, cum_ref):
        ig_ref[...] = g_ref[0:half, :] + b_ref[0:half, :]
        fg = g_ref[half:R, :] + b_ref[half:R, :]
        lf = jnp.minimum(fg, 0.0) - jnp.log(1.0 + jnp.exp(-jnp.abs(fg)))
        cum_ref[0:quarter, :] = _lane_cumsum(lf[0:quarter, :], False)
        cum_ref[quarter:half, :] = _lane_cumsum(lf[quarter:half, :], True)

    return pl.pallas_call(
        body, name="gate_prep",
        out_shape=[jax.ShapeDtypeStruct((half, LANES), F32), jax.ShapeDtypeStruct((half, LANES), F32)],
        compiler_params=_params(),
    )(gates_t, bias)


def _gate_bwd(gates_t, bias, di, dfc, dbl, nc):
    R = gates_t.shape[0]
    half, quarter = R // 2, R // 4

    def body(g_ref, b_ref, di_ref, df_ref, dbl_ref, dg_ref, db_ref):
        dg_ref[0:half, :] = di_ref[...]
        dfv = df_ref[...]
        within = jnp.concatenate([_lane_cumsum(dfv[0:quarter, :], True),
                                  _lane_cumsum(dfv[quarter:half, :], False)], axis=0)
        fg = g_ref[half:R, :] + b_ref[half:R, :]
        dg_ref[half:R, :] = (within + dbl_ref[...]) / (1.0 + jnp.exp(fg))
        rows = jnp.broadcast_to(jnp.sum(dg_ref[...], axis=-1, keepdims=True), (R, LANES))
        gr = lax.broadcasted_iota(jnp.int32, (N_GATES, R), 0)
        gc = lax.broadcasted_iota(jnp.int32, (N_GATES, R), 1)
        db_ref[...] = lax.dot_general((gc // nc == gr).astype(F32), rows, (((1,), (0,)), ((), ())),
                                      precision=lax.Precision.HIGHEST, preferred_element_type=F32)

    return pl.pallas_call(
        body, name="gate_bwd",
        out_shape=[jax.ShapeDtypeStruct((R, LANES), F32), jax.ShapeDtypeStruct((N_GATES, LANES), F32)],
        compiler_params=_params(),
    )(gates_t, bias, di, dfc, dbl)


def _chunk_index(d, c, nc, reverse):
    j = (nc - 1 - c) if reverse else c
    return j + d * (nc - 1 - 2 * j)


def _mlstm_chunk(d, q, k, vb, brow, igrow, C, nvec, m_prev, eye):
    L = BLK
    r = lax.broadcasted_iota(jnp.int32, (L, L), 0)
    c = lax.broadcasted_iota(jnp.int32, (L, L), 1)
    mask = (r - c) * (1 - 2 * d) >= 0
    bcol = _row_to_col(brow, eye)
    igcol = _row_to_col(igrow, eye)
    log_d = jnp.where(mask, bcol - brow + igrow, -jnp.inf)
    log_inter = bcol + m_prev
    m_t = jnp.maximum(log_inter, jnp.max(log_d, axis=-1, keepdims=True))
    d_mat = jnp.exp(log_d - m_t)
    inter = jnp.exp(log_inter - m_t)
    s = _dot_nt(q, k) * d_mat
    sb = s.astype(BF16)
    cb = C.astype(BF16)
    num = _dot(sb, vb) + inter * _dot_nt(q, cb)
    den = jnp.sum(s, axis=-1, keepdims=True) + inter * jnp.sum(q.astype(F32) * nvec, axis=-1, keepdims=True)
    floor = jnp.exp(-m_t)
    denom = jnp.maximum(jnp.abs(den), floor)
    b_last = jnp.where(d == 0, brow[:, L - 1:L], brow[:, 0:1])
    return dict(bcol=bcol, igcol=igcol, m_t=m_t, d_mat=d_mat, inter=inter, sb=sb, cb=cb, num=num, den=den,
                floor=floor, denom=denom, b_last=b_last)


def _head_cols(v0_ref, v1_ref, hd):
    ref = v0_ref if hd < M_HEADS // 2 else v1_ref
    lo = (hd % (M_HEADS // 2)) * M_HEAD_DIM
    return ref[:, lo:lo + M_HEAD_DIM]


def _mlstm_fwd(qk, proj, ig5, b5):
    S = qk.shape[0]
    nc = S // BLK
    L, Dh, H = BLK, M_HEAD_DIM, M_HEADS

    def body(q_ref, k_ref, v0_ref, v1_ref, ig_ref, b_ref, h_ref, cs_ref, ns_ref, ms_ref, C, nvec, m):
        d = pl.program_id(0)

        @pl.when(pl.program_id(1) == 0)
        def _():
            C[...] = jnp.zeros_like(C)
            nvec[...] = jnp.zeros_like(nvec)
            m[...] = jnp.zeros_like(m)

        eye = _eye_mask(L)
        for hd in range(H):
            cols = slice(hd * Dh, (hd + 1) * Dh)
            q, k = q_ref[:, cols], k_ref[:, cols]
            vf = _head_cols(v0_ref, v1_ref, hd)
            brow, igrow = b_ref[hd], ig_ref[hd]
            m_prev = m[hd, :, 0:1]
            cs_ref[hd] = C[hd].astype(BF16)
            ns_ref[hd] = nvec[hd]
            ms_ref[hd] = m[hd]
            f = _mlstm_chunk(d, q, k, vf.astype(BF16), brow, igrow, C[hd], nvec[hd], m_prev, eye)
            h_ref[:, cols] = f["num"] / f["denom"]
            b_last = f["b_last"]
            log_w = b_last - brow + igrow
            m_new = jnp.maximum(b_last + m_prev, jnp.max(log_w, axis=-1, keepdims=True))
            w_col = jnp.exp(b_last - f["bcol"] + f["igcol"] - m_new)
            decay = jnp.exp(b_last + m_prev - m_new)
            C[hd] = decay * C[hd] + _dot_tn((w_col * vf).astype(BF16), k)
            nvec[hd] = decay * nvec[hd] + jnp.sum(w_col * k.astype(F32), axis=0, keepdims=True)
            m[hd] = jnp.broadcast_to(m_new, (1, LANES))

    cidx = lambda d, c: _chunk_index(d, c, nc, False)
    gspec = pl.BlockSpec((None, H, None, 1, LANES), lambda d, c: (d, 0, cidx(d, c), 0, 0))
    st = lambda *blk: pl.BlockSpec((None, H, None) + blk, lambda d, c: (d, 0, cidx(d, c), 0, 0))
    half = M_WIDTH // 2
    return pl.pallas_call(
        body, name="mlstm_fwd", grid=(2, nc),
        in_specs=[pl.BlockSpec((L, M_WIDTH), lambda d, c: (cidx(d, c), 0)),
                  pl.BlockSpec((L, M_WIDTH), lambda d, c: (cidx(d, c), 1)),
                  pl.BlockSpec((L, half), lambda d, c: (cidx(d, c), OFF_VM // half)),
                  pl.BlockSpec((L, half), lambda d, c: (cidx(d, c), OFF_VM // half + 1)), gspec, gspec],
        out_specs=[pl.BlockSpec((None, L, M_WIDTH), lambda d, c: (d, cidx(d, c), 0)),
                   st(Dh, Dh), st(1, Dh), st(1, LANES)],
        out_shape=[jax.ShapeDtypeStruct((2, S, M_WIDTH), F32), jax.ShapeDtypeStruct((2, H, nc, Dh, Dh), BF16),
                   jax.ShapeDtypeStruct((2, H, nc, 1, Dh), F32), jax.ShapeDtypeStruct((2, H, nc, 1, LANES), F32)],
        scratch_shapes=[pltpu.VMEM((H, Dh, Dh), F32), pltpu.VMEM((H, 1, Dh), F32), pltpu.VMEM((H, 1, LANES), F32)],
        compiler_params=_params(),
    )(qk, qk, proj, proj, ig5, b5)


def _mlstm_bwd(qk, proj, ig5, b5, cs, ns, ms, dhm):
    S = qk.shape[0]
    nc = S // BLK
    L, Dh, H = BLK, M_HEAD_DIM, M_HEADS

    def body(q_ref, k_ref, v0_ref, v1_ref, ig_ref, b_ref, cs_ref, ns_ref, ms_ref, dh_ref,
             dq_ref, dk_ref, dv_ref, di_ref, df_ref, dbl_ref, R, rvec, mu):
        d = pl.program_id(0)

        @pl.when(pl.program_id(1) == 0)
        def _():
            R[...] = jnp.zeros_like(R)
            rvec[...] = jnp.zeros_like(rvec)
            mu[...] = jnp.full(mu.shape, NEG_BIG, F32)

        eye = _eye_mask(L)
        for hd in range(H):
            cols = slice(hd * Dh, (hd + 1) * Dh)
            q, k = q_ref[:, cols], k_ref[:, cols]
            qf, kf = q.astype(F32), k.astype(F32)
            vb = _head_cols(v0_ref, v1_ref, hd).astype(BF16)
            brow, igrow = b_ref[hd], ig_ref[hd]
            nprev = ns_ref[hd]
            m_prev = ms_ref[hd, :, 0:1]
            cprev = cs_ref[hd]
            f = _mlstm_chunk(d, q, k, vb, brow, igrow, cprev, nprev, m_prev, eye)
            dh = dh_ref[:, cols]
            inv = 1.0 / f["denom"]
            hcur = f["num"] * inv
            dnum = dh * inv
            active = jnp.abs(f["den"]) >= f["floor"]
            dden = (jnp.where(active, -jnp.sign(f["den"]), 0.0)
                    * jnp.sum(dh * hcur, axis=-1, keepdims=True) * inv)
            dnb = dnum.astype(BF16)
            dqk = ((_dot_nt(dnb, vb) + dden) * f["d_mat"]).astype(BF16)
            dq = _dot(dqk, k) + f["inter"] * (_dot(dnb, f["cb"]) + dden * nprev)
            mu_prev = mu[hd, :, 0:1]
            a_col = jnp.exp(f["b_last"] - f["bcol"] + f["igcol"] + mu_prev)
            rb = R[hd].astype(BF16)
            dv = _dot_tn(f["sb"], dnb) + a_col * _dot_nt(k, rb)
            dk_inter = a_col * (_dot(vb, rb) + rvec[hd])
            dk = _dot_tn(dqk, q) + dk_inter
            dq_ref[:, cols] = dq
            dk_ref[:, cols] = dk
            dv_ref[:, cols] = dv
            kdk = jnp.sum(kf * dk, axis=-1, keepdims=True)
            qdq = jnp.sum(qf * dq, axis=-1, keepdims=True)
            di_ref[hd] = _col_to_row(kdk, eye)
            df_ref[hd] = _col_to_row(qdq - kdk, eye)
            older = jnp.sum(jnp.sum(R[hd] * cprev.astype(F32), axis=-1, keepdims=True), axis=0, keepdims=True)
            older = older + jnp.sum(rvec[hd] * nprev, axis=-1, keepdims=True)
            dbl = (jnp.sum(jnp.sum(kf * dk_inter, axis=-1, keepdims=True), axis=0, keepdims=True)
                   + jnp.exp(f["b_last"] + mu_prev + m_prev) * older)
            dbl_ref[hd] = jnp.broadcast_to(dbl, (1, LANES))
            lw = f["bcol"] - f["m_t"]
            mu_new = jnp.maximum(f["b_last"] + mu_prev, jnp.max(lw, axis=0, keepdims=True))
            wq = jnp.exp(lw - mu_new)
            decay = jnp.exp(f["b_last"] + mu_prev - mu_new)
            R[hd] = decay * R[hd] + _dot_tn((wq * dnum).astype(BF16), q)
            rvec[hd] = decay * rvec[hd] + jnp.sum(wq * dden * qf, axis=0, keepdims=True)
            mu[hd] = jnp.broadcast_to(mu_new, (1, LANES))

    cidx = lambda d, c: _chunk_index(d, c, nc, True)
    gin = pl.BlockSpec((None, H, None, 1, LANES), lambda d, c: (d, 0, cidx(d, c), 0, 0))
    st = lambda *blk: pl.BlockSpec((None, H, None) + blk, lambda d, c: (d, 0, cidx(d, c), 0, 0))
    per_dir = pl.BlockSpec((None, L, M_WIDTH), lambda d, c: (d, cidx(d, c), 0))
    big = jax.ShapeDtypeStruct((2, S, M_WIDTH), F32)
    small = jax.ShapeDtypeStruct((2, H, nc, 1, LANES), F32)
    half = M_WIDTH // 2
    return pl.pallas_call(
        body, name="mlstm_bwd", grid=(2, nc),
        in_specs=[pl.BlockSpec((L, M_WIDTH), lambda d, c: (cidx(d, c), 0)),
                  pl.BlockSpec((L, M_WIDTH), lambda d, c: (cidx(d, c), 1)),
                  pl.BlockSpec((L, half), lambda d, c: (cidx(d, c), OFF_VM // half)),
                  pl.BlockSpec((L, half), lambda d, c: (cidx(d, c), OFF_VM // half + 1)), gin, gin,
                  st(Dh, Dh), st(1, Dh), st(1, LANES),
                  pl.BlockSpec((L, M_WIDTH), lambda d, c: (cidx(d, c), 0))],
        out_specs=[per_dir, per_dir, per_dir, gin, gin, gin],
        out_shape=[big, big, big, small, small, small],
        scratch_shapes=[pltpu.VMEM((H, Dh, Dh), F32), pltpu.VMEM((H, 1, Dh), F32), pltpu.VMEM((H, 1, LANES), F32)],
        compiler_params=_params(),
    )(qk, qk, proj, proj, ig5, b5, cs, ns, ms, dhm)


def _mlstm_out_fwd(hs, proj, gamma):
    S = hs.shape[1]
    tm = _row_tile(S, 512)
    Dh, H = M_HEAD_DIM, M_HEADS

    def body(hf_ref, hb_ref, o_ref, g_ref, y_ref):
        hm = hf_ref[...] + hb_ref[...]
        r = lax.rsqrt(jnp.mean(hm * hm, axis=-1, keepdims=True) + EPS)
        y_ref[...] = (_sigmoid(o_ref[...]) * (hm * r * g_ref[...])).astype(BF16)

    hspec = lambda d: pl.BlockSpec((None, tm, Dh), lambda i, h: (d, i, h))
    return pl.pallas_call(
        body, name="mlstm_out_fwd", grid=(S // tm, H),
        in_specs=[hspec(0), hspec(1), pl.BlockSpec((tm, Dh), lambda i, h: (i, OFF_OM // Dh + h)),
                  pl.BlockSpec((1, Dh), lambda i, h: (0, h))],
        out_specs=pl.BlockSpec((tm, Dh), lambda i, h: (i, h)),
        out_shape=jax.ShapeDtypeStruct((S, M_WIDTH), BF16),
        compiler_params=_params(),
    )(hs, hs, proj, gamma.reshape(1, M_WIDTH))


def _mlstm_out_bwd(hs, proj, gamma, dycat):
    S = hs.shape[1]
    tm = _row_tile(S, 512)
    Dh, H = M_HEAD_DIM, M_HEADS
    D_OFF = ATT_WIDTH // Dh

    def body(hf_ref, hb_ref, o_ref, g_ref, dy_ref, dh_ref, do_ref, dg_ref):
        hm = hf_ref[...] + hb_ref[...]
        r = lax.rsqrt(jnp.mean(hm * hm, axis=-1, keepdims=True) + EPS)
        hh = hm * r
        so = _sigmoid(o_ref[...])
        dy = dy_ref[...]
        gam = g_ref[...]
        do_ref[...] = (dy * hh * gam * so * (1.0 - so)).astype(BF16)
        dyn = dy * so
        gd = dyn * gam
        dh_ref[...] = r * (gd - hh * jnp.mean(hh * gd, axis=-1, keepdims=True))
        part = jnp.sum(dyn * hh, axis=0, keepdims=True)

        @pl.when(pl.program_id(1) == 0)
        def _():
            dg_ref[...] = part

        @pl.when(pl.program_id(1) > 0)
        def _():
            dg_ref[...] += part

    hspec = lambda d: pl.BlockSpec((None, tm, Dh), lambda h, i: (d, i, h))
    out = pl.BlockSpec((tm, Dh), lambda h, i: (i, h))
    vec = pl.BlockSpec((1, Dh), lambda h, i: (0, h))
    dhm, do, dg = pl.pallas_call(
        body, name="mlstm_out_bwd", grid=(H, S // tm),
        in_specs=[hspec(0), hspec(1), pl.BlockSpec((tm, Dh), lambda h, i: (i, OFF_OM // Dh + h)), vec,
                  pl.BlockSpec((tm, Dh), lambda h, i: (i, D_OFF + h))],
        out_specs=[out, out, vec],
        out_shape=[jax.ShapeDtypeStruct((S, M_WIDTH), F32), jax.ShapeDtypeStruct((S, M_WIDTH), BF16),
                   jax.ShapeDtypeStruct((1, M_WIDTH), F32)],
        compiler_params=_params(),
    )(hs, hs, proj, gamma.reshape(1, M_WIDTH), dycat)
    return dhm, do, dg.reshape(M_WIDTH)


def _sum_cast(a2, name):
    _, S, C = a2.shape
    tm = _row_tile(S, 512)

    def body(a_ref, b_ref, o_ref):
        o_ref[...] = (a_ref[...] + b_ref[...]).astype(BF16)

    spec = lambda d: pl.BlockSpec((None, tm, C), lambda i: (d, i, 0))
    return pl.pallas_call(
        body, name=name, grid=(S // tm,),
        in_specs=[spec(0), spec(1)], out_specs=pl.BlockSpec((tm, C), lambda i: (i, 0)),
        out_shape=jax.ShapeDtypeStruct((S, C), BF16),
        compiler_params=_params(),
    )(a2, a2)


def _adamw(w, g, m, v):
    shape = w.shape
    C = shape[-1]
    R = w.size // C
    tm = _elementwise_rows(R, C)

    def body(w_ref, g_ref, m_ref, v_ref, d_ref, mo_ref, vo_ref):
        gv = g_ref[...]
        mn = ADAM_B1 * m_ref[...] + (1.0 - ADAM_B1) * gv
        vn = ADAM_B2 * v_ref[...] + (1.0 - ADAM_B2) * (gv * gv)
        m_hat = mn / (1.0 - ADAM_B1 ** ADAM_STEP)
        v_hat = vn / (1.0 - ADAM_B2 ** ADAM_STEP)
        d_ref[...] = -ADAM_LR * (m_hat / (jnp.sqrt(v_hat) + ADAM_EPS) + ADAM_WD * w_ref[...])
        mo_ref[...] = mn
        vo_ref[...] = vn

    row = pl.BlockSpec((tm, C), lambda i: (i, 0))
    sds = jax.ShapeDtypeStruct((R, C), F32)
    outs = pl.pallas_call(
        body, name="adamw", grid=(R // tm,),
        in_specs=[row] * 4, out_specs=[row] * 3, out_shape=[sds] * 3,
        compiler_params=_params(),
    )(*[t.reshape(R, C) for t in (w, g, m, v)])
    return tuple(o.reshape(shape) for o in outs)


CID_WEIGHT_GATHER, CID_PAIR_SWAP, CID_SCATTER = 0, 1, 2


def _other_chips(x, y):
    return [(1 - x, y), (x, 1 - y), (1 - x, 1 - y)]


def _handshake(peers):
    barrier = pltpu.get_barrier_semaphore()
    for peer in peers:
        pl.semaphore_signal(barrier, inc=1, device_id=peer, device_id_type=MESH)
    pl.semaphore_wait(barrier, len(peers))


def _sequencer_call(body, name, out_type, sem_counts, collective_id, operands):
    return pl.kernel(
        body, name=name, out_type=out_type,
        mesh=plsc.ScalarSubcoreMesh(axis_name="sequencer", num_cores=1),
        scratch_types=[pltpu.SemaphoreType.DMA((n,)) for n in sem_counts],
        compiler_params=pltpu.CompilerParams(collective_id=collective_id),
    )(*operands)


def _comm_weight_gather(groups):
    T = len(groups)
    flat = [(t, sum(p.shape[1] for p in g[:i]), part) for t, g in enumerate(groups) for i, part in enumerate(g)]
    N = len(flat)

    def body(*refs):
        ins, outs = refs[:N], refs[N:N + T]
        send, recv, local = refs[N + T:]
        x, y, c = lax.axis_index("x"), lax.axis_index("y"), lax.axis_index("c")
        me = 2 * x + y
        chips = _other_chips(x, y)
        _handshake([(px, py, c) for px, py in chips])

        def place(i, chip):
            t, off, part = flat[i]
            return outs[t].at[chip, :, pl.ds(off, part.shape[1])]

        mine = []
        for i in range(N):
            cp = pltpu.make_async_copy(ins[i], place(i, me), local.at[i])
            cp.start()
            mine.append(cp)
            for k, (px, py) in enumerate(chips):
                pltpu.make_async_remote_copy(
                    src_ref=ins[i], dst_ref=place(i, me), send_sem=send.at[3 * i + k], recv_sem=recv.at[3 * i + k],
                    device_id=(px, py, c), device_id_type=MESH).start()
        for i in range(N):
            for k, (px, py) in enumerate(chips):
                cp = pltpu.make_async_remote_copy(
                    src_ref=ins[i], dst_ref=place(i, 2 * px + py), send_sem=send.at[3 * i + k],
                    recv_sem=recv.at[3 * i + k], device_id=(px, py, c), device_id_type=MESH)
                cp.wait_send()
                cp.wait_recv()
            mine[i].wait()

    out_type = tuple(jax.ShapeDtypeStruct((4, g[0].shape[0], sum(p.shape[1] for p in g)), g[0].dtype) for g in groups)
    return _sequencer_call(
        body, "comm_weight_gather", out_type, (3 * N, 3 * N, N), CID_WEIGHT_GATHER, [part for _, _, part in flat])


def _comm_to_sibling(xs, other_half=False):
    T = len(xs)

    def body(*refs):
        ins, outs = refs[:T], refs[T:2 * T]
        send, recv = refs[2 * T:]
        x, y, c = lax.axis_index("x"), lax.axis_index("y"), lax.axis_index("c")
        _handshake([(x, y, 1 - c)])
        cps = [pltpu.make_async_remote_copy(
            src_ref=ins[t].at[:, 1 - c] if other_half else ins[t], dst_ref=outs[t], send_sem=send.at[t],
            recv_sem=recv.at[t], device_id=(x, y, 1 - c), device_id_type=MESH) for t in range(T)]
        for cp in cps:
            cp.start()
        for cp in cps:
            cp.wait_send()
            cp.wait_recv()

    shapes = [s.shape[:1] + s.shape[2:] if other_half else s.shape for s in xs]
    return _sequencer_call(
        body, "comm_pair_swap", tuple(jax.ShapeDtypeStruct(sh, s.dtype) for sh, s in zip(shapes, xs)),
        (T, T), CID_PAIR_SWAP, xs)


def _comm_scatter_to_owners(ps):
    T = len(ps)

    def body(*refs):
        ins, outs = refs[:T], refs[T:2 * T]
        send, recv = refs[2 * T:]
        x, y, c = lax.axis_index("x"), lax.axis_index("y"), lax.axis_index("c")
        chips = _other_chips(x, y)
        _handshake([(px, py, c) for px, py in chips])
        cps = []
        for t in range(T):
            for k, (px, py) in enumerate(chips):
                cp = pltpu.make_async_remote_copy(
                    src_ref=ins[t].at[2 * px + py], dst_ref=outs[t].at[k], send_sem=send.at[3 * t + k],
                    recv_sem=recv.at[3 * t + k], device_id=(px, py, c), device_id_type=MESH)
                cp.start()
                cps.append(cp)
        for cp in cps:
            cp.wait_send()
            cp.wait_recv()

    return _sequencer_call(
        body, "comm_scatter_to_owners", tuple(jax.ShapeDtypeStruct((3,) + p.shape[1:], p.dtype) for p in ps),
        (3 * T, 3 * T), CID_SCATTER, ps)


def _comm_all_to_all_small(pack):
    R = pack.shape[0]

    def body(in_ref, out_ref, send, recv, local):
        x, y, c = lax.axis_index("x"), lax.axis_index("y"), lax.axis_index("c")
        me = 4 * x + 2 * y + c
        mine = pltpu.make_async_copy(in_ref, out_ref.at[me], local.at[0])
        mine.start()
        peers = []
        for k in range(1, 8):
            fx, fy, fc = (k >> 2) & 1, (k >> 1) & 1, k & 1
            peers.append((x + fx * (1 - 2 * x), y + fy * (1 - 2 * y), c + fc * (1 - 2 * c)))
        for k, peer in enumerate(peers):
            pltpu.make_async_remote_copy(
                src_ref=in_ref, dst_ref=out_ref.at[me], send_sem=send.at[k], recv_sem=recv.at[k],
                device_id=peer, device_id_type=MESH).start()
        for k, (px, py, pc) in enumerate(peers):
            cp = pltpu.make_async_remote_copy(
                src_ref=in_ref, dst_ref=out_ref.at[4 * px + 2 * py + pc], send_sem=send.at[k], recv_sem=recv.at[k],
                device_id=(px, py, pc), device_id_type=MESH)
            cp.wait_send()
            cp.wait_recv()
        mine.wait()

    return pl.pallas_call(
        body, name="comm_all_to_all_small",
        in_specs=[ANY], out_specs=ANY,
        out_shape=jax.ShapeDtypeStruct((8, R, LANES), F32),
        scratch_shapes=[pltpu.SemaphoreType.DMA((7,)), pltpu.SemaphoreType.DMA((7,)), pltpu.SemaphoreType.DMA((1,))],
    )(pack)


def _tie(*trees):
    return lax.optimization_barrier(trees)


class _ReduceScatter:
    def __init__(self, grads, dh):
        split = [g.reshape(4, 2, g.shape[1] // 2, g.shape[2]) for g in grads]
        self.split, self.dh = _tie(split, dh)
        self.got = _comm_to_sibling(self.split, other_half=True)

    def scatter(self, dh):
        split, got, dh = _tie(self.split, list(self.got), dh)
        pair = [_pair_sum(a, b) for a, b in zip(split, got)]
        self.pair, dh = _tie(pair, dh)
        self.arrived = _comm_scatter_to_owners(self.pair)
        return dh

    def gather(self, dh):
        pair, arrived, dh = _tie(self.pair, list(self.arrived), dh)
        halves = [_chip_sum(p, a) for p, a in zip(pair, arrived)]
        self.halves, dh = _tie(halves, dh)
        self.other = _comm_to_sibling(self.halves)
        return dh

    def settle(self, dh):
        self.other, dh = _tie(list(self.other), dh)
        return dh

    def result(self):
        south = lax.axis_index("c") == 0
        return [jnp.where(south, jnp.concatenate([mine, theirs]), jnp.concatenate([theirs, mine]))
                for mine, theirs in zip(self.halves, self.other)]


def _ffn_fwd(x, g_pre, g_post, wgu, wd):
    S, D = x.shape
    F = wd.shape[1]
    xn = _norm_fwd(x, g_pre)
    gu, h = _ffn_up(xn, wgu)
    wd, h = _tie(wd, h)
    f = _mm(h, wd, "nn", ja="r", jb="r", tm=_row_tile(S, ACT_ROWS), tn=D, tk=F, name="ffn_down")
    x_out = _resid_norm_fwd(x, f, g_post, 0.5)
    return x_out, (x, xn, gu, h, f), wd


def _ffn_bwd(dx, saved, g_pre, g_post, wgu, wd, mid=None, on_grads=None, late=None):
    x, xn, gu, h, f = saved
    S, D = x.shape
    J, F, _ = wd.shape
    tm, tk = _row_tile(S, ACT_ROWS), _row_tile(S, TOKEN_ROWS)
    df, dg_post = _norm_bwd(f, g_post, dx, 0.5, out_dtype=BF16)
    d_wd = _mm(h, df, "tn", ja="b", tm=F, tn=MODEL_COLS, tk=tk, name="ffn_dwd")
    dgu = _ffn_bwd_hidden(df, wd, gu)
    if mid is not None:
        dgu = mid(dgu)
    d_wgu = _mm(xn, dgu, "tn", jb="b", tm=MODEL_COLS, tn=F, tk=tk, name="ffn_dwgu")
    if on_grads is not None:
        dgu = on_grads(d_wgu, d_wd, dgu)
    dxn = _mm(dgu, wgu, "nt", ja="r", jb="r", tm=tm, tn=MODEL_COLS, tk=2 * F, name="ffn_dxn")
    if late is not None:
        dxn = late(dxn)
    dx_in, dg_pre = _norm_bwd(x, g_pre, dxn, 1.0, resid=dx)
    return dx_in, dg_pre, dg_post, d_wgu, d_wd


def _gates_layout(proj, b_gate, nc):
    gt = jnp.transpose(proj[:, OFF_G:OFF_G + N_GATES]).reshape(N_GATES * nc, LANES)
    bias = jnp.repeat(b_gate, nc).reshape(N_GATES * nc, 1)
    return gt, bias


def _mixer_fwd(x, p, cosf, sinf):
    S, D = x.shape
    nc = S // BLK
    hn = _norm_fwd(x, p["g_pre"])
    proj = _mm(hn, p["w_in"], "nn", tm=_row_tile(S, ACT_ROWS), tn=IN_PAD // 5, tk=D, name="in_proj")
    qr, kr, va = _rope_fwd(proj, cosf, sinf)
    y_att = _attn_fwd(qr, kr, va, p["sink"])
    qk = _conv5(proj, p["conv_w"], col0=OFF_QM, act=True, out_dtype=BF16, name="conv_silu")
    gt, bias = _gates_layout(proj, p["b_gate"], nc)
    ig, cum = _gate_prep(gt, bias)
    ig5, b5 = ig.reshape(2, M_HEADS, nc, 1, LANES), cum.reshape(2, M_HEADS, nc, 1, LANES)
    hs, cs, ns, ms = _mlstm_fwd(qk, proj, ig5, b5)
    y_m = _mlstm_out_fwd(hs, proj, p["gamma"])
    ycat = jnp.concatenate([y_att, y_m], axis=-1)
    w_out, ycat = _tie(p["w_out"], ycat)
    p["w_out"] = w_out
    mix = _mm(ycat, w_out, "nn", tm=_row_tile(S, ACT_ROWS), tn=MODEL_COLS, tk=D, name="out_proj")
    x_out = _resid_norm_fwd(x, mix, p["g_post"], 1.0)
    return x_out, (x, hn, proj, qr, kr, va, qk, gt, bias, ig5, b5, hs, cs, ns, ms, ycat, mix)


def _mixer_bwd(dx, saved, p, cosf, sinf, mid=None):
    x, hn, proj, qr, kr, va, qk, gt, bias, ig5, b5, hs, cs, ns, ms, ycat, mix = saved
    S, D = x.shape
    nc = S // BLK
    tm, tk = _row_tile(S, ACT_ROWS), _row_tile(S, TOKEN_ROWS)
    dmix, dg_post = _norm_bwd(mix, p["g_post"], dx, 1.0, out_dtype=BF16)
    d_wout = _mm(ycat, dmix, "tn", tm=MODEL_COLS, tn=MODEL_COLS, tk=tk, name="dw_out")
    dycat = _mm(dmix, p["w_out"], "nt", tm=tm, tn=MODEL_COLS, tk=D, name="d_ycat")
    if mid is not None:
        dycat = mid(dycat)
    dhm, d_om, d_gamma = _mlstm_out_bwd(hs, proj, p["gamma"], dycat)
    dq2, dk2, dv2, di, dfc, dbl = _mlstm_bwd(qk, proj, ig5, b5, cs, ns, ms, dhm)
    d_vm = _sum_cast(dv2, "dv_sum")
    dgt, db = _gate_bwd(gt, bias, di.reshape(-1, LANES), dfc.reshape(-1, LANES), dbl.reshape(-1, LANES), nc)
    dpre, d_conv = _conv_bwd_pre(proj, OFF_QM, p["conv_w"], dq2, dk2)
    d_qkm = _conv5(dpre, p["conv_w"][::-1], col0=0, act=False, out_dtype=BF16, name="conv_bwd_x")
    d_gates = jnp.transpose(dgt.reshape(N_GATES, S)).astype(BF16)
    d_gates = jnp.pad(d_gates, ((0, 0), (0, IN_PAD - IN_WIDTH)))
    dqr, dkr, dva, d_sink = _attn_bwd(qr, kr, va, p["sink"], dycat)
    d_att = _rope_bwd(dqr, dkr, dva, cosf, sinf)
    dproj = jnp.concatenate([d_att, d_qkm, d_vm, d_om, d_gates], axis=-1)
    d_win = _mm(hn, dproj, "tn", tm=MODEL_COLS, tn=IN_PAD // 5, tk=tk, name="dw_in")
    dhn = _mm(dproj, p["w_in"], "nt", tm=tm, tn=MODEL_COLS, tk=IN_PAD // 3, name="d_hn")
    dx_in, dg_pre = _norm_bwd(x, p["g_pre"], dhn, 1.0, resid=dx)
    small = dict(g_pre=dg_pre, g_post=dg_post, gamma=d_gamma, conv=d_conv, b_gate=db[:, 0], sink=d_sink)
    return dx_in, small, d_win, d_wout


def _pad_lanes(v):
    v = v.reshape(-1)
    return jnp.pad(v, (0, (-v.shape[0]) % LANES))


def kernel(x, ffn1_norm_pre, ffn1_norm_post, ffn1_w_gate, ffn1_w_up, ffn1_w_down, mix_norm_pre, mix_norm_post, w_in, b_gate, conv_w, attn_sink, mlstm_norm, w_out, ffn2_norm_pre, ffn2_norm_post, ffn2_w_gate, ffn2_w_up, ffn2_w_down, loss_target, m_ffn1_norm_pre, m_ffn1_norm_post, m_ffn1_w_gate, m_ffn1_w_up, m_ffn1_w_down, m_mix_norm_pre, m_mix_norm_post, m_w_in, m_b_gate, m_conv_w, m_attn_sink, m_mlstm_norm, m_w_out, m_ffn2_norm_pre, m_ffn2_norm_post, m_ffn2_w_gate, m_ffn2_w_up, m_ffn2_w_down, v_ffn1_norm_pre, v_ffn1_norm_post, v_ffn1_w_gate, v_ffn1_w_up, v_ffn1_w_down, v_mix_norm_pre, v_mix_norm_post, v_w_in, v_b_gate, v_conv_w, v_attn_sink, v_mlstm_norm, v_w_out, v_ffn2_norm_pre, v_ffn2_norm_post, v_ffn2_w_gate, v_ffn2_w_up, v_ffn2_w_down):
    names = ["ffn1_norm_pre", "ffn1_norm_post", "ffn1_w_gate", "ffn1_w_up", "ffn1_w_down", "mix_norm_pre",
             "mix_norm_post", "w_in", "b_gate", "conv_w", "attn_sink", "mlstm_norm", "w_out", "ffn2_norm_pre",
             "ffn2_norm_post", "ffn2_w_gate", "ffn2_w_up", "ffn2_w_down"]
    w = dict(zip(names, [ffn1_norm_pre, ffn1_norm_post, ffn1_w_gate, ffn1_w_up, ffn1_w_down, mix_norm_pre,
                         mix_norm_post, w_in, b_gate, conv_w, attn_sink, mlstm_norm, w_out, ffn2_norm_pre,
                         ffn2_norm_post, ffn2_w_gate, ffn2_w_up, ffn2_w_down]))
    mom_m = dict(zip(names, [m_ffn1_norm_pre, m_ffn1_norm_post, m_ffn1_w_gate, m_ffn1_w_up, m_ffn1_w_down,
                             m_mix_norm_pre, m_mix_norm_post, m_w_in, m_b_gate, m_conv_w, m_attn_sink,
                             m_mlstm_norm, m_w_out, m_ffn2_norm_pre, m_ffn2_norm_post, m_ffn2_w_gate,
                             m_ffn2_w_up, m_ffn2_w_down]))
    mom_v = dict(zip(names, [v_ffn1_norm_pre, v_ffn1_norm_post, v_ffn1_w_gate, v_ffn1_w_up, v_ffn1_w_down,
                             v_mix_norm_pre, v_mix_norm_post, v_w_in, v_b_gate, v_conv_w, v_attn_sink,
                             v_mlstm_norm, v_w_out, v_ffn2_norm_pre, v_ffn2_norm_post, v_ffn2_w_gate,
                             v_ffn2_w_up, v_ffn2_w_down]))
    xs = x[0]
    target = loss_target[0]
    S, D = xs.shape
    depth = w_in.shape[0]
    F = ffn1_w_gate.shape[-1]
    in_shard = w_in.shape[-1]
    conv_shard = conv_w.shape[-1]
    chip = 2 * lax.axis_index("x") + lax.axis_index("y")
    cosf, sinf = _rope_tables(S)

    gathered = []
    for l in range(depth):
        half = lambda name: w[name][l].astype(BF16)
        parts = dict(wgu1=[half("ffn1_w_gate"), half("ffn1_w_up")], wd1=[half("ffn1_w_down")],
                     win=[half("w_in")], conv=[jnp.pad(w["conv_w"][l], ((0, 8 - CONV_WIDTH), (0, 0)))],
                     wout=[half("w_out")], wgu2=[half("ffn2_w_gate"), half("ffn2_w_up")], wd2=[half("ffn2_w_down")])
        kernels = ([["wgu1"], ["wd1"], ["win", "conv"], ["wout"], ["wgu2"], ["wd2"]] if l == 0
                   else [["wgu1", "wd1"], ["win", "conv", "wout", "wgu2", "wd2"]])
        got = {}
        for names_k in kernels:
            got.update(zip(names_k, _comm_weight_gather([parts[n] for n in names_k])))
        gathered.append(got)

    def mixer_weights(l, win, wout, convg):
        win_full = jnp.concatenate([win[j] for j in range(4)], axis=-1)
        conv_full = jnp.concatenate([convg[j] for j in range(4)], axis=-1)[:CONV_WIDTH]
        return dict(w_in=jnp.pad(win_full, ((0, 0), (0, IN_PAD - IN_WIDTH))), w_out=wout.reshape(D, D),
                    conv_w=conv_full, g_pre=w["mix_norm_pre"][l], g_post=w["mix_norm_post"][l],
                    b_gate=w["b_gate"][l], sink=w["attn_sink"][l], gamma=w["mlstm_norm"][l])

    h = xs
    saved = []
    layers = []
    for l in range(depth):
        g = gathered[l]
        wgu1, h = _tie(g["wgu1"], h)
        h, s1, wd1 = _ffn_fwd(h, w["ffn1_norm_pre"][l], w["ffn1_norm_post"][l], wgu1, g["wd1"])
        (win, convg), h = _tie((g["win"], g["conv"]), h)
        mix_p = mixer_weights(l, win, g["wout"], convg)
        h, s2 = _mixer_fwd(h, mix_p, cosf, sinf)
        wgu2, h = _tie(g["wgu2"], h)
        h, s3, wd2 = _ffn_fwd(h, w["ffn2_norm_pre"][l], w["ffn2_norm_post"][l], wgu2, g["wd2"])
        layers.append(dict(wgu1=wgu1, wd1=wd1, wgu2=wgu2, wd2=wd2, mix=mix_p))
        saved.append((s1, s2, s3))
    dh, loss_tile = _loss_grad(h, target)

    big = {}
    small_rows = []
    old = None
    for l in reversed(range(depth)):
        p = layers[l]
        s1, s2, s3 = saved[l]
        dh, dg2_pre, dg2_post, d_wgu2, d_wd2 = _ffn_bwd(dh, s3, w["ffn2_norm_pre"][l], w["ffn2_norm_post"][l],
                                                       p["wgu2"], p["wd2"], mid=old["b"].scatter if old else None)
        if old is not None:
            dh = old["a"].settle(old["b"].gather(dh))
        rs_c = _ReduceScatter([d_wgu2, d_wd2], dh)
        dh, sm, d_win, d_wout = _mixer_bwd(rs_c.dh, s2, p["mix"], cosf, sinf, mid=rs_c.scatter)
        dh = rs_c.gather(dh)
        if old is not None:
            dh = old["b"].settle(dh)
        d_win4 = jnp.stack([d_win[:, j * in_shard:(j + 1) * in_shard] for j in range(4)])
        rs_a = _ReduceScatter([d_win4, d_wout.reshape(4, D // 4, D)], dh)
        if l > 0:
            dh, dg1_pre, dg1_post, d_wgu1, d_wd1 = _ffn_bwd(
                rs_a.dh, s1, w["ffn1_norm_pre"][l], w["ffn1_norm_post"][l], p["wgu1"], p["wd1"], mid=rs_a.scatter)
            dh = rs_c.settle(rs_a.gather(dh))
            rs_b = _ReduceScatter([d_wgu1, d_wd1], dh)
            dh = rs_b.dh
        else:
            last = []

            def start_last(d_wgu, d_wd, x):
                last.append(_ReduceScatter([d_wgu, d_wd], x))
                return last[0].dh

            dh, dg1_pre, dg1_post, d_wgu1, d_wd1 = _ffn_bwd(
                rs_a.dh, s1, w["ffn1_norm_pre"][l], w["ffn1_norm_post"][l], p["wgu1"], p["wd1"], mid=rs_a.scatter,
                on_grads=start_last, late=lambda x: last[0].scatter(x))
            dh = rs_c.settle(rs_a.gather(dh))
            rs_b = last[0]
        old = dict(a=rs_a, b=rs_b, c=rs_c)
        big[l] = old
        small_rows.append((l, [dg1_pre, dg1_post, sm["g_pre"], sm["g_post"], dg2_pre, dg2_post, sm["gamma"],
                               sm["conv"], sm["b_gate"], sm["sink"]]))
    big_a = {l: g["a"].result() + g["c"].result() for l, g in big.items()}

    small_rows.sort(key=lambda t: t[0])
    flat = [_pad_lanes(v) for _, vs in small_rows for v in vs] + [loss_tile[0]]
    sizes = [f.shape[0] for f in flat]
    pack = jnp.concatenate(flat)
    pack = jnp.pad(pack, (0, (-pack.shape[0]) % (8 * LANES))).reshape(-1, LANES)
    every = _comm_all_to_all_small(pack)
    total = _add_n([every[i] for i in range(8)], "small_sum").reshape(-1)
    pieces, off = [], 0
    for n in sizes:
        pieces.append(total[off:off + n])
        off += n
    loss = pieces[-1][0]
    per_layer = [pieces[10 * l:10 * l + 10] for l in range(depth)]

    def stack_small(i, shape):
        n = 1
        for s in shape:
            n *= s
        return jnp.stack([per_layer[l][i][:n].reshape(shape) for l in range(depth)])

    conv_full_grad = stack_small(7, (CONV_WIDTH, 4 * conv_shard))
    grads = {
        "ffn1_norm_pre": stack_small(0, (D,)), "ffn1_norm_post": stack_small(1, (D,)),
        "mix_norm_pre": stack_small(2, (D,)), "mix_norm_post": stack_small(3, (D,)),
        "ffn2_norm_pre": stack_small(4, (D,)), "ffn2_norm_post": stack_small(5, (D,)),
        "mlstm_norm": stack_small(6, (M_WIDTH,)),
        "conv_w": lax.dynamic_slice_in_dim(conv_full_grad, chip * conv_shard, conv_shard, 2),
        "b_gate": stack_small(8, (N_GATES,)), "attn_sink": stack_small(9, (ATT_HEADS,)),
    }
    gu2 = jnp.stack([big_a[l][2] for l in range(depth)])
    grads["ffn2_w_gate"], grads["ffn2_w_up"] = gu2[:, :, :F], gu2[:, :, F:]
    grads["ffn2_w_down"] = jnp.stack([big_a[l][3] for l in range(depth)])
    grads["w_in"] = jnp.stack([big_a[l][0] for l in range(depth)])
    grads["w_out"] = jnp.stack([big_a[l][1] for l in range(depth)])

    deltas, new_m, new_v = {}, {}, {}
    last_group = ("ffn1_w_gate", "ffn1_w_up", "ffn1_w_down")
    for n in names:
        if n not in last_group:
            deltas[n], new_m[n], new_v[n] = _adamw(w[n], grads[n], mom_m[n], mom_v[n])
    deltas, new_m, new_v = big[0]["b"].gather((deltas, new_m, new_v))
    big_b = {l: g["b"].result() for l, g in big.items()}
    gu1 = jnp.stack([big_b[l][0] for l in range(depth)])
    grads["ffn1_w_gate"], grads["ffn1_w_up"] = gu1[:, :, :F], gu1[:, :, F:]
    grads["ffn1_w_down"] = jnp.stack([big_b[l][1] for l in range(depth)])
    for n in last_group:
        deltas[n], new_m[n], new_v[n] = _adamw(w[n], grads[n], mom_m[n], mom_v[n])
    grad_x = dh[None]
    return (loss, grad_x, *[grads[n] for n in names], *[deltas[n] for n in names],
            *[new_m[n] for n in names], *[new_v[n] for n in names])
```

```python
import jax
import jax.numpy as jnp
from jax import lax
from jax.experimental import pallas as pl
from jax.experimental.pallas import tpu as pltpu
from jax.experimental.pallas import tpu_sc as plsc

F32 = jnp.float32
BF16 = jnp.bfloat16
MESH = pl.DeviceIdType.MESH
ANY = pl.BlockSpec(memory_space=pl.ANY)

VMEM_LIMIT_BYTES = 56 * 1024 * 1024
LANES = 128

EPS = 1e-6
ATT_HEADS = 8
ATT_KV_HEADS = 2
ATT_GROUP = ATT_HEADS // ATT_KV_HEADS
ATT_HEAD_DIM = 128
ATT_WIDTH = ATT_HEADS * ATT_HEAD_DIM
KV_WIDTH = ATT_KV_HEADS * ATT_HEAD_DIM
BLK = 128
M_HEADS = 4
M_HEAD_DIM = 256
M_WIDTH = M_HEADS * M_HEAD_DIM
CONV_WIDTH = 5
CONV_HALO = 8
ROPE_THETA = 10000.0
N_GATES = 4 * M_HEADS
OFF_QA, OFF_KA, OFF_VA = 0, ATT_WIDTH, ATT_WIDTH + KV_WIDTH
OFF_QM = ATT_WIDTH + 2 * KV_WIDTH
OFF_KM = OFF_QM + M_WIDTH
OFF_VM = OFF_KM + M_WIDTH
OFF_OM = OFF_VM + M_WIDTH
OFF_G = OFF_OM + M_WIDTH
IN_WIDTH = OFF_G + N_GATES
IN_PAD = OFF_G + LANES
NEG_BIG = -1e30

ADAM_LR, ADAM_B1, ADAM_B2, ADAM_EPS, ADAM_WD, ADAM_STEP = 0.001, 0.9, 0.999, 1e-08, 0.01, 10


def _params(**kw):
    return pltpu.CompilerParams(vmem_limit_bytes=VMEM_LIMIT_BYTES, **kw)


def _dot(a, b):
    return lax.dot_general(a, b, (((1,), (0,)), ((), ())), preferred_element_type=F32)


def _dot_nt(a, b):
    return lax.dot_general(a, b, (((1,), (1,)), ((), ())), preferred_element_type=F32)


def _dot_tn(a, b):
    return lax.dot_general(a, b, (((0,), (0,)), ((), ())), preferred_element_type=F32)


def _sigmoid(x):
    return 0.5 * jnp.tanh(0.5 * x) + 0.5


def _eye_mask(n):
    r = lax.broadcasted_iota(jnp.int32, (n, n), 0)
    c = lax.broadcasted_iota(jnp.int32, (n, n), 1)
    return r == c


def _row_to_col(row, eye):
    n = eye.shape[0]
    return jnp.sum(jnp.where(eye, jnp.broadcast_to(row, (n, n)), 0.0), axis=1, keepdims=True)


def _col_to_row(col, eye):
    n = eye.shape[0]
    return jnp.sum(jnp.where(eye, jnp.broadcast_to(col, (n, n)), 0.0), axis=0, keepdims=True)


def _mm(a, b, kind, *, tm, tn, tk, name, out_dtype=F32, ja=None, jb=None):
    a2, b2 = a.shape[-2:], b.shape[-2:]
    if kind == "nn":
        (M, K), (_, N) = a2, b2
    elif kind == "nt":
        (M, K), (N, _) = a2, b2
    else:
        (K, M), (_, N) = a2, b2
    J = a.shape[0] if ja else (b.shape[0] if jb else 1)
    batch = "b" in (ja, jb)
    red = "r" in (ja, jb)
    nk = K // tk
    nr = nk * (J if red else 1)
    grid = ((J if batch else 1), M // tm, N // tn, nr)

    def lead(mode, g, r):
        return g if mode == "b" else r // nk

    def a_map(g, i, n, r):
        kk = r % nk
        idx = (i, kk) if kind != "tn" else (kk, i)
        return idx if ja is None else (lead(ja, g, r),) + idx

    def b_map(g, i, n, r):
        kk = r % nk
        idx = (kk, n) if kind != "nt" else (n, kk)
        return idx if jb is None else (lead(jb, g, r),) + idx

    def o_map(g, i, n, r):
        return (g, i, n) if batch else (i, n)

    a_blk = (tm, tk) if kind != "tn" else (tk, tm)
    b_blk = (tk, tn) if kind != "nt" else (tn, tk)
    dot = {"nn": _dot, "nt": _dot_nt, "tn": _dot_tn}[kind]

    def body(a_ref, b_ref, o_ref, *scratch):
        part = dot(a_ref[...], b_ref[...])
        if nr == 1:
            o_ref[...] = part.astype(out_dtype)
        else:
            acc = scratch[0]
            r = pl.program_id(3)

            @pl.when(r == 0)
            def _():
                acc[...] = part

            @pl.when(r > 0)
            def _():
                acc[...] += part

            @pl.when(r == nr - 1)
            def _():
                o_ref[...] = acc[...].astype(out_dtype)

    return pl.pallas_call(
        body, name=name, grid=grid,
        in_specs=[pl.BlockSpec(a_blk if ja is None else (None,) + a_blk, a_map),
                  pl.BlockSpec(b_blk if jb is None else (None,) + b_blk, b_map)],
        out_specs=pl.BlockSpec((None, tm, tn) if batch else (tm, tn), o_map),
        out_shape=jax.ShapeDtypeStruct((J, M, N) if batch else (M, N), out_dtype),
        scratch_shapes=[pltpu.VMEM((tm, tn), F32)] if nr > 1 else [],
        compiler_params=_params(),
    )(a, b)


def _row_tile(S, want):
    return min(S, want)


ELEMENTWISE_TILE_BYTES = 1 << 20
ACT_ROWS = 1024
TOKEN_ROWS = 2048
MODEL_COLS = 1024


def _elementwise_rows(R, C):
    for cand in (1024, 512, 256, 128, 64, 32, 16, 8):
        if R % cand == 0 and R > cand and cand * C * 4 <= ELEMENTWISE_TILE_BYTES:
            return cand
    return R if R * C * 4 <= ELEMENTWISE_TILE_BYTES or R % 8 else 8


def _ffn_up(xn, wgu):
    S, D = xn.shape
    J, _, F2 = wgu.shape
    F = F2 // 2
    tm = _row_tile(S, 256)

    def body(x_ref, w_ref, gu_ref, h_ref):
        gu = _dot(x_ref[...], w_ref[...])
        g, u = gu[:, :F], gu[:, F:]
        gu_ref[...] = gu.astype(BF16)
        h_ref[...] = (g * _sigmoid(g) * u).astype(BF16)

    return pl.pallas_call(
        body, name="ffn_up", grid=(J, S // tm),
        in_specs=[pl.BlockSpec((tm, D), lambda j, i: (i, 0)),
                  pl.BlockSpec((None, D, F2), lambda j, i: (j, 0, 0))],
        out_specs=[pl.BlockSpec((None, tm, F2), lambda j, i: (j, i, 0)),
                   pl.BlockSpec((None, tm, F), lambda j, i: (j, i, 0))],
        out_shape=[jax.ShapeDtypeStruct((J, S, F2), BF16), jax.ShapeDtypeStruct((J, S, F), BF16)],
        compiler_params=_params(),
    )(xn, wgu)


def _ffn_bwd_hidden(df, wd, gu):
    S, D = df.shape
    J, F, _ = wd.shape
    F2 = 2 * F
    tm = _row_tile(S, 512)

    def body(df_ref, w_ref, gu_ref, o_ref):
        dh = _dot_nt(df_ref[...], w_ref[...])
        g = gu_ref[:, :F].astype(F32)
        u = gu_ref[:, F:].astype(F32)
        sg = _sigmoid(g)
        o_ref[:, :F] = (dh * u * (sg * (1.0 + g * (1.0 - sg)))).astype(BF16)
        o_ref[:, F:] = (dh * (g * sg)).astype(BF16)

    return pl.pallas_call(
        body, name="ffn_bwd_hidden", grid=(J, S // tm),
        in_specs=[pl.BlockSpec((tm, D), lambda j, i: (i, 0)),
                  pl.BlockSpec((None, F, D), lambda j, i: (j, 0, 0)),
                  pl.BlockSpec((None, tm, F2), lambda j, i: (j, i, 0))],
        out_specs=pl.BlockSpec((None, tm, F2), lambda j, i: (j, i, 0)),
        out_shape=jax.ShapeDtypeStruct((J, S, F2), BF16),
        compiler_params=_params(),
    )(df, wd, gu)


def _norm_fwd(x, g):
    S, D = x.shape
    tm = _row_tile(S, 512)

    def body(x_ref, g_ref, o_ref):
        xv = x_ref[...]
        r = lax.rsqrt(jnp.mean(xv * xv, axis=-1, keepdims=True) + EPS)
        o_ref[...] = (xv * r * g_ref[...]).astype(BF16)

    return pl.pallas_call(
        body, name="norm_fwd", grid=(S // tm,),
        in_specs=[pl.BlockSpec((tm, D), lambda i: (i, 0)), pl.BlockSpec((1, D), lambda i: (0, 0))],
        out_specs=pl.BlockSpec((tm, D), lambda i: (i, 0)),
        out_shape=jax.ShapeDtypeStruct((S, D), BF16),
        compiler_params=_params(),
    )(x, g.reshape(1, D))


def _resid_norm_fwd(x, f, g, alpha):
    S, D = x.shape
    tm = _row_tile(S, 512)

    def body(x_ref, f_ref, g_ref, o_ref):
        fv = f_ref[...]
        r = lax.rsqrt(jnp.mean(fv * fv, axis=-1, keepdims=True) + EPS)
        o_ref[...] = x_ref[...] + alpha * (fv * r * g_ref[...])

    return pl.pallas_call(
        body, name="resid_norm_fwd", grid=(S // tm,),
        in_specs=[pl.BlockSpec((tm, D), lambda i: (i, 0)), pl.BlockSpec((tm, D), lambda i: (i, 0)),
                  pl.BlockSpec((1, D), lambda i: (0, 0))],
        out_specs=pl.BlockSpec((tm, D), lambda i: (i, 0)),
        out_shape=jax.ShapeDtypeStruct((S, D), F32),
        compiler_params=_params(),
    )(x, f, g.reshape(1, D))


def _norm_bwd(x, g, dy, alpha, resid=None, out_dtype=F32):
    S, D = x.shape
    tm = _row_tile(S, 256)
    has_resid = resid is not None

    def body(*refs):
        x_ref, g_ref, dy_ref = refs[:3]
        res_ref = refs[3] if has_resid else None
        dx_ref, dg_ref = refs[-2:]
        xv = x_ref[...]
        r = lax.rsqrt(jnp.mean(xv * xv, axis=-1, keepdims=True) + EPS)
        xh = xv * r
        dyv = dy_ref[...].astype(F32) * alpha
        gdy = dyv * g_ref[...]
        dx = r * (gdy - xh * jnp.mean(xh * gdy, axis=-1, keepdims=True))
        if has_resid:
            dx = dx + res_ref[...]
        dx_ref[...] = dx.astype(out_dtype)
        part = jnp.sum(dyv * xh, axis=0, keepdims=True)

        @pl.when(pl.program_id(0) == 0)
        def _():
            dg_ref[...] = part

        @pl.when(pl.program_id(0) > 0)
        def _():
            dg_ref[...] += part

    row = pl.BlockSpec((tm, D), lambda i: (i, 0))
    vec = pl.BlockSpec((1, D), lambda i: (0, 0))
    ins = [x, g.reshape(1, D), dy] + ([resid] if has_resid else [])
    dx, dg = pl.pallas_call(
        body, name="norm_bwd_res" if has_resid else "norm_bwd", grid=(S // tm,),
        in_specs=[row, vec, row] + ([row] if has_resid else []),
        out_specs=[row, vec],
        out_shape=[jax.ShapeDtypeStruct((S, D), out_dtype), jax.ShapeDtypeStruct((1, D), F32)],
        compiler_params=_params(),
    )(*ins)
    return dx, dg.reshape(D)


def _loss_grad(y, target):
    S, D = y.shape
    tm = _row_tile(S, 512)

    def body(y_ref, t_ref, dy_ref, l_ref):
        err = y_ref[...] - t_ref[...]
        dy_ref[...] = err * (1.0 / D)
        part = jnp.sum(jnp.sum(err * err, axis=-1, keepdims=True) * (0.5 / D), axis=0, keepdims=True)
        part = jnp.broadcast_to(part, (8, LANES))

        @pl.when(pl.program_id(0) == 0)
        def _():
            l_ref[...] = part

        @pl.when(pl.program_id(0) > 0)
        def _():
            l_ref[...] += part

    row = pl.BlockSpec((tm, D), lambda i: (i, 0))
    return pl.pallas_call(
        body, name="loss_grad", grid=(S // tm,),
        in_specs=[row, row],
        out_specs=[row, pl.BlockSpec((8, LANES), lambda i: (0, 0))],
        out_shape=[jax.ShapeDtypeStruct((S, D), F32), jax.ShapeDtypeStruct((8, LANES), F32)],
        compiler_params=_params(),
    )(y, target)


def _add_n(xs, name):
    shape = xs[0].shape
    C = shape[-1]
    R = 1
    for s in shape[:-1]:
        R *= s
    tm = _elementwise_rows(R, C)

    def body(*refs):
        acc = refs[0][...]
        for r in refs[1:-1]:
            acc = acc + r[...]
        refs[-1][...] = acc

    row = pl.BlockSpec((tm, C), lambda i: (i, 0))
    out = pl.pallas_call(
        body, name=name, grid=(R // tm,),
        in_specs=[row] * len(xs), out_specs=row,
        out_shape=jax.ShapeDtypeStruct((R, C), F32),
        compiler_params=_params(),
    )(*[x.reshape(R, C) for x in xs])
    return out.reshape(shape)


def _pair_sum(split, got):
    J, _, r, C = split.shape
    tm = _elementwise_rows(r, C)
    core = lax.axis_index("c").astype(jnp.int32).reshape(1)

    def body(core_ref, a_ref, b_ref, o_ref):
        o_ref[...] = a_ref[...] + b_ref[...]

    row = pl.BlockSpec((None, tm, C), lambda j, i, core_ref: (j, i, 0))
    return pl.pallas_call(
        body, name="pair_sum",
        grid_spec=pltpu.PrefetchScalarGridSpec(
            num_scalar_prefetch=1, grid=(J, r // tm),
            in_specs=[pl.BlockSpec((None, None, tm, C), lambda j, i, core_ref: (j, core_ref[0], i, 0)), row],
            out_specs=row),
        out_shape=jax.ShapeDtypeStruct((J, r, C), F32),
        compiler_params=_params(),
    )(core, split, got)


def _chip_sum(pair, arrived):
    _, r, C = pair.shape
    tm = _elementwise_rows(r, C)
    chip = (2 * lax.axis_index("x") + lax.axis_index("y")).astype(jnp.int32).reshape(1)

    def body(chip_ref, own_ref, a_ref, b_ref, c_ref, o_ref):
        o_ref[...] = own_ref[...] + a_ref[...] + b_ref[...] + c_ref[...]

    part = lambda k: pl.BlockSpec((None, tm, C), lambda i, chip_ref: (k, i, 0))
    return pl.pallas_call(
        body, name="chip_sum",
        grid_spec=pltpu.PrefetchScalarGridSpec(
            num_scalar_prefetch=1, grid=(r // tm,),
            in_specs=[pl.BlockSpec((None, tm, C), lambda i, chip_ref: (chip_ref[0], i, 0)), part(0), part(1), part(2)],
            out_specs=pl.BlockSpec((tm, C), lambda i, chip_ref: (i, 0))),
        out_shape=jax.ShapeDtypeStruct((r, C), F32),
        compiler_params=_params(),
    )(chip, pair, arrived, arrived, arrived)


def _rope_tables(S):
    half = ATT_HEAD_DIM // 2
    inv_freq = ROPE_THETA ** (-jnp.arange(half, dtype=F32) / half)
    ang = jnp.arange(S, dtype=F32)[:, None] * inv_freq[None, :]
    cos, sin = jnp.cos(ang), jnp.sin(ang)
    return jnp.concatenate([cos, cos], axis=-1), jnp.concatenate([-sin, sin], axis=-1)


def _rotate(x, cosf, sinf):
    return x * cosf + pltpu.roll(x, ATT_HEAD_DIM // 2, 1) * sinf


def _rope_fwd(proj, cosf, sinf):
    S = proj.shape[0]
    tm = _row_tile(S, 512)

    def body(q_ref, k_ref, v_ref, c_ref, s_ref, qo_ref, ko_ref, vo_ref):
        c, s = c_ref[...], s_ref[...]
        for h in range(ATT_HEADS):
            sl = slice(h * ATT_HEAD_DIM, (h + 1) * ATT_HEAD_DIM)
            qo_ref[:, sl] = _rotate(q_ref[:, sl], c, s).astype(BF16)
        for h in range(ATT_KV_HEADS):
            sl = slice(h * ATT_HEAD_DIM, (h + 1) * ATT_HEAD_DIM)
            ko_ref[:, sl] = _rotate(k_ref[:, sl], c, s).astype(BF16)
        vo_ref[...] = v_ref[...].astype(BF16)

    tab = pl.BlockSpec((tm, ATT_HEAD_DIM), lambda i: (i, 0))
    return pl.pallas_call(
        body, name="rope_fwd", grid=(S // tm,),
        in_specs=[pl.BlockSpec((tm, ATT_WIDTH), lambda i: (i, 0)),
                  pl.BlockSpec((tm, KV_WIDTH), lambda i: (i, OFF_KA // KV_WIDTH)),
                  pl.BlockSpec((tm, KV_WIDTH), lambda i: (i, OFF_VA // KV_WIDTH)), tab, tab],
        out_specs=[pl.BlockSpec((tm, ATT_WIDTH), lambda i: (i, 0)),
                   pl.BlockSpec((tm, KV_WIDTH), lambda i: (i, 0)),
                   pl.BlockSpec((tm, KV_WIDTH), lambda i: (i, 0))],
        out_shape=[jax.ShapeDtypeStruct((S, ATT_WIDTH), BF16), jax.ShapeDtypeStruct((S, KV_WIDTH), BF16),
                   jax.ShapeDtypeStruct((S, KV_WIDTH), BF16)],
        compiler_params=_params(),
    )(proj, proj, proj, cosf, sinf)


def _rope_bwd(dq, dk, dv, cosf, sinf):
    S = dq.shape[0]
    tm = _row_tile(S, 512)
    W = ATT_WIDTH + 2 * KV_WIDTH

    def body(q_ref, k_ref, v_ref, c_ref, s_ref, o_ref):
        c, s = c_ref[...], -s_ref[...]
        for h in range(ATT_HEADS):
            sl = slice(h * ATT_HEAD_DIM, (h + 1) * ATT_HEAD_DIM)
            o_ref[:, sl] = _rotate(q_ref[:, sl], c, s).astype(BF16)
        for h in range(ATT_KV_HEADS):
            sl = slice(h * ATT_HEAD_DIM, (h + 1) * ATT_HEAD_DIM)
            o_ref[:, ATT_WIDTH + h * ATT_HEAD_DIM:ATT_WIDTH + (h + 1) * ATT_HEAD_DIM] = (
                _rotate(k_ref[:, sl], c, s).astype(BF16))
        o_ref[:, ATT_WIDTH + KV_WIDTH:] = v_ref[...].astype(BF16)

    tab = pl.BlockSpec((tm, ATT_HEAD_DIM), lambda i: (i, 0))
    return pl.pallas_call(
        body, name="rope_bwd", grid=(S // tm,),
        in_specs=[pl.BlockSpec((tm, ATT_WIDTH), lambda i: (i, 0)),
                  pl.BlockSpec((tm, KV_WIDTH), lambda i: (i, 0)),
                  pl.BlockSpec((tm, KV_WIDTH), lambda i: (i, 0)), tab, tab],
        out_specs=pl.BlockSpec((tm, W), lambda i: (i, 0)),
        out_shape=jax.ShapeDtypeStruct((S, W), BF16),
        compiler_params=_params(),
    )(dq, dk, dv, cosf, sinf)


def _attn_probs(q_ref, k_refs, sink_ref, kh, n, nb):
    G, L, Dh = ATT_GROUP, BLK, ATT_HEAD_DIM
    q4 = jnp.concatenate([q_ref[:, (kh * G + g) * Dh:(kh * G + g + 1) * Dh] for g in range(G)], axis=0)
    kcat = jnp.concatenate([r[:, kh * Dh:(kh + 1) * Dh] for r in k_refs], axis=0)
    s = _dot_nt(q4, kcat) * (Dh ** -0.5)
    row = lax.broadcasted_iota(jnp.int32, (G * L, 3 * L), 0) % L
    col = lax.broadcasted_iota(jnp.int32, (G * L, 3 * L), 1)
    kpos = (n - 1) * L + col
    mask = (jnp.abs(col - L - row) <= L) & (kpos >= 0) & (kpos < nb * L)
    s = jnp.where(mask, s, -jnp.inf)
    sink = jnp.concatenate([jnp.broadcast_to(sink_ref[kh, :, g:g + 1], (L, 1)) for g in range(G)], axis=0)
    m = jnp.maximum(jnp.max(s, axis=-1, keepdims=True), sink)
    p = jnp.exp(s - m)
    es = jnp.exp(sink - m)
    inv = 1.0 / (jnp.sum(p, axis=-1, keepdims=True) + es)
    return q4, kcat, p * inv, es * inv


def _kv_specs(nb):
    return [pl.BlockSpec((BLK, KV_WIDTH), lambda n: (jnp.maximum(n - 1, 0), 0)),
            pl.BlockSpec((BLK, KV_WIDTH), lambda n: (n, 0)),
            pl.BlockSpec((BLK, KV_WIDTH), lambda n: (jnp.minimum(n + 1, nb - 1), 0))]


def _attn_fwd(qr, kr, va, sink):
    S = qr.shape[0]
    nb = S // BLK
    G, Dh = ATT_GROUP, ATT_HEAD_DIM

    def body(q_ref, k0, k1, k2, v0, v1, v2, sink_ref, o_ref):
        n = pl.program_id(0)
        for kh in range(ATT_KV_HEADS):
            _, _, probs, _ = _attn_probs(q_ref, (k0, k1, k2), sink_ref, kh, n, nb)
            vcat = jnp.concatenate([r[:, kh * Dh:(kh + 1) * Dh] for r in (v0, v1, v2)], axis=0)
            out = _dot(probs.astype(BF16), vcat)
            for g in range(G):
                o_ref[:, (kh * G + g) * Dh:(kh * G + g + 1) * Dh] = out[g * BLK:(g + 1) * BLK, :].astype(BF16)

    qspec = pl.BlockSpec((BLK, ATT_WIDTH), lambda n: (n, 0))
    return pl.pallas_call(
        body, name="attn_fwd", grid=(nb,),
        in_specs=[qspec] + _kv_specs(nb) + _kv_specs(nb)
        + [pl.BlockSpec((ATT_KV_HEADS, 1, G), lambda n: (0, 0, 0))],
        out_specs=qspec,
        out_shape=jax.ShapeDtypeStruct((S, ATT_WIDTH), BF16),
        compiler_params=_params(),
    )(qr, kr, kr, kr, va, va, va, sink.reshape(ATT_KV_HEADS, 1, G))


def _attn_bwd(qr, kr, va, sink, dycat):
    S = qr.shape[0]
    nb = S // BLK
    G, L, Dh = ATT_GROUP, BLK, ATT_HEAD_DIM
    SP = S + 2 * L

    def body(q_ref, k0, k1, k2, v0, v1, v2, sink_ref, do_ref, dq_ref, dk_ref, dv_ref, ds_ref):
        n = pl.program_id(0)

        @pl.when(n == 0)
        def _():
            dk_ref[...] = jnp.zeros_like(dk_ref)
            dv_ref[...] = jnp.zeros_like(dv_ref)
            ds_ref[...] = jnp.zeros_like(ds_ref)

        rows = pl.ds(pl.multiple_of(n * L, L), 3 * L)
        lane = lax.broadcasted_iota(jnp.int32, (8, LANES), 1)
        for kh in range(ATT_KV_HEADS):
            q4, kcat, probs, psink = _attn_probs(q_ref, (k0, k1, k2), sink_ref, kh, n, nb)
            vcat = jnp.concatenate([r[:, kh * Dh:(kh + 1) * Dh] for r in (v0, v1, v2)], axis=0)
            do4 = jnp.concatenate([do_ref[:, (kh * G + g) * Dh:(kh * G + g + 1) * Dh] for g in range(G)], axis=0)
            pb = probs.astype(BF16)
            dob = do4.astype(BF16)
            out = _dot(pb, vcat)
            delta = jnp.sum(do4 * out, axis=-1, keepdims=True)
            dp = _dot_nt(dob, vcat)
            dsc = (probs * (dp - delta) * (Dh ** -0.5)).astype(BF16)
            dq4 = _dot(dsc, kcat)
            for g in range(G):
                dq_ref[:, (kh * G + g) * Dh:(kh * G + g + 1) * Dh] = dq4[g * L:(g + 1) * L, :]
            dk_ref[rows, kh * Dh:(kh + 1) * Dh] += _dot_tn(dsc, q4)
            dv_ref[rows, kh * Dh:(kh + 1) * Dh] += _dot_tn(pb, dob)
            dsink = jnp.zeros((8, LANES), F32)
            for g in range(G):
                val = -jnp.sum(psink[g * L:(g + 1) * L] * delta[g * L:(g + 1) * L], axis=0, keepdims=True)
                dsink = dsink + jnp.where(lane == g, jnp.broadcast_to(val, (8, LANES)), 0.0)
            ds_ref[kh] += dsink

    qspec = pl.BlockSpec((L, ATT_WIDTH), lambda n: (n, 0))
    accspec = pl.BlockSpec((SP, KV_WIDTH), lambda n: (0, 0))
    dq, dkp, dvp, dsink = pl.pallas_call(
        body, name="attn_bwd", grid=(nb,),
        in_specs=[qspec] + _kv_specs(nb) + _kv_specs(nb)
        + [pl.BlockSpec((ATT_KV_HEADS, 1, G), lambda n: (0, 0, 0)), qspec],
        out_specs=[qspec, accspec, accspec, pl.BlockSpec((ATT_KV_HEADS, 8, LANES), lambda n: (0, 0, 0))],
        out_shape=[jax.ShapeDtypeStruct((S, ATT_WIDTH), F32), jax.ShapeDtypeStruct((SP, KV_WIDTH), F32),
                   jax.ShapeDtypeStruct((SP, KV_WIDTH), F32),
                   jax.ShapeDtypeStruct((ATT_KV_HEADS, 8, LANES), F32)],
        compiler_params=_params(),
    )(qr, kr, kr, kr, va, va, va, sink.reshape(ATT_KV_HEADS, 1, G), dycat)
    return dq, dkp[L:L + S], dvp[L:L + S], dsink[:, 0, :G].reshape(ATT_HEADS)


CONV_COLS = 256
CONV_ROWS = 512


def _conv_shifts(xs, rows):
    total = xs.shape[0]
    out = []
    for j in range(CONV_WIDTH):
        shift = (CONV_WIDTH // 2 - j) % total
        out.append((pltpu.roll(xs, shift, 0) if shift else xs)[CONV_HALO:CONV_HALO + rows, :])
    return out


def _conv_taps(xs, w_ref, rows):
    acc = None
    for j, xj in enumerate(_conv_shifts(xs, rows)):
        term = xj * w_ref[j:j + 1, :]
        acc = term if acc is None else acc + term
    return acc


def _conv_window(x_ref, i, R, nrow):
    S = x_ref.shape[0]
    r0 = pl.multiple_of(i * R, R)
    top = x_ref[pl.ds(pl.multiple_of(jnp.maximum(r0 - CONV_HALO, 0), CONV_HALO), CONV_HALO), :]
    bot = x_ref[pl.ds(pl.multiple_of(jnp.minimum(r0 + R, S - CONV_HALO), CONV_HALO), CONV_HALO), :]
    top = jnp.where(i > 0, top, 0.0)
    bot = jnp.where(i < nrow - 1, bot, 0.0)
    return jnp.concatenate([top, x_ref[pl.ds(r0, R), :], bot], axis=0)


def _conv5(x, w, *, col0, act, out_dtype, name):
    S = x.shape[0]
    C = w.shape[1]
    R = _row_tile(S, CONV_ROWS)
    tc = CONV_COLS
    half_blocks = (C // 2) // tc
    nrow = S // R

    def body(x_ref, w_ref, o_ref):
        scale = jnp.where(pl.program_id(0) >= half_blocks, M_HEAD_DIM ** -0.5, 1.0)
        y = _conv_taps(_conv_window(x_ref, pl.program_id(1), R, nrow), w_ref, R)
        if act:
            y = y * _sigmoid(y) * scale
        o_ref[...] = y.astype(out_dtype)

    return pl.pallas_call(
        body, name=name, grid=(C // tc, nrow),
        in_specs=[pl.BlockSpec((S, tc), lambda c, i: (0, col0 // tc + c)),
                  pl.BlockSpec((CONV_WIDTH, tc), lambda c, i: (0, c))],
        out_specs=pl.BlockSpec((R, tc), lambda c, i: (i, c)),
        out_shape=jax.ShapeDtypeStruct((S, C), out_dtype),
        compiler_params=_params(),
    )(x, w)


def _conv_bwd_pre(x, col0, w, dq2, dk2):
    S = x.shape[0]
    C = w.shape[1]
    R = _row_tile(S, CONV_ROWS)
    tc = CONV_COLS
    half_blocks = (C // 2) // tc
    nrow = S // R

    def body(x_ref, w_ref, dqa_ref, dqb_ref, dka_ref, dkb_ref, o_ref, dw_ref, acc):
        is_k = pl.program_id(0) >= half_blocks
        i = pl.program_id(1)
        scale = jnp.where(is_k, M_HEAD_DIM ** -0.5, 1.0)

        @pl.when(i == 0)
        def _():
            acc[...] = jnp.zeros_like(acc)

        shifted = _conv_shifts(_conv_window(x_ref, i, R, nrow), R)
        y = shifted[0] * w_ref[0:1, :]
        for j in range(1, CONV_WIDTH):
            y = y + shifted[j] * w_ref[j:j + 1, :]
        sg = _sigmoid(y)
        dqv = dqa_ref[...] + dqb_ref[...]
        dkv = dka_ref[...] + dkb_ref[...]
        dpre = jnp.where(is_k, dkv, dqv) * scale * (sg * (1.0 + y * (1.0 - sg)))
        o_ref[...] = dpre
        for j in range(CONV_WIDTH):
            acc[j:j + 1, :] += jnp.sum(dpre * shifted[j], axis=0, keepdims=True)

        @pl.when(i == nrow - 1)
        def _():
            dw_ref[...] = acc[0:CONV_WIDTH, :]

    nqb = (C // 2) // tc
    qmap = lambda d: (lambda c, i: (d, i, jnp.minimum(c, nqb - 1)))
    kmap = lambda d: (lambda c, i: (d, i, jnp.maximum(c - nqb, 0)))
    gspec = lambda m: pl.BlockSpec((None, R, tc), m)
    return pl.pallas_call(
        body, name="conv_bwd_pre", grid=(C // tc, nrow),
        in_specs=[pl.BlockSpec((S, tc), lambda c, i: (0, col0 // tc + c)),
                  pl.BlockSpec((CONV_WIDTH, tc), lambda c, i: (0, c)),
                  gspec(qmap(0)), gspec(qmap(1)), gspec(kmap(0)), gspec(kmap(1))],
        out_specs=[pl.BlockSpec((R, tc), lambda c, i: (i, c)), pl.BlockSpec((CONV_WIDTH, tc), lambda c, i: (0, c))],
        out_shape=[jax.ShapeDtypeStruct((S, C), F32), jax.ShapeDtypeStruct((CONV_WIDTH, C), F32)],
        scratch_shapes=[pltpu.VMEM((8, tc), F32)],
        compiler_params=_params(),
    )(x, w, dq2, dq2, dk2, dk2)


def _lane_cumsum(x, reverse):
    lane = lax.broadcasted_iota(jnp.int32, x.shape, 1)
    sh = 1
    while sh < LANES:
        if reverse:
            x = x + jnp.where(lane < LANES - sh, pltpu.roll(x, LANES - sh, 1), 0.0)
        else:
            x = x + jnp.where(lane >= sh, pltpu.roll(x, sh, 1), 0.0)
        sh *= 2
    return x


def _gate_prep(gates_t, bias):
    R = gates_t.shape[0]
    half, quarter = R // 2, R // 4

    def body(g_ref, b_ref, ig_ref, cum_ref):
        ig_ref[...] = g_ref[0:half, :] + b_ref[0:half, :]
        fg = g_ref[half:R, :] + b_ref[half:R, :]
        lf = jnp.minimum(fg, 0.0) - jnp.log(1.0 + jnp.exp(-jnp.abs(fg)))
        cum_ref[0:quarter, :] = _lane_cumsum(lf[0:quarter, :], False)
        cum_ref[quarter:half, :] = _lane_cumsum(lf[quarter:half, :], True)

    return pl.pallas_call(
        body, name="gate_prep",
        out_shape=[jax.ShapeDtypeStruct((half, LANES), F32), jax.ShapeDtypeStruct((half, LANES), F32)],
        compiler_params=_params(),
    )(gates_t, bias)


def _gate_bwd(gates_t, bias, di, dfc, dbl, nc):
    R = gates_t.shape[0]
    half, quarter = R // 2, R // 4

    def body(g_ref, b_ref, di_ref, df_ref, dbl_ref, dg_ref, db_ref):
        dg_ref[0:half, :] = di_ref[...]
        dfv = df_ref[...]
        within = jnp.concatenate([_lane_cumsum(dfv[0:quarter, :], True),
                                  _lane_cumsum(dfv[quarter:half, :], False)], axis=0)
        fg = g_ref[half:R, :] + b_ref[half:R, :]
        dg_ref[half:R, :] = (within + dbl_ref[...]) / (1.0 + jnp.exp(fg))
        rows = jnp.broadcast_to(jnp.sum(dg_ref[...], axis=-1, keepdims=True), (R, LANES))
        gr = lax.broadcasted_iota(jnp.int32, (N_GATES, R), 0)
        gc = lax.broadcasted_iota(jnp.int32, (N_GATES, R), 1)
        db_ref[...] = lax.dot_general((gc // nc == gr).astype(F32), rows, (((1,), (0,)), ((), ())),
                                      precision=lax.Precision.HIGHEST, preferred_element_type=F32)

    return pl.pallas_call(
        body, name="gate_bwd",
        out_shape=[jax.ShapeDtypeStruct((R, LANES), F32), jax.ShapeDtypeStruct((N_GATES, LANES), F32)],
        compiler_params=_params(),
    )(gates_t, bias, di, dfc, dbl)


def _chunk_index(d, c, nc, reverse):
    j = (nc - 1 - c) if reverse else c
    return j + d * (nc - 1 - 2 * j)


def _mlstm_chunk(d, q, k, vb, brow, igrow, C, nvec, m_prev, eye):
    L = BLK
    r = lax.broadcasted_iota(jnp.int32, (L, L), 0)
    c = lax.broadcasted_iota(jnp.int32, (L, L), 1)
    mask = (r - c) * (1 - 2 * d) >= 0
    bcol = _row_to_col(brow, eye)
    igcol = _row_to_col(igrow, eye)
    log_d = jnp.where(mask, bcol - brow + igrow, -jnp.inf)
    log_inter = bcol + m_prev
    m_t = jnp.maximum(log_inter, jnp.max(log_d, axis=-1, keepdims=True))
    d_mat = jnp.exp(log_d - m_t)
    inter = jnp.exp(log_inter - m_t)
    s = _dot_nt(q, k) * d_mat
    sb = s.astype(BF16)
    cb = C.astype(BF16)
    num = _dot(sb, vb) + inter * _dot_nt(q, cb)
    den = jnp.sum(s, axis=-1, keepdims=True) + inter * jnp.sum(q.astype(F32) * nvec, axis=-1, keepdims=True)
    floor = jnp.exp(-m_t)
    denom = jnp.maximum(jnp.abs(den), floor)
    b_last = jnp.where(d == 0, brow[:, L - 1:L], brow[:, 0:1])
    return dict(bcol=bcol, igcol=igcol, m_t=m_t, d_mat=d_mat, inter=inter, sb=sb, cb=cb, num=num, den=den,
                floor=floor, denom=denom, b_last=b_last)


def _head_cols(v0_ref, v1_ref, hd):
    ref = v0_ref if hd < M_HEADS // 2 else v1_ref
    lo = (hd % (M_HEADS // 2)) * M_HEAD_DIM
    return ref[:, lo:lo + M_HEAD_DIM]


def _mlstm_fwd(qk, proj, ig5, b5):
    S = qk.shape[0]
    nc = S // BLK
    L, Dh, H = BLK, M_HEAD_DIM, M_HEADS

    def body(q_ref, k_ref, v0_ref, v1_ref, ig_ref, b_ref, h_ref, cs_ref, ns_ref, ms_ref, C, nvec, m):
        d = pl.program_id(0)

        @pl.when(pl.program_id(1) == 0)
        def _():
            C[...] = jnp.zeros_like(C)
            nvec[...] = jnp.zeros_like(nvec)
            m[...] = jnp.zeros_like(m)

        eye = _eye_mask(L)
        for hd in range(H):
            cols = slice(hd * Dh, (hd + 1) * Dh)
            q, k = q_ref[:, cols], k_ref[:, cols]
            vf = _head_cols(v0_ref, v1_ref, hd)
            brow, igrow = b_ref[hd], ig_ref[hd]
            m_prev = m[hd, :, 0:1]
            cs_ref[hd] = C[hd].astype(BF16)
            ns_ref[hd] = nvec[hd]
            ms_ref[hd] = m[hd]
            f = _mlstm_chunk(d, q, k, vf.astype(BF16), brow, igrow, C[hd], nvec[hd], m_prev, eye)
            h_ref[:, cols] = f["num"] / f["denom"]
            b_last = f["b_last"]
            log_w = b_last - brow + igrow
            m_new = jnp.maximum(b_last + m_prev, jnp.max(log_w, axis=-1, keepdims=True))
            w_col = jnp.exp(b_last - f["bcol"] + f["igcol"] - m_new)
            decay = jnp.exp(b_last + m_prev - m_new)
            C[hd] = decay * C[hd] + _dot_tn((w_col * vf).astype(BF16), k)
            nvec[hd] = decay * nvec[hd] + jnp.sum(w_col * k.astype(F32), axis=0, keepdims=True)
            m[hd] = jnp.broadcast_to(m_new, (1, LANES))

    cidx = lambda d, c: _chunk_index(d, c, nc, False)
    gspec = pl.BlockSpec((None, H, None, 1, LANES), lambda d, c: (d, 0, cidx(d, c), 0, 0))
    st = lambda *blk: pl.BlockSpec((None, H, None) + blk, lambda d, c: (d, 0, cidx(d, c), 0, 0))
    half = M_WIDTH // 2
    return pl.pallas_call(
        body, name="mlstm_fwd", grid=(2, nc),
        in_specs=[pl.BlockSpec((L, M_WIDTH), lambda d, c: (cidx(d, c), 0)),
                  pl.BlockSpec((L, M_WIDTH), lambda d, c: (cidx(d, c), 1)),
                  pl.BlockSpec((L, half), lambda d, c: (cidx(d, c), OFF_VM // half)),
                  pl.BlockSpec((L, half), lambda d, c: (cidx(d, c), OFF_VM // half + 1)), gspec, gspec],
        out_specs=[pl.BlockSpec((None, L, M_WIDTH), lambda d, c: (d, cidx(d, c), 0)),
                   st(Dh, Dh), st(1, Dh), st(1, LANES)],
        out_shape=[jax.ShapeDtypeStruct((2, S, M_WIDTH), F32), jax.ShapeDtypeStruct((2, H, nc, Dh, Dh), BF16),
                   jax.ShapeDtypeStruct((2, H, nc, 1, Dh), F32), jax.ShapeDtypeStruct((2, H, nc, 1, LANES), F32)],
        scratch_shapes=[pltpu.VMEM((H, Dh, Dh), F32), pltpu.VMEM((H, 1, Dh), F32), pltpu.VMEM((H, 1, LANES), F32)],
        compiler_params=_params(),
    )(qk, qk, proj, proj, ig5, b5)


def _mlstm_bwd(qk, proj, ig5, b5, cs, ns, ms, dhm):
    S = qk.shape[0]
    nc = S // BLK
    L, Dh, H = BLK, M_HEAD_DIM, M_HEADS

    def body(q_ref, k_ref, v0_ref, v1_ref, ig_ref, b_ref, cs_ref, ns_ref, ms_ref, dh_ref,
             dq_ref, dk_ref, dv_ref, di_ref, df_ref, dbl_ref, R, rvec, mu):
        d = pl.program_id(0)

        @pl.when(pl.program_id(1) == 0)
        def _():
            R[...] = jnp.zeros_like(R)
            rvec[...] = jnp.zeros_like(rvec)
            mu[...] = jnp.full(mu.shape, NEG_BIG, F32)

        eye = _eye_mask(L)
        for hd in range(H):
            cols = slice(hd * Dh, (hd + 1) * Dh)
            q, k = q_ref[:, cols], k_ref[:, cols]
            qf, kf = q.astype(F32), k.astype(F32)
            vb = _head_cols(v0_ref, v1_ref, hd).astype(BF16)
            brow, igrow = b_ref[hd], ig_ref[hd]
            nprev = ns_ref[hd]
            m_prev = ms_ref[hd, :, 0:1]
            cprev = cs_ref[hd]
            f = _mlstm_chunk(d, q, k, vb, brow, igrow, cprev, nprev, m_prev, eye)
            dh = dh_ref[:, cols]
            inv = 1.0 / f["denom"]
            hcur = f["num"] * inv
            dnum = dh * inv
            active = jnp.abs(f["den"]) >= f["floor"]
            dden = (jnp.where(active, -jnp.sign(f["den"]), 0.0)
                    * jnp.sum(dh * hcur, axis=-1, keepdims=True) * inv)
            dnb = dnum.astype(BF16)
            dqk = ((_dot_nt(dnb, vb) + dden) * f["d_mat"]).astype(BF16)
            dq = _dot(dqk, k) + f["inter"] * (_dot(dnb, f["cb"]) + dden * nprev)
            mu_prev = mu[hd, :, 0:1]
            a_col = jnp.exp(f["b_last"] - f["bcol"] + f["igcol"] + mu_prev)
            rb = R[hd].astype(BF16)
            dv = _dot_tn(f["sb"], dnb) + a_col * _dot_nt(k, rb)
            dk_inter = a_col * (_dot(vb, rb) + rvec[hd])
            dk = _dot_tn(dqk, q) + dk_inter
            dq_ref[:, cols] = dq
            dk_ref[:, cols] = dk
            dv_ref[:, cols] = dv
            kdk = jnp.sum(kf * dk, axis=-1, keepdims=True)
            qdq = jnp.sum(qf * dq, axis=-1, keepdims=True)
            di_ref[hd] = _col_to_row(kdk, eye)
            df_ref[hd] = _col_to_row(qdq - kdk, eye)
            older = jnp.sum(jnp.sum(R[hd] * cprev.astype(F32), axis=-1, keepdims=True), axis=0, keepdims=True)
            older = older + jnp.sum(rvec[hd] * nprev, axis=-1, keepdims=True)
            dbl = (jnp.sum(jnp.sum(kf * dk_inter, axis=-1, keepdims=True), axis=0, keepdims=True)
                   + jnp.exp(f["b_last"] + mu_prev + m_prev) * older)
            dbl_ref[hd] = jnp.broadcast_to(dbl, (1, LANES))
            lw = f["bcol"] - f["m_t"]
            mu_new = jnp.maximum(f["b_last"] + mu_prev, jnp.max(lw, axis=0, keepdims=True))
            wq = jnp.exp(lw - mu_new)
            decay = jnp.exp(f["b_last"] + mu_prev - mu_new)
            R[hd] = decay * R[hd] + _dot_tn((wq * dnum).astype(BF16), q)
            rvec[hd] = decay * rvec[hd] + jnp.sum(wq * dden * qf, axis=0, keepdims=True)
            mu[hd] = jnp.broadcast_to(mu_new, (1, LANES))

    cidx = lambda d, c: _chunk_index(d, c, nc, True)
    gin = pl.BlockSpec((None, H, None, 1, LANES), lambda d, c: (d, 0, cidx(d, c), 0, 0))
    st = lambda *blk: pl.BlockSpec((None, H, None) + blk, lambda d, c: (d, 0, cidx(d, c), 0, 0))
    per_dir = pl.BlockSpec((None, L, M_WIDTH), lambda d, c: (d, cidx(d, c), 0))
    big = jax.ShapeDtypeStruct((2, S, M_WIDTH), F32)
    small = jax.ShapeDtypeStruct((2, H, nc, 1, LANES), F32)
    half = M_WIDTH // 2
    return pl.pallas_call(
        body, name="mlstm_bwd", grid=(2, nc),
        in_specs=[pl.BlockSpec((L, M_WIDTH), lambda d, c: (cidx(d, c), 0)),
                  pl.BlockSpec((L, M_WIDTH), lambda d, c: (cidx(d, c), 1)),
                  pl.BlockSpec((L, half), lambda d, c: (cidx(d, c), OFF_VM // half)),
                  pl.BlockSpec((L, half), lambda d, c: (cidx(d, c), OFF_VM // half + 1)), gin, gin,
                  st(Dh, Dh), st(1, Dh), st(1, LANES),
                  pl.BlockSpec((L, M_WIDTH), lambda d, c: (cidx(d, c), 0))],
        out_specs=[per_dir, per_dir, per_dir, gin, gin, gin],
        out_shape=[big, big, big, small, small, small],
        scratch_shapes=[pltpu.VMEM((H, Dh, Dh), F32), pltpu.VMEM((H, 1, Dh), F32), pltpu.VMEM((H, 1, LANES), F32)],
        compiler_params=_params(),
    )(qk, qk, proj, proj, ig5, b5, cs, ns, ms, dhm)


def _mlstm_out_fwd(hs, proj, gamma):
    S = hs.shape[1]
    tm = _row_tile(S, 512)
    Dh, H = M_HEAD_DIM, M_HEADS

    def body(hf_ref, hb_ref, o_ref, g_ref, y_ref):
        hm = hf_ref[...] + hb_ref[...]
        r = lax.rsqrt(jnp.mean(hm * hm, axis=-1, keepdims=True) + EPS)
        y_ref[...] = (_sigmoid(o_ref[...]) * (hm * r * g_ref[...])).astype(BF16)

    hspec = lambda d: pl.BlockSpec((None, tm, Dh), lambda i, h: (d, i, h))
    return pl.pallas_call(
        body, name="mlstm_out_fwd", grid=(S // tm, H),
        in_specs=[hspec(0), hspec(1), pl.BlockSpec((tm, Dh), lambda i, h: (i, OFF_OM // Dh + h)),
                  pl.BlockSpec((1, Dh), lambda i, h: (0, h))],
        out_specs=pl.BlockSpec((tm, Dh), lambda i, h: (i, h)),
        out_shape=jax.ShapeDtypeStruct((S, M_WIDTH), BF16),
        compiler_params=_params(),
    )(hs, hs, proj, gamma.reshape(1, M_WIDTH))


def _mlstm_out_bwd(hs, proj, gamma, dycat):
    S = hs.shape[1]
    tm = _row_tile(S, 512)
    Dh, H = M_HEAD_DIM, M_HEADS
    D_OFF = ATT_WIDTH // Dh

    def body(hf_ref, hb_ref, o_ref, g_ref, dy_ref, dh_ref, do_ref, dg_ref):
        hm = hf_ref[...] + hb_ref[...]
        r = lax.rsqrt(jnp.mean(hm * hm, axis=-1, keepdims=True) + EPS)
        hh = hm * r
        so = _sigmoid(o_ref[...])
        dy = dy_ref[...]
        gam = g_ref[...]
        do_ref[...] = (dy * hh * gam * so * (1.0 - so)).astype(BF16)
        dyn = dy * so
        gd = dyn * gam
        dh_ref[...] = r * (gd - hh * jnp.mean(hh * gd, axis=-1, keepdims=True))
        part = jnp.sum(dyn * hh, axis=0, keepdims=True)

        @pl.when(pl.program_id(1) == 0)
        def _():
            dg_ref[...] = part

        @pl.when(pl.program_id(1) > 0)
        def _():
            dg_ref[...] += part

    hspec = lambda d: pl.BlockSpec((None, tm, Dh), lambda h, i: (d, i, h))
    out = pl.BlockSpec((tm, Dh), lambda h, i: (i, h))
    vec = pl.BlockSpec((1, Dh), lambda h, i: (0, h))
    dhm, do, dg = pl.pallas_call(
        body, name="mlstm_out_bwd", grid=(H, S // tm),
        in_specs=[hspec(0), hspec(1), pl.BlockSpec((tm, Dh), lambda h, i: (i, OFF_OM // Dh + h)), vec,
                  pl.BlockSpec((tm, Dh), lambda h, i: (i, D_OFF + h))],
        out_specs=[out, out, vec],
        out_shape=[jax.ShapeDtypeStruct((S, M_WIDTH), F32), jax.ShapeDtypeStruct((S, M_WIDTH), BF16),
                   jax.ShapeDtypeStruct((1, M_WIDTH), F32)],
        compiler_params=_params(),
    )(hs, hs, proj, gamma.reshape(1, M_WIDTH), dycat)
    return dhm, do, dg.reshape(M_WIDTH)


def _sum_cast(a2, name):
    _, S, C = a2.shape
    tm = _row_tile(S, 512)

    def body(a_ref, b_ref, o_ref):
        o_ref[...] = (a_ref[...] + b_ref[...]).astype(BF16)

    spec = lambda d: pl.BlockSpec((None, tm, C), lambda i: (d, i, 0))
    return pl.pallas_call(
        body, name=name, grid=(S // tm,),
        in_specs=[spec(0), spec(1)], out_specs=pl.BlockSpec((tm, C), lambda i: (i, 0)),
        out_shape=jax.ShapeDtypeStruct((S, C), BF16),
        compiler_params=_params(),
    )(a2, a2)


def _adamw(w, g, m, v):
    shape = w.shape
    C = shape[-1]
    R = w.size // C
    tm = _elementwise_rows(R, C)

    def body(w_ref, g_ref, m_ref, v_ref, d_ref, mo_ref, vo_ref):
        gv = g_ref[...]
        mn = ADAM_B1 * m_ref[...] + (1.0 - ADAM_B1) * gv
        vn = ADAM_B2 * v_ref[...] + (1.0 - ADAM_B2) * (gv * gv)
        m_hat = mn / (1.0 - ADAM_B1 ** ADAM_STEP)
        v_hat = vn / (1.0 - ADAM_B2 ** ADAM_STEP)
        d_ref[...] = -ADAM_LR * (m_hat / (jnp.sqrt(v_hat) + ADAM_EPS) + ADAM_WD * w_ref[...])
        mo_ref[...] = mn
        vo_ref[...] = vn

    row = pl.BlockSpec((tm, C), lambda i: (i, 0))
    sds = jax.ShapeDtypeStruct((R, C), F32)
    outs = pl.pallas_call(
        body, name="adamw", grid=(R // tm,),
        in_specs=[row] * 4, out_specs=[row] * 3, out_shape=[sds] * 3,
        compiler_params=_params(),
    )(*[t.reshape(R, C) for t in (w, g, m, v)])
    return tuple(o.reshape(shape) for o in outs)


CID_WEIGHT_GATHER, CID_PAIR_SWAP, CID_SCATTER, CID_SMALL = 0, 1, 2, 3


def _other_chips(x, y):
    return [(1 - x, y), (x, 1 - y), (1 - x, 1 - y)]


def _handshake(peers):
    barrier = pltpu.get_barrier_semaphore()
    for peer in peers:
        pl.semaphore_signal(barrier, inc=1, device_id=peer, device_id_type=MESH)
    pl.semaphore_wait(barrier, len(peers))


def _sequencer_call(body, name, out_type, sem_counts, collective_id, operands):
    return pl.kernel(
        body, name=name, out_type=out_type,
        mesh=plsc.ScalarSubcoreMesh(axis_name="sequencer", num_cores=1),
        scratch_types=[pltpu.SemaphoreType.DMA((n,)) for n in sem_counts],
        compiler_params=pltpu.CompilerParams(collective_id=collective_id),
    )(*operands)


def _comm_weight_gather(groups):
    T = len(groups)
    flat = [(t, sum(p.shape[1] for p in g[:i]), part) for t, g in enumerate(groups) for i, part in enumerate(g)]
    N = len(flat)

    def body(*refs):
        ins, outs = refs[:N], refs[N:N + T]
        send, recv, local = refs[N + T:]
        x, y, c = lax.axis_index("x"), lax.axis_index("y"), lax.axis_index("c")
        me = 2 * x + y
        chips = _other_chips(x, y)
        _handshake([(px, py, c) for px, py in chips])

        def place(i, chip):
            t, off, part = flat[i]
            return outs[t].at[chip, :, pl.ds(off, part.shape[1])]

        mine = []
        for i in range(N):
            cp = pltpu.make_async_copy(ins[i], place(i, me), local.at[i])
            cp.start()
            mine.append(cp)
            for k, (px, py) in enumerate(chips):
                pltpu.make_async_remote_copy(
                    src_ref=ins[i], dst_ref=place(i, me), send_sem=send.at[3 * i + k], recv_sem=recv.at[3 * i + k],
                    device_id=(px, py, c), device_id_type=MESH).start()
        for i in range(N):
            for k, (px, py) in enumerate(chips):
                cp = pltpu.make_async_remote_copy(
                    src_ref=ins[i], dst_ref=place(i, 2 * px + py), send_sem=send.at[3 * i + k],
                    recv_sem=recv.at[3 * i + k], device_id=(px, py, c), device_id_type=MESH)
                cp.wait_send()
                cp.wait_recv()
            mine[i].wait()

    out_type = tuple(jax.ShapeDtypeStruct((4, g[0].shape[0], sum(p.shape[1] for p in g)), g[0].dtype) for g in groups)
    return _sequencer_call(
        body, "comm_weight_gather", out_type, (3 * N, 3 * N, N), CID_WEIGHT_GATHER, [part for _, _, part in flat])


def _comm_to_sibling(xs, other_half=False):
    T = len(xs)

    def body(*refs):
        ins, outs = refs[:T], refs[T:2 * T]
        send, recv = refs[2 * T:]
        x, y, c = lax.axis_index("x"), lax.axis_index("y"), lax.axis_index("c")
        _handshake([(x, y, 1 - c)])
        cps = [pltpu.make_async_remote_copy(
            src_ref=ins[t].at[:, 1 - c] if other_half else ins[t], dst_ref=outs[t], send_sem=send.at[t],
            recv_sem=recv.at[t], device_id=(x, y, 1 - c), device_id_type=MESH) for t in range(T)]
        for cp in cps:
            cp.start()
        for cp in cps:
            cp.wait_send()
            cp.wait_recv()

    shapes = [s.shape[:1] + s.shape[2:] if other_half else s.shape for s in xs]
    return _sequencer_call(
        body, "comm_pair_swap", tuple(jax.ShapeDtypeStruct(sh, s.dtype) for sh, s in zip(shapes, xs)),
        (T, T), CID_PAIR_SWAP, xs)


def _comm_scatter_to_owners(ps):
    T = len(ps)

    def body(*refs):
        ins, outs = refs[:T], refs[T:2 * T]
        send, recv = refs[2 * T:]
        x, y, c = lax.axis_index("x"), lax.axis_index("y"), lax.axis_index("c")
        chips = _other_chips(x, y)
        _handshake([(px, py, c) for px, py in chips])
        cps = []
        for t in range(T):
            for k, (px, py) in enumerate(chips):
                cp = pltpu.make_async_remote_copy(
                    src_ref=ins[t].at[2 * px + py], dst_ref=outs[t].at[k], send_sem=send.at[3 * t + k],
                    recv_sem=recv.at[3 * t + k], device_id=(px, py, c), device_id_type=MESH)
                cp.start()
                cps.append(cp)
        for cp in cps:
            cp.wait_send()
            cp.wait_recv()

    return _sequencer_call(
        body, "comm_scatter_to_owners", tuple(jax.ShapeDtypeStruct((3,) + p.shape[1:], p.dtype) for p in ps),
        (3 * T, 3 * T), CID_SCATTER, ps)


def _comm_all_to_all_small(pack):
    R = pack.shape[0]

    def body(in_ref, out_ref, send, recv, local):
        x, y, c = lax.axis_index("x"), lax.axis_index("y"), lax.axis_index("c")
        me = 4 * x + 2 * y + c
        peers = []
        for k in range(1, 8):
            fx, fy, fc = (k >> 2) & 1, (k >> 1) & 1, k & 1
            peers.append((x + fx * (1 - 2 * x), y + fy * (1 - 2 * y), c + fc * (1 - 2 * c)))
        _handshake(peers)
        mine = pltpu.make_async_copy(in_ref, out_ref.at[me], local.at[0])
        mine.start()
        for k, peer in enumerate(peers):
            pltpu.make_async_remote_copy(
                src_ref=in_ref, dst_ref=out_ref.at[me], send_sem=send.at[k], recv_sem=recv.at[k],
                device_id=peer, device_id_type=MESH).start()
        for k, (px, py, pc) in enumerate(peers):
            cp = pltpu.make_async_remote_copy(
                src_ref=in_ref, dst_ref=out_ref.at[4 * px + 2 * py + pc], send_sem=send.at[k], recv_sem=recv.at[k],
                device_id=(px, py, pc), device_id_type=MESH)
            cp.wait_send()
            cp.wait_recv()
        mine.wait()

    return _sequencer_call(
        body, "comm_all_to_all_small", (jax.ShapeDtypeStruct((8, R, LANES), F32),), (7, 7, 1), CID_SMALL, [pack])[0]


def _tie(*trees):
    return lax.optimization_barrier(trees)


class _ReduceScatter:
    def __init__(self, grads, dh):
        split = [g.reshape(4, 2, g.shape[1] // 2, g.shape[2]) for g in grads]
        self.split, self.dh = _tie(split, dh)
        self.got = _comm_to_sibling(self.split, other_half=True)

    def scatter(self, dh):
        split, got, dh = _tie(self.split, list(self.got), dh)
        pair = [_pair_sum(a, b) for a, b in zip(split, got)]
        self.pair, dh = _tie(pair, dh)
        self.arrived = _comm_scatter_to_owners(self.pair)
        return dh

    def gather(self, dh):
        pair, arrived, dh = _tie(self.pair, list(self.arrived), dh)
        halves = [_chip_sum(p, a) for p, a in zip(pair, arrived)]
        self.halves, dh = _tie(halves, dh)
        self.other = _comm_to_sibling(self.halves)
        return dh

    def settle(self, dh):
        self.other, dh = _tie(list(self.other), dh)
        return dh

    def result(self):
        south = lax.axis_index("c") == 0
        return [jnp.where(south, jnp.concatenate([mine, theirs]), jnp.concatenate([theirs, mine]))
                for mine, theirs in zip(self.halves, self.other)]


def _ffn_fwd(x, g_pre, g_post, wgu, wd):
    S, D = x.shape
    F = wd.shape[1]
    xn = _norm_fwd(x, g_pre)
    gu, h = _ffn_up(xn, wgu)
    wd, h = _tie(wd, h)
    f = _mm(h, wd, "nn", ja="r", jb="r", tm=_row_tile(S, ACT_ROWS), tn=D, tk=F, name="ffn_down")
    x_out = _resid_norm_fwd(x, f, g_post, 0.5)
    return x_out, (x, xn, gu, h, f), wd


def _ffn_bwd(dx, saved, g_pre, g_post, wgu, wd, mid=None, on_grads=None, late=None):
    x, xn, gu, h, f = saved
    S, D = x.shape
    J, F, _ = wd.shape
    tm, tk = _row_tile(S, ACT_ROWS), _row_tile(S, TOKEN_ROWS)
    df, dg_post = _norm_bwd(f, g_post, dx, 0.5, out_dtype=BF16)
    d_wd = _mm(h, df, "tn", ja="b", tm=F, tn=MODEL_COLS, tk=tk, name="ffn_dwd")
    dgu = _ffn_bwd_hidden(df, wd, gu)
    if mid is not None:
        dgu = mid(dgu)
    d_wgu = _mm(xn, dgu, "tn", jb="b", tm=MODEL_COLS, tn=F, tk=tk, name="ffn_dwgu")
    if on_grads is not None:
        dgu = on_grads(d_wgu, d_wd, dgu)
    dxn = _mm(dgu, wgu, "nt", ja="r", jb="r", tm=tm, tn=MODEL_COLS, tk=2 * F, name="ffn_dxn")
    if late is not None:
        dxn = late(dxn)
    dx_in, dg_pre = _norm_bwd(x, g_pre, dxn, 1.0, resid=dx)
    return dx_in, dg_pre, dg_post, d_wgu, d_wd


def _gates_layout(proj, b_gate, nc):
    gt = jnp.transpose(proj[:, OFF_G:OFF_G + N_GATES]).reshape(N_GATES * nc, LANES)
    bias = jnp.repeat(b_gate, nc).reshape(N_GATES * nc, 1)
    return gt, bias


def _mixer_fwd(x, p, cosf, sinf):
    S, D = x.shape
    nc = S // BLK
    hn = _norm_fwd(x, p["g_pre"])
    proj = _mm(hn, p["w_in"], "nn", tm=_row_tile(S, ACT_ROWS), tn=IN_PAD // 5, tk=D, name="in_proj")
    qr, kr, va = _rope_fwd(proj, cosf, sinf)
    y_att = _attn_fwd(qr, kr, va, p["sink"])
    qk = _conv5(proj, p["conv_w"], col0=OFF_QM, act=True, out_dtype=BF16, name="conv_silu")
    gt, bias = _gates_layout(proj, p["b_gate"], nc)
    ig, cum = _gate_prep(gt, bias)
    ig5, b5 = ig.reshape(2, M_HEADS, nc, 1, LANES), cum.reshape(2, M_HEADS, nc, 1, LANES)
    hs, cs, ns, ms = _mlstm_fwd(qk, proj, ig5, b5)
    y_m = _mlstm_out_fwd(hs, proj, p["gamma"])
    ycat = jnp.concatenate([y_att, y_m], axis=-1)
    w_out, ycat = _tie(p["w_out"], ycat)
    p["w_out"] = w_out
    mix = _mm(ycat, w_out, "nn", tm=_row_tile(S, ACT_ROWS), tn=MODEL_COLS, tk=D, name="out_proj")
    x_out = _resid_norm_fwd(x, mix, p["g_post"], 1.0)
    return x_out, (x, hn, proj, qr, kr, va, qk, gt, bias, ig5, b5, hs, cs, ns, ms, ycat, mix)


def _mixer_bwd(dx, saved, p, cosf, sinf, mid=None):
    x, hn, proj, qr, kr, va, qk, gt, bias, ig5, b5, hs, cs, ns, ms, ycat, mix = saved
    S, D = x.shape
    nc = S // BLK
    tm, tk = _row_tile(S, ACT_ROWS), _row_tile(S, TOKEN_ROWS)
    dmix, dg_post = _norm_bwd(mix, p["g_post"], dx, 1.0, out_dtype=BF16)
    d_wout = _mm(ycat, dmix, "tn", tm=MODEL_COLS, tn=MODEL_COLS, tk=tk, name="dw_out")
    dycat = _mm(dmix, p["w_out"], "nt", tm=tm, tn=MODEL_COLS, tk=D, name="d_ycat")
    if mid is not None:
        dycat = mid(dycat)
    dhm, d_om, d_gamma = _mlstm_out_bwd(hs, proj, p["gamma"], dycat)
    dq2, dk2, dv2, di, dfc, dbl = _mlstm_bwd(qk, proj, ig5, b5, cs, ns, ms, dhm)
    d_vm = _sum_cast(dv2, "dv_sum")
    dgt, db = _gate_bwd(gt, bias, di.reshape(-1, LANES), dfc.reshape(-1, LANES), dbl.reshape(-1, LANES), nc)
    dpre, d_conv = _conv_bwd_pre(proj, OFF_QM, p["conv_w"], dq2, dk2)
    d_qkm = _conv5(dpre, p["conv_w"][::-1], col0=0, act=False, out_dtype=BF16, name="conv_bwd_x")
    d_gates = jnp.transpose(dgt.reshape(N_GATES, S)).astype(BF16)
    d_gates = jnp.pad(d_gates, ((0, 0), (0, IN_PAD - IN_WIDTH)))
    dqr, dkr, dva, d_sink = _attn_bwd(qr, kr, va, p["sink"], dycat)
    d_att = _rope_bwd(dqr, dkr, dva, cosf, sinf)
    dproj = jnp.concatenate([d_att, d_qkm, d_vm, d_om, d_gates], axis=-1)
    d_win = _mm(hn, dproj, "tn", tm=MODEL_COLS, tn=IN_PAD // 5, tk=tk, name="dw_in")
    dhn = _mm(dproj, p["w_in"], "nt", tm=tm, tn=MODEL_COLS, tk=IN_PAD // 3, name="d_hn")
    dx_in, dg_pre = _norm_bwd(x, p["g_pre"], dhn, 1.0, resid=dx)
    small = dict(g_pre=dg_pre, g_post=dg_post, gamma=d_gamma, conv=d_conv, b_gate=db[:, 0], sink=d_sink)
    return dx_in, small, d_win, d_wout


def _pad_lanes(v):
    v = v.reshape(-1)
    return jnp.pad(v, (0, (-v.shape[0]) % LANES))


def kernel(x, ffn1_norm_pre, ffn1_norm_post, ffn1_w_gate, ffn1_w_up, ffn1_w_down, mix_norm_pre, mix_norm_post, w_in, b_gate, conv_w, attn_sink, mlstm_norm, w_out, ffn2_norm_pre, ffn2_norm_post, ffn2_w_gate, ffn2_w_up, ffn2_w_down, loss_target, m_ffn1_norm_pre, m_ffn1_norm_post, m_ffn1_w_gate, m_ffn1_w_up, m_ffn1_w_down, m_mix_norm_pre, m_mix_norm_post, m_w_in, m_b_gate, m_conv_w, m_attn_sink, m_mlstm_norm, m_w_out, m_ffn2_norm_pre, m_ffn2_norm_post, m_ffn2_w_gate, m_ffn2_w_up, m_ffn2_w_down, v_ffn1_norm_pre, v_ffn1_norm_post, v_ffn1_w_gate, v_ffn1_w_up, v_ffn1_w_down, v_mix_norm_pre, v_mix_norm_post, v_w_in, v_b_gate, v_conv_w, v_attn_sink, v_mlstm_norm, v_w_out, v_ffn2_norm_pre, v_ffn2_norm_post, v_ffn2_w_gate, v_ffn2_w_up, v_ffn2_w_down):
    names = ["ffn1_norm_pre", "ffn1_norm_post", "ffn1_w_gate", "ffn1_w_up", "ffn1_w_down", "mix_norm_pre",
             "mix_norm_post", "w_in", "b_gate", "conv_w", "attn_sink", "mlstm_norm", "w_out", "ffn2_norm_pre",
             "ffn2_norm_post", "ffn2_w_gate", "ffn2_w_up", "ffn2_w_down"]
    w = dict(zip(names, [ffn1_norm_pre, ffn1_norm_post, ffn1_w_gate, ffn1_w_up, ffn1_w_down, mix_norm_pre,
                         mix_norm_post, w_in, b_gate, conv_w, attn_sink, mlstm_norm, w_out, ffn2_norm_pre,
                         ffn2_norm_post, ffn2_w_gate, ffn2_w_up, ffn2_w_down]))
    mom_m = dict(zip(names, [m_ffn1_norm_pre, m_ffn1_norm_post, m_ffn1_w_gate, m_ffn1_w_up, m_ffn1_w_down,
                             m_mix_norm_pre, m_mix_norm_post, m_w_in, m_b_gate, m_conv_w, m_attn_sink,
                             m_mlstm_norm, m_w_out, m_ffn2_norm_pre, m_ffn2_norm_post, m_ffn2_w_gate,
                             m_ffn2_w_up, m_ffn2_w_down]))
    mom_v = dict(zip(names, [v_ffn1_norm_pre, v_ffn1_norm_post, v_ffn1_w_gate, v_ffn1_w_up, v_ffn1_w_down,
                             v_mix_norm_pre, v_mix_norm_post, v_w_in, v_b_gate, v_conv_w, v_attn_sink,
                             v_mlstm_norm, v_w_out, v_ffn2_norm_pre, v_ffn2_norm_post, v_ffn2_w_gate,
                             v_ffn2_w_up, v_ffn2_w_down]))
    xs = x[0]
    target = loss_target[0]
    S, D = xs.shape
    depth = w_in.shape[0]
    F = ffn1_w_gate.shape[-1]
    in_shard = w_in.shape[-1]
    conv_shard = conv_w.shape[-1]
    chip = 2 * lax.axis_index("x") + lax.axis_index("y")
    cosf, sinf = _rope_tables(S)

    gathered = []
    for l in range(depth):
        half = lambda name: w[name][l].astype(BF16)
        parts = dict(wgu1=[half("ffn1_w_gate"), half("ffn1_w_up")], wd1=[half("ffn1_w_down")],
                     win=[half("w_in")], conv=[jnp.pad(w["conv_w"][l], ((0, 8 - CONV_WIDTH), (0, 0)))],
                     wout=[half("w_out")], wgu2=[half("ffn2_w_gate"), half("ffn2_w_up")], wd2=[half("ffn2_w_down")])
        kernels = ([["wgu1"], ["wd1"], ["win", "conv"], ["wout"], ["wgu2"], ["wd2"]] if l == 0
                   else [["wgu1", "wd1"], ["win", "conv", "wout", "wgu2", "wd2"]])
        got = {}
        for names_k in kernels:
            got.update(zip(names_k, _comm_weight_gather([parts[n] for n in names_k])))
        gathered.append(got)

    def mixer_weights(l, win, wout, convg):
        win_full = jnp.concatenate([win[j] for j in range(4)], axis=-1)
        conv_full = jnp.concatenate([convg[j] for j in range(4)], axis=-1)[:CONV_WIDTH]
        return dict(w_in=jnp.pad(win_full, ((0, 0), (0, IN_PAD - IN_WIDTH))), w_out=wout.reshape(D, D),
                    conv_w=conv_full, g_pre=w["mix_norm_pre"][l], g_post=w["mix_norm_post"][l],
                    b_gate=w["b_gate"][l], sink=w["attn_sink"][l], gamma=w["mlstm_norm"][l])

    h = xs
    saved = []
    layers = []
    for l in range(depth):
        g = gathered[l]
        wgu1, h = _tie(g["wgu1"], h)
        h, s1, wd1 = _ffn_fwd(h, w["ffn1_norm_pre"][l], w["ffn1_norm_post"][l], wgu1, g["wd1"])
        (win, convg), h = _tie((g["win"], g["conv"]), h)
        mix_p = mixer_weights(l, win, g["wout"], convg)
        h, s2 = _mixer_fwd(h, mix_p, cosf, sinf)
        wgu2, h = _tie(g["wgu2"], h)
        h, s3, wd2 = _ffn_fwd(h, w["ffn2_norm_pre"][l], w["ffn2_norm_post"][l], wgu2, g["wd2"])
        layers.append(dict(wgu1=wgu1, wd1=wd1, wgu2=wgu2, wd2=wd2, mix=mix_p))
        saved.append((s1, s2, s3))
    dh, loss_tile = _loss_grad(h, target)

    big = {}
    small_rows = []
    old = None
    for l in reversed(range(depth)):
        p = layers[l]
        s1, s2, s3 = saved[l]
        dh, dg2_pre, dg2_post, d_wgu2, d_wd2 = _ffn_bwd(dh, s3, w["ffn2_norm_pre"][l], w["ffn2_norm_post"][l],
                                                       p["wgu2"], p["wd2"], mid=old["b"].scatter if old else None)
        if old is not None:
            dh = old["a"].settle(old["b"].gather(dh))
        rs_c = _ReduceScatter([d_wgu2, d_wd2], dh)
        dh, sm, d_win, d_wout = _mixer_bwd(rs_c.dh, s2, p["mix"], cosf, sinf, mid=rs_c.scatter)
        dh = rs_c.gather(dh)
        if old is not None:
            dh = old["b"].settle(dh)
        d_win4 = jnp.stack([d_win[:, j * in_shard:(j + 1) * in_shard] for j in range(4)])
        rs_a = _ReduceScatter([d_win4, d_wout.reshape(4, D // 4, D)], dh)
        if l > 0:
            dh, dg1_pre, dg1_post, d_wgu1, d_wd1 = _ffn_bwd(
                rs_a.dh, s1, w["ffn1_norm_pre"][l], w["ffn1_norm_post"][l], p["wgu1"], p["wd1"], mid=rs_a.scatter)
            dh = rs_c.settle(rs_a.gather(dh))
            rs_b = _ReduceScatter([d_wgu1, d_wd1], dh)
            dh = rs_b.dh
        else:
            last = []

            def start_last(d_wgu, d_wd, x):
                last.append(_ReduceScatter([d_wgu, d_wd], x))
                return last[0].dh

            dh, dg1_pre, dg1_post, d_wgu1, d_wd1 = _ffn_bwd(
                rs_a.dh, s1, w["ffn1_norm_pre"][l], w["ffn1_norm_post"][l], p["wgu1"], p["wd1"], mid=rs_a.scatter,
                on_grads=start_last, late=lambda x: last[0].scatter(x))
            dh = rs_c.settle(rs_a.gather(dh))
            rs_b = last[0]
        old = dict(a=rs_a, b=rs_b, c=rs_c)
        big[l] = old
        small_rows.append((l, [dg1_pre, dg1_post, sm["g_pre"], sm["g_post"], dg2_pre, dg2_post, sm["gamma"],
                               sm["conv"], sm["b_gate"], sm["sink"]]))
    big_a = {l: g["a"].result() + g["c"].result() for l, g in big.items()}

    small_rows.sort(key=lambda t: t[0])
    flat = [_pad_lanes(v) for _, vs in small_rows for v in vs] + [loss_tile[0]]
    sizes = [f.shape[0] for f in flat]
    pack = jnp.concatenate(flat)
    pack = jnp.pad(pack, (0, (-pack.shape[0]) % (8 * LANES))).reshape(-1, LANES)
    every = _comm_all_to_all_small(pack)

    grads, deltas, new_m, new_v = {}, {}, {}, {}
    gu2 = jnp.stack([big_a[l][2] for l in range(depth)])
    grads["ffn2_w_gate"], grads["ffn2_w_up"] = gu2[:, :, :F], gu2[:, :, F:]
    grads["ffn2_w_down"] = jnp.stack([big_a[l][3] for l in range(depth)])
    grads["w_in"] = jnp.stack([big_a[l][0] for l in range(depth)])
    grads["w_out"] = jnp.stack([big_a[l][1] for l in range(depth)])
    for n in ("w_in", "w_out", "ffn2_w_gate", "ffn2_w_up", "ffn2_w_down"):
        deltas[n], new_m[n], new_v[n] = _adamw(w[n], grads[n], mom_m[n], mom_v[n])
    every, (deltas, new_m, new_v) = _tie(every, (deltas, new_m, new_v))

    total = _add_n([every[i] for i in range(8)], "small_sum").reshape(-1)
    pieces, off = [], 0
    for n in sizes:
        pieces.append(total[off:off + n])
        off += n
    loss = pieces[-1][0]
    per_layer = [pieces[10 * l:10 * l + 10] for l in range(depth)]

    def stack_small(i, shape):
        n = 1
        for s in shape:
            n *= s
        return jnp.stack([per_layer[l][i][:n].reshape(shape) for l in range(depth)])

    conv_full_grad = stack_small(7, (CONV_WIDTH, 4 * conv_shard))
    grads.update({
        "ffn1_norm_pre": stack_small(0, (D,)), "ffn1_norm_post": stack_small(1, (D,)),
        "mix_norm_pre": stack_small(2, (D,)), "mix_norm_post": stack_small(3, (D,)),
        "ffn2_norm_pre": stack_small(4, (D,)), "ffn2_norm_post": stack_small(5, (D,)),
        "mlstm_norm": stack_small(6, (M_WIDTH,)),
        "conv_w": lax.dynamic_slice_in_dim(conv_full_grad, chip * conv_shard, conv_shard, 2),
        "b_gate": stack_small(8, (N_GATES,)), "attn_sink": stack_small(9, (ATT_HEADS,)),
    })

    last_group = ("ffn1_w_gate", "ffn1_w_up", "ffn1_w_down")
    for n in names:
        if n not in last_group and n not in deltas:
            deltas[n], new_m[n], new_v[n] = _adamw(w[n], grads[n], mom_m[n], mom_v[n])
    deltas, new_m, new_v = big[0]["b"].gather((deltas, new_m, new_v))
    big_b = {l: g["b"].result() for l, g in big.items()}
    gu1 = jnp.stack([big_b[l][0] for l in range(depth)])
    grads["ffn1_w_gate"], grads["ffn1_w_up"] = gu1[:, :, :F], gu1[:, :, F:]
    grads["ffn1_w_down"] = jnp.stack([big_b[l][1] for l in range(depth)])
    for n in last_group:
        deltas[n], new_m[n], new_v[n] = _adamw(w[n], grads[n], mom_m[n], mom_v[n])
    grad_x = dh[None]
    return (loss, grad_x, *[grads[n] for n in names], *[deltas[n] for n in names],
            *[new_m[n] for n in names], *[new_v[n] for n in names])
```

```python
import jax
import jax.numpy as jnp
from jax import lax
from jax.experimental import pallas as pl
from jax.experimental.pallas import tpu as pltpu
from jax.experimental.pallas import tpu_sc as plsc

F32 = jnp.float32
BF16 = jnp.bfloat16
MESH = pl.DeviceIdType.MESH
ANY = pl.BlockSpec(memory_space=pl.ANY)

VMEM_LIMIT_BYTES = 56 * 1024 * 1024
LANES = 128

EPS = 1e-6
ATT_HEADS = 8
ATT_KV_HEADS = 2
ATT_GROUP = ATT_HEADS // ATT_KV_HEADS
ATT_HEAD_DIM = 128
ATT_WIDTH = ATT_HEADS * ATT_HEAD_DIM
KV_WIDTH = ATT_KV_HEADS * ATT_HEAD_DIM
BLK = 128
M_HEADS = 4
M_HEAD_DIM = 256
M_WIDTH = M_HEADS * M_HEAD_DIM
CONV_WIDTH = 5
CONV_HALO = 8
ROPE_THETA = 10000.0
N_GATES = 4 * M_HEADS
OFF_QA, OFF_KA, OFF_VA = 0, ATT_WIDTH, ATT_WIDTH + KV_WIDTH
OFF_QM = ATT_WIDTH + 2 * KV_WIDTH
OFF_KM = OFF_QM + M_WIDTH
OFF_VM = OFF_KM + M_WIDTH
OFF_OM = OFF_VM + M_WIDTH
OFF_G = OFF_OM + M_WIDTH
IN_WIDTH = OFF_G + N_GATES
IN_PAD = OFF_G + LANES
NEG_BIG = -1e30

ADAM_LR, ADAM_B1, ADAM_B2, ADAM_EPS, ADAM_WD, ADAM_STEP = 0.001, 0.9, 0.999, 1e-08, 0.01, 10


def _params(**kw):
    return pltpu.CompilerParams(vmem_limit_bytes=VMEM_LIMIT_BYTES, **kw)


def _dot(a, b):
    return lax.dot_general(a, b, (((1,), (0,)), ((), ())), preferred_element_type=F32)


def _dot_nt(a, b):
    return lax.dot_general(a, b, (((1,), (1,)), ((), ())), preferred_element_type=F32)


def _dot_tn(a, b):
    return lax.dot_general(a, b, (((0,), (0,)), ((), ())), preferred_element_type=F32)


def _sigmoid(x):
    return 0.5 * jnp.tanh(0.5 * x) + 0.5


def _eye_mask(n):
    r = lax.broadcasted_iota(jnp.int32, (n, n), 0)
    c = lax.broadcasted_iota(jnp.int32, (n, n), 1)
    return r == c


def _row_to_col(row, eye):
    n = eye.shape[0]
    return jnp.sum(jnp.where(eye, jnp.broadcast_to(row, (n, n)), 0.0), axis=1, keepdims=True)


def _col_to_row(col, eye):
    n = eye.shape[0]
    return jnp.sum(jnp.where(eye, jnp.broadcast_to(col, (n, n)), 0.0), axis=0, keepdims=True)


def _mm(a, b, kind, *, tm, tn, tk, name, out_dtype=F32, ja=None, jb=None):
    a2, b2 = a.shape[-2:], b.shape[-2:]
    if kind == "nn":
        (M, K), (_, N) = a2, b2
    elif kind == "nt":
        (M, K), (N, _) = a2, b2
    else:
        (K, M), (_, N) = a2, b2
    J = a.shape[0] if ja else (b.shape[0] if jb else 1)
    batch = "b" in (ja, jb)
    red = "r" in (ja, jb)
    nk = K // tk
    nr = nk * (J if red else 1)
    grid = ((J if batch else 1), M // tm, N // tn, nr)

    def lead(mode, g, r):
        return g if mode == "b" else r // nk

    def a_map(g, i, n, r):
        kk = r % nk
        idx = (i, kk) if kind != "tn" else (kk, i)
        return idx if ja is None else (lead(ja, g, r),) + idx

    def b_map(g, i, n, r):
        kk = r % nk
        idx = (kk, n) if kind != "nt" else (n, kk)
        return idx if jb is None else (lead(jb, g, r),) + idx

    def o_map(g, i, n, r):
        return (g, i, n) if batch else (i, n)

    a_blk = (tm, tk) if kind != "tn" else (tk, tm)
    b_blk = (tk, tn) if kind != "nt" else (tn, tk)
    dot = {"nn": _dot, "nt": _dot_nt, "tn": _dot_tn}[kind]

    def body(a_ref, b_ref, o_ref, *scratch):
        part = dot(a_ref[...], b_ref[...])
        if nr == 1:
            o_ref[...] = part.astype(out_dtype)
        else:
            acc = scratch[0]
            r = pl.program_id(3)

            @pl.when(r == 0)
            def _():
                acc[...] = part

            @pl.when(r > 0)
            def _():
                acc[...] += part

            @pl.when(r == nr - 1)
            def _():
                o_ref[...] = acc[...].astype(out_dtype)

    return pl.pallas_call(
        body, name=name, grid=grid,
        in_specs=[pl.BlockSpec(a_blk if ja is None else (None,) + a_blk, a_map),
                  pl.BlockSpec(b_blk if jb is None else (None,) + b_blk, b_map)],
        out_specs=pl.BlockSpec((None, tm, tn) if batch else (tm, tn), o_map),
        out_shape=jax.ShapeDtypeStruct((J, M, N) if batch else (M, N), out_dtype),
        scratch_shapes=[pltpu.VMEM((tm, tn), F32)] if nr > 1 else [],
        compiler_params=_params(),
    )(a, b)


def _row_tile(S, want):
    return min(S, want)


ELEMENTWISE_TILE_BYTES = 1 << 20
ACT_ROWS = 1024
TOKEN_ROWS = 2048
MODEL_COLS = 1024


def _elementwise_rows(R, C):
    for cand in (1024, 512, 256, 128, 64, 32, 16, 8):
        if R % cand == 0 and R > cand and cand * C * 4 <= ELEMENTWISE_TILE_BYTES:
            return cand
    return R if R * C * 4 <= ELEMENTWISE_TILE_BYTES or R % 8 else 8


def _ffn_up(xn, wgu):
    S, D = xn.shape
    J, _, F2 = wgu.shape
    F = F2 // 2
    tm = _row_tile(S, 512)

    def body(x_ref, w_ref, gu_ref, h_ref):
        gu = _dot(x_ref[...], w_ref[...])
        g, u = gu[:, :F], gu[:, F:]
        gu_ref[...] = gu.astype(BF16)
        h_ref[...] = (g * _sigmoid(g) * u).astype(BF16)

    return pl.pallas_call(
        body, name="ffn_up", grid=(J, S // tm),
        in_specs=[pl.BlockSpec((tm, D), lambda j, i: (i, 0)),
                  pl.BlockSpec((None, D, F2), lambda j, i: (j, 0, 0))],
        out_specs=[pl.BlockSpec((None, tm, F2), lambda j, i: (j, i, 0)),
                   pl.BlockSpec((None, tm, F), lambda j, i: (j, i, 0))],
        out_shape=[jax.ShapeDtypeStruct((J, S, F2), BF16), jax.ShapeDtypeStruct((J, S, F), BF16)],
        compiler_params=_params(),
    )(xn, wgu)


def _ffn_bwd_hidden(df, wd, gu):
    S, D = df.shape
    J, F, _ = wd.shape
    F2 = 2 * F
    tm = _row_tile(S, 512)

    def body(df_ref, w_ref, gu_ref, o_ref):
        dh = _dot_nt(df_ref[...], w_ref[...])
        g = gu_ref[:, :F].astype(F32)
        u = gu_ref[:, F:].astype(F32)
        sg = _sigmoid(g)
        o_ref[:, :F] = (dh * u * (sg * (1.0 + g * (1.0 - sg)))).astype(BF16)
        o_ref[:, F:] = (dh * (g * sg)).astype(BF16)

    return pl.pallas_call(
        body, name="ffn_bwd_hidden", grid=(J, S // tm),
        in_specs=[pl.BlockSpec((tm, D), lambda j, i: (i, 0)),
                  pl.BlockSpec((None, F, D), lambda j, i: (j, 0, 0)),
                  pl.BlockSpec((None, tm, F2), lambda j, i: (j, i, 0))],
        out_specs=pl.BlockSpec((None, tm, F2), lambda j, i: (j, i, 0)),
        out_shape=jax.ShapeDtypeStruct((J, S, F2), BF16),
        compiler_params=_params(),
    )(df, wd, gu)


def _norm_fwd(x, g):
    S, D = x.shape
    tm = _row_tile(S, 512)

    def body(x_ref, g_ref, o_ref):
        xv = x_ref[...]
        r = lax.rsqrt(jnp.mean(xv * xv, axis=-1, keepdims=True) + EPS)
        o_ref[...] = (xv * r * g_ref[...]).astype(BF16)

    return pl.pallas_call(
        body, name="norm_fwd", grid=(S // tm,),
        in_specs=[pl.BlockSpec((tm, D), lambda i: (i, 0)), pl.BlockSpec((1, D), lambda i: (0, 0))],
        out_specs=pl.BlockSpec((tm, D), lambda i: (i, 0)),
        out_shape=jax.ShapeDtypeStruct((S, D), BF16),
        compiler_params=_params(),
    )(x, g.reshape(1, D))


def _resid_norm_fwd(x, f, g, alpha):
    S, D = x.shape
    tm = _row_tile(S, 512)

    def body(x_ref, f_ref, g_ref, o_ref):
        fv = f_ref[...]
        r = lax.rsqrt(jnp.mean(fv * fv, axis=-1, keepdims=True) + EPS)
        o_ref[...] = x_ref[...] + alpha * (fv * r * g_ref[...])

    return pl.pallas_call(
        body, name="resid_norm_fwd", grid=(S // tm,),
        in_specs=[pl.BlockSpec((tm, D), lambda i: (i, 0)), pl.BlockSpec((tm, D), lambda i: (i, 0)),
                  pl.BlockSpec((1, D), lambda i: (0, 0))],
        out_specs=pl.BlockSpec((tm, D), lambda i: (i, 0)),
        out_shape=jax.ShapeDtypeStruct((S, D), F32),
        compiler_params=_params(),
    )(x, f, g.reshape(1, D))


def _norm_bwd(x, g, dy, alpha, resid=None, out_dtype=F32):
    S, D = x.shape
    tm = _row_tile(S, 256)
    has_resid = resid is not None

    def body(*refs):
        x_ref, g_ref, dy_ref = refs[:3]
        res_ref = refs[3] if has_resid else None
        dx_ref, dg_ref = refs[-2:]
        xv = x_ref[...]
        r = lax.rsqrt(jnp.mean(xv * xv, axis=-1, keepdims=True) + EPS)
        xh = xv * r
        dyv = dy_ref[...].astype(F32) * alpha
        gdy = dyv * g_ref[...]
        dx = r * (gdy - xh * jnp.mean(xh * gdy, axis=-1, keepdims=True))
        if has_resid:
            dx = dx + res_ref[...]
        dx_ref[...] = dx.astype(out_dtype)
        part = jnp.sum(dyv * xh, axis=0, keepdims=True)

        @pl.when(pl.program_id(0) == 0)
        def _():
            dg_ref[...] = part

        @pl.when(pl.program_id(0) > 0)
        def _():
            dg_ref[...] += part

    row = pl.BlockSpec((tm, D), lambda i: (i, 0))
    vec = pl.BlockSpec((1, D), lambda i: (0, 0))
    ins = [x, g.reshape(1, D), dy] + ([resid] if has_resid else [])
    dx, dg = pl.pallas_call(
        body, name="norm_bwd_res" if has_resid else "norm_bwd", grid=(S // tm,),
        in_specs=[row, vec, row] + ([row] if has_resid else []),
        out_specs=[row, vec],
        out_shape=[jax.ShapeDtypeStruct((S, D), out_dtype), jax.ShapeDtypeStruct((1, D), F32)],
        compiler_params=_params(),
    )(*ins)
    return dx, dg.reshape(D)


def _loss_grad(y, target):
    S, D = y.shape
    tm = _row_tile(S, 512)

    def body(y_ref, t_ref, dy_ref, l_ref):
        err = y_ref[...] - t_ref[...]
        dy_ref[...] = err * (1.0 / D)
        part = jnp.sum(jnp.sum(err * err, axis=-1, keepdims=True) * (0.5 / D), axis=0, keepdims=True)
        part = jnp.broadcast_to(part, (8, LANES))

        @pl.when(pl.program_id(0) == 0)
        def _():
            l_ref[...] = part

        @pl.when(pl.program_id(0) > 0)
        def _():
            l_ref[...] += part

    row = pl.BlockSpec((tm, D), lambda i: (i, 0))
    return pl.pallas_call(
        body, name="loss_grad", grid=(S // tm,),
        in_specs=[row, row],
        out_specs=[row, pl.BlockSpec((8, LANES), lambda i: (0, 0))],
        out_shape=[jax.ShapeDtypeStruct((S, D), F32), jax.ShapeDtypeStruct((8, LANES), F32)],
        compiler_params=_params(),
    )(y, target)


def _add_n(xs, name):
    shape = xs[0].shape
    C = shape[-1]
    R = 1
    for s in shape[:-1]:
        R *= s
    tm = _elementwise_rows(R, C)

    def body(*refs):
        acc = refs[0][...]
        for r in refs[1:-1]:
            acc = acc + r[...]
        refs[-1][...] = acc

    row = pl.BlockSpec((tm, C), lambda i: (i, 0))
    out = pl.pallas_call(
        body, name=name, grid=(R // tm,),
        in_specs=[row] * len(xs), out_specs=row,
        out_shape=jax.ShapeDtypeStruct((R, C), F32),
        compiler_params=_params(),
    )(*[x.reshape(R, C) for x in xs])
    return out.reshape(shape)


def _pair_sum(split, got):
    J, _, r, C = split.shape
    tm = _elementwise_rows(r, C)
    core = lax.axis_index("c").astype(jnp.int32).reshape(1)

    def body(core_ref, a_ref, b_ref, o_ref):
        o_ref[...] = a_ref[...] + b_ref[...]

    row = pl.BlockSpec((None, tm, C), lambda j, i, core_ref: (j, i, 0))
    return pl.pallas_call(
        body, name="pair_sum",
        grid_spec=pltpu.PrefetchScalarGridSpec(
            num_scalar_prefetch=1, grid=(J, r // tm),
            in_specs=[pl.BlockSpec((None, None, tm, C), lambda j, i, core_ref: (j, core_ref[0], i, 0)), row],
            out_specs=row),
        out_shape=jax.ShapeDtypeStruct((J, r, C), F32),
        compiler_params=_params(),
    )(core, split, got)


def _chip_sum(pair, arrived):
    _, r, C = pair.shape
    tm = _elementwise_rows(r, C)
    chip = (2 * lax.axis_index("x") + lax.axis_index("y")).astype(jnp.int32).reshape(1)

    def body(chip_ref, own_ref, a_ref, b_ref, c_ref, o_ref):
        o_ref[...] = own_ref[...] + a_ref[...] + b_ref[...] + c_ref[...]

    part = lambda k: pl.BlockSpec((None, tm, C), lambda i, chip_ref: (k, i, 0))
    return pl.pallas_call(
        body, name="chip_sum",
        grid_spec=pltpu.PrefetchScalarGridSpec(
            num_scalar_prefetch=1, grid=(r // tm,),
            in_specs=[pl.BlockSpec((None, tm, C), lambda i, chip_ref: (chip_ref[0], i, 0)), part(0), part(1), part(2)],
            out_specs=pl.BlockSpec((tm, C), lambda i, chip_ref: (i, 0))),
        out_shape=jax.ShapeDtypeStruct((r, C), F32),
        compiler_params=_params(),
    )(chip, pair, arrived, arrived, arrived)


def _rope_tables(S):
    half = ATT_HEAD_DIM // 2
    inv_freq = ROPE_THETA ** (-jnp.arange(half, dtype=F32) / half)
    ang = jnp.arange(S, dtype=F32)[:, None] * inv_freq[None, :]
    cos, sin = jnp.cos(ang), jnp.sin(ang)
    return jnp.concatenate([cos, cos], axis=-1), jnp.concatenate([-sin, sin], axis=-1)


def _rotate(x, cosf, sinf):
    return x * cosf + pltpu.roll(x, ATT_HEAD_DIM // 2, 1) * sinf


def _rope_fwd(proj, cosf, sinf):
    S = proj.shape[0]
    tm = _row_tile(S, 512)

    def body(q_ref, k_ref, v_ref, c_ref, s_ref, qo_ref, ko_ref, vo_ref):
        c, s = c_ref[...], s_ref[...]
        for h in range(ATT_HEADS):
            sl = slice(h * ATT_HEAD_DIM, (h + 1) * ATT_HEAD_DIM)
            qo_ref[:, sl] = _rotate(q_ref[:, sl], c, s).astype(BF16)
        for h in range(ATT_KV_HEADS):
            sl = slice(h * ATT_HEAD_DIM, (h + 1) * ATT_HEAD_DIM)
            ko_ref[:, sl] = _rotate(k_ref[:, sl], c, s).astype(BF16)
        vo_ref[...] = v_ref[...].astype(BF16)

    tab = pl.BlockSpec((tm, ATT_HEAD_DIM), lambda i: (i, 0))
    return pl.pallas_call(
        body, name="rope_fwd", grid=(S // tm,),
        in_specs=[pl.BlockSpec((tm, ATT_WIDTH), lambda i: (i, 0)),
                  pl.BlockSpec((tm, KV_WIDTH), lambda i: (i, OFF_KA // KV_WIDTH)),
                  pl.BlockSpec((tm, KV_WIDTH), lambda i: (i, OFF_VA // KV_WIDTH)), tab, tab],
        out_specs=[pl.BlockSpec((tm, ATT_WIDTH), lambda i: (i, 0)),
                   pl.BlockSpec((tm, KV_WIDTH), lambda i: (i, 0)),
                   pl.BlockSpec((tm, KV_WIDTH), lambda i: (i, 0))],
        out_shape=[jax.ShapeDtypeStruct((S, ATT_WIDTH), BF16), jax.ShapeDtypeStruct((S, KV_WIDTH), BF16),
                   jax.ShapeDtypeStruct((S, KV_WIDTH), BF16)],
        compiler_params=_params(),
    )(proj, proj, proj, cosf, sinf)


def _rope_bwd(dq, dk, dv, cosf, sinf):
    S = dq.shape[0]
    tm = _row_tile(S, 512)
    W = ATT_WIDTH + 2 * KV_WIDTH

    def body(q_ref, k_ref, v_ref, c_ref, s_ref, o_ref):
        c, s = c_ref[...], -s_ref[...]
        for h in range(ATT_HEADS):
            sl = slice(h * ATT_HEAD_DIM, (h + 1) * ATT_HEAD_DIM)
            o_ref[:, sl] = _rotate(q_ref[:, sl], c, s).astype(BF16)
        for h in range(ATT_KV_HEADS):
            sl = slice(h * ATT_HEAD_DIM, (h + 1) * ATT_HEAD_DIM)
            o_ref[:, ATT_WIDTH + h * ATT_HEAD_DIM:ATT_WIDTH + (h + 1) * ATT_HEAD_DIM] = (
                _rotate(k_ref[:, sl], c, s).astype(BF16))
        o_ref[:, ATT_WIDTH + KV_WIDTH:] = v_ref[...].astype(BF16)

    tab = pl.BlockSpec((tm, ATT_HEAD_DIM), lambda i: (i, 0))
    return pl.pallas_call(
        body, name="rope_bwd", grid=(S // tm,),
        in_specs=[pl.BlockSpec((tm, ATT_WIDTH), lambda i: (i, 0)),
                  pl.BlockSpec((tm, KV_WIDTH), lambda i: (i, 0)),
                  pl.BlockSpec((tm, KV_WIDTH), lambda i: (i, 0)), tab, tab],
        out_specs=pl.BlockSpec((tm, W), lambda i: (i, 0)),
        out_shape=jax.ShapeDtypeStruct((S, W), BF16),
        compiler_params=_params(),
    )(dq, dk, dv, cosf, sinf)


def _attn_probs(q_ref, k_refs, sink_ref, kh, n, nb):
    G, L, Dh = ATT_GROUP, BLK, ATT_HEAD_DIM
    q4 = jnp.concatenate([q_ref[:, (kh * G + g) * Dh:(kh * G + g + 1) * Dh] for g in range(G)], axis=0)
    kcat = jnp.concatenate([r[:, kh * Dh:(kh + 1) * Dh] for r in k_refs], axis=0)
    s = _dot_nt(q4, kcat) * (Dh ** -0.5)
    row = lax.broadcasted_iota(jnp.int32, (G * L, 3 * L), 0) % L
    col = lax.broadcasted_iota(jnp.int32, (G * L, 3 * L), 1)
    kpos = (n - 1) * L + col
    mask = (jnp.abs(col - L - row) <= L) & (kpos >= 0) & (kpos < nb * L)
    s = jnp.where(mask, s, -jnp.inf)
    sink = jnp.concatenate([jnp.broadcast_to(sink_ref[kh, :, g:g + 1], (L, 1)) for g in range(G)], axis=0)
    m = jnp.maximum(jnp.max(s, axis=-1, keepdims=True), sink)
    p = jnp.exp(s - m)
    es = jnp.exp(sink - m)
    inv = 1.0 / (jnp.sum(p, axis=-1, keepdims=True) + es)
    return q4, kcat, p * inv, es * inv


def _kv_specs(nb):
    return [pl.BlockSpec((BLK, KV_WIDTH), lambda n: (jnp.maximum(n - 1, 0), 0)),
            pl.BlockSpec((BLK, KV_WIDTH), lambda n: (n, 0)),
            pl.BlockSpec((BLK, KV_WIDTH), lambda n: (jnp.minimum(n + 1, nb - 1), 0))]


def _attn_fwd(qr, kr, va, sink):
    S = qr.shape[0]
    nb = S // BLK
    G, Dh = ATT_GROUP, ATT_HEAD_DIM

    def body(q_ref, k0, k1, k2, v0, v1, v2, sink_ref, o_ref):
        n = pl.program_id(0)
        for kh in range(ATT_KV_HEADS):
            _, _, probs, _ = _attn_probs(q_ref, (k0, k1, k2), sink_ref, kh, n, nb)
            vcat = jnp.concatenate([r[:, kh * Dh:(kh + 1) * Dh] for r in (v0, v1, v2)], axis=0)
            out = _dot(probs.astype(BF16), vcat)
            for g in range(G):
                o_ref[:, (kh * G + g) * Dh:(kh * G + g + 1) * Dh] = out[g * BLK:(g + 1) * BLK, :].astype(BF16)

    qspec = pl.BlockSpec((BLK, ATT_WIDTH), lambda n: (n, 0))
    return pl.pallas_call(
        body, name="attn_fwd", grid=(nb,),
        in_specs=[qspec] + _kv_specs(nb) + _kv_specs(nb)
        + [pl.BlockSpec((ATT_KV_HEADS, 1, G), lambda n: (0, 0, 0))],
        out_specs=qspec,
        out_shape=jax.ShapeDtypeStruct((S, ATT_WIDTH), BF16),
        compiler_params=_params(),
    )(qr, kr, kr, kr, va, va, va, sink.reshape(ATT_KV_HEADS, 1, G))


def _attn_bwd(qr, kr, va, sink, dycat):
    S = qr.shape[0]
    nb = S // BLK
    G, L, Dh = ATT_GROUP, BLK, ATT_HEAD_DIM
    SP = S + 2 * L

    def body(q_ref, k0, k1, k2, v0, v1, v2, sink_ref, do_ref, dq_ref, dk_ref, dv_ref, ds_ref):
        n = pl.program_id(0)

        @pl.when(n == 0)
        def _():
            dk_ref[...] = jnp.zeros_like(dk_ref)
            dv_ref[...] = jnp.zeros_like(dv_ref)
            ds_ref[...] = jnp.zeros_like(ds_ref)

        rows = pl.ds(pl.multiple_of(n * L, L), 3 * L)
        lane = lax.broadcasted_iota(jnp.int32, (8, LANES), 1)
        for kh in range(ATT_KV_HEADS):
            q4, kcat, probs, psink = _attn_probs(q_ref, (k0, k1, k2), sink_ref, kh, n, nb)
            vcat = jnp.concatenate([r[:, kh * Dh:(kh + 1) * Dh] for r in (v0, v1, v2)], axis=0)
            do4 = jnp.concatenate([do_ref[:, (kh * G + g) * Dh:(kh * G + g + 1) * Dh] for g in range(G)], axis=0)
            pb = probs.astype(BF16)
            dob = do4.astype(BF16)
            out = _dot(pb, vcat)
            delta = jnp.sum(do4 * out, axis=-1, keepdims=True)
            dp = _dot_nt(dob, vcat)
            dsc = (probs * (dp - delta) * (Dh ** -0.5)).astype(BF16)
            dq4 = _dot(dsc, kcat)
            for g in range(G):
                dq_ref[:, (kh * G + g) * Dh:(kh * G + g + 1) * Dh] = dq4[g * L:(g + 1) * L, :]
            dk_ref[rows, kh * Dh:(kh + 1) * Dh] += _dot_tn(dsc, q4)
            dv_ref[rows, kh * Dh:(kh + 1) * Dh] += _dot_tn(pb, dob)
            dsink = jnp.zeros((8, LANES), F32)
            for g in range(G):
                val = -jnp.sum(psink[g * L:(g + 1) * L] * delta[g * L:(g + 1) * L], axis=0, keepdims=True)
                dsink = dsink + jnp.where(lane == g, jnp.broadcast_to(val, (8, LANES)), 0.0)
            ds_ref[kh] += dsink

    qspec = pl.BlockSpec((L, ATT_WIDTH), lambda n: (n, 0))
    accspec = pl.BlockSpec((SP, KV_WIDTH), lambda n: (0, 0))
    dq, dkp, dvp, dsink = pl.pallas_call(
        body, name="attn_bwd", grid=(nb,),
        in_specs=[qspec] + _kv_specs(nb) + _kv_specs(nb)
        + [pl.BlockSpec((ATT_KV_HEADS, 1, G), lambda n: (0, 0, 0)), qspec],
        out_specs=[qspec, accspec, accspec, pl.BlockSpec((ATT_KV_HEADS, 8, LANES), lambda n: (0, 0, 0))],
        out_shape=[jax.ShapeDtypeStruct((S, ATT_WIDTH), F32), jax.ShapeDtypeStruct((SP, KV_WIDTH), F32),
                   jax.ShapeDtypeStruct((SP, KV_WIDTH), F32),
                   jax.ShapeDtypeStruct((ATT_KV_HEADS, 8, LANES), F32)],
        compiler_params=_params(),
    )(qr, kr, kr, kr, va, va, va, sink.reshape(ATT_KV_HEADS, 1, G), dycat)
    return dq, dkp[L:L + S], dvp[L:L + S], dsink[:, 0, :G].reshape(ATT_HEADS)


CONV_COLS = 256
CONV_ROWS = 512


def _conv_shifts(xs, rows):
    total = xs.shape[0]
    out = []
    for j in range(CONV_WIDTH):
        shift = (CONV_WIDTH // 2 - j) % total
        out.append((pltpu.roll(xs, shift, 0) if shift else xs)[CONV_HALO:CONV_HALO + rows, :])
    return out


def _conv_taps(xs, w_ref, rows):
    acc = None
    for j, xj in enumerate(_conv_shifts(xs, rows)):
        term = xj * w_ref[j:j + 1, :]
        acc = term if acc is None else acc + term
    return acc


def _conv_window(x_ref, i, R, nrow):
    S = x_ref.shape[0]
    r0 = pl.multiple_of(i * R, R)
    top = x_ref[pl.ds(pl.multiple_of(jnp.maximum(r0 - CONV_HALO, 0), CONV_HALO), CONV_HALO), :]
    bot = x_ref[pl.ds(pl.multiple_of(jnp.minimum(r0 + R, S - CONV_HALO), CONV_HALO), CONV_HALO), :]
    top = jnp.where(i > 0, top, 0.0)
    bot = jnp.where(i < nrow - 1, bot, 0.0)
    return jnp.concatenate([top, x_ref[pl.ds(r0, R), :], bot], axis=0)


def _conv5(x, w, *, col0, act, out_dtype, name):
    S = x.shape[0]
    C = w.shape[1]
    R = _row_tile(S, CONV_ROWS)
    tc = CONV_COLS
    half_blocks = (C // 2) // tc
    nrow = S // R

    def body(x_ref, w_ref, o_ref):
        scale = jnp.where(pl.program_id(0) >= half_blocks, M_HEAD_DIM ** -0.5, 1.0)
        y = _conv_taps(_conv_window(x_ref, pl.program_id(1), R, nrow), w_ref, R)
        if act:
            y = y * _sigmoid(y) * scale
        o_ref[...] = y.astype(out_dtype)

    return pl.pallas_call(
        body, name=name, grid=(C // tc, nrow),
        in_specs=[pl.BlockSpec((S, tc), lambda c, i: (0, col0 // tc + c)),
                  pl.BlockSpec((CONV_WIDTH, tc), lambda c, i: (0, c))],
        out_specs=pl.BlockSpec((R, tc), lambda c, i: (i, c)),
        out_shape=jax.ShapeDtypeStruct((S, C), out_dtype),
        compiler_params=_params(),
    )(x, w)


def _conv_bwd_pre(x, col0, w, dq2, dk2):
    S = x.shape[0]
    C = w.shape[1]
    R = _row_tile(S, CONV_ROWS)
    tc = CONV_COLS
    half_blocks = (C // 2) // tc
    nrow = S // R

    def body(x_ref, w_ref, dqa_ref, dqb_ref, dka_ref, dkb_ref, o_ref, dw_ref, acc):
        is_k = pl.program_id(0) >= half_blocks
        i = pl.program_id(1)
        scale = jnp.where(is_k, M_HEAD_DIM ** -0.5, 1.0)

        @pl.when(i == 0)
        def _():
            acc[...] = jnp.zeros_like(acc)

        shifted = _conv_shifts(_conv_window(x_ref, i, R, nrow), R)
        y = shifted[0] * w_ref[0:1, :]
        for j in range(1, CONV_WIDTH):
            y = y + shifted[j] * w_ref[j:j + 1, :]
        sg = _sigmoid(y)
        dqv = dqa_ref[...] + dqb_ref[...]
        dkv = dka_ref[...] + dkb_ref[...]
        dpre = jnp.where(is_k, dkv, dqv) * scale * (sg * (1.0 + y * (1.0 - sg)))
        o_ref[...] = dpre
        for j in range(CONV_WIDTH):
            acc[j:j + 1, :] += jnp.sum(dpre * shifted[j], axis=0, keepdims=True)

        @pl.when(i == nrow - 1)
        def _():
            dw_ref[...] = acc[0:CONV_WIDTH, :]

    nqb = (C // 2) // tc
    qmap = lambda d: (lambda c, i: (d, i, jnp.minimum(c, nqb - 1)))
    kmap = lambda d: (lambda c, i: (d, i, jnp.maximum(c - nqb, 0)))
    gspec = lambda m: pl.BlockSpec((None, R, tc), m)
    return pl.pallas_call(
        body, name="conv_bwd_pre", grid=(C // tc, nrow),
        in_specs=[pl.BlockSpec((S, tc), lambda c, i: (0, col0 // tc + c)),
                  pl.BlockSpec((CONV_WIDTH, tc), lambda c, i: (0, c)),
                  gspec(qmap(0)), gspec(qmap(1)), gspec(kmap(0)), gspec(kmap(1))],
        out_specs=[pl.BlockSpec((R, tc), lambda c, i: (i, c)), pl.BlockSpec((CONV_WIDTH, tc), lambda c, i: (0, c))],
        out_shape=[jax.ShapeDtypeStruct((S, C), F32), jax.ShapeDtypeStruct((CONV_WIDTH, C), F32)],
        scratch_shapes=[pltpu.VMEM((8, tc), F32)],
        compiler_params=_params(),
    )(x, w, dq2, dq2, dk2, dk2)


def _lane_cumsum(x, reverse):
    lane = lax.broadcasted_iota(jnp.int32, x.shape, 1)
    sh = 1
    while sh < LANES:
        if reverse:
            x = x + jnp.where(lane < LANES - sh, pltpu.roll(x, LANES - sh, 1), 0.0)
        else:
            x = x + jnp.where(lane >= sh, pltpu.roll(x, sh, 1), 0.0)
        sh *= 2
    return x


def _gate_prep(gates_t, bias):
    R = gates_t.shape[0]
    half, quarter = R // 2, R // 4

    def body(g_ref, b_ref, ig_ref, cum_ref):
        ig_ref[...] = g_ref[0:half, :] + b_ref[0:half, :]
        fg = g_ref[half:R, :] + b_ref[half:R, :]
        lf = jnp.minimum(fg, 0.0) - jnp.log(1.0 + jnp.exp(-jnp.abs(fg)))
        cum_ref[0:quarter, :] = _lane_cumsum(lf[0:quarter, :], False)
        cum_ref[quarter:half, :] = _lane_cumsum(lf[quarter:half, :], True)

    return pl.pallas_call(
        body, name="gate_prep",
        out_shape=[jax.ShapeDtypeStruct((half, LANES), F32), jax.ShapeDtypeStruct((half, LANES), F32)],
        compiler_params=_params(),
    )(gates_t, bias)


def _gate_bwd(gates_t, bias, di, dfc, dbl, nc):
    R = gates_t.shape[0]
    half, quarter = R // 2, R // 4

    def body(g_ref, b_ref, di_ref, df_ref, dbl_ref, dg_ref, db_ref):
        dg_ref[0:half, :] = di_ref[...]
        dfv = df_ref[...]
        within = jnp.concatenate([_lane_cumsum(dfv[0:quarter, :], True),
                                  _lane_cumsum(dfv[quarter:half, :], False)], axis=0)
        fg = g_ref[half:R, :] + b_ref[half:R, :]
        dg_ref[half:R, :] = (within + dbl_ref[...]) / (1.0 + jnp.exp(fg))
        rows = jnp.broadcast_to(jnp.sum(dg_ref[...], axis=-1, keepdims=True), (R, LANES))
        gr = lax.broadcasted_iota(jnp.int32, (N_GATES, R), 0)
        gc = lax.broadcasted_iota(jnp.int32, (N_GATES, R), 1)
        db_ref[...] = lax.dot_general((gc // nc == gr).astype(F32), rows, (((1,), (0,)), ((), ())),
                                      precision=lax.Precision.HIGHEST, preferred_element_type=F32)

    return pl.pallas_call(
        body, name="gate_bwd",
        out_shape=[jax.ShapeDtypeStruct((R, LANES), F32), jax.ShapeDtypeStruct((N_GATES, LANES), F32)],
        compiler_params=_params(),
    )(gates_t, bias, di, dfc, dbl)


def _chunk_index(d, c, nc, reverse):
    j = (nc - 1 - c) if reverse else c
    return j + d * (nc - 1 - 2 * j)


def _mlstm_chunk(d, q, k, vb, brow, igrow, C, nvec, m_prev, eye):
    L = BLK
    r = lax.broadcasted_iota(jnp.int32, (L, L), 0)
    c = lax.broadcasted_iota(jnp.int32, (L, L), 1)
    mask = (r - c) * (1 - 2 * d) >= 0
    bcol = _row_to_col(brow, eye)
    igcol = _row_to_col(igrow, eye)
    log_d = jnp.where(mask, bcol - brow + igrow, -jnp.inf)
    log_inter = bcol + m_prev
    m_t = jnp.maximum(log_inter, jnp.max(log_d, axis=-1, keepdims=True))
    d_mat = jnp.exp(log_d - m_t)
    inter = jnp.exp(log_inter - m_t)
    s = _dot_nt(q, k) * d_mat
    sb = s.astype(BF16)
    cb = C.astype(BF16)
    num = _dot(sb, vb) + inter * _dot_nt(q, cb)
    den = jnp.sum(s, axis=-1, keepdims=True) + inter * jnp.sum(q.astype(F32) * nvec, axis=-1, keepdims=True)
    floor = jnp.exp(-m_t)
    denom = jnp.maximum(jnp.abs(den), floor)
    b_last = jnp.where(d == 0, brow[:, L - 1:L], brow[:, 0:1])
    return dict(bcol=bcol, igcol=igcol, m_t=m_t, d_mat=d_mat, inter=inter, sb=sb, cb=cb, num=num, den=den,
                floor=floor, denom=denom, b_last=b_last)


def _head_cols(v0_ref, v1_ref, hd):
    ref = v0_ref if hd < M_HEADS // 2 else v1_ref
    lo = (hd % (M_HEADS // 2)) * M_HEAD_DIM
    return ref[:, lo:lo + M_HEAD_DIM]


def _mlstm_fwd(qk, proj, ig5, b5):
    S = qk.shape[0]
    nc = S // BLK
    L, Dh, H = BLK, M_HEAD_DIM, M_HEADS

    def body(q_ref, k_ref, v0_ref, v1_ref, ig_ref, b_ref, h_ref, cs_ref, ns_ref, ms_ref, C, nvec, m):
        d = pl.program_id(0)

        @pl.when(pl.program_id(1) == 0)
        def _():
            C[...] = jnp.zeros_like(C)
            nvec[...] = jnp.zeros_like(nvec)
            m[...] = jnp.zeros_like(m)

        eye = _eye_mask(L)
        for hd in range(H):
            cols = slice(hd * Dh, (hd + 1) * Dh)
            q, k = q_ref[:, cols], k_ref[:, cols]
            vf = _head_cols(v0_ref, v1_ref, hd)
            brow, igrow = b_ref[hd], ig_ref[hd]
            m_prev = m[hd, :, 0:1]
            cs_ref[hd] = C[hd].astype(BF16)
            ns_ref[hd] = nvec[hd]
            ms_ref[hd] = m[hd]
            f = _mlstm_chunk(d, q, k, vf.astype(BF16), brow, igrow, C[hd], nvec[hd], m_prev, eye)
            h_ref[:, cols] = f["num"] / f["denom"]
            b_last = f["b_last"]
            log_w = b_last - brow + igrow
            m_new = jnp.maximum(b_last + m_prev, jnp.max(log_w, axis=-1, keepdims=True))
            w_col = jnp.exp(b_last - f["bcol"] + f["igcol"] - m_new)
            decay = jnp.exp(b_last + m_prev - m_new)
            C[hd] = decay * C[hd] + _dot_tn((w_col * vf).astype(BF16), k)
            nvec[hd] = decay * nvec[hd] + jnp.sum(w_col * k.astype(F32), axis=0, keepdims=True)
            m[hd] = jnp.broadcast_to(m_new, (1, LANES))

    cidx = lambda d, c: _chunk_index(d, c, nc, False)
    gspec = pl.BlockSpec((None, H, None, 1, LANES), lambda d, c: (d, 0, cidx(d, c), 0, 0))
    st = lambda *blk: pl.BlockSpec((None, H, None) + blk, lambda d, c: (d, 0, cidx(d, c), 0, 0))
    half = M_WIDTH // 2
    return pl.pallas_call(
        body, name="mlstm_fwd", grid=(2, nc),
        in_specs=[pl.BlockSpec((L, M_WIDTH), lambda d, c: (cidx(d, c), 0)),
                  pl.BlockSpec((L, M_WIDTH), lambda d, c: (cidx(d, c), 1)),
                  pl.BlockSpec((L, half), lambda d, c: (cidx(d, c), OFF_VM // half)),
                  pl.BlockSpec((L, half), lambda d, c: (cidx(d, c), OFF_VM // half + 1)), gspec, gspec],
        out_specs=[pl.BlockSpec((None, L, M_WIDTH), lambda d, c: (d, cidx(d, c), 0)),
                   st(Dh, Dh), st(1, Dh), st(1, LANES)],
        out_shape=[jax.ShapeDtypeStruct((2, S, M_WIDTH), F32), jax.ShapeDtypeStruct((2, H, nc, Dh, Dh), BF16),
                   jax.ShapeDtypeStruct((2, H, nc, 1, Dh), F32), jax.ShapeDtypeStruct((2, H, nc, 1, LANES), F32)],
        scratch_shapes=[pltpu.VMEM((H, Dh, Dh), F32), pltpu.VMEM((H, 1, Dh), F32), pltpu.VMEM((H, 1, LANES), F32)],
        compiler_params=_params(),
    )(qk, qk, proj, proj, ig5, b5)


def _mlstm_bwd(qk, proj, ig5, b5, cs, ns, ms, dhm):
    S = qk.shape[0]
    nc = S // BLK
    L, Dh, H = BLK, M_HEAD_DIM, M_HEADS

    def body(q_ref, k_ref, v0_ref, v1_ref, ig_ref, b_ref, cs_ref, ns_ref, ms_ref, dh_ref,
             dq_ref, dk_ref, dv_ref, di_ref, df_ref, dbl_ref, R, rvec, mu):
        d = pl.program_id(0)

        @pl.when(pl.program_id(1) == 0)
        def _():
            R[...] = jnp.zeros_like(R)
            rvec[...] = jnp.zeros_like(rvec)
            mu[...] = jnp.full(mu.shape, NEG_BIG, F32)

        eye = _eye_mask(L)
        for hd in range(H):
            cols = slice(hd * Dh, (hd + 1) * Dh)
            q, k = q_ref[:, cols], k_ref[:, cols]
            qf, kf = q.astype(F32), k.astype(F32)
            vb = _head_cols(v0_ref, v1_ref, hd).astype(BF16)
            brow, igrow = b_ref[hd], ig_ref[hd]
            nprev = ns_ref[hd]
            m_prev = ms_ref[hd, :, 0:1]
            cprev = cs_ref[hd]
            f = _mlstm_chunk(d, q, k, vb, brow, igrow, cprev, nprev, m_prev, eye)
            dh = dh_ref[:, cols]
            inv = 1.0 / f["denom"]
            hcur = f["num"] * inv
            dnum = dh * inv
            active = jnp.abs(f["den"]) >= f["floor"]
            dden = (jnp.where(active, -jnp.sign(f["den"]), 0.0)
                    * jnp.sum(dh * hcur, axis=-1, keepdims=True) * inv)
            dnb = dnum.astype(BF16)
            dqk = ((_dot_nt(dnb, vb) + dden) * f["d_mat"]).astype(BF16)
            dq = _dot(dqk, k) + f["inter"] * (_dot(dnb, f["cb"]) + dden * nprev)
            mu_prev = mu[hd, :, 0:1]
            a_col = jnp.exp(f["b_last"] - f["bcol"] + f["igcol"] + mu_prev)
            rb = R[hd].astype(BF16)
            dv = _dot_tn(f["sb"], dnb) + a_col * _dot_nt(k, rb)
            dk_inter = a_col * (_dot(vb, rb) + rvec[hd])
            dk = _dot_tn(dqk, q) + dk_inter
            dq_ref[:, cols] = dq
            dk_ref[:, cols] = dk
            dv_ref[:, cols] = dv
            kdk = jnp.sum(kf * dk, axis=-1, keepdims=True)
            qdq = jnp.sum(qf * dq, axis=-1, keepdims=True)
            di_ref[hd] = _col_to_row(kdk, eye)
            df_ref[hd] = _col_to_row(qdq - kdk, eye)
            older = jnp.sum(jnp.sum(R[hd] * cprev.astype(F32), axis=-1, keepdims=True), axis=0, keepdims=True)
            older = older + jnp.sum(rvec[hd] * nprev, axis=-1, keepdims=True)
            dbl = (jnp.sum(jnp.sum(kf * dk_inter, axis=-1, keepdims=True), axis=0, keepdims=True)
                   + jnp.exp(f["b_last"] + mu_prev + m_prev) * older)
            dbl_ref[hd] = jnp.broadcast_to(dbl, (1, LANES))
            lw = f["bcol"] - f["m_t"]
            mu_new = jnp.maximum(f["b_last"] + mu_prev, jnp.max(lw, axis=0, keepdims=True))
            wq = jnp.exp(lw - mu_new)
            decay = jnp.exp(f["b_last"] + mu_prev - mu_new)
            R[hd] = decay * R[hd] + _dot_tn((wq * dnum).astype(BF16), q)
            rvec[hd] = decay * rvec[hd] + jnp.sum(wq * dden * qf, axis=0, keepdims=True)
            mu[hd] = jnp.broadcast_to(mu_new, (1, LANES))

    cidx = lambda d, c: _chunk_index(d, c, nc, True)
    gin = pl.BlockSpec((None, H, None, 1, LANES), lambda d, c: (d, 0, cidx(d, c), 0, 0))
    st = lambda *blk: pl.BlockSpec((None, H, None) + blk, lambda d, c: (d, 0, cidx(d, c), 0, 0))
    per_dir = pl.BlockSpec((None, L, M_WIDTH), lambda d, c: (d, cidx(d, c), 0))
    big = jax.ShapeDtypeStruct((2, S, M_WIDTH), F32)
    small = jax.ShapeDtypeStruct((2, H, nc, 1, LANES), F32)
    half = M_WIDTH // 2
    return pl.pallas_call(
        body, name="mlstm_bwd", grid=(2, nc),
        in_specs=[pl.BlockSpec((L, M_WIDTH), lambda d, c: (cidx(d, c), 0)),
                  pl.BlockSpec((L, M_WIDTH), lambda d, c: (cidx(d, c), 1)),
                  pl.BlockSpec((L, half), lambda d, c: (cidx(d, c), OFF_VM // half)),
                  pl.BlockSpec((L, half), lambda d, c: (cidx(d, c), OFF_VM // half + 1)), gin, gin,
                  st(Dh, Dh), st(1, Dh), st(1, LANES),
                  pl.BlockSpec((L, M_WIDTH), lambda d, c: (cidx(d, c), 0))],
        out_specs=[per_dir, per_dir, per_dir, gin, gin, gin],
        out_shape=[big, big, big, small, small, small],
        scratch_shapes=[pltpu.VMEM((H, Dh, Dh), F32), pltpu.VMEM((H, 1, Dh), F32), pltpu.VMEM((H, 1, LANES), F32)],
        compiler_params=_params(),
    )(qk, qk, proj, proj, ig5, b5, cs, ns, ms, dhm)


def _mlstm_out_fwd(hs, proj, gamma):
    S = hs.shape[1]
    tm = _row_tile(S, 512)
    Dh, H = M_HEAD_DIM, M_HEADS

    def body(hf_ref, hb_ref, o_ref, g_ref, y_ref):
        hm = hf_ref[...] + hb_ref[...]
        r = lax.rsqrt(jnp.mean(hm * hm, axis=-1, keepdims=True) + EPS)
        y_ref[...] = (_sigmoid(o_ref[...]) * (hm * r * g_ref[...])).astype(BF16)

    hspec = lambda d: pl.BlockSpec((None, tm, Dh), lambda i, h: (d, i, h))
    return pl.pallas_call(
        body, name="mlstm_out_fwd", grid=(S // tm, H),
        in_specs=[hspec(0), hspec(1), pl.BlockSpec((tm, Dh), lambda i, h: (i, OFF_OM // Dh + h)),
                  pl.BlockSpec((1, Dh), lambda i, h: (0, h))],
        out_specs=pl.BlockSpec((tm, Dh), lambda i, h: (i, h)),
        out_shape=jax.ShapeDtypeStruct((S, M_WIDTH), BF16),
        compiler_params=_params(),
    )(hs, hs, proj, gamma.reshape(1, M_WIDTH))


def _mlstm_out_bwd(hs, proj, gamma, dycat):
    S = hs.shape[1]
    tm = _row_tile(S, 512)
    Dh, H = M_HEAD_DIM, M_HEADS
    D_OFF = ATT_WIDTH // Dh

    def body(hf_ref, hb_ref, o_ref, g_ref, dy_ref, dh_ref, do_ref, dg_ref):
        hm = hf_ref[...] + hb_ref[...]
        r = lax.rsqrt(jnp.mean(hm * hm, axis=-1, keepdims=True) + EPS)
        hh = hm * r
        so = _sigmoid(o_ref[...])
        dy = dy_ref[...]
        gam = g_ref[...]
        do_ref[...] = (dy * hh * gam * so * (1.0 - so)).astype(BF16)
        dyn = dy * so
        gd = dyn * gam
        dh_ref[...] = r * (gd - hh * jnp.mean(hh * gd, axis=-1, keepdims=True))
        part = jnp.sum(dyn * hh, axis=0, keepdims=True)

        @pl.when(pl.program_id(1) == 0)
        def _():
            dg_ref[...] = part

        @pl.when(pl.program_id(1) > 0)
        def _():
            dg_ref[...] += part

    hspec = lambda d: pl.BlockSpec((None, tm, Dh), lambda h, i: (d, i, h))
    out = pl.BlockSpec((tm, Dh), lambda h, i: (i, h))
    vec = pl.BlockSpec((1, Dh), lambda h, i: (0, h))
    dhm, do, dg = pl.pallas_call(
        body, name="mlstm_out_bwd", grid=(H, S // tm),
        in_specs=[hspec(0), hspec(1), pl.BlockSpec((tm, Dh), lambda h, i: (i, OFF_OM // Dh + h)), vec,
                  pl.BlockSpec((tm, Dh), lambda h, i: (i, D_OFF + h))],
        out_specs=[out, out, vec],
        out_shape=[jax.ShapeDtypeStruct((S, M_WIDTH), F32), jax.ShapeDtypeStruct((S, M_WIDTH), BF16),
                   jax.ShapeDtypeStruct((1, M_WIDTH), F32)],
        compiler_params=_params(),
    )(hs, hs, proj, gamma.reshape(1, M_WIDTH), dycat)
    return dhm, do, dg.reshape(M_WIDTH)


def _sum_cast(a2, name):
    _, S, C = a2.shape
    tm = _row_tile(S, 512)

    def body(a_ref, b_ref, o_ref):
        o_ref[...] = (a_ref[...] + b_ref[...]).astype(BF16)

    spec = lambda d: pl.BlockSpec((None, tm, C), lambda i: (d, i, 0))
    return pl.pallas_call(
        body, name=name, grid=(S // tm,),
        in_specs=[spec(0), spec(1)], out_specs=pl.BlockSpec((tm, C), lambda i: (i, 0)),
        out_shape=jax.ShapeDtypeStruct((S, C), BF16),
        compiler_params=_params(),
    )(a2, a2)


def _adamw(w, g, m, v):
    shape = w.shape
    C = shape[-1]
    R = w.size // C
    tm = _elementwise_rows(R, C)

    def body(w_ref, g_ref, m_ref, v_ref, d_ref, mo_ref, vo_ref):
        gv = g_ref[...]
        mn = ADAM_B1 * m_ref[...] + (1.0 - ADAM_B1) * gv
        vn = ADAM_B2 * v_ref[...] + (1.0 - ADAM_B2) * (gv * gv)
        m_hat = mn / (1.0 - ADAM_B1 ** ADAM_STEP)
        v_hat = vn / (1.0 - ADAM_B2 ** ADAM_STEP)
        d_ref[...] = -ADAM_LR * (m_hat / (jnp.sqrt(v_hat) + ADAM_EPS) + ADAM_WD * w_ref[...])
        mo_ref[...] = mn
        vo_ref[...] = vn

    row = pl.BlockSpec((tm, C), lambda i: (i, 0))
    sds = jax.ShapeDtypeStruct((R, C), F32)
    outs = pl.pallas_call(
        body, name="adamw", grid=(R // tm,),
        in_specs=[row] * 4, out_specs=[row] * 3, out_shape=[sds] * 3,
        compiler_params=_params(),
    )(*[t.reshape(R, C) for t in (w, g, m, v)])
    return tuple(o.reshape(shape) for o in outs)


CID_WEIGHT_GATHER, CID_PAIR_SWAP, CID_SCATTER, CID_SMALL = 0, 1, 2, 3


def _other_chips(x, y):
    return [(1 - x, y), (x, 1 - y), (1 - x, 1 - y)]


def _handshake(peers):
    barrier = pltpu.get_barrier_semaphore()
    for peer in peers:
        pl.semaphore_signal(barrier, inc=1, device_id=peer, device_id_type=MESH)
    pl.semaphore_wait(barrier, len(peers))


def _sequencer_call(body, name, out_type, sem_counts, collective_id, operands):
    return pl.kernel(
        body, name=name, out_type=out_type,
        mesh=plsc.ScalarSubcoreMesh(axis_name="sequencer", num_cores=1),
        scratch_types=[pltpu.SemaphoreType.DMA((n,)) for n in sem_counts],
        compiler_params=pltpu.CompilerParams(collective_id=collective_id),
    )(*operands)


def _comm_weight_gather(groups):
    T = len(groups)
    flat = [(t, sum(p.shape[1] for p in g[:i]), part) for t, g in enumerate(groups) for i, part in enumerate(g)]
    N = len(flat)

    def body(*refs):
        ins, outs = refs[:N], refs[N:N + T]
        send, recv, local = refs[N + T:]
        x, y, c = lax.axis_index("x"), lax.axis_index("y"), lax.axis_index("c")
        me = 2 * x + y
        chips = _other_chips(x, y)
        _handshake([(px, py, c) for px, py in chips])

        def place(i, chip):
            t, off, part = flat[i]
            return outs[t].at[chip, :, pl.ds(off, part.shape[1])]

        mine = []
        for i in range(N):
            cp = pltpu.make_async_copy(ins[i], place(i, me), local.at[i])
            cp.start()
            mine.append(cp)
            for k, (px, py) in enumerate(chips):
                pltpu.make_async_remote_copy(
                    src_ref=ins[i], dst_ref=place(i, me), send_sem=send.at[3 * i + k], recv_sem=recv.at[3 * i + k],
                    device_id=(px, py, c), device_id_type=MESH).start()
        for i in range(N):
            for k, (px, py) in enumerate(chips):
                cp = pltpu.make_async_remote_copy(
                    src_ref=ins[i], dst_ref=place(i, 2 * px + py), send_sem=send.at[3 * i + k],
                    recv_sem=recv.at[3 * i + k], device_id=(px, py, c), device_id_type=MESH)
                cp.wait_send()
                cp.wait_recv()
            mine[i].wait()

    out_type = tuple(jax.ShapeDtypeStruct((4, g[0].shape[0], sum(p.shape[1] for p in g)), g[0].dtype) for g in groups)
    return _sequencer_call(
        body, "comm_weight_gather", out_type, (3 * N, 3 * N, N), CID_WEIGHT_GATHER, [part for _, _, part in flat])


def _comm_to_sibling(xs, other_half=False):
    T = len(xs)

    def body(*refs):
        ins, outs = refs[:T], refs[T:2 * T]
        send, recv = refs[2 * T:]
        x, y, c = lax.axis_index("x"), lax.axis_index("y"), lax.axis_index("c")
        _handshake([(x, y, 1 - c)])
        cps = [pltpu.make_async_remote_copy(
            src_ref=ins[t].at[:, 1 - c] if other_half else ins[t], dst_ref=outs[t], send_sem=send.at[t],
            recv_sem=recv.at[t], device_id=(x, y, 1 - c), device_id_type=MESH) for t in range(T)]
        for cp in cps:
            cp.start()
        for cp in cps:
            cp.wait_send()
            cp.wait_recv()

    shapes = [s.shape[:1] + s.shape[2:] if other_half else s.shape for s in xs]
    return _sequencer_call(
        body, "comm_pair_swap", tuple(jax.ShapeDtypeStruct(sh, s.dtype) for sh, s in zip(shapes, xs)),
        (T, T), CID_PAIR_SWAP, xs)


def _comm_scatter_to_owners(ps):
    T = len(ps)

    def body(*refs):
        ins, outs = refs[:T], refs[T:2 * T]
        send, recv = refs[2 * T:]
        x, y, c = lax.axis_index("x"), lax.axis_index("y"), lax.axis_index("c")
        chips = _other_chips(x, y)
        _handshake([(px, py, c) for px, py in chips])
        cps = []
        for t in range(T):
            for k, (px, py) in enumerate(chips):
                cp = pltpu.make_async_remote_copy(
                    src_ref=ins[t].at[2 * px + py], dst_ref=outs[t].at[k], send_sem=send.at[3 * t + k],
                    recv_sem=recv.at[3 * t + k], device_id=(px, py, c), device_id_type=MESH)
                cp.start()
                cps.append(cp)
        for cp in cps:
            cp.wait_send()
            cp.wait_recv()

    return _sequencer_call(
        body, "comm_scatter_to_owners", tuple(jax.ShapeDtypeStruct((3,) + p.shape[1:], p.dtype) for p in ps),
        (3 * T, 3 * T), CID_SCATTER, ps)


def _comm_all_to_all_small(pack):
    R = pack.shape[0]

    def body(in_ref, out_ref, send, recv, local):
        x, y, c = lax.axis_index("x"), lax.axis_index("y"), lax.axis_index("c")
        me = 4 * x + 2 * y + c
        peers = []
        for k in range(1, 8):
            fx, fy, fc = (k >> 2) & 1, (k >> 1) & 1, k & 1
            peers.append((x + fx * (1 - 2 * x), y + fy * (1 - 2 * y), c + fc * (1 - 2 * c)))
        _handshake(peers)
        mine = pltpu.make_async_copy(in_ref, out_ref.at[me], local.at[0])
        mine.start()
        for k, peer in enumerate(peers):
            pltpu.make_async_remote_copy(
                src_ref=in_ref, dst_ref=out_ref.at[me], send_sem=send.at[k], recv_sem=recv.at[k],
                device_id=peer, device_id_type=MESH).start()
        for k, (px, py, pc) in enumerate(peers):
            cp = pltpu.make_async_remote_copy(
                src_ref=in_ref, dst_ref=out_ref.at[4 * px + 2 * py + pc], send_sem=send.at[k], recv_sem=recv.at[k],
                device_id=(px, py, pc), device_id_type=MESH)
            cp.wait_send()
            cp.wait_recv()
        mine.wait()

    return _sequencer_call(
        body, "comm_all_to_all_small", (jax.ShapeDtypeStruct((8, R, LANES), F32),), (7, 7, 1), CID_SMALL, [pack])[0]


def _tie(*trees):
    return lax.optimization_barrier(trees)


class _ReduceScatter:
    def __init__(self, grads, dh):
        split = [g.reshape(4, 2, g.shape[1] // 2, g.shape[2]) for g in grads]
        self.split, self.dh = _tie(split, dh)
        self.got = _comm_to_sibling(self.split, other_half=True)

    def scatter(self, dh):
        split, got, dh = _tie(self.split, list(self.got), dh)
        pair = [_pair_sum(a, b) for a, b in zip(split, got)]
        self.pair, dh = _tie(pair, dh)
        self.arrived = _comm_scatter_to_owners(self.pair)
        return dh

    def gather(self, dh):
        pair, arrived, dh = _tie(self.pair, list(self.arrived), dh)
        halves = [_chip_sum(p, a) for p, a in zip(pair, arrived)]
        self.halves, dh = _tie(halves, dh)
        self.other = _comm_to_sibling(self.halves)
        return dh

    def settle(self, dh):
        self.other, dh = _tie(list(self.other), dh)
        return dh

    def result(self):
        south = lax.axis_index("c") == 0
        return [jnp.where(south, jnp.concatenate([mine, theirs]), jnp.concatenate([theirs, mine]))
                for mine, theirs in zip(self.halves, self.other)]


def _ffn_fwd(x, g_pre, g_post, wgu, wd):
    S, D = x.shape
    F = wd.shape[1]
    xn = _norm_fwd(x, g_pre)
    gu, h = _ffn_up(xn, wgu)
    wd, h = _tie(wd, h)
    f = _mm(h, wd, "nn", ja="r", jb="r", tm=_row_tile(S, ACT_ROWS), tn=D, tk=F, name="ffn_down")
    x_out = _resid_norm_fwd(x, f, g_post, 0.5)
    return x_out, (x, xn, gu, h, f), wd


def _ffn_bwd(dx, saved, g_pre, g_post, wgu, wd, mid=None, on_grads=None, late=None):
    x, xn, gu, h, f = saved
    S, D = x.shape
    J, F, _ = wd.shape
    tm, tk = _row_tile(S, ACT_ROWS), _row_tile(S, TOKEN_ROWS)
    df, dg_post = _norm_bwd(f, g_post, dx, 0.5, out_dtype=BF16)
    d_wd = _mm(h, df, "tn", ja="b", tm=F, tn=MODEL_COLS, tk=tk, name="ffn_dwd")
    dgu = _ffn_bwd_hidden(df, wd, gu)
    if mid is not None:
        dgu = mid(dgu)
    d_wgu = _mm(xn, dgu, "tn", jb="b", tm=MODEL_COLS, tn=F, tk=tk, name="ffn_dwgu")
    if on_grads is not None:
        dgu = on_grads(d_wgu, d_wd, dgu)
    dxn = _mm(dgu, wgu, "nt", ja="r", jb="r", tm=tm, tn=MODEL_COLS, tk=2 * F, name="ffn_dxn")
    if late is not None:
        dxn = late(dxn)
    dx_in, dg_pre = _norm_bwd(x, g_pre, dxn, 1.0, resid=dx)
    return dx_in, dg_pre, dg_post, d_wgu, d_wd


def _gates_layout(proj, b_gate, nc):
    gt = jnp.transpose(proj[:, OFF_G:OFF_G + N_GATES]).reshape(N_GATES * nc, LANES)
    bias = jnp.repeat(b_gate, nc).reshape(N_GATES * nc, 1)
    return gt, bias


def _mixer_fwd(x, p, cosf, sinf):
    S, D = x.shape
    nc = S // BLK
    hn = _norm_fwd(x, p["g_pre"])
    proj = _mm(hn, p["w_in"], "nn", tm=_row_tile(S, ACT_ROWS), tn=IN_PAD // 5, tk=D, name="in_proj")
    qr, kr, va = _rope_fwd(proj, cosf, sinf)
    y_att = _attn_fwd(qr, kr, va, p["sink"])
    qk = _conv5(proj, p["conv_w"], col0=OFF_QM, act=True, out_dtype=BF16, name="conv_silu")
    gt, bias = _gates_layout(proj, p["b_gate"], nc)
    ig, cum = _gate_prep(gt, bias)
    ig5, b5 = ig.reshape(2, M_HEADS, nc, 1, LANES), cum.reshape(2, M_HEADS, nc, 1, LANES)
    hs, cs, ns, ms = _mlstm_fwd(qk, proj, ig5, b5)
    y_m = _mlstm_out_fwd(hs, proj, p["gamma"])
    ycat = jnp.concatenate([y_att, y_m], axis=-1)
    w_out, ycat = _tie(p["w_out"], ycat)
    p["w_out"] = w_out
    mix = _mm(ycat, w_out, "nn", tm=_row_tile(S, ACT_ROWS), tn=MODEL_COLS, tk=D, name="out_proj")
    x_out = _resid_norm_fwd(x, mix, p["g_post"], 1.0)
    return x_out, (x, hn, proj, qr, kr, va, qk, gt, bias, ig5, b5, hs, cs, ns, ms, ycat, mix)


def _mixer_bwd(dx, saved, p, cosf, sinf, mid=None):
    x, hn, proj, qr, kr, va, qk, gt, bias, ig5, b5, hs, cs, ns, ms, ycat, mix = saved
    S, D = x.shape
    nc = S // BLK
    tm, tk = _row_tile(S, ACT_ROWS), _row_tile(S, TOKEN_ROWS)
    dmix, dg_post = _norm_bwd(mix, p["g_post"], dx, 1.0, out_dtype=BF16)
    d_wout = _mm(ycat, dmix, "tn", tm=MODEL_COLS, tn=MODEL_COLS, tk=tk, name="dw_out")
    dycat = _mm(dmix, p["w_out"], "nt", tm=tm, tn=MODEL_COLS, tk=D, name="d_ycat")
    if mid is not None:
        dycat = mid(dycat)
    dhm, d_om, d_gamma = _mlstm_out_bwd(hs, proj, p["gamma"], dycat)
    dq2, dk2, dv2, di, dfc, dbl = _mlstm_bwd(qk, proj, ig5, b5, cs, ns, ms, dhm)
    d_vm = _sum_cast(dv2, "dv_sum")
    dgt, db = _gate_bwd(gt, bias, di.reshape(-1, LANES), dfc.reshape(-1, LANES), dbl.reshape(-1, LANES), nc)
    dpre, d_conv = _conv_bwd_pre(proj, OFF_QM, p["conv_w"], dq2, dk2)
    d_qkm = _conv5(dpre, p["conv_w"][::-1], col0=0, act=False, out_dtype=BF16, name="conv_bwd_x")
    d_gates = jnp.transpose(dgt.reshape(N_GATES, S)).astype(BF16)
    d_gates = jnp.pad(d_gates, ((0, 0), (0, IN_PAD - IN_WIDTH)))
    dqr, dkr, dva, d_sink = _attn_bwd(qr, kr, va, p["sink"], dycat)
    d_att = _rope_bwd(dqr, dkr, dva, cosf, sinf)
    dproj = jnp.concatenate([d_att, d_qkm, d_vm, d_om, d_gates], axis=-1)
    d_win = _mm(hn, dproj, "tn", tm=MODEL_COLS, tn=IN_PAD // 5, tk=tk, name="dw_in")
    dhn = _mm(dproj, p["w_in"], "nt", tm=tm, tn=MODEL_COLS, tk=IN_PAD // 3, name="d_hn")
    dx_in, dg_pre = _norm_bwd(x, p["g_pre"], dhn, 1.0, resid=dx)
    small = dict(g_pre=dg_pre, g_post=dg_post, gamma=d_gamma, conv=d_conv, b_gate=db[:, 0], sink=d_sink)
    return dx_in, small, d_win, d_wout


def _pad_lanes(v):
    v = v.reshape(-1)
    return jnp.pad(v, (0, (-v.shape[0]) % LANES))


def kernel(x, ffn1_norm_pre, ffn1_norm_post, ffn1_w_gate, ffn1_w_up, ffn1_w_down, mix_norm_pre, mix_norm_post, w_in, b_gate, conv_w, attn_sink, mlstm_norm, w_out, ffn2_norm_pre, ffn2_norm_post, ffn2_w_gate, ffn2_w_up, ffn2_w_down, loss_target, m_ffn1_norm_pre, m_ffn1_norm_post, m_ffn1_w_gate, m_ffn1_w_up, m_ffn1_w_down, m_mix_norm_pre, m_mix_norm_post, m_w_in, m_b_gate, m_conv_w, m_attn_sink, m_mlstm_norm, m_w_out, m_ffn2_norm_pre, m_ffn2_norm_post, m_ffn2_w_gate, m_ffn2_w_up, m_ffn2_w_down, v_ffn1_norm_pre, v_ffn1_norm_post, v_ffn1_w_gate, v_ffn1_w_up, v_ffn1_w_down, v_mix_norm_pre, v_mix_norm_post, v_w_in, v_b_gate, v_conv_w, v_attn_sink, v_mlstm_norm, v_w_out, v_ffn2_norm_pre, v_ffn2_norm_post, v_ffn2_w_gate, v_ffn2_w_up, v_ffn2_w_down):
    names = ["ffn1_norm_pre", "ffn1_norm_post", "ffn1_w_gate", "ffn1_w_up", "ffn1_w_down", "mix_norm_pre",
             "mix_norm_post", "w_in", "b_gate", "conv_w", "attn_sink", "mlstm_norm", "w_out", "ffn2_norm_pre",
             "ffn2_norm_post", "ffn2_w_gate", "ffn2_w_up", "ffn2_w_down"]
    w = dict(zip(names, [ffn1_norm_pre, ffn1_norm_post, ffn1_w_gate, ffn1_w_up, ffn1_w_down, mix_norm_pre,
                         mix_norm_post, w_in, b_gate, conv_w, attn_sink, mlstm_norm, w_out, ffn2_norm_pre,
                         ffn2_norm_post, ffn2_w_gate, ffn2_w_up, ffn2_w_down]))
    mom_m = dict(zip(names, [m_ffn1_norm_pre, m_ffn1_norm_post, m_ffn1_w_gate, m_ffn1_w_up, m_ffn1_w_down,
                             m_mix_norm_pre, m_mix_norm_post, m_w_in, m_b_gate, m_conv_w, m_attn_sink,
                             m_mlstm_norm, m_w_out, m_ffn2_norm_pre, m_ffn2_norm_post, m_ffn2_w_gate,
                             m_ffn2_w_up, m_ffn2_w_down]))
    mom_v = dict(zip(names, [v_ffn1_norm_pre, v_ffn1_norm_post, v_ffn1_w_gate, v_ffn1_w_up, v_ffn1_w_down,
                             v_mix_norm_pre, v_mix_norm_post, v_w_in, v_b_gate, v_conv_w, v_attn_sink,
                             v_mlstm_norm, v_w_out, v_ffn2_norm_pre, v_ffn2_norm_post, v_ffn2_w_gate,
                             v_ffn2_w_up, v_ffn2_w_down]))
    xs = x[0]
    target = loss_target[0]
    S, D = xs.shape
    depth = w_in.shape[0]
    F = ffn1_w_gate.shape[-1]
    in_shard = w_in.shape[-1]
    conv_shard = conv_w.shape[-1]
    chip = 2 * lax.axis_index("x") + lax.axis_index("y")
    cosf, sinf = _rope_tables(S)

    gathered = []
    for l in range(depth):
        half = lambda name: w[name][l].astype(BF16)
        parts = dict(wgu1=[half("ffn1_w_gate"), half("ffn1_w_up")], wd1=[half("ffn1_w_down")],
                     win=[half("w_in")], conv=[jnp.pad(w["conv_w"][l], ((0, 8 - CONV_WIDTH), (0, 0)))],
                     wout=[half("w_out")], wgu2=[half("ffn2_w_gate"), half("ffn2_w_up")], wd2=[half("ffn2_w_down")])
        kernels = ([["wgu1"], ["wd1"], ["win", "conv"], ["wout"], ["wgu2"], ["wd2"]] if l == 0
                   else [["wgu1", "wd1"], ["win", "conv", "wout", "wgu2", "wd2"]])
        got = {}
        for names_k in kernels:
            got.update(zip(names_k, _comm_weight_gather([parts[n] for n in names_k])))
        gathered.append(got)

    def mixer_weights(l, win, wout, convg):
        win_full = jnp.concatenate([win[j] for j in range(4)], axis=-1)
        conv_full = jnp.concatenate([convg[j] for j in range(4)], axis=-1)[:CONV_WIDTH]
        return dict(w_in=jnp.pad(win_full, ((0, 0), (0, IN_PAD - IN_WIDTH))), w_out=wout.reshape(D, D),
                    conv_w=conv_full, g_pre=w["mix_norm_pre"][l], g_post=w["mix_norm_post"][l],
                    b_gate=w["b_gate"][l], sink=w["attn_sink"][l], gamma=w["mlstm_norm"][l])

    h = xs
    saved = []
    layers = []
    for l in range(depth):
        g = gathered[l]
        wgu1, h = _tie(g["wgu1"], h)
        h, s1, wd1 = _ffn_fwd(h, w["ffn1_norm_pre"][l], w["ffn1_norm_post"][l], wgu1, g["wd1"])
        (win, convg), h = _tie((g["win"], g["conv"]), h)
        mix_p = mixer_weights(l, win, g["wout"], convg)
        h, s2 = _mixer_fwd(h, mix_p, cosf, sinf)
        wgu2, h = _tie(g["wgu2"], h)
        h, s3, wd2 = _ffn_fwd(h, w["ffn2_norm_pre"][l], w["ffn2_norm_post"][l], wgu2, g["wd2"])
        layers.append(dict(wgu1=wgu1, wd1=wd1, wgu2=wgu2, wd2=wd2, mix=mix_p))
        saved.append((s1, s2, s3))
    dh, loss_tile = _loss_grad(h, target)

    big = {}
    small_rows = []
    old = None
    for l in reversed(range(depth)):
        p = layers[l]
        s1, s2, s3 = saved[l]
        dh, dg2_pre, dg2_post, d_wgu2, d_wd2 = _ffn_bwd(dh, s3, w["ffn2_norm_pre"][l], w["ffn2_norm_post"][l],
                                                       p["wgu2"], p["wd2"], mid=old["b"].scatter if old else None)
        if old is not None:
            dh = old["a"].settle(old["b"].gather(dh))
        rs_c = _ReduceScatter([d_wgu2, d_wd2], dh)
        dh, sm, d_win, d_wout = _mixer_bwd(rs_c.dh, s2, p["mix"], cosf, sinf, mid=rs_c.scatter)
        dh = rs_c.gather(dh)
        if old is not None:
            dh = old["b"].settle(dh)
        d_win4 = jnp.stack([d_win[:, j * in_shard:(j + 1) * in_shard] for j in range(4)])
        rs_a = _ReduceScatter([d_win4, d_wout.reshape(4, D // 4, D)], dh)
        if l > 0:
            dh, dg1_pre, dg1_post, d_wgu1, d_wd1 = _ffn_bwd(
                rs_a.dh, s1, w["ffn1_norm_pre"][l], w["ffn1_norm_post"][l], p["wgu1"], p["wd1"], mid=rs_a.scatter)
            dh = rs_c.settle(rs_a.gather(dh))
            rs_b = _ReduceScatter([d_wgu1, d_wd1], dh)
            dh = rs_b.dh
        else:
            last = []

            def start_last(d_wgu, d_wd, x):
                last.append(_ReduceScatter([d_wgu, d_wd], x))
                return last[0].dh

            dh, dg1_pre, dg1_post, d_wgu1, d_wd1 = _ffn_bwd(
                rs_a.dh, s1, w["ffn1_norm_pre"][l], w["ffn1_norm_post"][l], p["wgu1"], p["wd1"], mid=rs_a.scatter,
                on_grads=start_last, late=lambda x: last[0].scatter(x))
            dh = rs_c.settle(rs_a.gather(dh))
            rs_b = last[0]
        old = dict(a=rs_a, b=rs_b, c=rs_c)
        big[l] = old
        small_rows.append((l, [dg1_pre, dg1_post, sm["g_pre"], sm["g_post"], dg2_pre, dg2_post, sm["gamma"],
                               sm["conv"], sm["b_gate"], sm["sink"]]))
    big_a = {l: g["a"].result() + g["c"].result() for l, g in big.items()}

    small_rows.sort(key=lambda t: t[0])
    flat = [_pad_lanes(v) for _, vs in small_rows for v in vs] + [loss_tile[0]]
    sizes = [f.shape[0] for f in flat]
    pack = jnp.concatenate(flat)
    pack = jnp.pad(pack, (0, (-pack.shape[0]) % (8 * LANES))).reshape(-1, LANES)
    every = _comm_all_to_all_small(pack)

    grads, deltas, new_m, new_v = {}, {}, {}, {}
    gu2 = jnp.stack([big_a[l][2] for l in range(depth)])
    grads["ffn2_w_gate"], grads["ffn2_w_up"] = gu2[:, :, :F], gu2[:, :, F:]
    grads["ffn2_w_down"] = jnp.stack([big_a[l][3] for l in range(depth)])
    grads["w_in"] = jnp.stack([big_a[l][0] for l in range(depth)])
    grads["w_out"] = jnp.stack([big_a[l][1] for l in range(depth)])
    for n in ("w_in", "w_out", "ffn2_w_gate", "ffn2_w_up", "ffn2_w_down"):
        deltas[n], new_m[n], new_v[n] = _adamw(w[n], grads[n], mom_m[n], mom_v[n])
    every, (deltas, new_m, new_v) = _tie(every, (deltas, new_m, new_v))

    total = _add_n([every[i] for i in range(8)], "small_sum").reshape(-1)
    pieces, off = [], 0
    for n in sizes:
        pieces.append(total[off:off + n])
        off += n
    loss = pieces[-1][0]
    per_layer = [pieces[10 * l:10 * l + 10] for l in range(depth)]

    def stack_small(i, shape):
        n = 1
        for s in shape:
            n *= s
        return jnp.stack([per_layer[l][i][:n].reshape(shape) for l in range(depth)])

    conv_full_grad = stack_small(7, (CONV_WIDTH, 4 * conv_shard))
    grads.update({
        "ffn1_norm_pre": stack_small(0, (D,)), "ffn1_norm_post": stack_small(1, (D,)),
        "mix_norm_pre": stack_small(2, (D,)), "mix_norm_post": stack_small(3, (D,)),
        "ffn2_norm_pre": stack_small(4, (D,)), "ffn2_norm_post": stack_small(5, (D,)),
        "mlstm_norm": stack_small(6, (M_WIDTH,)),
        "conv_w": lax.dynamic_slice_in_dim(conv_full_grad, chip * conv_shard, conv_shard, 2),
        "b_gate": stack_small(8, (N_GATES,)), "attn_sink": stack_small(9, (ATT_HEADS,)),
    })

    last_group = ("ffn1_w_gate", "ffn1_w_up", "ffn1_w_down")
    for n in names:
        if n not in last_group and n not in deltas:
            deltas[n], new_m[n], new_v[n] = _adamw(w[n], grads[n], mom_m[n], mom_v[n])
    deltas, new_m, new_v = big[0]["b"].gather((deltas, new_m, new_v))
    big_b = {l: g["b"].result() for l, g in big.items()}
    gu1 = jnp.stack([big_b[l][0] for l in range(depth)])
    grads["ffn1_w_gate"], grads["ffn1_w_up"] = gu1[:, :, :F], gu1[:, :, F:]
    grads["ffn1_w_down"] = jnp.stack([big_b[l][1] for l in range(depth)])
    for n in last_group:
        deltas[n], new_m[n], new_v[n] = _adamw(w[n], grads[n], mom_m[n], mom_v[n])
    grad_x = dh[None]
    return (loss, grad_x, *[grads[n] for n in names], *[deltas[n] for n in names],
            *[new_m[n] for n in names], *[new_v[n] for n in names])
```
